```python
import math
import jax, jax.numpy as jnp
from jax import lax
import numpy as np

D_MODEL = 1024
BATCH = 8
SEQ = 2048
DEPTH = 1
DEC_BATCH = 128
DEC_SEQ = 4
PAST_LEN = 16384
PAGE_SIZE = 128

D_MIX = D_MODEL
GLA_HEADS = 4
GLA_DV = (D_MIX // 2) // GLA_HEADS
GLA_DK = GLA_DV // 2
GLA_KW = GLA_HEADS * GLA_DK
GLA_VW = GLA_HEADS * GLA_DV
GLA_GATE_RANK = 16
GLA_GATE_TAU = 16.0
SSD_INNER = D_MIX - GLA_VW
SSD_HEAD_DIM = 64
SSD_HEADS = SSD_INNER // SSD_HEAD_DIM
SSD_GROUPS = 2
SSD_STATE = 128
SSD_CONV = 4
SSD_CONV_CH = SSD_INNER + 2 * SSD_GROUPS * SSD_STATE
CHUNK = 64
MEM_LEN = 256
XA_HEADS = 4
XA_HEAD_DIM = D_MODEL // XA_HEADS
D_FF = -(-8 * D_MODEL // (3 * 256)) * 256
IN_SIZES = (GLA_KW, GLA_KW, GLA_VW, GLA_VW, GLA_GATE_RANK, SSD_INNER, SSD_CONV_CH, SSD_HEADS)
IN_COLS = sum(IN_SIZES)
EPS = 1e-6

kernel_name = 'hymba_gla_ssd_xmem_step'


def rmsnorm(x, w):
    xf = x.astype(jnp.float32)
    y = xf * lax.rsqrt(jnp.mean(xf * xf, axis=-1, keepdims=True) + EPS) * w.astype(jnp.float32)
    return y.astype(x.dtype)


def to_chunks(a, c):
    b, t = a.shape[:2]
    return jnp.swapaxes(a.reshape(b, t // c, c, *a.shape[2:]), 0, 1)


def from_chunks(a):
    nc, b, c = a.shape[:3]
    return jnp.swapaxes(a, 0, 1).reshape(b, nc * c, *a.shape[3:])


def gla_chunked(q, k, v, logf, s0):
    c = math.gcd(q.shape[1], CHUNK)
    causal = jnp.tril(jnp.ones((c, c), dtype=bool))

    def step(s, inp):
        qc, kc, vc, gc = inp
        b = jnp.cumsum(gc, axis=1)
        qt = qc * jnp.exp(b)
        kt = kc * jnp.exp(-b)
        att = jnp.where(causal, jnp.einsum('bthk,bshk->bhts', qt, kt), 0.0)
        o = jnp.einsum('bhts,bshv->bthv', att, vc) + jnp.einsum('bthk,bhkv->bthv', qt, s)
        bl = b[:, -1]
        s = jnp.exp(bl)[..., None] * s + jnp.einsum('bshk,bshv->bhkv', kc * jnp.exp(bl[:, None] - b), vc)
        return s, o

    s, o = lax.scan(step, s0, (to_chunks(q, c), to_chunks(k, c), to_chunks(v, c), to_chunks(logf, c)))
    return from_chunks(o), s


def ssd_chunked(x, dt, a_head, bh, ch, s0):
    c = math.gcd(x.shape[1], CHUNK)
    causal = jnp.tril(jnp.ones((c, c), dtype=bool))[None, :, :, None]

    def step(s, inp):
        xc, dtc, bc, cc = inp
        lc = jnp.cumsum(dtc * a_head, axis=1)
        seg = lc[:, :, None, :] - lc[:, None, :, :]
        decay = jnp.exp(jnp.where(causal, seg, -jnp.inf))
        cb = jnp.einsum('bthn,bshn->btsh', cc, bc)
        y = (jnp.einsum('btsh,bshp->bthp', cb * decay * dtc[:, None], xc)
             + jnp.einsum('bthn,bhpn->bthp', cc, s) * jnp.exp(lc)[..., None])
        ll = lc[:, -1]
        w = jnp.exp(ll[:, None] - lc) * dtc
        s = jnp.exp(ll)[:, :, None, None] * s + jnp.einsum('bsh,bshp,bshn->bhpn', w, xc, bc)
        return s, y

    s, y = lax.scan(step, s0, (to_chunks(x, c), to_chunks(dt, c), to_chunks(bh, c), to_chunks(ch, c)))
    return from_chunks(y), s


def mixer(hn, s_gla, s_ssm, conv_buf, lw):
    f32 = jnp.float32
    bsz, t, _ = hn.shape
    cuts = np.cumsum(IN_SIZES)[:-1].tolist()
    q, k, v, g, glr, z, xbc, dt_raw = jnp.split(hn @ lw['w_in'], cuts, axis=-1)
    q = q.reshape(bsz, t, GLA_HEADS, GLA_DK).astype(f32) * (GLA_DK ** -0.5)
    k = k.reshape(bsz, t, GLA_HEADS, GLA_DK).astype(f32)
    v = v.reshape(bsz, t, GLA_HEADS, GLA_DV).astype(f32)
    logf = jax.nn.log_sigmoid((glr @ lw['gla_gate_w2'] + lw['gla_gate_b']).astype(f32)) / GLA_GATE_TAU
    logf = logf.reshape(bsz, t, GLA_HEADS, GLA_DK)
    o, s_gla = gla_chunked(q, k, v, logf, s_gla.astype(f32))
    o = rmsnorm(o, lw['gla_norm_w']) * jax.nn.silu(g.reshape(bsz, t, GLA_HEADS, GLA_DV).astype(f32))
    o_gla = o.reshape(bsz, t, GLA_VW)
    full = jnp.concatenate([conv_buf.astype(xbc.dtype), xbc], axis=1)
    conv = lw['ssd_conv_b']
    for j in range(SSD_CONV):
        conv = conv + full[:, j:j + t] * lw['ssd_conv_w'][j]
    new_buf = full[:, full.shape[1] - (SSD_CONV - 1):]
    xbc_act = jax.nn.silu(conv.astype(f32))
    xs, bm, cm = jnp.split(xbc_act, [SSD_INNER, SSD_INNER + SSD_GROUPS * SSD_STATE], axis=-1)
    xs = xs.reshape(bsz, t, SSD_HEADS, SSD_HEAD_DIM)
    rep = SSD_HEADS // SSD_GROUPS
    bh = jnp.repeat(bm.reshape(bsz, t, SSD_GROUPS, SSD_STATE), rep, axis=2)
    chh = jnp.repeat(cm.reshape(bsz, t, SSD_GROUPS, SSD_STATE), rep, axis=2)
    dt = jax.nn.softplus((dt_raw + lw['ssd_dt_bias']).astype(f32))
    a_head = -jnp.exp(lw['ssd_A_log'].astype(f32))
    y, s_ssm = ssd_chunked(xs, dt, a_head, bh, chh, s_ssm.astype(f32))
    y = y + lw['ssd_D'].astype(f32)[:, None] * xs
    y = y.reshape(bsz, t, SSD_INNER) * jax.nn.silu(z.astype(f32))
    gs = SSD_INNER // SSD_GROUPS
    y = rmsnorm(y.reshape(bsz, t, SSD_GROUPS, gs), lw['ssd_norm_w'].reshape(SSD_GROUPS, gs)).reshape(bsz, t, SSD_INNER)
    mixed = jnp.concatenate([o_gla, y], axis=-1).astype(hn.dtype) @ lw['w_out']
    return mixed, s_gla, s_ssm, new_buf


def mem_kv(mem, mem_norm_w, w_xk, w_xv):
    b, m, _ = mem.shape
    mn = rmsnorm(mem, mem_norm_w)
    mk = (mn @ w_xk).reshape(b, m, XA_HEADS, XA_HEAD_DIM)
    mv = (mn @ w_xv).reshape(b, m, XA_HEADS, XA_HEAD_DIM)
    return mk, mv


def cross_attend(hn, mk, mv, w_xq, w_xo):
    b, t, _ = hn.shape
    q = (hn @ w_xq).reshape(b, t, XA_HEADS, XA_HEAD_DIM).astype(jnp.float32)
    s = jnp.einsum('bthd,bmhd->bhtm', q, mk.astype(jnp.float32)) * (XA_HEAD_DIM ** -0.5)
    p = jax.nn.softmax(s, axis=-1)
    o = jnp.einsum('bhtm,bmhd->bthd', p, mv.astype(jnp.float32)).reshape(b, t, D_MODEL)
    return o.astype(hn.dtype) @ w_xo


def decoder_layer(x, s_gla, s_ssm, conv_buf, mk, mv, lw):
    m, s_gla, s_ssm, conv_buf = mixer(rmsnorm(x, lw['ln_mix_pre']), s_gla, s_ssm, conv_buf, lw)
    h = x + rmsnorm(m, lw['ln_mix_post'])
    a = cross_attend(rmsnorm(h, lw['ln_xa_pre']), mk, mv, lw['w_xq'], lw['w_xo'])
    h = h + rmsnorm(a, lw['ln_xa_post'])
    hf = rmsnorm(h, lw['ln_ffn_pre'])
    f = (jax.nn.silu(hf @ lw['w_gate']) * (hf @ lw['w_up'])) @ lw['w_down']
    h = h + rmsnorm(f, lw['ln_ffn_post'])
    return h, s_gla, s_ssm, conv_buf


def setup_inputs(seed: int = 0) -> dict:
    key = jax.random.key(seed)
    ks = iter(jax.random.split(key, 48))

    def nrm(shape, scale):
        return jax.random.normal(next(ks), shape, jnp.float32) * scale

    def gain(n):
        return 1.0 + nrm((DEPTH, n), 0.02)

    L = DEPTH
    dt0 = jnp.exp(jax.random.uniform(next(ks), (L, SSD_HEADS), jnp.float32,
                                     math.log(1e-3), math.log(1e-1)))
    dt_bias = dt0 + jnp.log(-jnp.expm1(-dt0))
    a_log = jnp.log(jax.random.uniform(next(ks), (L, SSD_HEADS), jnp.float32, 1.0, 16.0))
    return {
        'x_prompt': nrm((BATCH, SEQ, D_MODEL), 1.0),
        'x_sample': nrm((DEC_BATCH, DEC_SEQ, D_MODEL), 1.0),
        'mem_prompt': nrm((BATCH, MEM_LEN, D_MODEL), 1.0),
        'state_gla': nrm((L, DEC_BATCH, GLA_HEADS, GLA_DK, GLA_DV), 0.1),
        'state_ssm': nrm((L, DEC_BATCH, SSD_HEADS, SSD_HEAD_DIM, SSD_STATE), 0.1),
        'state_conv': nrm((L, DEC_BATCH, SSD_CONV - 1, SSD_CONV_CH), 1.0),
        'cache_mem_k': nrm((L, DEC_BATCH, MEM_LEN, XA_HEADS, XA_HEAD_DIM), 1.0),
        'cache_mem_v': nrm((L, DEC_BATCH, MEM_LEN, XA_HEADS, XA_HEAD_DIM), 1.0),
        'ln_mix_pre': gain(D_MODEL),
        'ln_mix_post': gain(D_MODEL),
        'w_in': nrm((L, D_MODEL, IN_COLS), D_MODEL ** -0.5),
        'gla_gate_w2': nrm((L, GLA_GATE_RANK, GLA_KW), GLA_GATE_RANK ** -0.5),
        'gla_gate_b': nrm((L, GLA_KW), 0.1),
        'gla_norm_w': gain(GLA_DV),
        'ssd_conv_w': nrm((L, SSD_CONV, SSD_CONV_CH), SSD_CONV ** -0.5),
        'ssd_conv_b': nrm((L, SSD_CONV_CH), 0.02),
        'ssd_dt_bias': dt_bias,
        'ssd_A_log': a_log,
        'ssd_D': 1.0 + nrm((L, SSD_HEADS), 0.1),
        'ssd_norm_w': gain(SSD_INNER),
        'w_out': nrm((L, D_MIX, D_MODEL), D_MIX ** -0.5),
        'ln_xa_pre': gain(D_MODEL),
        'ln_xa_post': gain(D_MODEL),
        'mem_norm_w': gain(D_MODEL),
        'w_xq': nrm((L, D_MODEL, D_MODEL), D_MODEL ** -0.5),
        'w_xk': nrm((L, D_MODEL, D_MODEL), D_MODEL ** -0.5),
        'w_xv': nrm((L, D_MODEL, D_MODEL), D_MODEL ** -0.5),
        'w_xo': nrm((L, D_MODEL, D_MODEL), D_MODEL ** -0.5),
        'ln_ffn_pre': gain(D_MODEL),
        'ln_ffn_post': gain(D_MODEL),
        'w_gate': nrm((L, D_MODEL, D_FF), D_MODEL ** -0.5),
        'w_up': nrm((L, D_MODEL, D_FF), D_MODEL ** -0.5),
        'w_down': nrm((L, D_FF, D_MODEL), D_FF ** -0.5),
    }


def reference(x_prompt, x_sample, mem_prompt, state_gla, state_ssm, state_conv, cache_mem_k, cache_mem_v,
              ln_mix_pre, ln_mix_post, w_in, gla_gate_w2, gla_gate_b, gla_norm_w, ssd_conv_w, ssd_conv_b,
              ssd_dt_bias, ssd_A_log, ssd_D, ssd_norm_w, w_out, ln_xa_pre, ln_xa_post, mem_norm_w,
              w_xq, w_xk, w_xv, w_xo, ln_ffn_pre, ln_ffn_post, w_gate, w_up, w_down):
    bp = x_prompt.shape[0]
    hp, hs = x_prompt, x_sample
    gla_p, ssm_p, conv_p, mk_p, mv_p = [], [], [], [], []
    gla_s, ssm_s, conv_s = [], [], []
    for l in range(DEPTH):
        lw = dict(ln_mix_pre=ln_mix_pre[l], ln_mix_post=ln_mix_post[l], w_in=w_in[l],
                  gla_gate_w2=gla_gate_w2[l], gla_gate_b=gla_gate_b[l], gla_norm_w=gla_norm_w[l],
                  ssd_conv_w=ssd_conv_w[l], ssd_conv_b=ssd_conv_b[l], ssd_dt_bias=ssd_dt_bias[l],
                  ssd_A_log=ssd_A_log[l], ssd_D=ssd_D[l], ssd_norm_w=ssd_norm_w[l], w_out=w_out[l],
                  ln_xa_pre=ln_xa_pre[l], ln_xa_post=ln_xa_post[l], w_xq=w_xq[l], w_xo=w_xo[l],
                  ln_ffn_pre=ln_ffn_pre[l], ln_ffn_post=ln_ffn_post[l],
                  w_gate=w_gate[l], w_up=w_up[l], w_down=w_down[l])
        mk, mv = mem_kv(mem_prompt, mem_norm_w[l], w_xk[l], w_xv[l])
        s0_gla = jnp.zeros((bp, GLA_HEADS, GLA_DK, GLA_DV), jnp.float32)
        s0_ssm = jnp.zeros((bp, SSD_HEADS, SSD_HEAD_DIM, SSD_STATE), jnp.float32)
        c0 = jnp.zeros((bp, SSD_CONV - 1, SSD_CONV_CH), x_prompt.dtype)
        hp, sg, ss, cb = decoder_layer(hp, s0_gla, s0_ssm, c0, mk, mv, lw)
        gla_p.append(sg.astype(state_gla.dtype))
        ssm_p.append(ss.astype(state_ssm.dtype))
        conv_p.append(cb.astype(state_conv.dtype))
        mk_p.append(mk.astype(cache_mem_k.dtype))
        mv_p.append(mv.astype(cache_mem_v.dtype))
        hs, sg, ss, cb = decoder_layer(hs, state_gla[l], state_ssm[l], state_conv[l],
                                       cache_mem_k[l], cache_mem_v[l], lw)
        gla_s.append(sg.astype(state_gla.dtype))
        ssm_s.append(ss.astype(state_ssm.dtype))
        conv_s.append(cb.astype(state_conv.dtype))
    return (hp, hs, jnp.stack(gla_p), jnp.stack(ssm_p), jnp.stack(conv_p), jnp.stack(mk_p), jnp.stack(mv_p),
            jnp.stack(gla_s), jnp.stack(ssm_s), jnp.stack(conv_s))
```

```python
import functools

import jax
import jax.numpy as jnp
from jax import lax
from jax.experimental import pallas as pl
from jax.experimental.pallas import tpu as pltpu

F32 = jnp.float32
BF16 = jnp.bfloat16

D_MODEL = 1024
GLA_HEADS = 4
GLA_DK = 64
GLA_DV = 128
GLA_KW = GLA_HEADS * GLA_DK
GLA_VW = GLA_HEADS * GLA_DV
GLA_GATE_RANK = 16
GLA_GATE_TAU = 16.0
SSD_INNER = 512
SSD_HEAD_DIM = 64
SSD_HEADS = 8
SSD_GROUPS = 2
SSD_STATE = 128
SSD_CONV = 4
SSD_CONV_CH = 1024
XA_HEADS = 4
XA_HEAD_DIM = 256
D_FF = 2816
EPS = 1e-6

COL_Q = 0
COL_K = 256
COL_V = 512
COL_G = 1024
COL_Z = 1536
COL_XBC = 2048
COL_SMALL = 3072
DT_LANE = GLA_GATE_RANK
PROJ_COLS = 3200

GLA_CHUNK = 64
TILE = 256
SAMPLE_SEQS = 16
ATTN_SEQS = 4
VMEM_LIMIT = 56 * 1024 * 1024


def _bf(x):
    return x.astype(BF16)


def _dot(a, b):
    return jnp.dot(a, b, preferred_element_type=F32)


def _dot_nt(a, b):
    return lax.dot_general(a, b, (((1,), (1,)), ((), ())), preferred_element_type=F32)


def _dot_tn(a, b):
    return lax.dot_general(a, b, (((0,), (0,)), ((), ())), preferred_element_type=F32)


def _dot3(m, a):
    hi = _bf(a)
    r = a - hi.astype(F32)
    mid = _bf(r)
    lo = _bf(r - mid.astype(F32))
    return _dot(m, hi) + _dot(m, mid) + _dot(m, lo)


def _rms(x, w):
    ms = jnp.mean(x * x, axis=-1, keepdims=True)
    return x * lax.rsqrt(ms + EPS) * w


def _silu(x):
    return x / (1.0 + jnp.exp(-x))


def _softplus(x):
    return jnp.maximum(x, 0.0) + jnp.log1p(jnp.exp(-jnp.abs(x)))


def _log_sigmoid(x):
    return jnp.minimum(x, 0.0) - jnp.log1p(jnp.exp(-jnp.abs(x)))


def _seg_masks(rows, seq_len):
    t = lax.broadcasted_iota(jnp.int32, (rows, rows), 0)
    u = lax.broadcasted_iota(jnp.int32, (rows, rows), 1)
    if seq_len == rows:
        same = None
        tril = u <= t
    else:
        shift = seq_len.bit_length() - 1
        assert 1 << shift == seq_len
        same = (t >> shift) == (u >> shift)
        tril = jnp.logical_and(same, u <= t)
    return same, tril


def _mask_to_bf16(mask, rows):
    if mask is None:
        return jnp.ones((rows, rows), BF16)
    return jnp.where(mask, 1.0, 0.0).astype(BF16)


def _row_in_seq(rows, seq_len, s, width):
    r = lax.broadcasted_iota(jnp.int32, (rows, width), 0)
    return jnp.logical_and(r >= s * seq_len, r < (s + 1) * seq_len)


def _pad_rows(x, rows):
    if x.shape[0] >= rows:
        return x
    return jnp.concatenate([x, jnp.zeros((rows - x.shape[0], x.shape[1]), x.dtype)], axis=0)


def _gla_chunk(q, k, v, lf, get_state, put_state, seq_len):
    rows = q.shape[0]
    n_seq = rows // seq_len
    same, tril = _seg_masks(rows, seq_len)
    tril_b = _mask_to_bf16(tril, rows)
    same_b = _mask_to_bf16(same, rows)

    b = _dot3(tril_b, lf)
    bl = _dot3(same_b, lf)
    qt = q * jnp.exp(b)
    kt = k * jnp.exp(-b)
    kd = k * jnp.exp(bl - b)
    bl_t = _pad_rows(bl, 128).T

    lane = lax.broadcasted_iota(jnp.int32, (rows, GLA_KW), 1)
    lhs = _bf(jnp.concatenate(
        [jnp.where((lane >= h * GLA_DK) & (lane < (h + 1) * GLA_DK), qt, 0.0)
         for h in range(GLA_HEADS)], axis=0))
    att = _dot_nt(lhs, _bf(kt))
    t4 = lax.broadcasted_iota(jnp.int32, (GLA_HEADS * rows, rows), 0) & (rows - 1)
    u4 = lax.broadcasted_iota(jnp.int32, (GLA_HEADS * rows, rows), 1)
    causal4 = u4 <= t4
    if n_seq > 1:
        shift = seq_len.bit_length() - 1
        causal4 = jnp.logical_and(causal4, (t4 >> shift) == (u4 >> shift))
    att = _bf(jnp.where(causal4, att, 0.0))
    vb = _bf(v)
    kdb = _bf(kd)

    o_inter = None
    r4 = lax.broadcasted_iota(jnp.int32, (GLA_HEADS * rows, GLA_DV), 0) & (rows - 1)
    for s in range(n_seq):
        st = get_state(s)
        oi = _dot(lhs, _bf(st))
        if n_seq == 1:
            o_inter = oi
            kd_s = kdb
        else:
            m4 = jnp.logical_and(r4 >= s * seq_len, r4 < (s + 1) * seq_len)
            oi = jnp.where(m4, oi, 0.0)
            o_inter = oi if o_inter is None else o_inter + oi
            kd_s = jnp.where(_row_in_seq(rows, seq_len, s, GLA_KW), kdb, jnp.zeros_like(kdb))
        upd = _dot_tn(kd_s, vb)
        upd_d = jnp.concatenate(
            [upd[h * GLA_DK:(h + 1) * GLA_DK, h * GLA_DV:(h + 1) * GLA_DV]
             for h in range(GLA_HEADS)], axis=0)
        c0 = s * seq_len
        decay = jnp.exp(jnp.broadcast_to(bl_t[:, c0:c0 + 1], (GLA_KW, GLA_DV)))
        put_state(s, decay * st + upd_d)

    outs = []
    for h in range(GLA_HEADS):
        o_h = _dot(att[h * rows:(h + 1) * rows], vb[:, h * GLA_DV:(h + 1) * GLA_DV])
        outs.append(o_h + o_inter[h * rows:(h + 1) * rows])
    return outs


def _ssd_tile(xs, bm, cm, dt_raw_blk, dtb_blk, a_blk, d_row, get_state, put_state, seq_len):
    rows = xs.shape[0]
    n_seq = rows // seq_len
    same, tril = _seg_masks(rows, seq_len)
    tril_b = _mask_to_bf16(tril, rows)
    same_b = _mask_to_bf16(same, rows)

    dt = _softplus(dt_raw_blk + dtb_blk)
    dta = dt * a_blk
    lc = _dot3(tril_b, dta)
    ll = _dot3(same_b, dta)
    elc = jnp.exp(lc)
    w = jnp.exp(ll - lc) * dt
    ell = jnp.exp(ll)
    lc_t = _pad_rows(lc, 128).T[:, :rows]
    dt_t = _pad_rows(dt, 128).T[:, :rows]

    bmb = _bf(bm)
    cmb = _bf(cm)
    cb = [_dot_nt(cmb[:, g * SSD_STATE:(g + 1) * SSD_STATE],
                  bmb[:, g * SSD_STATE:(g + 1) * SSD_STATE]) for g in range(SSD_GROUPS)]

    lane128 = lax.broadcasted_iota(jnp.int32, (rows, 128), 1)
    lo128 = lane128 < SSD_HEAD_DIM
    sub128 = lax.broadcasted_iota(jnp.int32, (128, 128), 0) < SSD_HEAD_DIM

    def col(x, h, width):
        return jnp.broadcast_to(x[:, DT_LANE + h:DT_LANE + h + 1], (x.shape[0], width))

    heads_per_group = SSD_HEADS // SSD_GROUPS
    ys = []
    for j in range(SSD_HEADS // 2):
        g = (2 * j) // heads_per_group
        xs_p = xs[:, 128 * j:128 * (j + 1)]
        w_heads = []
        for h in (2 * j, 2 * j + 1):
            seg = col(lc, h, rows) - jnp.broadcast_to(lc_t[DT_LANE + h:DT_LANE + h + 1, :], (rows, rows))
            dtr = jnp.broadcast_to(dt_t[DT_LANE + h:DT_LANE + h + 1, :], (rows, rows))
            w_heads.append(_bf(jnp.where(tril, cb[g] * jnp.exp(seg) * dtr, 0.0)))
        x_lo = _bf(jnp.where(lo128, xs_p, 0.0))
        x_hi = _bf(jnp.where(lo128, 0.0, xs_p))
        y_p = _dot(w_heads[0], x_lo) + _dot(w_heads[1], x_hi)

        elc_p = jnp.where(lo128, col(elc, 2 * j, 128), col(elc, 2 * j + 1, 128))
        w_p = jnp.where(lo128, col(w, 2 * j, 128), col(w, 2 * j + 1, 128))
        xw = _bf(xs_p * w_p)
        bm_g = bmb[:, g * SSD_STATE:(g + 1) * SSD_STATE]
        cm_g = cmb[:, g * SSD_STATE:(g + 1) * SSD_STATE]
        y_inter = None
        for s in range(n_seq):
            st = get_state(s, j)
            ci = _dot_nt(cm_g, _bf(st))
            if n_seq == 1:
                y_inter = ci
                xw_s = xw
            else:
                msk = _row_in_seq(rows, seq_len, s, 128)
                ci = jnp.where(msk, ci, 0.0)
                y_inter = ci if y_inter is None else y_inter + ci
                xw_s = jnp.where(msk, xw, jnp.zeros_like(xw))
            upd = _dot_tn(xw_s, bm_g)
            r0 = s * seq_len
            e0 = jnp.broadcast_to(ell[r0:r0 + 1, DT_LANE + 2 * j:DT_LANE + 2 * j + 1], (128, 128))
            e1 = jnp.broadcast_to(ell[r0:r0 + 1, DT_LANE + 2 * j + 1:DT_LANE + 2 * j + 2], (128, 128))
            put_state(s, j, jnp.where(sub128, e0, e1) * st + upd)
        ys.append(y_p + y_inter * elc_p + d_row[:, 128 * j:128 * (j + 1)] * xs_p)
    return jnp.concatenate(ys, axis=1)


def _gate_logf(small_blk, gw2_ref, gb_ref):
    logits = _dot(_bf(small_blk), gw2_ref[...]) + gb_ref[...]
    return _log_sigmoid(logits) * (1.0 / GLA_GATE_TAU)


def _gla_finish(o_h, g_h, gnw):
    return _rms(o_h, gnw) * _silu(g_h)


def _ssd_finish(y, z, snw):
    y = y * _silu(z)
    gs = SSD_INNER // SSD_GROUPS
    parts = [_rms(y[:, g * gs:(g + 1) * gs], snw[:, g * gs:(g + 1) * gs]) for g in range(SSD_GROUPS)]
    return jnp.concatenate(parts, axis=1)


def _memkv_kernel(mem_ref, nw_ref, wk_ref, wv_ref, k_ref, v_ref, kb_ref, vb_ref):
    mn = _bf(_rms(mem_ref[0], nw_ref[...]))
    mk = _dot(mn, wk_ref[...])
    mv = _dot(mn, wv_ref[...])
    k_ref[0] = mk
    v_ref[0] = mv
    kb_ref[0] = _bf(mk)
    vb_ref[0] = _bf(mv)


def _const_spec(shape):
    nd = len(shape)
    return pl.BlockSpec(shape, lambda *_: (0,) * nd, pipeline_mode=pl.Buffered(1))


def _memkv(mem, nw, wk, wv):
    b, m, d = mem.shape
    blk = pl.BlockSpec((1, m, d), lambda i: (i, 0, 0))
    return pl.pallas_call(
        _memkv_kernel,
        grid=(b,),
        in_specs=[blk, _const_spec((1, d)), _const_spec((d, d)), _const_spec((d, d))],
        out_specs=[blk, blk, blk, blk],
        out_shape=[jax.ShapeDtypeStruct((b, m, d), F32)] * 2 + [jax.ShapeDtypeStruct((b, m, d), BF16)] * 2,
        compiler_params=pltpu.CompilerParams(dimension_semantics=("arbitrary",),
                                             vmem_limit_bytes=VMEM_LIMIT),
        name="mem_kv",
    )(mem, nw, wk, wv)


def _mixer_prompt_kernel(x_ref, lnpre_ref, win_ref, gw2_ref, gb_ref, gnw_ref, cw_ref, cb_ref,
                         dtb_ref, a_ref, d_ref, snw_ref, wout_ref, lnpost_ref,
                         h_ref, sg_out, ss_out, conv_out,
                         proj_s, xpad_s, mixed_s, sg_s, ss_s):
    t = pl.program_id(1)
    rows = TILE

    @pl.when(t == 0)
    def _():
        sg_s[...] = jnp.zeros_like(sg_s)
        ss_s[...] = jnp.zeros_like(ss_s)
        xpad_s[0:8, :] = jnp.zeros((8, SSD_CONV_CH), F32)

    x = x_ref[0]
    hn = _bf(_rms(x, lnpre_ref[...]))
    proj_s[...] = _dot(hn, win_ref[...])

    small = proj_s[:, COL_SMALL:COL_SMALL + 128]
    logf = _gate_logf(small, gw2_ref, gb_ref)

    xbc = proj_s[:, COL_XBC:COL_XBC + SSD_CONV_CH]
    xpad_s[8:8 + rows, :] = xbc
    conv = cb_ref[...]
    for j in range(SSD_CONV):
        conv = conv + xpad_s[5 + j:5 + j + rows, :] * cw_ref[j:j + 1, :]
    xpad_s[5:8, :] = xpad_s[5 + rows:8 + rows, :]
    xa = _silu(conv)

    def get_ss(s, j):
        return ss_s[128 * j:128 * (j + 1), :]

    def put_ss(s, j, val):
        ss_s[128 * j:128 * (j + 1), :] = val

    y = _ssd_tile(xa[:, :SSD_INNER], xa[:, SSD_INNER:SSD_INNER + 256], xa[:, SSD_INNER + 256:],
                  small, dtb_ref[...], a_ref[...], d_ref[...], get_ss, put_ss, rows)
    mixed_s[:, GLA_VW:] = _ssd_finish(y, proj_s[:, COL_Z:COL_Z + SSD_INNER], snw_ref[...])

    def get_sg(s):
        return sg_s[...]

    def put_sg(s, val):
        sg_s[...] = val

    for c in range(rows // GLA_CHUNK):
        r0 = c * GLA_CHUNK
        sl = slice(r0, r0 + GLA_CHUNK)
        outs = _gla_chunk(proj_s[sl, COL_Q:COL_Q + GLA_KW], proj_s[sl, COL_K:COL_K + GLA_KW],
                          proj_s[sl, COL_V:COL_V + GLA_VW], logf[sl], get_sg, put_sg, GLA_CHUNK)
        for h in range(GLA_HEADS):
            g_h = proj_s[sl, COL_G + h * GLA_DV:COL_G + (h + 1) * GLA_DV]
            mixed_s[sl, h * GLA_DV:(h + 1) * GLA_DV] = _gla_finish(outs[h], g_h, gnw_ref[...])

    m = _dot(_bf(mixed_s[...]), wout_ref[...])
    h_ref[0] = x + _rms(m, lnpost_ref[...])

    @pl.when(t == pl.num_programs(1) - 1)
    def _():
        sg_out[0] = sg_s[...].reshape(GLA_HEADS, GLA_DK, GLA_DV)
        ss_out[0] = ss_s[...].reshape(SSD_HEADS, SSD_HEAD_DIM, SSD_STATE)
        conv_out[0] = xpad_s[5:8, :]


def _mixer_prompt(x, p):
    b, t, d = x.shape
    nt = t // TILE
    consts = [p["ln_mix_pre"], p["w_in"], p["gw2"], p["gb"], p["gnw"], p["conv_w"], p["conv_b"],
              p["dtb"], p["a_blk"], p["d_row"], p["snw"], p["w_out"], p["ln_mix_post"]]
    tok = pl.BlockSpec((1, TILE, d), lambda i, j: (i, j, 0))
    return pl.pallas_call(
        _mixer_prompt_kernel,
        grid=(b, nt),
        in_specs=[tok] + [_const_spec(c.shape) for c in consts],
        out_specs=[tok,
                   pl.BlockSpec((1, GLA_HEADS, GLA_DK, GLA_DV), lambda i, j: (i, 0, 0, 0)),
                   pl.BlockSpec((1, SSD_HEADS, SSD_HEAD_DIM, SSD_STATE), lambda i, j: (i, 0, 0, 0)),
                   pl.BlockSpec((1, SSD_CONV - 1, SSD_CONV_CH), lambda i, j: (i, 0, 0))],
        out_shape=[jax.ShapeDtypeStruct((b, t, d), F32),
                   jax.ShapeDtypeStruct((b, GLA_HEADS, GLA_DK, GLA_DV), F32),
                   jax.ShapeDtypeStruct((b, SSD_HEADS, SSD_HEAD_DIM, SSD_STATE), F32),
                   jax.ShapeDtypeStruct((b, SSD_CONV - 1, SSD_CONV_CH), F32)],
        scratch_shapes=[pltpu.VMEM((TILE, PROJ_COLS), F32),
                        pltpu.VMEM((TILE + 8, SSD_CONV_CH), F32),
                        pltpu.VMEM((TILE, D_MODEL), F32),
                        pltpu.VMEM((GLA_KW, GLA_DV), F32),
                        pltpu.VMEM((SSD_INNER, SSD_STATE), F32)],
        compiler_params=pltpu.CompilerParams(dimension_semantics=("arbitrary", "arbitrary"),
                                             vmem_limit_bytes=VMEM_LIMIT),
        name="mixer_prompt",
    )(x, *consts)


def _attend(q, mk, mv):
    outs = []
    for h in range(XA_HEADS):
        sl = slice(h * XA_HEAD_DIM, (h + 1) * XA_HEAD_DIM)
        s = _dot_nt(q[:, sl], mk[:, sl])
        e = jnp.exp(s - jnp.max(s, axis=-1, keepdims=True))
        pr = e / jnp.sum(e, axis=-1, keepdims=True)
        outs.append(_dot(_bf(pr), mv[:, sl]))
    return jnp.concatenate(outs, axis=1)


def _ffn(h, lnpre, lnpost, wg_ref, wu_ref, wd_ref):
    hf = _bf(_rms(h, lnpre))
    gt = _dot(hf, wg_ref[...])
    up = _dot(hf, wu_ref[...])
    f = _dot(_bf(_silu(gt) * up), wd_ref[...])
    return h + _rms(f, lnpost)


def _attn_ffn_prompt_kernel(h_ref, mk_ref, mv_ref, lnxa_ref, wq_ref, wo_ref, lnxap_ref,
                            lnf_ref, lnfp_ref, wg_ref, wu_ref, wd_ref, y_ref):
    h = h_ref[0]
    q = _bf(_dot(_bf(_rms(h, lnxa_ref[...])), wq_ref[...]))
    a = _attend(q, mk_ref[0], mv_ref[0])
    a = _dot(_bf(a), wo_ref[...])
    h = h + _rms(a, lnxap_ref[...])
    y_ref[0] = _ffn(h, lnf_ref[...], lnfp_ref[...], wg_ref, wu_ref, wd_ref)


def _attn_ffn_prompt(h, mkb, mvb, p):
    b, t, d = h.shape
    nt = t // TILE
    consts = [p["ln_xa_pre"], p["w_xq"], p["w_xo"], p["ln_xa_post"], p["ln_ffn_pre"], p["ln_ffn_post"],
              p["w_gate"], p["w_up"], p["w_down"]]
    tok = pl.BlockSpec((1, TILE, d), lambda i, j: (i, j, 0))
    mem = pl.BlockSpec((1, mkb.shape[1], d), lambda i, j: (i, 0, 0))
    return pl.pallas_call(
        _attn_ffn_prompt_kernel,
        grid=(b, nt),
        in_specs=[tok, mem, mem] + [_const_spec(c.shape) for c in consts],
        out_specs=tok,
        out_shape=jax.ShapeDtypeStruct((b, t, d), F32),
        compiler_params=pltpu.CompilerParams(dimension_semantics=("arbitrary", "arbitrary"),
                                             vmem_limit_bytes=VMEM_LIMIT),
        name="attn_ffn_prompt",
    )(h, mkb, mvb, *consts)


def _proj_sample_kernel(x_ref, lnpre_ref, win_ref, proj_ref):
    proj_ref[...] = _dot(_bf(_rms(x_ref[...], lnpre_ref[...])), win_ref[...])


def _proj_sample(x, p):
    n, d = x.shape
    return pl.pallas_call(
        _proj_sample_kernel,
        grid=(1,),
        in_specs=[_const_spec((n, d)), _const_spec((1, d)), _const_spec(p["w_in"].shape)],
        out_specs=pl.BlockSpec((n, PROJ_COLS), lambda i: (0, 0)),
        out_shape=jax.ShapeDtypeStruct((n, PROJ_COLS), F32),
        compiler_params=pltpu.CompilerParams(dimension_semantics=("arbitrary",),
                                             vmem_limit_bytes=VMEM_LIMIT),
        name="proj_sample",
    )(x, p["ln_mix_pre"], p["w_in"])


def _mixer_sample_kernel(seq_len, proj_ref, tail_ref, sg_ref, ss_ref, gw2_ref, gb_ref, gnw_ref,
                         cw_ref, cb_ref, dtb_ref, a_ref, d_ref, snw_ref,
                         mixed_ref, sg_out, ss_out, xpad_s, tpad_s):
    rows = proj_ref.shape[0]
    n_seq = rows // seq_len
    small = proj_ref[:, COL_SMALL:COL_SMALL + 128]
    logf = _gate_logf(small, gw2_ref, gb_ref)

    xbc = proj_ref[:, COL_XBC:COL_XBC + SSD_CONV_CH]
    xpad_s[0:8, :] = jnp.zeros((8, SSD_CONV_CH), F32)
    xpad_s[8:8 + rows, :] = xbc
    tpad_s[0:rows, :] = tail_ref[...]
    tpad_s[rows:rows + 8, :] = jnp.zeros((8, SSD_CONV_CH), F32)
    pos = lax.broadcasted_iota(jnp.int32, (rows, SSD_CONV_CH), 0) & (seq_len - 1)
    conv = cb_ref[...]
    for j in range(SSD_CONV - 1):
        back = SSD_CONV - 1 - j
        prev = jnp.where(pos >= back, xpad_s[8 - back:8 - back + rows, :],
                         tpad_s[seq_len - back:seq_len - back + rows, :])
        conv = conv + prev * cw_ref[j:j + 1, :]
    conv = conv + xbc * cw_ref[SSD_CONV - 1:SSD_CONV, :]
    xa = _silu(conv)

    def get_ss(s, j):
        return ss_ref[s, 2 * j:2 * j + 2].reshape(2 * SSD_HEAD_DIM, SSD_STATE)

    def put_ss(s, j, val):
        ss_out[s, 2 * j:2 * j + 2] = val.reshape(2, SSD_HEAD_DIM, SSD_STATE)

    y = _ssd_tile(xa[:, :SSD_INNER], xa[:, SSD_INNER:SSD_INNER + 256], xa[:, SSD_INNER + 256:],
                  small, dtb_ref[...], a_ref[...], d_ref[...], get_ss, put_ss, seq_len)
    mixed_ref[:, GLA_VW:] = _ssd_finish(y, proj_ref[:, COL_Z:COL_Z + SSD_INNER], snw_ref[...])

    def get_sg(s):
        return sg_ref[s].reshape(GLA_KW, GLA_DV)

    def put_sg(s, val):
        sg_out[s] = val.reshape(GLA_HEADS, GLA_DK, GLA_DV)

    outs = _gla_chunk(proj_ref[:, COL_Q:COL_Q + GLA_KW], proj_ref[:, COL_K:COL_K + GLA_KW],
                      proj_ref[:, COL_V:COL_V + GLA_VW], logf, get_sg, put_sg, seq_len)
    for h in range(GLA_HEADS):
        g_h = proj_ref[:, COL_G + h * GLA_DV:COL_G + (h + 1) * GLA_DV]
        mixed_ref[:, h * GLA_DV:(h + 1) * GLA_DV] = _gla_finish(outs[h], g_h, gnw_ref[...])


def _mixer_sample(proj, tail, sg, ss, p, seq_len):
    n = proj.shape[0]
    rows = SAMPLE_SEQS * seq_len
    nb = sg.shape[0]
    consts = [p["gw2"], p["gb"], p["gnw"], p["conv_w"], p["conv_b"], p["dtb"], p["a_blk"], p["d_row"],
              p["snw"]]
    sg_spec = pl.BlockSpec((SAMPLE_SEQS, GLA_HEADS, GLA_DK, GLA_DV), lambda i: (i, 0, 0, 0))
    ss_spec = pl.BlockSpec((SAMPLE_SEQS, SSD_HEADS, SSD_HEAD_DIM, SSD_STATE), lambda i: (i, 0, 0, 0))
    return pl.pallas_call(
        functools.partial(_mixer_sample_kernel, seq_len),
        grid=(nb // SAMPLE_SEQS,),
        in_specs=[pl.BlockSpec((rows, PROJ_COLS), lambda i: (i, 0)),
                  pl.BlockSpec((rows, SSD_CONV_CH), lambda i: (i, 0)),
                  sg_spec, ss_spec] + [_const_spec(c.shape) for c in consts],
        out_specs=[pl.BlockSpec((rows, D_MODEL), lambda i: (i, 0)), sg_spec, ss_spec],
        out_shape=[jax.ShapeDtypeStruct((n, D_MODEL), F32),
                   jax.ShapeDtypeStruct(sg.shape, F32), jax.ShapeDtypeStruct(ss.shape, F32)],
        scratch_shapes=[pltpu.VMEM((rows + 8, SSD_CONV_CH), F32),
                        pltpu.VMEM((rows + 8, SSD_CONV_CH), F32)],
        compiler_params=pltpu.CompilerParams(dimension_semantics=("arbitrary",),
                                             vmem_limit_bytes=VMEM_LIMIT),
        name="mixer_sample",
    )(proj, tail, sg, ss, *consts)


def _post_mix_sample_kernel(x_ref, mixed_ref, wout_ref, lnpost_ref, lnxa_ref, wq_ref, h_ref, q_ref):
    m = _dot(_bf(mixed_ref[...]), wout_ref[...])
    h = x_ref[...] + _rms(m, lnpost_ref[...])
    h_ref[...] = h
    q_ref[...] = _bf(_dot(_bf(_rms(h, lnxa_ref[...])), wq_ref[...]))


def _post_mix_sample(x, mixed, p):
    n, d = x.shape
    full = pl.BlockSpec((n, d), lambda i: (0, 0))
    consts = [p["w_out"], p["ln_mix_post"], p["ln_xa_pre"], p["w_xq"]]
    return pl.pallas_call(
        _post_mix_sample_kernel,
        grid=(1,),
        in_specs=[full, full] + [_const_spec(c.shape) for c in consts],
        out_specs=[full, full],
        out_shape=[jax.ShapeDtypeStruct((n, d), F32), jax.ShapeDtypeStruct((n, d), BF16)],
        compiler_params=pltpu.CompilerParams(dimension_semantics=("arbitrary",),
                                             vmem_limit_bytes=VMEM_LIMIT),
        name="post_mix_sample",
    )(x, mixed, *consts)


def _attn_sample_kernel(seq_len, q_ref, k_ref, v_ref, a_ref):
    rows = q_ref.shape[0]
    q = q_ref[...]
    out = jnp.zeros((rows, D_MODEL), F32)
    for s in range(rows // seq_len):
        o = _attend(q, _bf(k_ref[s]), _bf(v_ref[s]))
        out = jnp.where(_row_in_seq(rows, seq_len, s, D_MODEL), o, out)
    a_ref[...] = out


def _attn_sample(q, ck, cv, seq_len):
    n, d = q.shape
    nb, m, _ = ck.shape
    rows = ATTN_SEQS * seq_len
    kv = pl.BlockSpec((ATTN_SEQS, m, d), lambda i: (i, 0, 0))
    tok = pl.BlockSpec((rows, d), lambda i: (i, 0))
    return pl.pallas_call(
        functools.partial(_attn_sample_kernel, seq_len),
        grid=(nb // ATTN_SEQS,),
        in_specs=[tok, kv, kv],
        out_specs=tok,
        out_shape=jax.ShapeDtypeStruct((n, d), F32),
        compiler_params=pltpu.CompilerParams(dimension_semantics=("arbitrary",),
                                             vmem_limit_bytes=VMEM_LIMIT),
        name="attn_sample",
    )(q, ck, cv)


def _post_attn_sample_kernel(h_ref, a_ref, wo_ref, lnxap_ref, lnf_ref, lnfp_ref, wg_ref, wu_ref, wd_ref,
                             y_ref):
    a = _dot(_bf(a_ref[...]), wo_ref[...])
    h = h_ref[...] + _rms(a, lnxap_ref[...])
    y_ref[...] = _ffn(h, lnf_ref[...], lnfp_ref[...], wg_ref, wu_ref, wd_ref)


def _post_attn_sample(h, a, p):
    n, d = h.shape
    full = pl.BlockSpec((n, d), lambda i: (0, 0))
    consts = [p["w_xo"], p["ln_xa_post"], p["ln_ffn_pre"], p["ln_ffn_post"], p["w_gate"], p["w_up"],
              p["w_down"]]
    return pl.pallas_call(
        _post_attn_sample_kernel,
        grid=(1,),
        in_specs=[full, full] + [_const_spec(c.shape) for c in consts],
        out_specs=full,
        out_shape=jax.ShapeDtypeStruct((n, d), F32),
        compiler_params=pltpu.CompilerParams(dimension_semantics=("arbitrary",),
                                             vmem_limit_bytes=VMEM_LIMIT),
        name="post_attn_sample",
    )(h, a, *consts)


def _pack_params(ln_mix_pre, ln_mix_post, w_in, gla_gate_w2, gla_gate_b, gla_norm_w, ssd_conv_w,
                 ssd_conv_b, ssd_dt_bias, ssd_A_log, ssd_D, ssd_norm_w, w_out, ln_xa_pre, ln_xa_post,
                 mem_norm_w, w_xq, w_xk, w_xv, w_xo, ln_ffn_pre, ln_ffn_post, w_gate, w_up, w_down):
    sizes = (GLA_KW, GLA_KW, GLA_VW, GLA_VW, GLA_GATE_RANK, SSD_INNER, SSD_CONV_CH, SSD_HEADS)
    offs = [0]
    for sz in sizes:
        offs.append(offs[-1] + sz)
    wq, wk, wv, wg, wglr, wz, wxbc, wdt = [w_in[:, offs[i]:offs[i + 1]] for i in range(len(sizes))]
    pad = jnp.zeros((D_MODEL, PROJ_COLS - COL_SMALL - GLA_GATE_RANK - SSD_HEADS), F32)
    w_in_p = jnp.concatenate([wq * (GLA_DK ** -0.5), wk, wv, wg, wz, wxbc, wglr, wdt, pad], axis=1)

    def small_row(v):
        return jnp.zeros((1, 128), F32).at[0, DT_LANE:DT_LANE + SSD_HEADS].set(v)

    row = lambda v: v.reshape(1, -1)
    return {
        "ln_mix_pre": row(ln_mix_pre), "ln_mix_post": row(ln_mix_post),
        "w_in": _bf(w_in_p),
        "gw2": _bf(jnp.zeros((128, GLA_KW), F32).at[:GLA_GATE_RANK].set(gla_gate_w2)),
        "gb": row(gla_gate_b), "gnw": row(gla_norm_w),
        "conv_w": ssd_conv_w, "conv_b": row(ssd_conv_b),
        "dtb": small_row(ssd_dt_bias),
        "a_blk": small_row(-jnp.exp(ssd_A_log)),
        "d_row": jnp.repeat(ssd_D, SSD_HEAD_DIM).reshape(1, SSD_INNER),
        "snw": row(ssd_norm_w),
        "w_out": _bf(w_out),
        "ln_xa_pre": row(ln_xa_pre), "ln_xa_post": row(ln_xa_post), "mem_norm_w": row(mem_norm_w),
        "w_xq": _bf(w_xq * (XA_HEAD_DIM ** -0.5)), "w_xk": _bf(w_xk), "w_xv": _bf(w_xv), "w_xo": _bf(w_xo),
        "ln_ffn_pre": row(ln_ffn_pre), "ln_ffn_post": row(ln_ffn_post),
        "w_gate": _bf(w_gate), "w_up": _bf(w_up), "w_down": _bf(w_down),
    }


def kernel(x_prompt, x_sample, mem_prompt, state_gla, state_ssm, state_conv, cache_mem_k, cache_mem_v,
           ln_mix_pre, ln_mix_post, w_in, gla_gate_w2, gla_gate_b, gla_norm_w, ssd_conv_w, ssd_conv_b,
           ssd_dt_bias, ssd_A_log, ssd_D, ssd_norm_w, w_out, ln_xa_pre, ln_xa_post, mem_norm_w,
           w_xq, w_xk, w_xv, w_xo, ln_ffn_pre, ln_ffn_post, w_gate, w_up, w_down):
    assert w_in.shape[0] == 1, "single-layer kernel"
    layer = [a[0] for a in (ln_mix_pre, ln_mix_post, w_in, gla_gate_w2, gla_gate_b, gla_norm_w, ssd_conv_w,
                            ssd_conv_b, ssd_dt_bias, ssd_A_log, ssd_D, ssd_norm_w, w_out, ln_xa_pre,
                            ln_xa_post, mem_norm_w, w_xq, w_xk, w_xv, w_xo, ln_ffn_pre, ln_ffn_post,
                            w_gate, w_up, w_down)]
    p = _pack_params(*layer)
    bp, mem_len, _ = mem_prompt.shape
    bs, ts, _ = x_sample.shape
    assert ts == SSD_CONV, "the padded conv-tail layout assumes one new row per conv tap"

    mk, mv, mkb, mvb = _memkv(mem_prompt, p["mem_norm_w"], p["w_xk"], p["w_xv"])
    h_p, gla_p, ssm_p, conv_p = _mixer_prompt(x_prompt, p)
    y_p = _attn_ffn_prompt(h_p, mkb, mvb, p)

    xs = x_sample.reshape(bs * ts, D_MODEL)
    proj = _proj_sample(xs, p)
    tail = jnp.pad(state_conv[0], ((0, 0), (1, 0), (0, 0))).reshape(bs * ts, SSD_CONV_CH)
    mixed, gla_s, ssm_s = _mixer_sample(proj, tail, state_gla[0], state_ssm[0], p, ts)
    h_s, q_s = _post_mix_sample(xs, mixed, p)
    a_s = _attn_sample(q_s, cache_mem_k[0].reshape(bs, mem_len, D_MODEL),
                       cache_mem_v[0].reshape(bs, mem_len, D_MODEL), ts)
    y_s = _post_attn_sample(h_s, a_s, p)
    conv_s = proj[:, COL_XBC:COL_XBC + SSD_CONV_CH].reshape(bs, ts, SSD_CONV_CH)[:, ts - (SSD_CONV - 1):]

    kv_shape = (1, bp, mem_len, XA_HEADS, XA_HEAD_DIM)
    return (y_p, y_s.reshape(bs, ts, D_MODEL), gla_p[None], ssm_p[None], conv_p[None],
            mk.reshape(kv_shape), mv.reshape(kv_shape), gla_s[None], ssm_s[None], conv_s[None])
```

```python
import functools

import jax
import jax.numpy as jnp
from jax import lax
from jax.experimental import pallas as pl
from jax.experimental.pallas import tpu as pltpu

F32 = jnp.float32
BF16 = jnp.bfloat16

D_MODEL = 1024
GLA_HEADS = 4
GLA_DK = 64
GLA_DV = 128
GLA_KW = GLA_HEADS * GLA_DK
GLA_VW = GLA_HEADS * GLA_DV
GLA_GATE_RANK = 16
GLA_GATE_TAU = 16.0
SSD_INNER = 512
SSD_HEAD_DIM = 64
SSD_HEADS = 8
SSD_GROUPS = 2
SSD_STATE = 128
SSD_CONV = 4
SSD_CONV_CH = 1024
XA_HEADS = 4
XA_HEAD_DIM = 256
D_FF = 2816
EPS = 1e-6

COL_Q = 0
COL_K = 256
COL_V = 512
COL_G = 1024
COL_Z = 1536
COL_XBC = 2048
COL_SMALL = 3072
DT_LANE = GLA_GATE_RANK
PROJ_COLS = 3200

GLA_CHUNK = 64
TILE = 256
SAMPLE_SEQS = 16
ATTN_SEQS = 4
VMEM_LIMIT = 56 * 1024 * 1024


def _bf(x):
    return x.astype(BF16)


def _dot(a, b):
    return jnp.dot(a, b, preferred_element_type=F32)


def _dot_nt(a, b):
    return lax.dot_general(a, b, (((1,), (1,)), ((), ())), preferred_element_type=F32)


def _dot_tn(a, b):
    return lax.dot_general(a, b, (((0,), (0,)), ((), ())), preferred_element_type=F32)


def _dot3(m, a):
    hi = _bf(a)
    r = a - hi.astype(F32)
    mid = _bf(r)
    lo = _bf(r - mid.astype(F32))
    return _dot(m, hi) + _dot(m, mid) + _dot(m, lo)


def _rms(x, w):
    ms = jnp.mean(x * x, axis=-1, keepdims=True)
    return x * lax.rsqrt(ms + EPS) * w


def _silu(x):
    return x / (1.0 + jnp.exp(-x))


def _softplus(x):
    return jnp.maximum(x, 0.0) + jnp.log1p(jnp.exp(-jnp.abs(x)))


def _log_sigmoid(x):
    return jnp.minimum(x, 0.0) - jnp.log1p(jnp.exp(-jnp.abs(x)))


def _seg_masks(rows, seq_len):
    t = lax.broadcasted_iota(jnp.int32, (rows, rows), 0)
    u = lax.broadcasted_iota(jnp.int32, (rows, rows), 1)
    if seq_len == rows:
        same = None
        tril = u <= t
    else:
        shift = seq_len.bit_length() - 1
        assert 1 << shift == seq_len
        same = (t >> shift) == (u >> shift)
        tril = jnp.logical_and(same, u <= t)
    return same, tril


def _mask_to_bf16(mask, rows):
    if mask is None:
        return jnp.ones((rows, rows), BF16)
    return jnp.where(mask, 1.0, 0.0).astype(BF16)


def _row_in_seq(rows, seq_len, s, width):
    r = lax.broadcasted_iota(jnp.int32, (rows, width), 0)
    return jnp.logical_and(r >= s * seq_len, r < (s + 1) * seq_len)


def _pad_rows(x, rows):
    if x.shape[0] >= rows:
        return x
    return jnp.concatenate([x, jnp.zeros((rows - x.shape[0], x.shape[1]), x.dtype)], axis=0)


def _gla_chunk(q, k, v, lf, get_state, put_state, seq_len):
    rows = q.shape[0]
    n_seq = rows // seq_len
    same, tril = _seg_masks(rows, seq_len)
    tril_b = _mask_to_bf16(tril, rows)
    same_b = _mask_to_bf16(same, rows)

    b = _dot3(tril_b, lf)
    bl = _dot3(same_b, lf)
    qt = q * jnp.exp(b)
    kt = k * jnp.exp(-b)
    kd = k * jnp.exp(bl - b)
    bl_t = _pad_rows(bl, 128).T

    lane = lax.broadcasted_iota(jnp.int32, (rows, GLA_KW), 1)
    lhs = _bf(jnp.concatenate(
        [jnp.where((lane >= h * GLA_DK) & (lane < (h + 1) * GLA_DK), qt, 0.0)
         for h in range(GLA_HEADS)], axis=0))
    att = _dot_nt(lhs, _bf(kt))
    t4 = lax.broadcasted_iota(jnp.int32, (GLA_HEADS * rows, rows), 0) & (rows - 1)
    u4 = lax.broadcasted_iota(jnp.int32, (GLA_HEADS * rows, rows), 1)
    causal4 = u4 <= t4
    if n_seq > 1:
        shift = seq_len.bit_length() - 1
        causal4 = jnp.logical_and(causal4, (t4 >> shift) == (u4 >> shift))
    att = _bf(jnp.where(causal4, att, 0.0))
    vb = _bf(v)
    kdb = _bf(kd)

    o_inter = None
    r4 = lax.broadcasted_iota(jnp.int32, (GLA_HEADS * rows, GLA_DV), 0) & (rows - 1)
    for s in range(n_seq):
        st = get_state(s)
        oi = _dot(lhs, _bf(st))
        if n_seq == 1:
            o_inter = oi
            kd_s = kdb
        else:
            m4 = jnp.logical_and(r4 >= s * seq_len, r4 < (s + 1) * seq_len)
            oi = jnp.where(m4, oi, 0.0)
            o_inter = oi if o_inter is None else o_inter + oi
            kd_s = jnp.where(_row_in_seq(rows, seq_len, s, GLA_KW), kdb, jnp.zeros_like(kdb))
        upd = _dot_tn(kd_s, vb)
        upd_d = jnp.concatenate(
            [upd[h * GLA_DK:(h + 1) * GLA_DK, h * GLA_DV:(h + 1) * GLA_DV]
             for h in range(GLA_HEADS)], axis=0)
        c0 = s * seq_len
        decay = jnp.exp(jnp.broadcast_to(bl_t[:, c0:c0 + 1], (GLA_KW, GLA_DV)))
        put_state(s, decay * st + upd_d)

    outs = []
    for h in range(GLA_HEADS):
        o_h = _dot(att[h * rows:(h + 1) * rows], vb[:, h * GLA_DV:(h + 1) * GLA_DV])
        outs.append(o_h + o_inter[h * rows:(h + 1) * rows])
    return outs


def _ssd_tile(xs, bm, cm, dt_raw_blk, dtb_blk, a_blk, d_row, get_state, put_state, seq_len):
    rows = xs.shape[0]
    n_seq = rows // seq_len
    same, tril = _seg_masks(rows, seq_len)
    tril_b = _mask_to_bf16(tril, rows)
    same_b = _mask_to_bf16(same, rows)

    dt = _softplus(dt_raw_blk + dtb_blk)
    dta = dt * a_blk
    lc = _dot3(tril_b, dta)
    ll = _dot3(same_b, dta)
    elc = jnp.exp(lc)
    w = jnp.exp(ll - lc) * dt
    ell = jnp.exp(ll)
    lc_t = _pad_rows(lc, 128).T[:, :rows]
    dt_t = _pad_rows(dt, 128).T[:, :rows]

    bmb = _bf(bm)
    cmb = _bf(cm)
    cb = [_dot_nt(cmb[:, g * SSD_STATE:(g + 1) * SSD_STATE],
                  bmb[:, g * SSD_STATE:(g + 1) * SSD_STATE]) for g in range(SSD_GROUPS)]

    lane128 = lax.broadcasted_iota(jnp.int32, (rows, 128), 1)
    lo128 = lane128 < SSD_HEAD_DIM
    sub128 = lax.broadcasted_iota(jnp.int32, (128, 128), 0) < SSD_HEAD_DIM

    def col(x, h, width):
        return jnp.broadcast_to(x[:, DT_LANE + h:DT_LANE + h + 1], (x.shape[0], width))

    heads_per_group = SSD_HEADS // SSD_GROUPS
    ys = []
    for j in range(SSD_HEADS // 2):
        g = (2 * j) // heads_per_group
        xs_p = xs[:, 128 * j:128 * (j + 1)]
        w_heads = []
        for h in (2 * j, 2 * j + 1):
            seg = col(lc, h, rows) - jnp.broadcast_to(lc_t[DT_LANE + h:DT_LANE + h + 1, :], (rows, rows))
            dtr = jnp.broadcast_to(dt_t[DT_LANE + h:DT_LANE + h + 1, :], (rows, rows))
            w_heads.append(_bf(jnp.where(tril, cb[g] * jnp.exp(seg) * dtr, 0.0)))
        x_lo = _bf(jnp.where(lo128, xs_p, 0.0))
        x_hi = _bf(jnp.where(lo128, 0.0, xs_p))
        y_p = _dot(w_heads[0], x_lo) + _dot(w_heads[1], x_hi)

        elc_p = jnp.where(lo128, col(elc, 2 * j, 128), col(elc, 2 * j + 1, 128))
        w_p = jnp.where(lo128, col(w, 2 * j, 128), col(w, 2 * j + 1, 128))
        xw = _bf(xs_p * w_p)
        bm_g = bmb[:, g * SSD_STATE:(g + 1) * SSD_STATE]
        cm_g = cmb[:, g * SSD_STATE:(g + 1) * SSD_STATE]
        y_inter = None
        for s in range(n_seq):
            st = get_state(s, j)
            ci = _dot_nt(cm_g, _bf(st))
            if n_seq == 1:
                y_inter = ci
                xw_s = xw
            else:
                msk = _row_in_seq(rows, seq_len, s, 128)
                ci = jnp.where(msk, ci, 0.0)
                y_inter = ci if y_inter is None else y_inter + ci
                xw_s = jnp.where(msk, xw, jnp.zeros_like(xw))
            upd = _dot_tn(xw_s, bm_g)
            r0 = s * seq_len
            e0 = jnp.broadcast_to(ell[r0:r0 + 1, DT_LANE + 2 * j:DT_LANE + 2 * j + 1], (128, 128))
            e1 = jnp.broadcast_to(ell[r0:r0 + 1, DT_LANE + 2 * j + 1:DT_LANE + 2 * j + 2], (128, 128))
            put_state(s, j, jnp.where(sub128, e0, e1) * st + upd)
        ys.append(y_p + y_inter * elc_p + d_row[:, 128 * j:128 * (j + 1)] * xs_p)
    return jnp.concatenate(ys, axis=1)


def _gate_logf(small_blk, gw2_ref, gb_ref):
    logits = _dot(_bf(small_blk), gw2_ref[...]) + gb_ref[...]
    return _log_sigmoid(logits) * (1.0 / GLA_GATE_TAU)


def _gla_finish(o_h, g_h, gnw):
    return _rms(o_h, gnw) * _silu(g_h)


def _ssd_finish(y, z, snw):
    y = y * _silu(z)
    gs = SSD_INNER // SSD_GROUPS
    parts = [_rms(y[:, g * gs:(g + 1) * gs], snw[:, g * gs:(g + 1) * gs]) for g in range(SSD_GROUPS)]
    return jnp.concatenate(parts, axis=1)


def _memkv_kernel(mem_ref, nw_ref, wk_ref, wv_ref, k_ref, v_ref, kb_ref, vb_ref):
    mn = _bf(_rms(mem_ref[0], nw_ref[...]))
    mk = _dot(mn, wk_ref[...])
    mv = _dot(mn, wv_ref[...])
    for h in range(XA_HEADS):
        k_ref[0, 0, :, h, :] = mk[:, h * XA_HEAD_DIM:(h + 1) * XA_HEAD_DIM]
        v_ref[0, 0, :, h, :] = mv[:, h * XA_HEAD_DIM:(h + 1) * XA_HEAD_DIM]
    kb_ref[0] = _bf(mk)
    vb_ref[0] = _bf(mv)


def _const_spec(shape):
    nd = len(shape)
    return pl.BlockSpec(shape, lambda *_: (0,) * nd, pipeline_mode=pl.Buffered(1))


def _memkv(mem, nw, wk, wv):
    b, m, d = mem.shape
    blk = pl.BlockSpec((1, m, d), lambda i: (i, 0, 0))
    cache_blk = pl.BlockSpec((1, 1, m, XA_HEADS, XA_HEAD_DIM), lambda i: (0, i, 0, 0, 0))
    cache_shape = jax.ShapeDtypeStruct((1, b, m, XA_HEADS, XA_HEAD_DIM), F32)
    return pl.pallas_call(
        _memkv_kernel,
        grid=(b,),
        in_specs=[blk, _const_spec((1, d)), _const_spec((d, d)), _const_spec((d, d))],
        out_specs=[cache_blk, cache_blk, blk, blk],
        out_shape=[cache_shape] * 2 + [jax.ShapeDtypeStruct((b, m, d), BF16)] * 2,
        compiler_params=pltpu.CompilerParams(dimension_semantics=("arbitrary",),
                                             vmem_limit_bytes=VMEM_LIMIT),
        name="mem_kv",
    )(mem, nw, wk, wv)


def _mixer_prompt_kernel(x_ref, lnpre_ref, win_ref, gw2_ref, gb_ref, gnw_ref, cw_ref, cb_ref,
                         dtb_ref, a_ref, d_ref, snw_ref, wout_ref, lnpost_ref,
                         h_ref, sg_out, ss_out, conv_out,
                         proj_s, xpad_s, mixed_s, sg_s, ss_s):
    t = pl.program_id(1)
    rows = TILE

    @pl.when(t == 0)
    def _():
        sg_s[...] = jnp.zeros_like(sg_s)
        ss_s[...] = jnp.zeros_like(ss_s)
        xpad_s[0:8, :] = jnp.zeros((8, SSD_CONV_CH), F32)

    x = x_ref[0]
    hn = _bf(_rms(x, lnpre_ref[...]))
    proj_s[...] = _dot(hn, win_ref[...])

    small = proj_s[:, COL_SMALL:COL_SMALL + 128]
    logf = _gate_logf(small, gw2_ref, gb_ref)

    xbc = proj_s[:, COL_XBC:COL_XBC + SSD_CONV_CH]
    xpad_s[8:8 + rows, :] = xbc
    conv = cb_ref[...]
    for j in range(SSD_CONV):
        conv = conv + xpad_s[5 + j:5 + j + rows, :] * cw_ref[j:j + 1, :]
    xpad_s[5:8, :] = xpad_s[5 + rows:8 + rows, :]
    xa = _silu(conv)

    def get_ss(s, j):
        return ss_s[128 * j:128 * (j + 1), :]

    def put_ss(s, j, val):
        ss_s[128 * j:128 * (j + 1), :] = val

    y = _ssd_tile(xa[:, :SSD_INNER], xa[:, SSD_INNER:SSD_INNER + 256], xa[:, SSD_INNER + 256:],
                  small, dtb_ref[...], a_ref[...], d_ref[...], get_ss, put_ss, rows)
    mixed_s[:, GLA_VW:] = _ssd_finish(y, proj_s[:, COL_Z:COL_Z + SSD_INNER], snw_ref[...])

    def get_sg(s):
        return sg_s[...]

    def put_sg(s, val):
        sg_s[...] = val

    for c in range(rows // GLA_CHUNK):
        r0 = c * GLA_CHUNK
        sl = slice(r0, r0 + GLA_CHUNK)
        outs = _gla_chunk(proj_s[sl, COL_Q:COL_Q + GLA_KW], proj_s[sl, COL_K:COL_K + GLA_KW],
                          proj_s[sl, COL_V:COL_V + GLA_VW], logf[sl], get_sg, put_sg, GLA_CHUNK)
        for h in range(GLA_HEADS):
            g_h = proj_s[sl, COL_G + h * GLA_DV:COL_G + (h + 1) * GLA_DV]
            mixed_s[sl, h * GLA_DV:(h + 1) * GLA_DV] = _gla_finish(outs[h], g_h, gnw_ref[...])

    m = _dot(_bf(mixed_s[...]), wout_ref[...])
    h_ref[0] = x + _rms(m, lnpost_ref[...])

    @pl.when(t == pl.num_programs(1) - 1)
    def _():
        sg_out[0] = sg_s[...].reshape(GLA_HEADS, GLA_DK, GLA_DV)
        ss_out[0] = ss_s[...].reshape(SSD_HEADS, SSD_HEAD_DIM, SSD_STATE)
        conv_out[0] = xpad_s[5:8, :]


def _mixer_prompt(x, p):
    b, t, d = x.shape
    nt = t // TILE
    consts = [p["ln_mix_pre"], p["w_in"], p["gw2"], p["gb"], p["gnw"], p["conv_w"], p["conv_b"],
              p["dtb"], p["a_blk"], p["d_row"], p["snw"], p["w_out"], p["ln_mix_post"]]
    tok = pl.BlockSpec((1, TILE, d), lambda i, j: (i, j, 0))
    return pl.pallas_call(
        _mixer_prompt_kernel,
        grid=(b, nt),
        in_specs=[tok] + [_const_spec(c.shape) for c in consts],
        out_specs=[tok,
                   pl.BlockSpec((1, GLA_HEADS, GLA_DK, GLA_DV), lambda i, j: (i, 0, 0, 0)),
                   pl.BlockSpec((1, SSD_HEADS, SSD_HEAD_DIM, SSD_STATE), lambda i, j: (i, 0, 0, 0)),
                   pl.BlockSpec((1, SSD_CONV - 1, SSD_CONV_CH), lambda i, j: (i, 0, 0))],
        out_shape=[jax.ShapeDtypeStruct((b, t, d), F32),
                   jax.ShapeDtypeStruct((b, GLA_HEADS, GLA_DK, GLA_DV), F32),
                   jax.ShapeDtypeStruct((b, SSD_HEADS, SSD_HEAD_DIM, SSD_STATE), F32),
                   jax.ShapeDtypeStruct((b, SSD_CONV - 1, SSD_CONV_CH), F32)],
        scratch_shapes=[pltpu.VMEM((TILE, PROJ_COLS), F32),
                        pltpu.VMEM((TILE + 8, SSD_CONV_CH), F32),
                        pltpu.VMEM((TILE, D_MODEL), F32),
                        pltpu.VMEM((GLA_KW, GLA_DV), F32),
                        pltpu.VMEM((SSD_INNER, SSD_STATE), F32)],
        compiler_params=pltpu.CompilerParams(dimension_semantics=("arbitrary", "arbitrary"),
                                             vmem_limit_bytes=VMEM_LIMIT),
        name="mixer_prompt",
    )(x, *consts)


def _attend(q, get_k, get_v):
    outs = []
    for h in range(XA_HEADS):
        s = _dot_nt(q[:, h * XA_HEAD_DIM:(h + 1) * XA_HEAD_DIM], get_k(h))
        e = jnp.exp(s - jnp.max(s, axis=-1, keepdims=True))
        pr = e / jnp.sum(e, axis=-1, keepdims=True)
        outs.append(_dot(_bf(pr), get_v(h)))
    return jnp.concatenate(outs, axis=1)


def _ffn(h, lnpre, lnpost, wg_ref, wu_ref, wd_ref):
    hf = _bf(_rms(h, lnpre))
    gt = _dot(hf, wg_ref[...])
    up = _dot(hf, wu_ref[...])
    f = _dot(_bf(_silu(gt) * up), wd_ref[...])
    return h + _rms(f, lnpost)


def _attn_ffn_prompt_kernel(h_ref, mk_ref, mv_ref, lnxa_ref, wq_ref, wo_ref, lnxap_ref,
                            lnf_ref, lnfp_ref, wg_ref, wu_ref, wd_ref, y_ref):
    h = h_ref[0]
    q = _bf(_dot(_bf(_rms(h, lnxa_ref[...])), wq_ref[...]))
    a = _attend(q, lambda hd: mk_ref[0, :, hd * XA_HEAD_DIM:(hd + 1) * XA_HEAD_DIM],
                lambda hd: mv_ref[0, :, hd * XA_HEAD_DIM:(hd + 1) * XA_HEAD_DIM])
    a = _dot(_bf(a), wo_ref[...])
    h = h + _rms(a, lnxap_ref[...])
    y_ref[0] = _ffn(h, lnf_ref[...], lnfp_ref[...], wg_ref, wu_ref, wd_ref)


def _attn_ffn_prompt(h, mkb, mvb, p):
    b, t, d = h.shape
    nt = t // TILE
    consts = [p["ln_xa_pre"], p["w_xq"], p["w_xo"], p["ln_xa_post"], p["ln_ffn_pre"], p["ln_ffn_post"],
              p["w_gate"], p["w_up"], p["w_down"]]
    tok = pl.BlockSpec((1, TILE, d), lambda i, j: (i, j, 0))
    mem = pl.BlockSpec((1, mkb.shape[1], d), lambda i, j: (i, 0, 0))
    return pl.pallas_call(
        _attn_ffn_prompt_kernel,
        grid=(b, nt),
        in_specs=[tok, mem, mem] + [_const_spec(c.shape) for c in consts],
        out_specs=tok,
        out_shape=jax.ShapeDtypeStruct((b, t, d), F32),
        compiler_params=pltpu.CompilerParams(dimension_semantics=("arbitrary", "arbitrary"),
                                             vmem_limit_bytes=VMEM_LIMIT),
        name="attn_ffn_prompt",
    )(h, mkb, mvb, *consts)


def _proj_sample_kernel(x_ref, lnpre_ref, win_ref, proj_ref):
    proj_ref[...] = _dot(_bf(_rms(x_ref[...], lnpre_ref[...])), win_ref[...])


def _proj_sample(x, p):
    n, d = x.shape
    return pl.pallas_call(
        _proj_sample_kernel,
        grid=(1,),
        in_specs=[_const_spec((n, d)), _const_spec((1, d)), _const_spec(p["w_in"].shape)],
        out_specs=pl.BlockSpec((n, PROJ_COLS), lambda i: (0, 0)),
        out_shape=jax.ShapeDtypeStruct((n, PROJ_COLS), F32),
        compiler_params=pltpu.CompilerParams(dimension_semantics=("arbitrary",),
                                             vmem_limit_bytes=VMEM_LIMIT),
        name="proj_sample",
    )(x, p["ln_mix_pre"], p["w_in"])


def _mixer_sample_kernel(seq_len, proj_ref, tail_ref, sg_ref, ss_ref, gw2_ref, gb_ref, gnw_ref,
                         cw_ref, cb_ref, dtb_ref, a_ref, d_ref, snw_ref,
                         mixed_ref, sg_out, ss_out, xpad_s, tpad_s):
    rows = proj_ref.shape[0]
    n_seq = rows // seq_len
    small = proj_ref[:, COL_SMALL:COL_SMALL + 128]
    logf = _gate_logf(small, gw2_ref, gb_ref)

    xbc = proj_ref[:, COL_XBC:COL_XBC + SSD_CONV_CH]
    xpad_s[0:8, :] = jnp.zeros((8, SSD_CONV_CH), F32)
    xpad_s[8:8 + rows, :] = xbc
    tpad_s[0:rows, :] = tail_ref[...]
    tpad_s[rows:rows + 8, :] = jnp.zeros((8, SSD_CONV_CH), F32)
    pos = lax.broadcasted_iota(jnp.int32, (rows, SSD_CONV_CH), 0) & (seq_len - 1)
    conv = cb_ref[...]
    for j in range(SSD_CONV - 1):
        back = SSD_CONV - 1 - j
        prev = jnp.where(pos >= back, xpad_s[8 - back:8 - back + rows, :],
                         tpad_s[seq_len - back:seq_len - back + rows, :])
        conv = conv + prev * cw_ref[j:j + 1, :]
    conv = conv + xbc * cw_ref[SSD_CONV - 1:SSD_CONV, :]
    xa = _silu(conv)

    def get_ss(s, j):
        return ss_ref[s, 2 * j:2 * j + 2].reshape(2 * SSD_HEAD_DIM, SSD_STATE)

    def put_ss(s, j, val):
        ss_out[s, 2 * j:2 * j + 2] = val.reshape(2, SSD_HEAD_DIM, SSD_STATE)

    y = _ssd_tile(xa[:, :SSD_INNER], xa[:, SSD_INNER:SSD_INNER + 256], xa[:, SSD_INNER + 256:],
                  small, dtb_ref[...], a_ref[...], d_ref[...], get_ss, put_ss, seq_len)
    mixed_ref[:, GLA_VW:] = _ssd_finish(y, proj_ref[:, COL_Z:COL_Z + SSD_INNER], snw_ref[...])

    def get_sg(s):
        return sg_ref[s].reshape(GLA_KW, GLA_DV)

    def put_sg(s, val):
        sg_out[s] = val.reshape(GLA_HEADS, GLA_DK, GLA_DV)

    outs = _gla_chunk(proj_ref[:, COL_Q:COL_Q + GLA_KW], proj_ref[:, COL_K:COL_K + GLA_KW],
                      proj_ref[:, COL_V:COL_V + GLA_VW], logf, get_sg, put_sg, seq_len)
    for h in range(GLA_HEADS):
        g_h = proj_ref[:, COL_G + h * GLA_DV:COL_G + (h + 1) * GLA_DV]
        mixed_ref[:, h * GLA_DV:(h + 1) * GLA_DV] = _gla_finish(outs[h], g_h, gnw_ref[...])


def _mixer_sample(proj, tail, sg, ss, p, seq_len):
    n = proj.shape[0]
    rows = SAMPLE_SEQS * seq_len
    nb = sg.shape[0]
    consts = [p["gw2"], p["gb"], p["gnw"], p["conv_w"], p["conv_b"], p["dtb"], p["a_blk"], p["d_row"],
              p["snw"]]
    sg_spec = pl.BlockSpec((SAMPLE_SEQS, GLA_HEADS, GLA_DK, GLA_DV), lambda i: (i, 0, 0, 0))
    ss_spec = pl.BlockSpec((SAMPLE_SEQS, SSD_HEADS, SSD_HEAD_DIM, SSD_STATE), lambda i: (i, 0, 0, 0))
    return pl.pallas_call(
        functools.partial(_mixer_sample_kernel, seq_len),
        grid=(nb // SAMPLE_SEQS,),
        in_specs=[pl.BlockSpec((rows, PROJ_COLS), lambda i: (i, 0)),
                  pl.BlockSpec((rows, SSD_CONV_CH), lambda i: (i, 0)),
                  sg_spec, ss_spec] + [_const_spec(c.shape) for c in consts],
        out_specs=[pl.BlockSpec((rows, D_MODEL), lambda i: (i, 0)), sg_spec, ss_spec],
        out_shape=[jax.ShapeDtypeStruct((n, D_MODEL), F32),
                   jax.ShapeDtypeStruct(sg.shape, F32), jax.ShapeDtypeStruct(ss.shape, F32)],
        scratch_shapes=[pltpu.VMEM((rows + 8, SSD_CONV_CH), F32),
                        pltpu.VMEM((rows + 8, SSD_CONV_CH), F32)],
        compiler_params=pltpu.CompilerParams(dimension_semantics=("arbitrary",),
                                             vmem_limit_bytes=VMEM_LIMIT),
        name="mixer_sample",
    )(proj, tail, sg, ss, *consts)


def _post_mix_sample_kernel(x_ref, mixed_ref, wout_ref, lnpost_ref, lnxa_ref, wq_ref, h_ref, q_ref):
    m = _dot(_bf(mixed_ref[...]), wout_ref[...])
    h = x_ref[...] + _rms(m, lnpost_ref[...])
    h_ref[...] = h
    q_ref[...] = _bf(_dot(_bf(_rms(h, lnxa_ref[...])), wq_ref[...]))


def _post_mix_sample(x, mixed, p):
    n, d = x.shape
    full = pl.BlockSpec((n, d), lambda i: (0, 0))
    consts = [p["w_out"], p["ln_mix_post"], p["ln_xa_pre"], p["w_xq"]]
    return pl.pallas_call(
        _post_mix_sample_kernel,
        grid=(1,),
        in_specs=[full, full] + [_const_spec(c.shape) for c in consts],
        out_specs=[full, full],
        out_shape=[jax.ShapeDtypeStruct((n, d), F32), jax.ShapeDtypeStruct((n, d), BF16)],
        compiler_params=pltpu.CompilerParams(dimension_semantics=("arbitrary",),
                                             vmem_limit_bytes=VMEM_LIMIT),
        name="post_mix_sample",
    )(x, mixed, *consts)


def _attn_sample_kernel(seq_len, q_ref, k_ref, v_ref, a_ref):
    rows = q_ref.shape[0]
    q = q_ref[...]
    out = jnp.zeros((rows, D_MODEL), F32)
    for s in range(rows // seq_len):
        o = _attend(q, lambda hd: _bf(k_ref[0, s, :, hd, :]), lambda hd: _bf(v_ref[0, s, :, hd, :]))
        out = jnp.where(_row_in_seq(rows, seq_len, s, D_MODEL), o, out)
    a_ref[...] = out


def _attn_sample(q, ck, cv, seq_len):
    n, d = q.shape
    _, nb, m, nh, hd = ck.shape
    rows = ATTN_SEQS * seq_len
    kv = pl.BlockSpec((1, ATTN_SEQS, m, nh, hd), lambda i: (0, i, 0, 0, 0))
    tok = pl.BlockSpec((rows, d), lambda i: (i, 0))
    return pl.pallas_call(
        functools.partial(_attn_sample_kernel, seq_len),
        grid=(nb // ATTN_SEQS,),
        in_specs=[tok, kv, kv],
        out_specs=tok,
        out_shape=jax.ShapeDtypeStruct((n, d), F32),
        compiler_params=pltpu.CompilerParams(dimension_semantics=("arbitrary",),
                                             vmem_limit_bytes=VMEM_LIMIT),
        name="attn_sample",
    )(q, ck, cv)


def _post_attn_sample_kernel(h_ref, a_ref, wo_ref, lnxap_ref, lnf_ref, lnfp_ref, wg_ref, wu_ref, wd_ref,
                             y_ref):
    a = _dot(_bf(a_ref[...]), wo_ref[...])
    h = h_ref[...] + _rms(a, lnxap_ref[...])
    y_ref[...] = _ffn(h, lnf_ref[...], lnfp_ref[...], wg_ref, wu_ref, wd_ref)


def _post_attn_sample(h, a, p):
    n, d = h.shape
    full = pl.BlockSpec((n, d), lambda i: (0, 0))
    consts = [p["w_xo"], p["ln_xa_post"], p["ln_ffn_pre"], p["ln_ffn_post"], p["w_gate"], p["w_up"],
              p["w_down"]]
    return pl.pallas_call(
        _post_attn_sample_kernel,
        grid=(1,),
        in_specs=[full, full] + [_const_spec(c.shape) for c in consts],
        out_specs=full,
        out_shape=jax.ShapeDtypeStruct((n, d), F32),
        compiler_params=pltpu.CompilerParams(dimension_semantics=("arbitrary",),
                                             vmem_limit_bytes=VMEM_LIMIT),
        name="post_attn_sample",
    )(h, a, *consts)


def _pack_params(ln_mix_pre, ln_mix_post, w_in, gla_gate_w2, gla_gate_b, gla_norm_w, ssd_conv_w,
                 ssd_conv_b, ssd_dt_bias, ssd_A_log, ssd_D, ssd_norm_w, w_out, ln_xa_pre, ln_xa_post,
                 mem_norm_w, w_xq, w_xk, w_xv, w_xo, ln_ffn_pre, ln_ffn_post, w_gate, w_up, w_down):
    sizes = (GLA_KW, GLA_KW, GLA_VW, GLA_VW, GLA_GATE_RANK, SSD_INNER, SSD_CONV_CH, SSD_HEADS)
    offs = [0]
    for sz in sizes:
        offs.append(offs[-1] + sz)
    wq, wk, wv, wg, wglr, wz, wxbc, wdt = [w_in[:, offs[i]:offs[i + 1]] for i in range(len(sizes))]
    pad = jnp.zeros((D_MODEL, PROJ_COLS - COL_SMALL - GLA_GATE_RANK - SSD_HEADS), F32)
    w_in_p = jnp.concatenate([wq * (GLA_DK ** -0.5), wk, wv, wg, wz, wxbc, wglr, wdt, pad], axis=1)

    def small_row(v):
        return jnp.zeros((1, 128), F32).at[0, DT_LANE:DT_LANE + SSD_HEADS].set(v)

    row = lambda v: v.reshape(1, -1)
    return {
        "ln_mix_pre": row(ln_mix_pre), "ln_mix_post": row(ln_mix_post),
        "w_in": _bf(w_in_p),
        "gw2": _bf(jnp.zeros((128, GLA_KW), F32).at[:GLA_GATE_RANK].set(gla_gate_w2)),
        "gb": row(gla_gate_b), "gnw": row(gla_norm_w),
        "conv_w": ssd_conv_w, "conv_b": row(ssd_conv_b),
        "dtb": small_row(ssd_dt_bias),
        "a_blk": small_row(-jnp.exp(ssd_A_log)),
        "d_row": jnp.repeat(ssd_D, SSD_HEAD_DIM).reshape(1, SSD_INNER),
        "snw": row(ssd_norm_w),
        "w_out": _bf(w_out),
        "ln_xa_pre": row(ln_xa_pre), "ln_xa_post": row(ln_xa_post), "mem_norm_w": row(mem_norm_w),
        "w_xq": _bf(w_xq * (XA_HEAD_DIM ** -0.5)), "w_xk": _bf(w_xk), "w_xv": _bf(w_xv), "w_xo": _bf(w_xo),
        "ln_ffn_pre": row(ln_ffn_pre), "ln_ffn_post": row(ln_ffn_post),
        "w_gate": _bf(w_gate), "w_up": _bf(w_up), "w_down": _bf(w_down),
    }


def kernel(x_prompt, x_sample, mem_prompt, state_gla, state_ssm, state_conv, cache_mem_k, cache_mem_v,
           ln_mix_pre, ln_mix_post, w_in, gla_gate_w2, gla_gate_b, gla_norm_w, ssd_conv_w, ssd_conv_b,
           ssd_dt_bias, ssd_A_log, ssd_D, ssd_norm_w, w_out, ln_xa_pre, ln_xa_post, mem_norm_w,
           w_xq, w_xk, w_xv, w_xo, ln_ffn_pre, ln_ffn_post, w_gate, w_up, w_down):
    assert w_in.shape[0] == 1, "single-layer kernel"
    layer = [a[0] for a in (ln_mix_pre, ln_mix_post, w_in, gla_gate_w2, gla_gate_b, gla_norm_w, ssd_conv_w,
                            ssd_conv_b, ssd_dt_bias, ssd_A_log, ssd_D, ssd_norm_w, w_out, ln_xa_pre,
                            ln_xa_post, mem_norm_w, w_xq, w_xk, w_xv, w_xo, ln_ffn_pre, ln_ffn_post,
                            w_gate, w_up, w_down)]
    p = _pack_params(*layer)
    bp, mem_len, _ = mem_prompt.shape
    bs, ts, _ = x_sample.shape
    assert ts == SSD_CONV, "the padded conv-tail layout assumes one new row per conv tap"

    mk, mv, mkb, mvb = _memkv(mem_prompt, p["mem_norm_w"], p["w_xk"], p["w_xv"])
    h_p, gla_p, ssm_p, conv_p = _mixer_prompt(x_prompt, p)
    y_p = _attn_ffn_prompt(h_p, mkb, mvb, p)

    xs = x_sample.reshape(bs * ts, D_MODEL)
    proj = _proj_sample(xs, p)
    tail = jnp.pad(state_conv[0], ((0, 0), (1, 0), (0, 0))).reshape(bs * ts, SSD_CONV_CH)
    mixed, gla_s, ssm_s = _mixer_sample(proj, tail, state_gla[0], state_ssm[0], p, ts)
    h_s, q_s = _post_mix_sample(xs, mixed, p)
    a_s = _attn_sample(q_s, cache_mem_k, cache_mem_v, ts)
    y_s = _post_attn_sample(h_s, a_s, p)
    conv_s = proj[:, COL_XBC:COL_XBC + SSD_CONV_CH].reshape(bs, ts, SSD_CONV_CH)[:, ts - (SSD_CONV - 1):]

    return (y_p, y_s.reshape(bs, ts, D_MODEL), gla_p[None], ssm_p[None], conv_p[None],
            mk, mv, gla_s[None], ssm_s[None], conv_s[None])
```

```python
import functools

import jax
import jax.numpy as jnp
from jax import lax
from jax.experimental import pallas as pl
from jax.experimental.pallas import tpu as pltpu

F32 = jnp.float32
BF16 = jnp.bfloat16

D_MODEL = 1024
GLA_HEADS = 4
GLA_DK = 64
GLA_DV = 128
GLA_KW = GLA_HEADS * GLA_DK
GLA_VW = GLA_HEADS * GLA_DV
GLA_GATE_RANK = 16
GLA_GATE_TAU = 16.0
SSD_INNER = 512
SSD_HEAD_DIM = 64
SSD_HEADS = 8
SSD_GROUPS = 2
SSD_STATE = 128
SSD_CONV = 4
SSD_CONV_CH = 1024
XA_HEADS = 4
XA_HEAD_DIM = 256
D_FF = 2816
EPS = 1e-6

COL_Q = 0
COL_K = 256
COL_V = 512
COL_G = 1024
COL_Z = 1536
COL_XBC = 2048
COL_SMALL = 3072
DT_LANE = GLA_GATE_RANK
PROJ_COLS = 3200

GLA_CHUNK = 64
TILE = 256
SAMPLE_SEQS = 16
ATTN_SEQS = 4
VMEM_LIMIT = 56 * 1024 * 1024


def _bf(x):
    return x.astype(BF16)


def _dot(a, b):
    return jnp.dot(a, b, preferred_element_type=F32)


def _dot_nt(a, b):
    return lax.dot_general(a, b, (((1,), (1,)), ((), ())), preferred_element_type=F32)


def _dot_tn(a, b):
    return lax.dot_general(a, b, (((0,), (0,)), ((), ())), preferred_element_type=F32)


def _dot3(m, a):
    hi = _bf(a)
    r = a - hi.astype(F32)
    mid = _bf(r)
    lo = _bf(r - mid.astype(F32))
    return _dot(m, hi) + _dot(m, mid) + _dot(m, lo)


def _rms(x, w):
    ms = jnp.mean(x * x, axis=-1, keepdims=True)
    return x * lax.rsqrt(ms + EPS) * w


def _silu(x):
    return x / (1.0 + jnp.exp(-x))


def _softplus(x):
    return jnp.maximum(x, 0.0) + jnp.log1p(jnp.exp(-jnp.abs(x)))


def _log_sigmoid(x):
    return jnp.minimum(x, 0.0) - jnp.log1p(jnp.exp(-jnp.abs(x)))


def _seg_masks(rows, seq_len):
    t = lax.broadcasted_iota(jnp.int32, (rows, rows), 0)
    u = lax.broadcasted_iota(jnp.int32, (rows, rows), 1)
    if seq_len == rows:
        same = None
        tril = u <= t
    else:
        shift = seq_len.bit_length() - 1
        assert 1 << shift == seq_len
        same = (t >> shift) == (u >> shift)
        tril = jnp.logical_and(same, u <= t)
    return same, tril


def _mask_to_bf16(mask, rows):
    if mask is None:
        return jnp.ones((rows, rows), BF16)
    return jnp.where(mask, 1.0, 0.0).astype(BF16)


def _row_in_seq(rows, seq_len, s, width):
    r = lax.broadcasted_iota(jnp.int32, (rows, width), 0)
    return jnp.logical_and(r >= s * seq_len, r < (s + 1) * seq_len)


def _pad_rows(x, rows):
    if x.shape[0] >= rows:
        return x
    return jnp.concatenate([x, jnp.zeros((rows - x.shape[0], x.shape[1]), x.dtype)], axis=0)


def _gla_chunk(q, k, v, lf, get_state, put_state, seq_len):
    rows = q.shape[0]
    n_seq = rows // seq_len
    same, tril = _seg_masks(rows, seq_len)
    tril_b = _mask_to_bf16(tril, rows)
    same_b = _mask_to_bf16(same, rows)

    b = _dot3(tril_b, lf)
    bl = _dot3(same_b, lf)
    qt = q * jnp.exp(b)
    kt = k * jnp.exp(-b)
    kd = k * jnp.exp(bl - b)
    bl_t = _pad_rows(bl, 128).T

    lane = lax.broadcasted_iota(jnp.int32, (rows, GLA_KW), 1)
    lhs = _bf(jnp.concatenate(
        [jnp.where((lane >= h * GLA_DK) & (lane < (h + 1) * GLA_DK), qt, 0.0)
         for h in range(GLA_HEADS)], axis=0))
    att = _dot_nt(lhs, _bf(kt))
    t4 = lax.broadcasted_iota(jnp.int32, (GLA_HEADS * rows, rows), 0) & (rows - 1)
    u4 = lax.broadcasted_iota(jnp.int32, (GLA_HEADS * rows, rows), 1)
    causal4 = u4 <= t4
    if n_seq > 1:
        shift = seq_len.bit_length() - 1
        causal4 = jnp.logical_and(causal4, (t4 >> shift) == (u4 >> shift))
    att = _bf(jnp.where(causal4, att, 0.0))
    vb = _bf(v)
    kdb = _bf(kd)

    o_inter = None
    r4 = lax.broadcasted_iota(jnp.int32, (GLA_HEADS * rows, GLA_DV), 0) & (rows - 1)
    for s in range(n_seq):
        st = get_state(s)
        oi = _dot(lhs, _bf(st))
        if n_seq == 1:
            o_inter = oi
            kd_s = kdb
        else:
            m4 = jnp.logical_and(r4 >= s * seq_len, r4 < (s + 1) * seq_len)
            oi = jnp.where(m4, oi, 0.0)
            o_inter = oi if o_inter is None else o_inter + oi
            kd_s = jnp.where(_row_in_seq(rows, seq_len, s, GLA_KW), kdb, jnp.zeros_like(kdb))
        upd = _dot_tn(kd_s, vb)
        upd_d = jnp.concatenate(
            [upd[h * GLA_DK:(h + 1) * GLA_DK, h * GLA_DV:(h + 1) * GLA_DV]
             for h in range(GLA_HEADS)], axis=0)
        c0 = s * seq_len
        decay = jnp.exp(jnp.broadcast_to(bl_t[:, c0:c0 + 1], (GLA_KW, GLA_DV)))
        put_state(s, decay * st + upd_d)

    outs = []
    for h in range(GLA_HEADS):
        o_h = _dot(att[h * rows:(h + 1) * rows], vb[:, h * GLA_DV:(h + 1) * GLA_DV])
        outs.append(o_h + o_inter[h * rows:(h + 1) * rows])
    return outs


def _ssd_tile(xs, bm, cm, dt_raw_blk, dtb_blk, a_blk, d_row, get_state, put_state, seq_len):
    rows = xs.shape[0]
    n_seq = rows // seq_len
    same, tril = _seg_masks(rows, seq_len)
    tril_b = _mask_to_bf16(tril, rows)
    same_b = _mask_to_bf16(same, rows)

    dt = _softplus(dt_raw_blk + dtb_blk)
    dta = dt * a_blk
    lc = _dot3(tril_b, dta)
    ll = _dot3(same_b, dta)
    elc = jnp.exp(lc)
    w = jnp.exp(ll - lc) * dt
    ell = jnp.exp(ll)
    lc_t = _pad_rows(lc, 128).T[:, :rows]
    dt_t = _pad_rows(dt, 128).T[:, :rows]

    bmb = _bf(bm)
    cmb = _bf(cm)
    cb = [_dot_nt(cmb[:, g * SSD_STATE:(g + 1) * SSD_STATE],
                  bmb[:, g * SSD_STATE:(g + 1) * SSD_STATE]) for g in range(SSD_GROUPS)]

    lane128 = lax.broadcasted_iota(jnp.int32, (rows, 128), 1)
    lo128 = lane128 < SSD_HEAD_DIM
    sub128 = lax.broadcasted_iota(jnp.int32, (128, 128), 0) < SSD_HEAD_DIM

    def col(x, h, width):
        return jnp.broadcast_to(x[:, DT_LANE + h:DT_LANE + h + 1], (x.shape[0], width))

    heads_per_group = SSD_HEADS // SSD_GROUPS
    ys = []
    for j in range(SSD_HEADS // 2):
        g = (2 * j) // heads_per_group
        xs_p = xs[:, 128 * j:128 * (j + 1)]
        w_heads = []
        for h in (2 * j, 2 * j + 1):
            seg = col(lc, h, rows) - jnp.broadcast_to(lc_t[DT_LANE + h:DT_LANE + h + 1, :], (rows, rows))
            dtr = jnp.broadcast_to(dt_t[DT_LANE + h:DT_LANE + h + 1, :], (rows, rows))
            w_heads.append(_bf(jnp.where(tril, cb[g] * jnp.exp(seg) * dtr, 0.0)))
        x_lo = _bf(jnp.where(lo128, xs_p, 0.0))
        x_hi = _bf(jnp.where(lo128, 0.0, xs_p))
        y_p = _dot(w_heads[0], x_lo) + _dot(w_heads[1], x_hi)

        elc_p = jnp.where(lo128, col(elc, 2 * j, 128), col(elc, 2 * j + 1, 128))
        w_p = jnp.where(lo128, col(w, 2 * j, 128), col(w, 2 * j + 1, 128))
        xw = _bf(xs_p * w_p)
        bm_g = bmb[:, g * SSD_STATE:(g + 1) * SSD_STATE]
        cm_g = cmb[:, g * SSD_STATE:(g + 1) * SSD_STATE]
        y_inter = None
        for s in range(n_seq):
            st = get_state(s, j)
            ci = _dot_nt(cm_g, _bf(st))
            if n_seq == 1:
                y_inter = ci
                xw_s = xw
            else:
                msk = _row_in_seq(rows, seq_len, s, 128)
                ci = jnp.where(msk, ci, 0.0)
                y_inter = ci if y_inter is None else y_inter + ci
                xw_s = jnp.where(msk, xw, jnp.zeros_like(xw))
            upd = _dot_tn(xw_s, bm_g)
            r0 = s * seq_len
            e0 = jnp.broadcast_to(ell[r0:r0 + 1, DT_LANE + 2 * j:DT_LANE + 2 * j + 1], (128, 128))
            e1 = jnp.broadcast_to(ell[r0:r0 + 1, DT_LANE + 2 * j + 1:DT_LANE + 2 * j + 2], (128, 128))
            put_state(s, j, jnp.where(sub128, e0, e1) * st + upd)
        ys.append(y_p + y_inter * elc_p + d_row[:, 128 * j:128 * (j + 1)] * xs_p)
    return jnp.concatenate(ys, axis=1)


def _gate_logf(small_blk, gw2_ref, gb_ref):
    logits = _dot(_bf(small_blk), gw2_ref[...]) + gb_ref[...]
    return _log_sigmoid(logits) * (1.0 / GLA_GATE_TAU)


def _gla_finish(o_h, g_h, gnw):
    return _rms(o_h, gnw) * _silu(g_h)


def _ssd_finish(y, z, snw):
    y = y * _silu(z)
    gs = SSD_INNER // SSD_GROUPS
    parts = [_rms(y[:, g * gs:(g + 1) * gs], snw[:, g * gs:(g + 1) * gs]) for g in range(SSD_GROUPS)]
    return jnp.concatenate(parts, axis=1)


def _memkv_kernel(mem_ref, nw_ref, wk_ref, wv_ref, k_ref, v_ref, kb_ref, vb_ref):
    mn = _bf(_rms(mem_ref[0], nw_ref[...]))
    mk = _dot(mn, wk_ref[...])
    mv = _dot(mn, wv_ref[...])
    for h in range(XA_HEADS):
        k_ref[0, 0, :, h, :] = mk[:, h * XA_HEAD_DIM:(h + 1) * XA_HEAD_DIM]
        v_ref[0, 0, :, h, :] = mv[:, h * XA_HEAD_DIM:(h + 1) * XA_HEAD_DIM]
    kb_ref[0] = _bf(mk)
    vb_ref[0] = _bf(mv)


def _const_spec(shape):
    nd = len(shape)
    return pl.BlockSpec(shape, lambda *_: (0,) * nd, pipeline_mode=pl.Buffered(1))


def _memkv(mem, nw, wk, wv):
    b, m, d = mem.shape
    blk = pl.BlockSpec((1, m, d), lambda i: (i, 0, 0))
    cache_blk = pl.BlockSpec((1, 1, m, XA_HEADS, XA_HEAD_DIM), lambda i: (0, i, 0, 0, 0))
    cache_shape = jax.ShapeDtypeStruct((1, b, m, XA_HEADS, XA_HEAD_DIM), F32)
    return pl.pallas_call(
        _memkv_kernel,
        grid=(b,),
        in_specs=[blk, _const_spec((1, d)), _const_spec((d, d)), _const_spec((d, d))],
        out_specs=[cache_blk, cache_blk, blk, blk],
        out_shape=[cache_shape] * 2 + [jax.ShapeDtypeStruct((b, m, d), BF16)] * 2,
        compiler_params=pltpu.CompilerParams(dimension_semantics=("arbitrary",),
                                             vmem_limit_bytes=VMEM_LIMIT),
        name="mem_kv",
    )(mem, nw, wk, wv)


def _mixer_prompt_kernel(x_ref, lnpre_ref, win_ref, gw2_ref, gb_ref, gnw_ref, cw_ref, cb_ref,
                         dtb_ref, a_ref, d_ref, snw_ref, wout_ref, lnpost_ref,
                         h_ref, sg_out, ss_out, conv_out,
                         proj_s, xpad_s, mixed_s, sg_s, ss_s):
    t = pl.program_id(1)
    rows = TILE

    @pl.when(t == 0)
    def _():
        sg_s[...] = jnp.zeros_like(sg_s)
        ss_s[...] = jnp.zeros_like(ss_s)
        xpad_s[0:8, :] = jnp.zeros((8, SSD_CONV_CH), F32)

    x = x_ref[0]
    hn = _bf(_rms(x, lnpre_ref[...]))
    proj_s[...] = _dot(hn, win_ref[...])

    small = proj_s[:, COL_SMALL:COL_SMALL + 128]
    logf = _gate_logf(small, gw2_ref, gb_ref)

    xbc = proj_s[:, COL_XBC:COL_XBC + SSD_CONV_CH]
    xpad_s[8:8 + rows, :] = xbc
    conv = cb_ref[...]
    for j in range(SSD_CONV):
        conv = conv + xpad_s[5 + j:5 + j + rows, :] * cw_ref[j:j + 1, :]
    xpad_s[5:8, :] = xpad_s[5 + rows:8 + rows, :]
    xa = _silu(conv)

    def get_ss(s, j):
        return ss_s[128 * j:128 * (j + 1), :]

    def put_ss(s, j, val):
        ss_s[128 * j:128 * (j + 1), :] = val

    y = _ssd_tile(xa[:, :SSD_INNER], xa[:, SSD_INNER:SSD_INNER + 256], xa[:, SSD_INNER + 256:],
                  small, dtb_ref[...], a_ref[...], d_ref[...], get_ss, put_ss, rows)
    mixed_s[:, GLA_VW:] = _ssd_finish(y, proj_s[:, COL_Z:COL_Z + SSD_INNER], snw_ref[...])

    def get_sg(s):
        return sg_s[...]

    def put_sg(s, val):
        sg_s[...] = val

    for c in range(rows // GLA_CHUNK):
        r0 = c * GLA_CHUNK
        sl = slice(r0, r0 + GLA_CHUNK)
        outs = _gla_chunk(proj_s[sl, COL_Q:COL_Q + GLA_KW], proj_s[sl, COL_K:COL_K + GLA_KW],
                          proj_s[sl, COL_V:COL_V + GLA_VW], logf[sl], get_sg, put_sg, GLA_CHUNK)
        for h in range(GLA_HEADS):
            g_h = proj_s[sl, COL_G + h * GLA_DV:COL_G + (h + 1) * GLA_DV]
            mixed_s[sl, h * GLA_DV:(h + 1) * GLA_DV] = _gla_finish(outs[h], g_h, gnw_ref[...])

    m = _dot(_bf(mixed_s[...]), wout_ref[...])
    h_ref[0] = x + _rms(m, lnpost_ref[...])

    @pl.when(t == pl.num_programs(1) - 1)
    def _():
        sg_out[0] = sg_s[...].reshape(GLA_HEADS, GLA_DK, GLA_DV)
        ss_out[0] = ss_s[...].reshape(SSD_HEADS, SSD_HEAD_DIM, SSD_STATE)
        conv_out[0] = xpad_s[5:8, :]


def _mixer_prompt(x, p):
    b, t, d = x.shape
    nt = t // TILE
    consts = [p["ln_mix_pre"], p["w_in"], p["gw2"], p["gb"], p["gnw"], p["conv_w"], p["conv_b"],
              p["dtb"], p["a_blk"], p["d_row"], p["snw"], p["w_out"], p["ln_mix_post"]]
    tok = pl.BlockSpec((1, TILE, d), lambda i, j: (i, j, 0))
    return pl.pallas_call(
        _mixer_prompt_kernel,
        grid=(b, nt),
        in_specs=[tok] + [_const_spec(c.shape) for c in consts],
        out_specs=[tok,
                   pl.BlockSpec((1, GLA_HEADS, GLA_DK, GLA_DV), lambda i, j: (i, 0, 0, 0)),
                   pl.BlockSpec((1, SSD_HEADS, SSD_HEAD_DIM, SSD_STATE), lambda i, j: (i, 0, 0, 0)),
                   pl.BlockSpec((1, SSD_CONV - 1, SSD_CONV_CH), lambda i, j: (i, 0, 0))],
        out_shape=[jax.ShapeDtypeStruct((b, t, d), F32),
                   jax.ShapeDtypeStruct((b, GLA_HEADS, GLA_DK, GLA_DV), F32),
                   jax.ShapeDtypeStruct((b, SSD_HEADS, SSD_HEAD_DIM, SSD_STATE), F32),
                   jax.ShapeDtypeStruct((b, SSD_CONV - 1, SSD_CONV_CH), F32)],
        scratch_shapes=[pltpu.VMEM((TILE, PROJ_COLS), F32),
                        pltpu.VMEM((TILE + 8, SSD_CONV_CH), F32),
                        pltpu.VMEM((TILE, D_MODEL), F32),
                        pltpu.VMEM((GLA_KW, GLA_DV), F32),
                        pltpu.VMEM((SSD_INNER, SSD_STATE), F32)],
        compiler_params=pltpu.CompilerParams(dimension_semantics=("arbitrary", "arbitrary"),
                                             vmem_limit_bytes=VMEM_LIMIT),
        name="mixer_prompt",
    )(x, *consts)


def _attend(q, get_k, get_v):
    outs = []
    for h in range(XA_HEADS):
        s = _dot_nt(q[:, h * XA_HEAD_DIM:(h + 1) * XA_HEAD_DIM], get_k(h))
        e = jnp.exp(s - jnp.max(s, axis=-1, keepdims=True))
        pr = e / jnp.sum(e, axis=-1, keepdims=True)
        outs.append(_dot(_bf(pr), get_v(h)))
    return jnp.concatenate(outs, axis=1)


def _ffn(h, lnpre, lnpost, wg_ref, wu_ref, wd_ref):
    hf = _bf(_rms(h, lnpre))
    gt = _dot(hf, wg_ref[...])
    up = _dot(hf, wu_ref[...])
    f = _dot(_bf(_silu(gt) * up), wd_ref[...])
    return h + _rms(f, lnpost)


def _attn_ffn_prompt_kernel(h_ref, mk_ref, mv_ref, lnxa_ref, wq_ref, wo_ref, lnxap_ref,
                            lnf_ref, lnfp_ref, wg_ref, wu_ref, wd_ref, y_ref):
    h = h_ref[0]
    q = _bf(_dot(_bf(_rms(h, lnxa_ref[...])), wq_ref[...]))
    a = _attend(q, lambda hd: mk_ref[0, :, hd * XA_HEAD_DIM:(hd + 1) * XA_HEAD_DIM],
                lambda hd: mv_ref[0, :, hd * XA_HEAD_DIM:(hd + 1) * XA_HEAD_DIM])
    a = _dot(_bf(a), wo_ref[...])
    h = h + _rms(a, lnxap_ref[...])
    y_ref[0] = _ffn(h, lnf_ref[...], lnfp_ref[...], wg_ref, wu_ref, wd_ref)


def _attn_ffn_prompt(h, mkb, mvb, p):
    b, t, d = h.shape
    nt = t // TILE
    consts = [p["ln_xa_pre"], p["w_xq"], p["w_xo"], p["ln_xa_post"], p["ln_ffn_pre"], p["ln_ffn_post"],
              p["w_gate"], p["w_up"], p["w_down"]]
    tok = pl.BlockSpec((1, TILE, d), lambda i, j: (i, j, 0))
    mem = pl.BlockSpec((1, mkb.shape[1], d), lambda i, j: (i, 0, 0))
    return pl.pallas_call(
        _attn_ffn_prompt_kernel,
        grid=(b, nt),
        in_specs=[tok, mem, mem] + [_const_spec(c.shape) for c in consts],
        out_specs=tok,
        out_shape=jax.ShapeDtypeStruct((b, t, d), F32),
        compiler_params=pltpu.CompilerParams(dimension_semantics=("arbitrary", "arbitrary"),
                                             vmem_limit_bytes=VMEM_LIMIT),
        name="attn_ffn_prompt",
    )(h, mkb, mvb, *consts)


def _proj_sample_kernel(x_ref, lnpre_ref, win_ref, proj_ref):
    proj_ref[...] = _dot(_bf(_rms(x_ref[...], lnpre_ref[...])), win_ref[...])


def _proj_sample(x, p):
    n, d = x.shape
    return pl.pallas_call(
        _proj_sample_kernel,
        grid=(1,),
        in_specs=[_const_spec((n, d)), _const_spec((1, d)), _const_spec(p["w_in"].shape)],
        out_specs=pl.BlockSpec((n, PROJ_COLS), lambda i: (0, 0)),
        out_shape=jax.ShapeDtypeStruct((n, PROJ_COLS), F32),
        compiler_params=pltpu.CompilerParams(dimension_semantics=("arbitrary",),
                                             vmem_limit_bytes=VMEM_LIMIT),
        name="proj_sample",
    )(x, p["ln_mix_pre"], p["w_in"])


def _mixer_sample_kernel(seq_len, proj_ref, tail_ref, sg_ref, ss_ref, gw2_ref, gb_ref, gnw_ref,
                         cw_ref, cb_ref, dtb_ref, a_ref, d_ref, snw_ref,
                         mixed_ref, sg_out, ss_out, xpad_s, tpad_s):
    rows = proj_ref.shape[0]
    n_seq = rows // seq_len
    small = proj_ref[:, COL_SMALL:COL_SMALL + 128]
    logf = _gate_logf(small, gw2_ref, gb_ref)

    xbc = proj_ref[:, COL_XBC:COL_XBC + SSD_CONV_CH]
    xpad_s[0:8, :] = jnp.zeros((8, SSD_CONV_CH), F32)
    xpad_s[8:8 + rows, :] = xbc
    tpad_s[0:rows, :] = tail_ref[...]
    tpad_s[rows:rows + 8, :] = jnp.zeros((8, SSD_CONV_CH), F32)
    pos = lax.broadcasted_iota(jnp.int32, (rows, SSD_CONV_CH), 0) & (seq_len - 1)
    conv = cb_ref[...]
    for j in range(SSD_CONV - 1):
        back = SSD_CONV - 1 - j
        prev = jnp.where(pos >= back, xpad_s[8 - back:8 - back + rows, :],
                         tpad_s[seq_len - back:seq_len - back + rows, :])
        conv = conv + prev * cw_ref[j:j + 1, :]
    conv = conv + xbc * cw_ref[SSD_CONV - 1:SSD_CONV, :]
    xa = _silu(conv)

    def get_ss(s, j):
        return ss_ref[s, 2 * j:2 * j + 2].reshape(2 * SSD_HEAD_DIM, SSD_STATE)

    def put_ss(s, j, val):
        ss_out[s, 2 * j:2 * j + 2] = val.reshape(2, SSD_HEAD_DIM, SSD_STATE)

    y = _ssd_tile(xa[:, :SSD_INNER], xa[:, SSD_INNER:SSD_INNER + 256], xa[:, SSD_INNER + 256:],
                  small, dtb_ref[...], a_ref[...], d_ref[...], get_ss, put_ss, seq_len)
    mixed_ref[:, GLA_VW:] = _ssd_finish(y, proj_ref[:, COL_Z:COL_Z + SSD_INNER], snw_ref[...])

    def get_sg(s):
        return sg_ref[s].reshape(GLA_KW, GLA_DV)

    def put_sg(s, val):
        sg_out[s] = val.reshape(GLA_HEADS, GLA_DK, GLA_DV)

    outs = _gla_chunk(proj_ref[:, COL_Q:COL_Q + GLA_KW], proj_ref[:, COL_K:COL_K + GLA_KW],
                      proj_ref[:, COL_V:COL_V + GLA_VW], logf, get_sg, put_sg, seq_len)
    for h in range(GLA_HEADS):
        g_h = proj_ref[:, COL_G + h * GLA_DV:COL_G + (h + 1) * GLA_DV]
        mixed_ref[:, h * GLA_DV:(h + 1) * GLA_DV] = _gla_finish(outs[h], g_h, gnw_ref[...])


def _mixer_sample(proj, tail, sg, ss, p, seq_len):
    n = proj.shape[0]
    rows = SAMPLE_SEQS * seq_len
    nb = sg.shape[0]
    consts = [p["gw2"], p["gb"], p["gnw"], p["conv_w"], p["conv_b"], p["dtb"], p["a_blk"], p["d_row"],
              p["snw"]]
    sg_spec = pl.BlockSpec((SAMPLE_SEQS, GLA_HEADS, GLA_DK, GLA_DV), lambda i: (i, 0, 0, 0))
    ss_spec = pl.BlockSpec((SAMPLE_SEQS, SSD_HEADS, SSD_HEAD_DIM, SSD_STATE), lambda i: (i, 0, 0, 0))
    return pl.pallas_call(
        functools.partial(_mixer_sample_kernel, seq_len),
        grid=(nb // SAMPLE_SEQS,),
        in_specs=[pl.BlockSpec((rows, PROJ_COLS), lambda i: (i, 0)),
                  pl.BlockSpec((rows, SSD_CONV_CH), lambda i: (i, 0)),
                  sg_spec, ss_spec] + [_const_spec(c.shape) for c in consts],
        out_specs=[pl.BlockSpec((rows, D_MODEL), lambda i: (i, 0)), sg_spec, ss_spec],
        out_shape=[jax.ShapeDtypeStruct((n, D_MODEL), F32),
                   jax.ShapeDtypeStruct(sg.shape, F32), jax.ShapeDtypeStruct(ss.shape, F32)],
        scratch_shapes=[pltpu.VMEM((rows + 8, SSD_CONV_CH), F32),
                        pltpu.VMEM((rows + 8, SSD_CONV_CH), F32)],
        compiler_params=pltpu.CompilerParams(dimension_semantics=("arbitrary",),
                                             vmem_limit_bytes=VMEM_LIMIT),
        name="mixer_sample",
    )(proj, tail, sg, ss, *consts)


def _post_mix_sample_kernel(x_ref, mixed_ref, wout_ref, lnpost_ref, lnxa_ref, wq_ref, h_ref, q_ref):
    m = _dot(_bf(mixed_ref[...]), wout_ref[...])
    h = x_ref[...] + _rms(m, lnpost_ref[...])
    h_ref[...] = h
    q_ref[...] = _bf(_dot(_bf(_rms(h, lnxa_ref[...])), wq_ref[...]))


def _post_mix_sample(x, mixed, p):
    n, d = x.shape
    full = pl.BlockSpec((n, d), lambda i: (0, 0))
    consts = [p["w_out"], p["ln_mix_post"], p["ln_xa_pre"], p["w_xq"]]
    return pl.pallas_call(
        _post_mix_sample_kernel,
        grid=(1,),
        in_specs=[full, full] + [_const_spec(c.shape) for c in consts],
        out_specs=[full, full],
        out_shape=[jax.ShapeDtypeStruct((n, d), F32), jax.ShapeDtypeStruct((n, d), BF16)],
        compiler_params=pltpu.CompilerParams(dimension_semantics=("arbitrary",),
                                             vmem_limit_bytes=VMEM_LIMIT),
        name="post_mix_sample",
    )(x, mixed, *consts)


def _attn_sample_kernel(seq_len, q_ref, k_hbm, v_hbm, a_ref, kbuf, vbuf, sem):
    step = pl.program_id(0)
    rows = q_ref.shape[0]
    n_seq = rows // seq_len

    def copies(at_step, slot):
        out = []
        for s in range(n_seq):
            for hd in range(XA_HEADS):
                b = at_step * n_seq + s
                out.append(pltpu.make_async_copy(k_hbm.at[0, b, :, hd, :], kbuf.at[slot, s, hd], sem.at[0, slot]))
                out.append(pltpu.make_async_copy(v_hbm.at[0, b, :, hd, :], vbuf.at[slot, s, hd], sem.at[1, slot]))
        return out

    @pl.when(step == 0)
    def _():
        for cp in copies(0, 0):
            cp.start()

    @pl.when(step + 1 < pl.num_programs(0))
    def _():
        for cp in copies(step + 1, (step + 1) % 2):
            cp.start()

    slot = step % 2
    for cp in copies(step, slot):
        cp.wait()

    q = q_ref[...]
    out = jnp.zeros((rows, D_MODEL), F32)
    for s in range(n_seq):
        o = _attend(q, lambda hd: _bf(kbuf[slot, s, hd]), lambda hd: _bf(vbuf[slot, s, hd]))
        out = jnp.where(_row_in_seq(rows, seq_len, s, D_MODEL), o, out)
    a_ref[...] = out


def _attn_sample(q, ck, cv, seq_len):
    n, d = q.shape
    _, nb, m, nh, hd = ck.shape
    rows = ATTN_SEQS * seq_len
    tok = pl.BlockSpec((rows, d), lambda i: (i, 0))
    hbm = pl.BlockSpec(memory_space=pl.ANY)
    return pl.pallas_call(
        functools.partial(_attn_sample_kernel, seq_len),
        grid=(nb // ATTN_SEQS,),
        in_specs=[tok, hbm, hbm],
        out_specs=tok,
        out_shape=jax.ShapeDtypeStruct((n, d), F32),
        scratch_shapes=[pltpu.VMEM((2, ATTN_SEQS, nh, m, hd), F32),
                        pltpu.VMEM((2, ATTN_SEQS, nh, m, hd), F32),
                        pltpu.SemaphoreType.DMA((2, 2))],
        compiler_params=pltpu.CompilerParams(dimension_semantics=("arbitrary",),
                                             vmem_limit_bytes=VMEM_LIMIT),
        name="attn_sample",
    )(q, ck, cv)


def _post_attn_sample_kernel(h_ref, a_ref, wo_ref, lnxap_ref, lnf_ref, lnfp_ref, wg_ref, wu_ref, wd_ref,
                             y_ref):
    a = _dot(_bf(a_ref[...]), wo_ref[...])
    h = h_ref[...] + _rms(a, lnxap_ref[...])
    y_ref[...] = _ffn(h, lnf_ref[...], lnfp_ref[...], wg_ref, wu_ref, wd_ref)


def _post_attn_sample(h, a, p):
    n, d = h.shape
    full = pl.BlockSpec((n, d), lambda i: (0, 0))
    consts = [p["w_xo"], p["ln_xa_post"], p["ln_ffn_pre"], p["ln_ffn_post"], p["w_gate"], p["w_up"],
              p["w_down"]]
    return pl.pallas_call(
        _post_attn_sample_kernel,
        grid=(1,),
        in_specs=[full, full] + [_const_spec(c.shape) for c in consts],
        out_specs=full,
        out_shape=jax.ShapeDtypeStruct((n, d), F32),
        compiler_params=pltpu.CompilerParams(dimension_semantics=("arbitrary",),
                                             vmem_limit_bytes=VMEM_LIMIT),
        name="post_attn_sample",
    )(h, a, *consts)


def _pack_params(ln_mix_pre, ln_mix_post, w_in, gla_gate_w2, gla_gate_b, gla_norm_w, ssd_conv_w,
                 ssd_conv_b, ssd_dt_bias, ssd_A_log, ssd_D, ssd_norm_w, w_out, ln_xa_pre, ln_xa_post,
                 mem_norm_w, w_xq, w_xk, w_xv, w_xo, ln_ffn_pre, ln_ffn_post, w_gate, w_up, w_down):
    sizes = (GLA_KW, GLA_KW, GLA_VW, GLA_VW, GLA_GATE_RANK, SSD_INNER, SSD_CONV_CH, SSD_HEADS)
    offs = [0]
    for sz in sizes:
        offs.append(offs[-1] + sz)
    wq, wk, wv, wg, wglr, wz, wxbc, wdt = [w_in[:, offs[i]:offs[i + 1]] for i in range(len(sizes))]
    pad = jnp.zeros((D_MODEL, PROJ_COLS - COL_SMALL - GLA_GATE_RANK - SSD_HEADS), F32)
    w_in_p = jnp.concatenate([wq * (GLA_DK ** -0.5), wk, wv, wg, wz, wxbc, wglr, wdt, pad], axis=1)

    def small_row(v):
        return jnp.zeros((1, 128), F32).at[0, DT_LANE:DT_LANE + SSD_HEADS].set(v)

    row = lambda v: v.reshape(1, -1)
    return {
        "ln_mix_pre": row(ln_mix_pre), "ln_mix_post": row(ln_mix_post),
        "w_in": _bf(w_in_p),
        "gw2": _bf(jnp.zeros((128, GLA_KW), F32).at[:GLA_GATE_RANK].set(gla_gate_w2)),
        "gb": row(gla_gate_b), "gnw": row(gla_norm_w),
        "conv_w": ssd_conv_w, "conv_b": row(ssd_conv_b),
        "dtb": small_row(ssd_dt_bias),
        "a_blk": small_row(-jnp.exp(ssd_A_log)),
        "d_row": jnp.repeat(ssd_D, SSD_HEAD_DIM).reshape(1, SSD_INNER),
        "snw": row(ssd_norm_w),
        "w_out": _bf(w_out),
        "ln_xa_pre": row(ln_xa_pre), "ln_xa_post": row(ln_xa_post), "mem_norm_w": row(mem_norm_w),
        "w_xq": _bf(w_xq * (XA_HEAD_DIM ** -0.5)), "w_xk": _bf(w_xk), "w_xv": _bf(w_xv), "w_xo": _bf(w_xo),
        "ln_ffn_pre": row(ln_ffn_pre), "ln_ffn_post": row(ln_ffn_post),
        "w_gate": _bf(w_gate), "w_up": _bf(w_up), "w_down": _bf(w_down),
    }


def kernel(x_prompt, x_sample, mem_prompt, state_gla, state_ssm, state_conv, cache_mem_k, cache_mem_v,
           ln_mix_pre, ln_mix_post, w_in, gla_gate_w2, gla_gate_b, gla_norm_w, ssd_conv_w, ssd_conv_b,
           ssd_dt_bias, ssd_A_log, ssd_D, ssd_norm_w, w_out, ln_xa_pre, ln_xa_post, mem_norm_w,
           w_xq, w_xk, w_xv, w_xo, ln_ffn_pre, ln_ffn_post, w_gate, w_up, w_down):
    assert w_in.shape[0] == 1, "single-layer kernel"
    layer = [a[0] for a in (ln_mix_pre, ln_mix_post, w_in, gla_gate_w2, gla_gate_b, gla_norm_w, ssd_conv_w,
                            ssd_conv_b, ssd_dt_bias, ssd_A_log, ssd_D, ssd_norm_w, w_out, ln_xa_pre,
                            ln_xa_post, mem_norm_w, w_xq, w_xk, w_xv, w_xo, ln_ffn_pre, ln_ffn_post,
                            w_gate, w_up, w_down)]
    p = _pack_params(*layer)
    bp, mem_len, _ = mem_prompt.shape
    bs, ts, _ = x_sample.shape
    assert ts == SSD_CONV, "the padded conv-tail layout assumes one new row per conv tap"

    mk, mv, mkb, mvb = _memkv(mem_prompt, p["mem_norm_w"], p["w_xk"], p["w_xv"])
    h_p, gla_p, ssm_p, conv_p = _mixer_prompt(x_prompt, p)
    y_p = _attn_ffn_prompt(h_p, mkb, mvb, p)

    xs = x_sample.reshape(bs * ts, D_MODEL)
    proj = _proj_sample(xs, p)
    tail = jnp.pad(state_conv[0], ((0, 0), (1, 0), (0, 0))).reshape(bs * ts, SSD_CONV_CH)
    mixed, gla_s, ssm_s = _mixer_sample(proj, tail, state_gla[0], state_ssm[0], p, ts)
    h_s, q_s = _post_mix_sample(xs, mixed, p)
    a_s = _attn_sample(q_s, cache_mem_k, cache_mem_v, ts)
    y_s = _post_attn_sample(h_s, a_s, p)
    conv_s = proj[:, COL_XBC:COL_XBC + SSD_CONV_CH].reshape(bs, ts, SSD_CONV_CH)[:, ts - (SSD_CONV - 1):]

    return (y_p, y_s.reshape(bs, ts, D_MODEL), gla_p[None], ssm_p[None], conv_p[None],
            mk, mv, gla_s[None], ssm_s[None], conv_s[None])
```

```python
import functools

import jax
import jax.numpy as jnp
from jax import lax
from jax.experimental import pallas as pl
from jax.experimental.pallas import tpu as pltpu

F32 = jnp.float32
BF16 = jnp.bfloat16

D_MODEL = 1024
GLA_HEADS = 4
GLA_DK = 64
GLA_DV = 128
GLA_KW = GLA_HEADS * GLA_DK
GLA_VW = GLA_HEADS * GLA_DV
GLA_GATE_RANK = 16
GLA_GATE_TAU = 16.0
SSD_INNER = 512
SSD_HEAD_DIM = 64
SSD_HEADS = 8
SSD_GROUPS = 2
SSD_STATE = 128
SSD_CONV = 4
SSD_CONV_CH = 1024
XA_HEADS = 4
XA_HEAD_DIM = 256
D_FF = 2816
EPS = 1e-6

COL_Q = 0
COL_K = 256
COL_V = 512
COL_G = 1024
COL_Z = 1536
COL_XBC = 2048
COL_SMALL = 3072
DT_LANE = GLA_GATE_RANK
PROJ_COLS = 3200

GLA_CHUNK = 64
TILE = 256
FFN_SUBTILES = 2
MIX_SEQS = 2
SAMPLE_SEQS = 16
ATTN_SEQS = 4
VMEM_LIMIT = 56 * 1024 * 1024


def _bf(x):
    return x.astype(BF16)


def _dot(a, b):
    return jnp.dot(a, b, preferred_element_type=F32)


def _dot_nt(a, b):
    return lax.dot_general(a, b, (((1,), (1,)), ((), ())), preferred_element_type=F32)


def _dot_tn(a, b):
    return lax.dot_general(a, b, (((0,), (0,)), ((), ())), preferred_element_type=F32)


def _dot3(m, a):
    hi = _bf(a)
    r = a - hi.astype(F32)
    mid = _bf(r)
    lo = _bf(r - mid.astype(F32))
    return _dot(m, hi) + _dot(m, mid) + _dot(m, lo)


def _rms(x, w):
    ms = jnp.mean(x * x, axis=-1, keepdims=True)
    return x * lax.rsqrt(ms + EPS) * w


def _silu(x):
    return x / (1.0 + jnp.exp(-x))


def _softplus(x):
    return jnp.maximum(x, 0.0) + jnp.log1p(jnp.exp(-jnp.abs(x)))


def _log_sigmoid(x):
    return jnp.minimum(x, 0.0) - jnp.log1p(jnp.exp(-jnp.abs(x)))


def _seg_masks(rows, seq_len):
    t = lax.broadcasted_iota(jnp.int32, (rows, rows), 0)
    u = lax.broadcasted_iota(jnp.int32, (rows, rows), 1)
    if seq_len == rows:
        same = None
        tril = u <= t
    else:
        shift = seq_len.bit_length() - 1
        assert 1 << shift == seq_len
        same = (t >> shift) == (u >> shift)
        tril = jnp.logical_and(same, u <= t)
    return same, tril


def _mask_to_bf16(mask, rows):
    if mask is None:
        return jnp.ones((rows, rows), BF16)
    return jnp.where(mask, 1.0, 0.0).astype(BF16)


def _row_in_seq(rows, seq_len, s, width):
    r = lax.broadcasted_iota(jnp.int32, (rows, width), 0)
    return jnp.logical_and(r >= s * seq_len, r < (s + 1) * seq_len)


def _interleave(chains):
    chains = list(chains)
    while chains:
        alive = []
        for c in chains:
            try:
                next(c)
                alive.append(c)
            except StopIteration:
                pass
        chains = alive


def _pad_rows(x, rows):
    if x.shape[0] >= rows:
        return x
    return jnp.concatenate([x, jnp.zeros((rows - x.shape[0], x.shape[1]), x.dtype)], axis=0)


def _gla_chunk(q, k, v, lf, get_state, put_state, seq_len):
    rows = q.shape[0]
    n_seq = rows // seq_len
    same, tril = _seg_masks(rows, seq_len)
    tril_b = _mask_to_bf16(tril, rows)
    same_b = _mask_to_bf16(same, rows)

    b = _dot3(tril_b, lf)
    bl = _dot3(same_b, lf)
    qt = q * jnp.exp(b)
    kt = k * jnp.exp(-b)
    kd = k * jnp.exp(bl - b)
    bl_t = _pad_rows(bl, 128).T

    lane = lax.broadcasted_iota(jnp.int32, (rows, GLA_KW), 1)
    lhs = _bf(jnp.concatenate(
        [jnp.where((lane >= h * GLA_DK) & (lane < (h + 1) * GLA_DK), qt, 0.0)
         for h in range(GLA_HEADS)], axis=0))
    att = _dot_nt(lhs, _bf(kt))
    t4 = lax.broadcasted_iota(jnp.int32, (GLA_HEADS * rows, rows), 0) & (rows - 1)
    u4 = lax.broadcasted_iota(jnp.int32, (GLA_HEADS * rows, rows), 1)
    causal4 = u4 <= t4
    if n_seq > 1:
        shift = seq_len.bit_length() - 1
        causal4 = jnp.logical_and(causal4, (t4 >> shift) == (u4 >> shift))
    att = _bf(jnp.where(causal4, att, 0.0))
    vb = _bf(v)
    kdb = _bf(kd)

    o_inter = None
    r4 = lax.broadcasted_iota(jnp.int32, (GLA_HEADS * rows, GLA_DV), 0) & (rows - 1)
    for s in range(n_seq):
        st = get_state(s)
        oi = _dot(lhs, _bf(st))
        if n_seq == 1:
            o_inter = oi
            kd_s = kdb
        else:
            m4 = jnp.logical_and(r4 >= s * seq_len, r4 < (s + 1) * seq_len)
            oi = jnp.where(m4, oi, 0.0)
            o_inter = oi if o_inter is None else o_inter + oi
            kd_s = jnp.where(_row_in_seq(rows, seq_len, s, GLA_KW), kdb, jnp.zeros_like(kdb))
        upd = _dot_tn(kd_s, vb)
        upd_d = jnp.concatenate(
            [upd[h * GLA_DK:(h + 1) * GLA_DK, h * GLA_DV:(h + 1) * GLA_DV]
             for h in range(GLA_HEADS)], axis=0)
        c0 = s * seq_len
        decay = jnp.exp(jnp.broadcast_to(bl_t[:, c0:c0 + 1], (GLA_KW, GLA_DV)))
        put_state(s, decay * st + upd_d)

    outs = []
    for h in range(GLA_HEADS):
        o_h = _dot(att[h * rows:(h + 1) * rows], vb[:, h * GLA_DV:(h + 1) * GLA_DV])
        outs.append(o_h + o_inter[h * rows:(h + 1) * rows])
    return outs


def _ssd_tile(xs, bm, cm, dt_raw_blk, dtb_blk, a_blk, d_row, get_state, put_state, seq_len):
    rows = xs.shape[0]
    n_seq = rows // seq_len
    same, tril = _seg_masks(rows, seq_len)
    tril_b = _mask_to_bf16(tril, rows)
    same_b = _mask_to_bf16(same, rows)

    dt = _softplus(dt_raw_blk + dtb_blk)
    dta = dt * a_blk
    lc = _dot3(tril_b, dta)
    ll = _dot3(same_b, dta)
    elc = jnp.exp(lc)
    w = jnp.exp(ll - lc) * dt
    ell = jnp.exp(ll)
    lc_t = _pad_rows(lc, 128).T[:, :rows]
    dt_t = _pad_rows(dt, 128).T[:, :rows]

    bmb = _bf(bm)
    cmb = _bf(cm)
    cb = [_dot_nt(cmb[:, g * SSD_STATE:(g + 1) * SSD_STATE],
                  bmb[:, g * SSD_STATE:(g + 1) * SSD_STATE]) for g in range(SSD_GROUPS)]

    lane128 = lax.broadcasted_iota(jnp.int32, (rows, 128), 1)
    lo128 = lane128 < SSD_HEAD_DIM
    sub128 = lax.broadcasted_iota(jnp.int32, (128, 128), 0) < SSD_HEAD_DIM

    def col(x, h, width):
        return jnp.broadcast_to(x[:, DT_LANE + h:DT_LANE + h + 1], (x.shape[0], width))

    heads_per_group = SSD_HEADS // SSD_GROUPS
    ys = []
    for j in range(SSD_HEADS // 2):
        g = (2 * j) // heads_per_group
        xs_p = xs[:, 128 * j:128 * (j + 1)]
        w_heads = []
        for h in (2 * j, 2 * j + 1):
            seg = col(lc, h, rows) - jnp.broadcast_to(lc_t[DT_LANE + h:DT_LANE + h + 1, :], (rows, rows))
            dtr = jnp.broadcast_to(dt_t[DT_LANE + h:DT_LANE + h + 1, :], (rows, rows))
            w_heads.append(_bf(jnp.where(tril, cb[g] * jnp.exp(seg) * dtr, 0.0)))
        x_lo = _bf(jnp.where(lo128, xs_p, 0.0))
        x_hi = _bf(jnp.where(lo128, 0.0, xs_p))
        y_p = _dot(w_heads[0], x_lo) + _dot(w_heads[1], x_hi)

        elc_p = jnp.where(lo128, col(elc, 2 * j, 128), col(elc, 2 * j + 1, 128))
        w_p = jnp.where(lo128, col(w, 2 * j, 128), col(w, 2 * j + 1, 128))
        xw = _bf(xs_p * w_p)
        bm_g = bmb[:, g * SSD_STATE:(g + 1) * SSD_STATE]
        cm_g = cmb[:, g * SSD_STATE:(g + 1) * SSD_STATE]
        y_inter = None
        for s in range(n_seq):
            st = get_state(s, j)
            ci = _dot_nt(cm_g, _bf(st))
            if n_seq == 1:
                y_inter = ci
                xw_s = xw
            else:
                msk = _row_in_seq(rows, seq_len, s, 128)
                ci = jnp.where(msk, ci, 0.0)
                y_inter = ci if y_inter is None else y_inter + ci
                xw_s = jnp.where(msk, xw, jnp.zeros_like(xw))
            upd = _dot_tn(xw_s, bm_g)
            r0 = s * seq_len
            e0 = jnp.broadcast_to(ell[r0:r0 + 1, DT_LANE + 2 * j:DT_LANE + 2 * j + 1], (128, 128))
            e1 = jnp.broadcast_to(ell[r0:r0 + 1, DT_LANE + 2 * j + 1:DT_LANE + 2 * j + 2], (128, 128))
            put_state(s, j, jnp.where(sub128, e0, e1) * st + upd)
        ys.append(y_p + y_inter * elc_p + d_row[:, 128 * j:128 * (j + 1)] * xs_p)
    return jnp.concatenate(ys, axis=1)


def _gate_logf(small_blk, gw2_ref, gb_ref):
    logits = _dot(_bf(small_blk), gw2_ref[...]) + gb_ref[...]
    return _log_sigmoid(logits) * (1.0 / GLA_GATE_TAU)


def _gla_finish(o_h, g_h, gnw):
    return _rms(o_h, gnw) * _silu(g_h)


def _ssd_finish(y, z, snw):
    y = y * _silu(z)
    gs = SSD_INNER // SSD_GROUPS
    parts = [_rms(y[:, g * gs:(g + 1) * gs], snw[:, g * gs:(g + 1) * gs]) for g in range(SSD_GROUPS)]
    return jnp.concatenate(parts, axis=1)


def _memkv_kernel(mem_ref, nw_ref, wk_ref, wv_ref, k_ref, v_ref, kb_ref, vb_ref):
    mn = _bf(_rms(mem_ref[0], nw_ref[...]))
    mk = _dot(mn, wk_ref[...])
    mv = _dot(mn, wv_ref[...])
    for h in range(XA_HEADS):
        k_ref[0, 0, :, h, :] = mk[:, h * XA_HEAD_DIM:(h + 1) * XA_HEAD_DIM]
        v_ref[0, 0, :, h, :] = mv[:, h * XA_HEAD_DIM:(h + 1) * XA_HEAD_DIM]
    kb_ref[0] = _bf(mk)
    vb_ref[0] = _bf(mv)


def _const_spec(shape):
    nd = len(shape)
    return pl.BlockSpec(shape, lambda *_: (0,) * nd, pipeline_mode=pl.Buffered(1))


def _memkv(mem, nw, wk, wv):
    b, m, d = mem.shape
    blk = pl.BlockSpec((1, m, d), lambda i: (i, 0, 0))
    cache_blk = pl.BlockSpec((1, 1, m, XA_HEADS, XA_HEAD_DIM), lambda i: (0, i, 0, 0, 0))
    cache_shape = jax.ShapeDtypeStruct((1, b, m, XA_HEADS, XA_HEAD_DIM), F32)
    return pl.pallas_call(
        _memkv_kernel,
        grid=(b,),
        in_specs=[blk, _const_spec((1, d)), _const_spec((d, d)), _const_spec((d, d))],
        out_specs=[cache_blk, cache_blk, blk, blk],
        out_shape=[cache_shape] * 2 + [jax.ShapeDtypeStruct((b, m, d), BF16)] * 2,
        compiler_params=pltpu.CompilerParams(dimension_semantics=("arbitrary",),
                                             vmem_limit_bytes=VMEM_LIMIT),
        name="mem_kv",
    )(mem, nw, wk, wv)


def _mixer_prompt_one(x_ref, h_ref, proj_s, xpad_s, mixed_s, sg_s, ss_s,
                      lnpre_ref, win_ref, gw2_ref, gb_ref, gnw_ref, cw_ref, cb_ref,
                      dtb_ref, a_ref, d_ref, snw_ref, wout_ref, lnpost_ref):
    rows = TILE
    x = x_ref[...]
    hn = _bf(_rms(x, lnpre_ref[...]))
    yield
    proj_s[...] = _dot(hn, win_ref[...])
    yield

    small = proj_s[:, COL_SMALL:COL_SMALL + 128]
    logf = _gate_logf(small, gw2_ref, gb_ref)

    xbc = proj_s[:, COL_XBC:COL_XBC + SSD_CONV_CH]
    xpad_s[8:8 + rows, :] = xbc
    conv = cb_ref[...]
    for j in range(SSD_CONV):
        conv = conv + xpad_s[5 + j:5 + j + rows, :] * cw_ref[j:j + 1, :]
    xpad_s[5:8, :] = xpad_s[5 + rows:8 + rows, :]
    xa = _silu(conv)
    yield

    def get_ss(s, j):
        return ss_s[128 * j:128 * (j + 1), :]

    def put_ss(s, j, val):
        ss_s[128 * j:128 * (j + 1), :] = val

    y = _ssd_tile(xa[:, :SSD_INNER], xa[:, SSD_INNER:SSD_INNER + 256], xa[:, SSD_INNER + 256:],
                  small, dtb_ref[...], a_ref[...], d_ref[...], get_ss, put_ss, rows)
    mixed_s[:, GLA_VW:] = _ssd_finish(y, proj_s[:, COL_Z:COL_Z + SSD_INNER], snw_ref[...])
    yield

    def get_sg(s):
        return sg_s[...]

    def put_sg(s, val):
        sg_s[...] = val

    for c in range(rows // GLA_CHUNK):
        r0 = c * GLA_CHUNK
        sl = slice(r0, r0 + GLA_CHUNK)
        outs = _gla_chunk(proj_s[sl, COL_Q:COL_Q + GLA_KW], proj_s[sl, COL_K:COL_K + GLA_KW],
                          proj_s[sl, COL_V:COL_V + GLA_VW], logf[sl], get_sg, put_sg, GLA_CHUNK)
        for h in range(GLA_HEADS):
            g_h = proj_s[sl, COL_G + h * GLA_DV:COL_G + (h + 1) * GLA_DV]
            mixed_s[sl, h * GLA_DV:(h + 1) * GLA_DV] = _gla_finish(outs[h], g_h, gnw_ref[...])
        yield

    m = _dot(_bf(mixed_s[...]), wout_ref[...])
    yield
    h_ref[...] = x + _rms(m, lnpost_ref[...])


N_MIXER_WEIGHTS = 13


def _mixer_prompt_kernel(x_ref, *rest):
    consts = rest[:N_MIXER_WEIGHTS]
    h_ref, sg_out, ss_out, conv_out = rest[N_MIXER_WEIGHTS:N_MIXER_WEIGHTS + 4]
    proj_s, xpad_s, mixed_s, sg_s, ss_s = rest[N_MIXER_WEIGHTS + 4:]
    t = pl.program_id(1)

    @pl.when(t == 0)
    def _():
        sg_s[...] = jnp.zeros_like(sg_s)
        ss_s[...] = jnp.zeros_like(ss_s)
        xpad_s[:, 0:8, :] = jnp.zeros((MIX_SEQS, 8, SSD_CONV_CH), F32)

    _interleave([_mixer_prompt_one(x_ref.at[i], h_ref.at[i], proj_s.at[i], xpad_s.at[i], mixed_s.at[i],
                                   sg_s.at[i], ss_s.at[i], *consts) for i in range(MIX_SEQS)])

    @pl.when(t == pl.num_programs(1) - 1)
    def _():
        sg_out[...] = sg_s[...].reshape(MIX_SEQS, GLA_HEADS, GLA_DK, GLA_DV)
        ss_out[...] = ss_s[...].reshape(MIX_SEQS, SSD_HEADS, SSD_HEAD_DIM, SSD_STATE)
        conv_out[...] = xpad_s[:, 5:8, :]


def _mixer_prompt(x, p):
    b, t, d = x.shape
    nt = t // TILE
    consts = [p["ln_mix_pre"], p["w_in"], p["gw2"], p["gb"], p["gnw"], p["conv_w"], p["conv_b"],
              p["dtb"], p["a_blk"], p["d_row"], p["snw"], p["w_out"], p["ln_mix_post"]]
    assert len(consts) == N_MIXER_WEIGHTS
    tok = pl.BlockSpec((MIX_SEQS, TILE, d), lambda i, j: (i, j, 0))
    return pl.pallas_call(
        _mixer_prompt_kernel,
        grid=(b // MIX_SEQS, nt),
        in_specs=[tok] + [_const_spec(c.shape) for c in consts],
        out_specs=[tok,
                   pl.BlockSpec((MIX_SEQS, GLA_HEADS, GLA_DK, GLA_DV), lambda i, j: (i, 0, 0, 0)),
                   pl.BlockSpec((MIX_SEQS, SSD_HEADS, SSD_HEAD_DIM, SSD_STATE), lambda i, j: (i, 0, 0, 0)),
                   pl.BlockSpec((MIX_SEQS, SSD_CONV - 1, SSD_CONV_CH), lambda i, j: (i, 0, 0))],
        out_shape=[jax.ShapeDtypeStruct((b, t, d), F32),
                   jax.ShapeDtypeStruct((b, GLA_HEADS, GLA_DK, GLA_DV), F32),
                   jax.ShapeDtypeStruct((b, SSD_HEADS, SSD_HEAD_DIM, SSD_STATE), F32),
                   jax.ShapeDtypeStruct((b, SSD_CONV - 1, SSD_CONV_CH), F32)],
        scratch_shapes=[pltpu.VMEM((MIX_SEQS, TILE, PROJ_COLS), F32),
                        pltpu.VMEM((MIX_SEQS, TILE + 8, SSD_CONV_CH), F32),
                        pltpu.VMEM((MIX_SEQS, TILE, D_MODEL), F32),
                        pltpu.VMEM((MIX_SEQS, GLA_KW, GLA_DV), F32),
                        pltpu.VMEM((MIX_SEQS, SSD_INNER, SSD_STATE), F32)],
        compiler_params=pltpu.CompilerParams(dimension_semantics=("arbitrary", "arbitrary"),
                                             vmem_limit_bytes=VMEM_LIMIT),
        name="mixer_prompt",
    )(x, *consts)


def _attend(q, get_k, get_v):
    outs = []
    for h in range(XA_HEADS):
        s = _dot_nt(q[:, h * XA_HEAD_DIM:(h + 1) * XA_HEAD_DIM], get_k(h))
        e = jnp.exp(s - jnp.max(s, axis=-1, keepdims=True))
        pr = e / jnp.sum(e, axis=-1, keepdims=True)
        outs.append(_dot(_bf(pr), get_v(h)))
    return jnp.concatenate(outs, axis=1)


def _ffn(h, lnpre, lnpost, wg_ref, wu_ref, wd_ref):
    hf = _bf(_rms(h, lnpre))
    gt = _dot(hf, wg_ref[...])
    up = _dot(hf, wu_ref[...])
    f = _dot(_bf(_silu(gt) * up), wd_ref[...])
    return h + _rms(f, lnpost)


def _attn_ffn_prompt_kernel(h_ref, mk_ref, mv_ref, lnxa_ref, wq_ref, wo_ref, lnxap_ref,
                            lnf_ref, lnfp_ref, wg_ref, wu_ref, wd_ref, y_ref):
    def sub_tile(sub):
        rs = slice(sub * TILE, (sub + 1) * TILE)
        h = h_ref[0, rs, :]
        hn = _bf(_rms(h, lnxa_ref[...]))
        yield
        q = _bf(_dot(hn, wq_ref[...]))
        yield
        a = _bf(_attend(q, lambda hd: mk_ref[0, :, hd * XA_HEAD_DIM:(hd + 1) * XA_HEAD_DIM],
                        lambda hd: mv_ref[0, :, hd * XA_HEAD_DIM:(hd + 1) * XA_HEAD_DIM]))
        yield
        a = _dot(a, wo_ref[...])
        yield
        h = h + _rms(a, lnxap_ref[...])
        hf = _bf(_rms(h, lnf_ref[...]))
        yield
        gt = _dot(hf, wg_ref[...])
        yield
        up = _dot(hf, wu_ref[...])
        yield
        act = _bf(_silu(gt) * up)
        yield
        f = _dot(act, wd_ref[...])
        yield
        y_ref[0, rs, :] = h + _rms(f, lnfp_ref[...])

    _interleave([sub_tile(sub) for sub in range(FFN_SUBTILES)])


def _attn_ffn_prompt(h, mkb, mvb, p):
    b, t, d = h.shape
    rows = FFN_SUBTILES * TILE
    consts = [p["ln_xa_pre"], p["w_xq"], p["w_xo"], p["ln_xa_post"], p["ln_ffn_pre"], p["ln_ffn_post"],
              p["w_gate"], p["w_up"], p["w_down"]]
    tok = pl.BlockSpec((1, rows, d), lambda i, j: (i, j, 0))
    mem = pl.BlockSpec((1, mkb.shape[1], d), lambda i, j: (i, 0, 0))
    return pl.pallas_call(
        _attn_ffn_prompt_kernel,
        grid=(b, t // rows),
        in_specs=[tok, mem, mem] + [_const_spec(c.shape) for c in consts],
        out_specs=tok,
        out_shape=jax.ShapeDtypeStruct((b, t, d), F32),
        compiler_params=pltpu.CompilerParams(dimension_semantics=("arbitrary", "arbitrary"),
                                             vmem_limit_bytes=VMEM_LIMIT),
        name="attn_ffn_prompt",
    )(h, mkb, mvb, *consts)


def _proj_sample_kernel(x_ref, lnpre_ref, win_ref, proj_ref):
    proj_ref[...] = _dot(_bf(_rms(x_ref[...], lnpre_ref[...])), win_ref[...])


def _proj_sample(x, p):
    n, d = x.shape
    return pl.pallas_call(
        _proj_sample_kernel,
        grid=(1,),
        in_specs=[_const_spec((n, d)), _const_spec((1, d)), _const_spec(p["w_in"].shape)],
        out_specs=pl.BlockSpec((n, PROJ_COLS), lambda i: (0, 0)),
        out_shape=jax.ShapeDtypeStruct((n, PROJ_COLS), F32),
        compiler_params=pltpu.CompilerParams(dimension_semantics=("arbitrary",),
                                             vmem_limit_bytes=VMEM_LIMIT),
        name="proj_sample",
    )(x, p["ln_mix_pre"], p["w_in"])


def _mixer_sample_kernel(seq_len, proj_ref, tail_ref, sg_ref, ss_ref, gw2_ref, gb_ref, gnw_ref,
                         cw_ref, cb_ref, dtb_ref, a_ref, d_ref, snw_ref,
                         mixed_ref, sg_out, ss_out, xpad_s, tpad_s):
    rows = proj_ref.shape[0]
    n_seq = rows // seq_len
    small = proj_ref[:, COL_SMALL:COL_SMALL + 128]
    logf = _gate_logf(small, gw2_ref, gb_ref)

    xbc = proj_ref[:, COL_XBC:COL_XBC + SSD_CONV_CH]
    xpad_s[0:8, :] = jnp.zeros((8, SSD_CONV_CH), F32)
    xpad_s[8:8 + rows, :] = xbc
    tpad_s[0:rows, :] = tail_ref[...]
    tpad_s[rows:rows + 8, :] = jnp.zeros((8, SSD_CONV_CH), F32)
    pos = lax.broadcasted_iota(jnp.int32, (rows, SSD_CONV_CH), 0) & (seq_len - 1)
    conv = cb_ref[...]
    for j in range(SSD_CONV - 1):
        back = SSD_CONV - 1 - j
        prev = jnp.where(pos >= back, xpad_s[8 - back:8 - back + rows, :],
                         tpad_s[seq_len - back:seq_len - back + rows, :])
        conv = conv + prev * cw_ref[j:j + 1, :]
    conv = conv + xbc * cw_ref[SSD_CONV - 1:SSD_CONV, :]
    xa = _silu(conv)

    def get_ss(s, j):
        return ss_ref[s, 2 * j:2 * j + 2].reshape(2 * SSD_HEAD_DIM, SSD_STATE)

    def put_ss(s, j, val):
        ss_out[s, 2 * j:2 * j + 2] = val.reshape(2, SSD_HEAD_DIM, SSD_STATE)

    y = _ssd_tile(xa[:, :SSD_INNER], xa[:, SSD_INNER:SSD_INNER + 256], xa[:, SSD_INNER + 256:],
                  small, dtb_ref[...], a_ref[...], d_ref[...], get_ss, put_ss, seq_len)
    mixed_ref[:, GLA_VW:] = _ssd_finish(y, proj_ref[:, COL_Z:COL_Z + SSD_INNER], snw_ref[...])

    def get_sg(s):
        return sg_ref[s].reshape(GLA_KW, GLA_DV)

    def put_sg(s, val):
        sg_out[s] = val.reshape(GLA_HEADS, GLA_DK, GLA_DV)

    outs = _gla_chunk(proj_ref[:, COL_Q:COL_Q + GLA_KW], proj_ref[:, COL_K:COL_K + GLA_KW],
                      proj_ref[:, COL_V:COL_V + GLA_VW], logf, get_sg, put_sg, seq_len)
    for h in range(GLA_HEADS):
        g_h = proj_ref[:, COL_G + h * GLA_DV:COL_G + (h + 1) * GLA_DV]
        mixed_ref[:, h * GLA_DV:(h + 1) * GLA_DV] = _gla_finish(outs[h], g_h, gnw_ref[...])


def _mixer_sample(proj, tail, sg, ss, p, seq_len):
    n = proj.shape[0]
    rows = SAMPLE_SEQS * seq_len
    nb = sg.shape[0]
    consts = [p["gw2"], p["gb"], p["gnw"], p["conv_w"], p["conv_b"], p["dtb"], p["a_blk"], p["d_row"],
              p["snw"]]
    sg_spec = pl.BlockSpec((SAMPLE_SEQS, GLA_HEADS, GLA_DK, GLA_DV), lambda i: (i, 0, 0, 0))
    ss_spec = pl.BlockSpec((SAMPLE_SEQS, SSD_HEADS, SSD_HEAD_DIM, SSD_STATE), lambda i: (i, 0, 0, 0))
    return pl.pallas_call(
        functools.partial(_mixer_sample_kernel, seq_len),
        grid=(nb // SAMPLE_SEQS,),
        in_specs=[pl.BlockSpec((rows, PROJ_COLS), lambda i: (i, 0)),
                  pl.BlockSpec((rows, SSD_CONV_CH), lambda i: (i, 0)),
                  sg_spec, ss_spec] + [_const_spec(c.shape) for c in consts],
        out_specs=[pl.BlockSpec((rows, D_MODEL), lambda i: (i, 0)), sg_spec, ss_spec],
        out_shape=[jax.ShapeDtypeStruct((n, D_MODEL), F32),
                   jax.ShapeDtypeStruct(sg.shape, F32), jax.ShapeDtypeStruct(ss.shape, F32)],
        scratch_shapes=[pltpu.VMEM((rows + 8, SSD_CONV_CH), F32),
                        pltpu.VMEM((rows + 8, SSD_CONV_CH), F32)],
        compiler_params=pltpu.CompilerParams(dimension_semantics=("arbitrary",),
                                             vmem_limit_bytes=VMEM_LIMIT),
        name="mixer_sample",
    )(proj, tail, sg, ss, *consts)


def _post_mix_sample_kernel(x_ref, mixed_ref, wout_ref, lnpost_ref, lnxa_ref, wq_ref, h_ref, q_ref):
    m = _dot(_bf(mixed_ref[...]), wout_ref[...])
    h = x_ref[...] + _rms(m, lnpost_ref[...])
    h_ref[...] = h
    q_ref[...] = _bf(_dot(_bf(_rms(h, lnxa_ref[...])), wq_ref[...]))


def _post_mix_sample(x, mixed, p):
    n, d = x.shape
    full = pl.BlockSpec((n, d), lambda i: (0, 0))
    consts = [p["w_out"], p["ln_mix_post"], p["ln_xa_pre"], p["w_xq"]]
    return pl.pallas_call(
        _post_mix_sample_kernel,
        grid=(1,),
        in_specs=[full, full] + [_const_spec(c.shape) for c in consts],
        out_specs=[full, full],
        out_shape=[jax.ShapeDtypeStruct((n, d), F32), jax.ShapeDtypeStruct((n, d), BF16)],
        compiler_params=pltpu.CompilerParams(dimension_semantics=("arbitrary",),
                                             vmem_limit_bytes=VMEM_LIMIT),
        name="post_mix_sample",
    )(x, mixed, *consts)


def _attn_sample_kernel(seq_len, q_ref, k_hbm, v_hbm, a_ref, kbuf, vbuf, sem):
    step = pl.program_id(0)
    rows = q_ref.shape[0]
    n_seq = rows // seq_len

    def copies(at_step, slot):
        out = []
        for s in range(n_seq):
            for hd in range(XA_HEADS):
                b = at_step * n_seq + s
                out.append(pltpu.make_async_copy(k_hbm.at[0, b, :, hd, :], kbuf.at[slot, s, hd], sem.at[0, slot]))
                out.append(pltpu.make_async_copy(v_hbm.at[0, b, :, hd, :], vbuf.at[slot, s, hd], sem.at[1, slot]))
        return out

    @pl.when(step == 0)
    def _():
        for cp in copies(0, 0):
            cp.start()

    @pl.when(step + 1 < pl.num_programs(0))
    def _():
        for cp in copies(step + 1, (step + 1) % 2):
            cp.start()

    slot = step % 2
    for cp in copies(step, slot):
        cp.wait()

    q = q_ref[...]
    out = jnp.zeros((rows, D_MODEL), F32)
    for s in range(n_seq):
        o = _attend(q, lambda hd: _bf(kbuf[slot, s, hd]), lambda hd: _bf(vbuf[slot, s, hd]))
        out = jnp.where(_row_in_seq(rows, seq_len, s, D_MODEL), o, out)
    a_ref[...] = out


def _attn_sample(q, ck, cv, seq_len):
    n, d = q.shape
    _, nb, m, nh, hd = ck.shape
    rows = ATTN_SEQS * seq_len
    tok = pl.BlockSpec((rows, d), lambda i: (i, 0))
    hbm = pl.BlockSpec(memory_space=pl.ANY)
    return pl.pallas_call(
        functools.partial(_attn_sample_kernel, seq_len),
        grid=(nb // ATTN_SEQS,),
        in_specs=[tok, hbm, hbm],
        out_specs=tok,
        out_shape=jax.ShapeDtypeStruct((n, d), F32),
        scratch_shapes=[pltpu.VMEM((2, ATTN_SEQS, nh, m, hd), F32),
                        pltpu.VMEM((2, ATTN_SEQS, nh, m, hd), F32),
                        pltpu.SemaphoreType.DMA((2, 2))],
        compiler_params=pltpu.CompilerParams(dimension_semantics=("arbitrary",),
                                             vmem_limit_bytes=VMEM_LIMIT),
        name="attn_sample",
    )(q, ck, cv)


def _post_attn_sample_kernel(h_ref, a_ref, wo_ref, lnxap_ref, lnf_ref, lnfp_ref, wg_ref, wu_ref, wd_ref,
                             y_ref):
    a = _dot(_bf(a_ref[...]), wo_ref[...])
    h = h_ref[...] + _rms(a, lnxap_ref[...])
    y_ref[...] = _ffn(h, lnf_ref[...], lnfp_ref[...], wg_ref, wu_ref, wd_ref)


def _post_attn_sample(h, a, p):
    n, d = h.shape
    full = pl.BlockSpec((n, d), lambda i: (0, 0))
    consts = [p["w_xo"], p["ln_xa_post"], p["ln_ffn_pre"], p["ln_ffn_post"], p["w_gate"], p["w_up"],
              p["w_down"]]
    return pl.pallas_call(
        _post_attn_sample_kernel,
        grid=(1,),
        in_specs=[full, full] + [_const_spec(c.shape) for c in consts],
        out_specs=full,
        out_shape=jax.ShapeDtypeStruct((n, d), F32),
        compiler_params=pltpu.CompilerParams(dimension_semantics=("arbitrary",),
                                             vmem_limit_bytes=VMEM_LIMIT),
        name="post_attn_sample",
    )(h, a, *consts)


def _pack_params(ln_mix_pre, ln_mix_post, w_in, gla_gate_w2, gla_gate_b, gla_norm_w, ssd_conv_w,
                 ssd_conv_b, ssd_dt_bias, ssd_A_log, ssd_D, ssd_norm_w, w_out, ln_xa_pre, ln_xa_post,
                 mem_norm_w, w_xq, w_xk, w_xv, w_xo, ln_ffn_pre, ln_ffn_post, w_gate, w_up, w_down):
    sizes = (GLA_KW, GLA_KW, GLA_VW, GLA_VW, GLA_GATE_RANK, SSD_INNER, SSD_CONV_CH, SSD_HEADS)
    offs = [0]
    for sz in sizes:
        offs.append(offs[-1] + sz)
    wq, wk, wv, wg, wglr, wz, wxbc, wdt = [w_in[:, offs[i]:offs[i + 1]] for i in range(len(sizes))]
    pad = jnp.zeros((D_MODEL, PROJ_COLS - COL_SMALL - GLA_GATE_RANK - SSD_HEADS), F32)
    w_in_p = jnp.concatenate([wq * (GLA_DK ** -0.5), wk, wv, wg, wz, wxbc, wglr, wdt, pad], axis=1)

    def small_row(v):
        return jnp.zeros((1, 128), F32).at[0, DT_LANE:DT_LANE + SSD_HEADS].set(v)

    row = lambda v: v.reshape(1, -1)
    return {
        "ln_mix_pre": row(ln_mix_pre), "ln_mix_post": row(ln_mix_post),
        "w_in": _bf(w_in_p),
        "gw2": _bf(jnp.zeros((128, GLA_KW), F32).at[:GLA_GATE_RANK].set(gla_gate_w2)),
        "gb": row(gla_gate_b), "gnw": row(gla_norm_w),
        "conv_w": ssd_conv_w, "conv_b": row(ssd_conv_b),
        "dtb": small_row(ssd_dt_bias),
        "a_blk": small_row(-jnp.exp(ssd_A_log)),
        "d_row": jnp.repeat(ssd_D, SSD_HEAD_DIM).reshape(1, SSD_INNER),
        "snw": row(ssd_norm_w),
        "w_out": _bf(w_out),
        "ln_xa_pre": row(ln_xa_pre), "ln_xa_post": row(ln_xa_post), "mem_norm_w": row(mem_norm_w),
        "w_xq": _bf(w_xq * (XA_HEAD_DIM ** -0.5)), "w_xk": _bf(w_xk), "w_xv": _bf(w_xv), "w_xo": _bf(w_xo),
        "ln_ffn_pre": row(ln_ffn_pre), "ln_ffn_post": row(ln_ffn_post),
        "w_gate": _bf(w_gate), "w_up": _bf(w_up), "w_down": _bf(w_down),
    }


def kernel(x_prompt, x_sample, mem_prompt, state_gla, state_ssm, state_conv, cache_mem_k, cache_mem_v,
           ln_mix_pre, ln_mix_post, w_in, gla_gate_w2, gla_gate_b, gla_norm_w, ssd_conv_w, ssd_conv_b,
           ssd_dt_bias, ssd_A_log, ssd_D, ssd_norm_w, w_out, ln_xa_pre, ln_xa_post, mem_norm_w,
           w_xq, w_xk, w_xv, w_xo, ln_ffn_pre, ln_ffn_post, w_gate, w_up, w_down):
    assert w_in.shape[0] == 1, "single-layer kernel"
    layer = [a[0] for a in (ln_mix_pre, ln_mix_post, w_in, gla_gate_w2, gla_gate_b, gla_norm_w, ssd_conv_w,
                            ssd_conv_b, ssd_dt_bias, ssd_A_log, ssd_D, ssd_norm_w, w_out, ln_xa_pre,
                            ln_xa_post, mem_norm_w, w_xq, w_xk, w_xv, w_xo, ln_ffn_pre, ln_ffn_post,
                            w_gate, w_up, w_down)]
    p = _pack_params(*layer)
    bp, mem_len, _ = mem_prompt.shape
    bs, ts, _ = x_sample.shape
    assert ts == SSD_CONV, "the padded conv-tail layout assumes one new row per conv tap"

    mk, mv, mkb, mvb = _memkv(mem_prompt, p["mem_norm_w"], p["w_xk"], p["w_xv"])
    h_p, gla_p, ssm_p, conv_p = _mixer_prompt(x_prompt, p)
    y_p = _attn_ffn_prompt(h_p, mkb, mvb, p)

    xs = x_sample.reshape(bs * ts, D_MODEL)
    proj = _proj_sample(xs, p)
    tail = jnp.pad(state_conv[0], ((0, 0), (1, 0), (0, 0))).reshape(bs * ts, SSD_CONV_CH)
    mixed, gla_s, ssm_s = _mixer_sample(proj, tail, state_gla[0], state_ssm[0], p, ts)
    h_s, q_s = _post_mix_sample(xs, mixed, p)
    a_s = _attn_sample(q_s, cache_mem_k, cache_mem_v, ts)
    y_s = _post_attn_sample(h_s, a_s, p)
    conv_s = proj[:, COL_XBC:COL_XBC + SSD_CONV_CH].reshape(bs, ts, SSD_CONV_CH)[:, ts - (SSD_CONV - 1):]

    return (y_p, y_s.reshape(bs, ts, D_MODEL), gla_p[None], ssm_p[None], conv_p[None],
            mk, mv, gla_s[None], ssm_s[None], conv_s[None])
```

```python
import functools

import jax
import jax.numpy as jnp
from jax import lax
from jax.experimental import pallas as pl
from jax.experimental.pallas import tpu as pltpu

F32 = jnp.float32
BF16 = jnp.bfloat16

D_MODEL = 1024
GLA_HEADS = 4
GLA_DK = 64
GLA_DV = 128
GLA_KW = GLA_HEADS * GLA_DK
GLA_VW = GLA_HEADS * GLA_DV
GLA_GATE_RANK = 16
GLA_GATE_TAU = 16.0
SSD_INNER = 512
SSD_HEAD_DIM = 64
SSD_HEADS = 8
SSD_GROUPS = 2
SSD_STATE = 128
SSD_CONV = 4
SSD_CONV_CH = 1024
XA_HEADS = 4
XA_HEAD_DIM = 256
D_FF = 2816
EPS = 1e-6

COL_Q = 0
COL_K = 256
COL_V = 512
COL_G = 1024
COL_Z = 1536
COL_XBC = 2048
COL_SMALL = 3072
DT_LANE = GLA_GATE_RANK
PROJ_COLS = 3200

GLA_CHUNK = 64
TILE = 256
FFN_SUBTILES = 2
MIX_SEQS = 2
SAMPLE_SEQS = 16
ATTN_SEQS = 4
VMEM_LIMIT = 56 * 1024 * 1024


def _bf(x):
    return x.astype(BF16)


def _dot(a, b):
    return jnp.dot(a, b, preferred_element_type=F32)


def _dot_nt(a, b):
    return lax.dot_general(a, b, (((1,), (1,)), ((), ())), preferred_element_type=F32)


def _dot_tn(a, b):
    return lax.dot_general(a, b, (((0,), (0,)), ((), ())), preferred_element_type=F32)


def _dot3(m, a):
    hi = _bf(a)
    r = a - hi.astype(F32)
    mid = _bf(r)
    lo = _bf(r - mid.astype(F32))
    return _dot(m, hi) + _dot(m, mid) + _dot(m, lo)


def _rms(x, w):
    ms = jnp.mean(x * x, axis=-1, keepdims=True)
    return x * lax.rsqrt(ms + EPS) * w


def _silu(x):
    return x / (1.0 + jnp.exp(-x))


def _softplus(x):
    e = jnp.exp(-jnp.abs(x))
    u = 1.0 + e
    log1p_e = jnp.where(u == 1.0, e, jnp.log(u) * (e / (u - 1.0)))
    return jnp.maximum(x, 0.0) + log1p_e


def _log_sigmoid(x):
    return jnp.minimum(x, 0.0) - jnp.log(1.0 + jnp.exp(-jnp.abs(x)))


def _seg_masks(rows, seq_len):
    t = lax.broadcasted_iota(jnp.int32, (rows, rows), 0)
    u = lax.broadcasted_iota(jnp.int32, (rows, rows), 1)
    if seq_len == rows:
        same = None
        tril = u <= t
    else:
        shift = seq_len.bit_length() - 1
        assert 1 << shift == seq_len
        same = (t >> shift) == (u >> shift)
        tril = jnp.logical_and(same, u <= t)
    return same, tril


def _mask_to_bf16(mask, rows):
    if mask is None:
        return jnp.ones((rows, rows), BF16)
    return jnp.where(mask, 1.0, 0.0).astype(BF16)


def _row_in_seq(rows, seq_len, s, width):
    r = lax.broadcasted_iota(jnp.int32, (rows, width), 0)
    return jnp.logical_and(r >= s * seq_len, r < (s + 1) * seq_len)


def _interleave(chains):
    chains = list(chains)
    while chains:
        alive = []
        for c in chains:
            try:
                next(c)
                alive.append(c)
            except StopIteration:
                pass
        chains = alive


def _pad_rows(x, rows):
    if x.shape[0] >= rows:
        return x
    return jnp.concatenate([x, jnp.zeros((rows - x.shape[0], x.shape[1]), x.dtype)], axis=0)


def _gla_chunk(q, k, v, lf, get_state, put_state, seq_len):
    rows = q.shape[0]
    n_seq = rows // seq_len
    same, tril = _seg_masks(rows, seq_len)
    tril_b = _mask_to_bf16(tril, rows)
    same_b = _mask_to_bf16(same, rows)

    b = _dot3(tril_b, lf)
    bl = _dot3(same_b, lf)
    qt = q * jnp.exp(b)
    kt = k * jnp.exp(-b)
    kd = k * jnp.exp(bl - b)
    bl_t = _pad_rows(bl, 128).T

    lane = lax.broadcasted_iota(jnp.int32, (rows, GLA_KW), 1)
    lhs = _bf(jnp.concatenate(
        [jnp.where((lane >= h * GLA_DK) & (lane < (h + 1) * GLA_DK), qt, 0.0)
         for h in range(GLA_HEADS)], axis=0))
    att = _dot_nt(lhs, _bf(kt))
    t4 = lax.broadcasted_iota(jnp.int32, (GLA_HEADS * rows, rows), 0) & (rows - 1)
    u4 = lax.broadcasted_iota(jnp.int32, (GLA_HEADS * rows, rows), 1)
    causal4 = u4 <= t4
    if n_seq > 1:
        shift = seq_len.bit_length() - 1
        causal4 = jnp.logical_and(causal4, (t4 >> shift) == (u4 >> shift))
    att = _bf(jnp.where(causal4, att, 0.0))
    vb = _bf(v)
    kdb = _bf(kd)

    o_inter = None
    r4 = lax.broadcasted_iota(jnp.int32, (GLA_HEADS * rows, GLA_DV), 0) & (rows - 1)
    for s in range(n_seq):
        st = get_state(s)
        oi = _dot(lhs, _bf(st))
        if n_seq == 1:
            o_inter = oi
            kd_s = kdb
        else:
            m4 = jnp.logical_and(r4 >= s * seq_len, r4 < (s + 1) * seq_len)
            oi = jnp.where(m4, oi, 0.0)
            o_inter = oi if o_inter is None else o_inter + oi
            kd_s = jnp.where(_row_in_seq(rows, seq_len, s, GLA_KW), kdb, jnp.zeros_like(kdb))
        upd = _dot_tn(kd_s, vb)
        upd_d = jnp.concatenate(
            [upd[h * GLA_DK:(h + 1) * GLA_DK, h * GLA_DV:(h + 1) * GLA_DV]
             for h in range(GLA_HEADS)], axis=0)
        c0 = s * seq_len
        decay = jnp.exp(jnp.broadcast_to(bl_t[:, c0:c0 + 1], (GLA_KW, GLA_DV)))
        put_state(s, decay * st + upd_d)

    outs = []
    for h in range(GLA_HEADS):
        o_h = _dot(att[h * rows:(h + 1) * rows], vb[:, h * GLA_DV:(h + 1) * GLA_DV])
        outs.append(o_h + o_inter[h * rows:(h + 1) * rows])
    return outs


def _ssd_tile(xs, bm, cm, dt_raw_blk, dtb_blk, a_blk, d_row, get_state, put_state, seq_len):
    rows = xs.shape[0]
    n_seq = rows // seq_len
    same, tril = _seg_masks(rows, seq_len)
    tril_b = _mask_to_bf16(tril, rows)
    same_b = _mask_to_bf16(same, rows)

    dt = _softplus(dt_raw_blk + dtb_blk)
    dta = dt * a_blk
    lc = _dot3(tril_b, dta)
    ll = _dot3(same_b, dta)
    elc = jnp.exp(lc)
    w = jnp.exp(ll - lc) * dt
    ell = jnp.exp(ll)
    lc_t = _pad_rows(lc, 128).T[:, :rows]
    dt_t = _pad_rows(dt, 128).T[:, :rows]

    bmb = _bf(bm)
    cmb = _bf(cm)
    cb = [_dot_nt(cmb[:, g * SSD_STATE:(g + 1) * SSD_STATE],
                  bmb[:, g * SSD_STATE:(g + 1) * SSD_STATE]) for g in range(SSD_GROUPS)]

    lane128 = lax.broadcasted_iota(jnp.int32, (rows, 128), 1)
    lo128 = lane128 < SSD_HEAD_DIM
    sub128 = lax.broadcasted_iota(jnp.int32, (128, 128), 0) < SSD_HEAD_DIM

    def col(x, h, width):
        return jnp.broadcast_to(x[:, DT_LANE + h:DT_LANE + h + 1], (x.shape[0], width))

    heads_per_group = SSD_HEADS // SSD_GROUPS
    ys = []
    for j in range(SSD_HEADS // 2):
        g = (2 * j) // heads_per_group
        xs_p = xs[:, 128 * j:128 * (j + 1)]
        w_heads = []
        for h in (2 * j, 2 * j + 1):
            seg = col(lc, h, rows) - jnp.broadcast_to(lc_t[DT_LANE + h:DT_LANE + h + 1, :], (rows, rows))
            dtr = jnp.broadcast_to(dt_t[DT_LANE + h:DT_LANE + h + 1, :], (rows, rows))
            w_heads.append(_bf(jnp.where(tril, cb[g] * jnp.exp(seg) * dtr, 0.0)))
        x_lo = _bf(jnp.where(lo128, xs_p, 0.0))
        x_hi = _bf(jnp.where(lo128, 0.0, xs_p))
        y_p = _dot(w_heads[0], x_lo) + _dot(w_heads[1], x_hi)

        elc_p = jnp.where(lo128, col(elc, 2 * j, 128), col(elc, 2 * j + 1, 128))
        w_p = jnp.where(lo128, col(w, 2 * j, 128), col(w, 2 * j + 1, 128))
        xw = _bf(xs_p * w_p)
        bm_g = bmb[:, g * SSD_STATE:(g + 1) * SSD_STATE]
        cm_g = cmb[:, g * SSD_STATE:(g + 1) * SSD_STATE]
        y_inter = None
        for s in range(n_seq):
            st = get_state(s, j)
            ci = _dot_nt(cm_g, _bf(st))
            if n_seq == 1:
                y_inter = ci
                xw_s = xw
            else:
                msk = _row_in_seq(rows, seq_len, s, 128)
                ci = jnp.where(msk, ci, 0.0)
                y_inter = ci if y_inter is None else y_inter + ci
                xw_s = jnp.where(msk, xw, jnp.zeros_like(xw))
            upd = _dot_tn(xw_s, bm_g)
            r0 = s * seq_len
            e0 = jnp.broadcast_to(ell[r0:r0 + 1, DT_LANE + 2 * j:DT_LANE + 2 * j + 1], (128, 128))
            e1 = jnp.broadcast_to(ell[r0:r0 + 1, DT_LANE + 2 * j + 1:DT_LANE + 2 * j + 2], (128, 128))
            put_state(s, j, jnp.where(sub128, e0, e1) * st + upd)
        ys.append(y_p + y_inter * elc_p + d_row[:, 128 * j:128 * (j + 1)] * xs_p)
    return jnp.concatenate(ys, axis=1)


def _gate_logf(small_blk, gw2_ref, gb_ref):
    logits = _dot(_bf(small_blk), gw2_ref[...]) + gb_ref[...]
    return _log_sigmoid(logits) * (1.0 / GLA_GATE_TAU)


def _gla_finish(o_h, g_h, gnw):
    return _rms(o_h, gnw) * _silu(g_h)


def _ssd_finish(y, z, snw):
    y = y * _silu(z)
    gs = SSD_INNER // SSD_GROUPS
    parts = [_rms(y[:, g * gs:(g + 1) * gs], snw[:, g * gs:(g + 1) * gs]) for g in range(SSD_GROUPS)]
    return jnp.concatenate(parts, axis=1)


def _memkv_kernel(mem_ref, nw_ref, wk_ref, wv_ref, k_ref, v_ref, kb_ref, vb_ref):
    mn = _bf(_rms(mem_ref[0], nw_ref[...]))
    mk = _dot(mn, wk_ref[...])
    mv = _dot(mn, wv_ref[...])
    for h in range(XA_HEADS):
        k_ref[0, 0, :, h, :] = mk[:, h * XA_HEAD_DIM:(h + 1) * XA_HEAD_DIM]
        v_ref[0, 0, :, h, :] = mv[:, h * XA_HEAD_DIM:(h + 1) * XA_HEAD_DIM]
    kb_ref[0] = _bf(mk)
    vb_ref[0] = _bf(mv)


def _const_spec(shape):
    nd = len(shape)
    return pl.BlockSpec(shape, lambda *_: (0,) * nd, pipeline_mode=pl.Buffered(1))


def _memkv(mem, nw, wk, wv):
    b, m, d = mem.shape
    blk = pl.BlockSpec((1, m, d), lambda i: (i, 0, 0))
    cache_blk = pl.BlockSpec((1, 1, m, XA_HEADS, XA_HEAD_DIM), lambda i: (0, i, 0, 0, 0))
    cache_shape = jax.ShapeDtypeStruct((1, b, m, XA_HEADS, XA_HEAD_DIM), F32)
    return pl.pallas_call(
        _memkv_kernel,
        grid=(b,),
        in_specs=[blk, _const_spec((1, d)), _const_spec((d, d)), _const_spec((d, d))],
        out_specs=[cache_blk, cache_blk, blk, blk],
        out_shape=[cache_shape] * 2 + [jax.ShapeDtypeStruct((b, m, d), BF16)] * 2,
        compiler_params=pltpu.CompilerParams(dimension_semantics=("arbitrary",),
                                             vmem_limit_bytes=VMEM_LIMIT),
        name="mem_kv",
    )(mem, nw, wk, wv)


def _mixer_prompt_one(x_ref, h_ref, proj_s, xpad_s, mixed_s, sg_s, ss_s,
                      lnpre_ref, win_ref, gw2_ref, gb_ref, gnw_ref, cw_ref, cb_ref,
                      dtb_ref, a_ref, d_ref, snw_ref, wout_ref, lnpost_ref):
    rows = TILE
    x = x_ref[...]
    hn = _bf(_rms(x, lnpre_ref[...]))
    yield
    proj_s[...] = _dot(hn, win_ref[...])
    yield

    small = proj_s[:, COL_SMALL:COL_SMALL + 128]
    logf = _gate_logf(small, gw2_ref, gb_ref)

    xbc = proj_s[:, COL_XBC:COL_XBC + SSD_CONV_CH]
    xpad_s[8:8 + rows, :] = xbc
    conv = cb_ref[...]
    for j in range(SSD_CONV):
        conv = conv + xpad_s[5 + j:5 + j + rows, :] * cw_ref[j:j + 1, :]
    xpad_s[5:8, :] = xpad_s[5 + rows:8 + rows, :]
    xa = _silu(conv)
    yield

    def get_ss(s, j):
        return ss_s[128 * j:128 * (j + 1), :]

    def put_ss(s, j, val):
        ss_s[128 * j:128 * (j + 1), :] = val

    y = _ssd_tile(xa[:, :SSD_INNER], xa[:, SSD_INNER:SSD_INNER + 256], xa[:, SSD_INNER + 256:],
                  small, dtb_ref[...], a_ref[...], d_ref[...], get_ss, put_ss, rows)
    mixed_s[:, GLA_VW:] = _ssd_finish(y, proj_s[:, COL_Z:COL_Z + SSD_INNER], snw_ref[...])
    yield

    def get_sg(s):
        return sg_s[...]

    def put_sg(s, val):
        sg_s[...] = val

    for c in range(rows // GLA_CHUNK):
        r0 = c * GLA_CHUNK
        sl = slice(r0, r0 + GLA_CHUNK)
        outs = _gla_chunk(proj_s[sl, COL_Q:COL_Q + GLA_KW], proj_s[sl, COL_K:COL_K + GLA_KW],
                          proj_s[sl, COL_V:COL_V + GLA_VW], logf[sl], get_sg, put_sg, GLA_CHUNK)
        for h in range(GLA_HEADS):
            g_h = proj_s[sl, COL_G + h * GLA_DV:COL_G + (h + 1) * GLA_DV]
            mixed_s[sl, h * GLA_DV:(h + 1) * GLA_DV] = _gla_finish(outs[h], g_h, gnw_ref[...])
        yield

    m = _dot(_bf(mixed_s[...]), wout_ref[...])
    yield
    h_ref[...] = x + _rms(m, lnpost_ref[...])


N_MIXER_WEIGHTS = 13
N_MIXER_SCRATCH = 5


def _mixer_prompt_kernel(x_ref, *rest):
    consts = rest[:N_MIXER_WEIGHTS]
    h_ref, sg_out, ss_out, conv_out = rest[N_MIXER_WEIGHTS:N_MIXER_WEIGHTS + 4]
    scratch = rest[N_MIXER_WEIGHTS + 4:]
    per_seq = [scratch[i * N_MIXER_SCRATCH:(i + 1) * N_MIXER_SCRATCH] for i in range(MIX_SEQS)]
    t = pl.program_id(1)

    @pl.when(t == 0)
    def _():
        for _, xpad_s, _, sg_s, ss_s in per_seq:
            sg_s[...] = jnp.zeros_like(sg_s)
            ss_s[...] = jnp.zeros_like(ss_s)
            xpad_s[0:8, :] = jnp.zeros((8, SSD_CONV_CH), F32)

    _interleave([_mixer_prompt_one(x_ref.at[i], h_ref.at[i], *per_seq[i], *consts) for i in range(MIX_SEQS)])

    @pl.when(t == pl.num_programs(1) - 1)
    def _():
        for i, (_, xpad_s, _, sg_s, ss_s) in enumerate(per_seq):
            sg_out[i] = sg_s[...].reshape(GLA_HEADS, GLA_DK, GLA_DV)
            ss_out[i] = ss_s[...].reshape(SSD_HEADS, SSD_HEAD_DIM, SSD_STATE)
            conv_out[i] = xpad_s[5:8, :]


def _mixer_prompt(x, p):
    b, t, d = x.shape
    nt = t // TILE
    consts = [p["ln_mix_pre"], p["w_in"], p["gw2"], p["gb"], p["gnw"], p["conv_w"], p["conv_b"],
              p["dtb"], p["a_blk"], p["d_row"], p["snw"], p["w_out"], p["ln_mix_post"]]
    assert len(consts) == N_MIXER_WEIGHTS
    tok = pl.BlockSpec((MIX_SEQS, TILE, d), lambda i, j: (i, j, 0))
    return pl.pallas_call(
        _mixer_prompt_kernel,
        grid=(b // MIX_SEQS, nt),
        in_specs=[tok] + [_const_spec(c.shape) for c in consts],
        out_specs=[tok,
                   pl.BlockSpec((MIX_SEQS, GLA_HEADS, GLA_DK, GLA_DV), lambda i, j: (i, 0, 0, 0)),
                   pl.BlockSpec((MIX_SEQS, SSD_HEADS, SSD_HEAD_DIM, SSD_STATE), lambda i, j: (i, 0, 0, 0)),
                   pl.BlockSpec((MIX_SEQS, SSD_CONV - 1, SSD_CONV_CH), lambda i, j: (i, 0, 0))],
        out_shape=[jax.ShapeDtypeStruct((b, t, d), F32),
                   jax.ShapeDtypeStruct((b, GLA_HEADS, GLA_DK, GLA_DV), F32),
                   jax.ShapeDtypeStruct((b, SSD_HEADS, SSD_HEAD_DIM, SSD_STATE), F32),
                   jax.ShapeDtypeStruct((b, SSD_CONV - 1, SSD_CONV_CH), F32)],
        scratch_shapes=[pltpu.VMEM((TILE, PROJ_COLS), F32),
                        pltpu.VMEM((TILE + 8, SSD_CONV_CH), F32),
                        pltpu.VMEM((TILE, D_MODEL), F32),
                        pltpu.VMEM((GLA_KW, GLA_DV), F32),
                        pltpu.VMEM((SSD_INNER, SSD_STATE), F32)] * MIX_SEQS,
        compiler_params=pltpu.CompilerParams(dimension_semantics=("arbitrary", "arbitrary"),
                                             vmem_limit_bytes=VMEM_LIMIT),
        name="mixer_prompt",
    )(x, *consts)


def _attend(q, get_k, get_v):
    outs = []
    for h in range(XA_HEADS):
        s = _dot_nt(q[:, h * XA_HEAD_DIM:(h + 1) * XA_HEAD_DIM], get_k(h))
        e = jnp.exp(s - jnp.max(s, axis=-1, keepdims=True))
        pr = e / jnp.sum(e, axis=-1, keepdims=True)
        outs.append(_dot(_bf(pr), get_v(h)))
    return jnp.concatenate(outs, axis=1)


def _ffn(h, lnpre, lnpost, wg_ref, wu_ref, wd_ref):
    hf = _bf(_rms(h, lnpre))
    gt = _dot(hf, wg_ref[...])
    up = _dot(hf, wu_ref[...])
    f = _dot(_bf(_silu(gt) * up), wd_ref[...])
    return h + _rms(f, lnpost)


def _attn_ffn_prompt_kernel(h_ref, mk_ref, mv_ref, lnxa_ref, wq_ref, wo_ref, lnxap_ref,
                            lnf_ref, lnfp_ref, wg_ref, wu_ref, wd_ref, y_ref):
    def sub_tile(sub):
        rs = slice(sub * TILE, (sub + 1) * TILE)
        h = h_ref[0, rs, :]
        hn = _bf(_rms(h, lnxa_ref[...]))
        yield
        q = _bf(_dot(hn, wq_ref[...]))
        yield
        a = _bf(_attend(q, lambda hd: mk_ref[0, :, hd * XA_HEAD_DIM:(hd + 1) * XA_HEAD_DIM],
                        lambda hd: mv_ref[0, :, hd * XA_HEAD_DIM:(hd + 1) * XA_HEAD_DIM]))
        yield
        a = _dot(a, wo_ref[...])
        yield
        h = h + _rms(a, lnxap_ref[...])
        hf = _bf(_rms(h, lnf_ref[...]))
        yield
        gt = _dot(hf, wg_ref[...])
        yield
        up = _dot(hf, wu_ref[...])
        yield
        act = _bf(_silu(gt) * up)
        yield
        f = _dot(act, wd_ref[...])
        yield
        y_ref[0, rs, :] = h + _rms(f, lnfp_ref[...])

    _interleave([sub_tile(sub) for sub in range(FFN_SUBTILES)])


def _attn_ffn_prompt(h, mkb, mvb, p):
    b, t, d = h.shape
    rows = FFN_SUBTILES * TILE
    consts = [p["ln_xa_pre"], p["w_xq"], p["w_xo"], p["ln_xa_post"], p["ln_ffn_pre"], p["ln_ffn_post"],
              p["w_gate"], p["w_up"], p["w_down"]]
    tok = pl.BlockSpec((1, rows, d), lambda i, j: (i, j, 0))
    mem = pl.BlockSpec((1, mkb.shape[1], d), lambda i, j: (i, 0, 0))
    return pl.pallas_call(
        _attn_ffn_prompt_kernel,
        grid=(b, t // rows),
        in_specs=[tok, mem, mem] + [_const_spec(c.shape) for c in consts],
        out_specs=tok,
        out_shape=jax.ShapeDtypeStruct((b, t, d), F32),
        compiler_params=pltpu.CompilerParams(dimension_semantics=("arbitrary", "arbitrary"),
                                             vmem_limit_bytes=VMEM_LIMIT),
        name="attn_ffn_prompt",
    )(h, mkb, mvb, *consts)


def _proj_sample_kernel(x_ref, lnpre_ref, win_ref, proj_ref):
    proj_ref[...] = _dot(_bf(_rms(x_ref[...], lnpre_ref[...])), win_ref[...])


def _proj_sample(x, p):
    n, d = x.shape
    return pl.pallas_call(
        _proj_sample_kernel,
        grid=(1,),
        in_specs=[_const_spec((n, d)), _const_spec((1, d)), _const_spec(p["w_in"].shape)],
        out_specs=pl.BlockSpec((n, PROJ_COLS), lambda i: (0, 0)),
        out_shape=jax.ShapeDtypeStruct((n, PROJ_COLS), F32),
        compiler_params=pltpu.CompilerParams(dimension_semantics=("arbitrary",),
                                             vmem_limit_bytes=VMEM_LIMIT),
        name="proj_sample",
    )(x, p["ln_mix_pre"], p["w_in"])


def _mixer_sample_kernel(seq_len, proj_ref, tail_ref, sg_ref, ss_ref, gw2_ref, gb_ref, gnw_ref,
                         cw_ref, cb_ref, dtb_ref, a_ref, d_ref, snw_ref,
                         mixed_ref, sg_out, ss_out, xpad_s, tpad_s):
    rows = proj_ref.shape[0]
    n_seq = rows // seq_len
    small = proj_ref[:, COL_SMALL:COL_SMALL + 128]
    logf = _gate_logf(small, gw2_ref, gb_ref)

    xbc = proj_ref[:, COL_XBC:COL_XBC + SSD_CONV_CH]
    xpad_s[0:8, :] = jnp.zeros((8, SSD_CONV_CH), F32)
    xpad_s[8:8 + rows, :] = xbc
    tpad_s[0:rows, :] = tail_ref[...]
    tpad_s[rows:rows + 8, :] = jnp.zeros((8, SSD_CONV_CH), F32)
    pos = lax.broadcasted_iota(jnp.int32, (rows, SSD_CONV_CH), 0) & (seq_len - 1)
    conv = cb_ref[...]
    for j in range(SSD_CONV - 1):
        back = SSD_CONV - 1 - j
        prev = jnp.where(pos >= back, xpad_s[8 - back:8 - back + rows, :],
                         tpad_s[seq_len - back:seq_len - back + rows, :])
        conv = conv + prev * cw_ref[j:j + 1, :]
    conv = conv + xbc * cw_ref[SSD_CONV - 1:SSD_CONV, :]
    xa = _silu(conv)

    def get_ss(s, j):
        return ss_ref[s, 2 * j:2 * j + 2].reshape(2 * SSD_HEAD_DIM, SSD_STATE)

    def put_ss(s, j, val):
        ss_out[s, 2 * j:2 * j + 2] = val.reshape(2, SSD_HEAD_DIM, SSD_STATE)

    y = _ssd_tile(xa[:, :SSD_INNER], xa[:, SSD_INNER:SSD_INNER + 256], xa[:, SSD_INNER + 256:],
                  small, dtb_ref[...], a_ref[...], d_ref[...], get_ss, put_ss, seq_len)
    mixed_ref[:, GLA_VW:] = _ssd_finish(y, proj_ref[:, COL_Z:COL_Z + SSD_INNER], snw_ref[...])

    def get_sg(s):
        return sg_ref[s].reshape(GLA_KW, GLA_DV)

    def put_sg(s, val):
        sg_out[s] = val.reshape(GLA_HEADS, GLA_DK, GLA_DV)

    outs = _gla_chunk(proj_ref[:, COL_Q:COL_Q + GLA_KW], proj_ref[:, COL_K:COL_K + GLA_KW],
                      proj_ref[:, COL_V:COL_V + GLA_VW], logf, get_sg, put_sg, seq_len)
    for h in range(GLA_HEADS):
        g_h = proj_ref[:, COL_G + h * GLA_DV:COL_G + (h + 1) * GLA_DV]
        mixed_ref[:, h * GLA_DV:(h + 1) * GLA_DV] = _gla_finish(outs[h], g_h, gnw_ref[...])


def _mixer_sample(proj, tail, sg, ss, p, seq_len):
    n = proj.shape[0]
    rows = SAMPLE_SEQS * seq_len
    nb = sg.shape[0]
    consts = [p["gw2"], p["gb"], p["gnw"], p["conv_w"], p["conv_b"], p["dtb"], p["a_blk"], p["d_row"],
              p["snw"]]
    sg_spec = pl.BlockSpec((SAMPLE_SEQS, GLA_HEADS, GLA_DK, GLA_DV), lambda i: (i, 0, 0, 0))
    ss_spec = pl.BlockSpec((SAMPLE_SEQS, SSD_HEADS, SSD_HEAD_DIM, SSD_STATE), lambda i: (i, 0, 0, 0))
    return pl.pallas_call(
        functools.partial(_mixer_sample_kernel, seq_len),
        grid=(nb // SAMPLE_SEQS,),
        in_specs=[pl.BlockSpec((rows, PROJ_COLS), lambda i: (i, 0)),
                  pl.BlockSpec((rows, SSD_CONV_CH), lambda i: (i, 0)),
                  sg_spec, ss_spec] + [_const_spec(c.shape) for c in consts],
        out_specs=[pl.BlockSpec((rows, D_MODEL), lambda i: (i, 0)), sg_spec, ss_spec],
        out_shape=[jax.ShapeDtypeStruct((n, D_MODEL), F32),
                   jax.ShapeDtypeStruct(sg.shape, F32), jax.ShapeDtypeStruct(ss.shape, F32)],
        scratch_shapes=[pltpu.VMEM((rows + 8, SSD_CONV_CH), F32),
                        pltpu.VMEM((rows + 8, SSD_CONV_CH), F32)],
        compiler_params=pltpu.CompilerParams(dimension_semantics=("arbitrary",),
                                             vmem_limit_bytes=VMEM_LIMIT),
        name="mixer_sample",
    )(proj, tail, sg, ss, *consts)


def _post_mix_sample_kernel(x_ref, mixed_ref, wout_ref, lnpost_ref, lnxa_ref, wq_ref, h_ref, q_ref):
    m = _dot(_bf(mixed_ref[...]), wout_ref[...])
    h = x_ref[...] + _rms(m, lnpost_ref[...])
    h_ref[...] = h
    q_ref[...] = _bf(_dot(_bf(_rms(h, lnxa_ref[...])), wq_ref[...]))


def _post_mix_sample(x, mixed, p):
    n, d = x.shape
    full = pl.BlockSpec((n, d), lambda i: (0, 0))
    consts = [p["w_out"], p["ln_mix_post"], p["ln_xa_pre"], p["w_xq"]]
    return pl.pallas_call(
        _post_mix_sample_kernel,
        grid=(1,),
        in_specs=[full, full] + [_const_spec(c.shape) for c in consts],
        out_specs=[full, full],
        out_shape=[jax.ShapeDtypeStruct((n, d), F32), jax.ShapeDtypeStruct((n, d), BF16)],
        compiler_params=pltpu.CompilerParams(dimension_semantics=("arbitrary",),
                                             vmem_limit_bytes=VMEM_LIMIT),
        name="post_mix_sample",
    )(x, mixed, *consts)


def _attn_sample_kernel(seq_len, q_ref, k_hbm, v_hbm, a_ref, kbuf, vbuf, sem):
    step = pl.program_id(0)
    rows = q_ref.shape[0]
    n_seq = rows // seq_len

    def copies(at_step, slot):
        out = []
        for s in range(n_seq):
            for hd in range(XA_HEADS):
                b = at_step * n_seq + s
                out.append(pltpu.make_async_copy(k_hbm.at[0, b, :, hd, :], kbuf.at[slot, s, hd], sem.at[0, slot]))
                out.append(pltpu.make_async_copy(v_hbm.at[0, b, :, hd, :], vbuf.at[slot, s, hd], sem.at[1, slot]))
        return out

    @pl.when(step == 0)
    def _():
        for cp in copies(0, 0):
            cp.start()

    @pl.when(step + 1 < pl.num_programs(0))
    def _():
        for cp in copies(step + 1, (step + 1) % 2):
            cp.start()

    slot = step % 2
    for cp in copies(step, slot):
        cp.wait()

    q = q_ref[...]
    results = {}

    def one(s, hd):
        sc = _dot_nt(q[:, hd * XA_HEAD_DIM:(hd + 1) * XA_HEAD_DIM], _bf(kbuf[slot, s, hd]))
        yield
        e = jnp.exp(sc - jnp.max(sc, axis=-1, keepdims=True))
        pr = _bf(e / jnp.sum(e, axis=-1, keepdims=True))
        yield
        results[s, hd] = _dot(pr, _bf(vbuf[slot, s, hd]))

    _interleave([one(s, hd) for s in range(n_seq) for hd in range(XA_HEADS)])
    for hd in range(XA_HEADS):
        sl = slice(hd * XA_HEAD_DIM, (hd + 1) * XA_HEAD_DIM)
        out = results[0, hd]
        for s in range(1, n_seq):
            out = jnp.where(_row_in_seq(rows, seq_len, s, XA_HEAD_DIM), results[s, hd], out)
        a_ref[:, sl] = out


def _attn_sample(q, ck, cv, seq_len):
    n, d = q.shape
    _, nb, m, nh, hd = ck.shape
    rows = ATTN_SEQS * seq_len
    tok = pl.BlockSpec((rows, d), lambda i: (i, 0))
    hbm = pl.BlockSpec(memory_space=pl.ANY)
    return pl.pallas_call(
        functools.partial(_attn_sample_kernel, seq_len),
        grid=(nb // ATTN_SEQS,),
        in_specs=[tok, hbm, hbm],
        out_specs=tok,
        out_shape=jax.ShapeDtypeStruct((n, d), F32),
        scratch_shapes=[pltpu.VMEM((2, ATTN_SEQS, nh, m, hd), F32),
                        pltpu.VMEM((2, ATTN_SEQS, nh, m, hd), F32),
                        pltpu.SemaphoreType.DMA((2, 2))],
        compiler_params=pltpu.CompilerParams(dimension_semantics=("arbitrary",),
                                             vmem_limit_bytes=VMEM_LIMIT),
        name="attn_sample",
    )(q, ck, cv)


def _post_attn_sample_kernel(h_ref, a_ref, wo_ref, lnxap_ref, lnf_ref, lnfp_ref, wg_ref, wu_ref, wd_ref,
                             y_ref):
    a = _dot(_bf(a_ref[...]), wo_ref[...])
    h = h_ref[...] + _rms(a, lnxap_ref[...])
    y_ref[...] = _ffn(h, lnf_ref[...], lnfp_ref[...], wg_ref, wu_ref, wd_ref)


def _post_attn_sample(h, a, p):
    n, d = h.shape
    full = pl.BlockSpec((n, d), lambda i: (0, 0))
    consts = [p["w_xo"], p["ln_xa_post"], p["ln_ffn_pre"], p["ln_ffn_post"], p["w_gate"], p["w_up"],
              p["w_down"]]
    return pl.pallas_call(
        _post_attn_sample_kernel,
        grid=(1,),
        in_specs=[full, full] + [_const_spec(c.shape) for c in consts],
        out_specs=full,
        out_shape=jax.ShapeDtypeStruct((n, d), F32),
        compiler_params=pltpu.CompilerParams(dimension_semantics=("arbitrary",),
                                             vmem_limit_bytes=VMEM_LIMIT),
        name="post_attn_sample",
    )(h, a, *consts)


def _pack_params(ln_mix_pre, ln_mix_post, w_in, gla_gate_w2, gla_gate_b, gla_norm_w, ssd_conv_w,
                 ssd_conv_b, ssd_dt_bias, ssd_A_log, ssd_D, ssd_norm_w, w_out, ln_xa_pre, ln_xa_post,
                 mem_norm_w, w_xq, w_xk, w_xv, w_xo, ln_ffn_pre, ln_ffn_post, w_gate, w_up, w_down):
    sizes = (GLA_KW, GLA_KW, GLA_VW, GLA_VW, GLA_GATE_RANK, SSD_INNER, SSD_CONV_CH, SSD_HEADS)
    offs = [0]
    for sz in sizes:
        offs.append(offs[-1] + sz)
    wq, wk, wv, wg, wglr, wz, wxbc, wdt = [w_in[:, offs[i]:offs[i + 1]] for i in range(len(sizes))]
    pad = jnp.zeros((D_MODEL, PROJ_COLS - COL_SMALL - GLA_GATE_RANK - SSD_HEADS), F32)
    w_in_p = jnp.concatenate([wq * (GLA_DK ** -0.5), wk, wv, wg, wz, wxbc, wglr, wdt, pad], axis=1)

    def small_row(v):
        return jnp.zeros((1, 128), F32).at[0, DT_LANE:DT_LANE + SSD_HEADS].set(v)

    row = lambda v: v.reshape(1, -1)
    return {
        "ln_mix_pre": row(ln_mix_pre), "ln_mix_post": row(ln_mix_post),
        "w_in": _bf(w_in_p),
        "gw2": _bf(jnp.zeros((128, GLA_KW), F32).at[:GLA_GATE_RANK].set(gla_gate_w2)),
        "gb": row(gla_gate_b), "gnw": row(gla_norm_w),
        "conv_w": ssd_conv_w, "conv_b": row(ssd_conv_b),
        "dtb": small_row(ssd_dt_bias),
        "a_blk": small_row(-jnp.exp(ssd_A_log)),
        "d_row": jnp.repeat(ssd_D, SSD_HEAD_DIM).reshape(1, SSD_INNER),
        "snw": row(ssd_norm_w),
        "w_out": _bf(w_out),
        "ln_xa_pre": row(ln_xa_pre), "ln_xa_post": row(ln_xa_post), "mem_norm_w": row(mem_norm_w),
        "w_xq": _bf(w_xq * (XA_HEAD_DIM ** -0.5)), "w_xk": _bf(w_xk), "w_xv": _bf(w_xv), "w_xo": _bf(w_xo),
        "ln_ffn_pre": row(ln_ffn_pre), "ln_ffn_post": row(ln_ffn_post),
        "w_gate": _bf(w_gate), "w_up": _bf(w_up), "w_down": _bf(w_down),
    }


def kernel(x_prompt, x_sample, mem_prompt, state_gla, state_ssm, state_conv, cache_mem_k, cache_mem_v,
           ln_mix_pre, ln_mix_post, w_in, gla_gate_w2, gla_gate_b, gla_norm_w, ssd_conv_w, ssd_conv_b,
           ssd_dt_bias, ssd_A_log, ssd_D, ssd_norm_w, w_out, ln_xa_pre, ln_xa_post, mem_norm_w,
           w_xq, w_xk, w_xv, w_xo, ln_ffn_pre, ln_ffn_post, w_gate, w_up, w_down):
    assert w_in.shape[0] == 1, "single-layer kernel"
    layer = [a[0] for a in (ln_mix_pre, ln_mix_post, w_in, gla_gate_w2, gla_gate_b, gla_norm_w, ssd_conv_w,
                            ssd_conv_b, ssd_dt_bias, ssd_A_log, ssd_D, ssd_norm_w, w_out, ln_xa_pre,
                            ln_xa_post, mem_norm_w, w_xq, w_xk, w_xv, w_xo, ln_ffn_pre, ln_ffn_post,
                            w_gate, w_up, w_down)]
    p = _pack_params(*layer)
    bp, mem_len, _ = mem_prompt.shape
    bs, ts, _ = x_sample.shape
    assert ts == SSD_CONV, "the padded conv-tail layout assumes one new row per conv tap"

    mk, mv, mkb, mvb = _memkv(mem_prompt, p["mem_norm_w"], p["w_xk"], p["w_xv"])
    h_p, gla_p, ssm_p, conv_p = _mixer_prompt(x_prompt, p)
    y_p = _attn_ffn_prompt(h_p, mkb, mvb, p)

    xs = x_sample.reshape(bs * ts, D_MODEL)
    proj = _proj_sample(xs, p)
    tail = jnp.pad(state_conv[0], ((0, 0), (1, 0), (0, 0))).reshape(bs * ts, SSD_CONV_CH)
    mixed, gla_s, ssm_s = _mixer_sample(proj, tail, state_gla[0], state_ssm[0], p, ts)
    h_s, q_s = _post_mix_sample(xs, mixed, p)
    a_s = _attn_sample(q_s, cache_mem_k, cache_mem_v, ts)
    y_s = _post_attn_sample(h_s, a_s, p)
    conv_s = proj[:, COL_XBC:COL_XBC + SSD_CONV_CH].reshape(bs, ts, SSD_CONV_CH)[:, ts - (SSD_CONV - 1):]

    return (y_p, y_s.reshape(bs, ts, D_MODEL), gla_p[None], ssm_p[None], conv_p[None],
            mk, mv, gla_s[None], ssm_s[None], conv_s[None])
```

```python
import functools

import jax
import jax.numpy as jnp
from jax import lax
from jax.experimental import pallas as pl
from jax.experimental.pallas import tpu as pltpu

F32 = jnp.float32
BF16 = jnp.bfloat16

D_MODEL = 1024
GLA_HEADS = 4
GLA_DK = 64
GLA_DV = 128
GLA_KW = GLA_HEADS * GLA_DK
GLA_VW = GLA_HEADS * GLA_DV
GLA_GATE_RANK = 16
GLA_GATE_TAU = 16.0
SSD_INNER = 512
SSD_HEAD_DIM = 64
SSD_HEADS = 8
SSD_GROUPS = 2
SSD_STATE = 128
SSD_CONV = 4
SSD_CONV_CH = 1024
XA_HEADS = 4
XA_HEAD_DIM = 256
D_FF = 2816
EPS = 1e-6

COL_Q = 0
COL_K = 256
COL_V = 512
COL_G = 1024
COL_Z = 1536
COL_XBC = 2048
COL_SMALL = 3072
DT_LANE = GLA_GATE_RANK
PROJ_COLS = 3200

GLA_CHUNK = 64
TILE = 256
FFN_SUBTILES = 2
MIX_SEQS = 4
SAMPLE_SEQS = 16
ATTN_SEQS = 4
VMEM_LIMIT = 56 * 1024 * 1024


def _bf(x):
    return x.astype(BF16)


def _dot(a, b):
    return jnp.dot(a, b, preferred_element_type=F32)


def _dot_nt(a, b):
    return lax.dot_general(a, b, (((1,), (1,)), ((), ())), preferred_element_type=F32)


def _dot_tn(a, b):
    return lax.dot_general(a, b, (((0,), (0,)), ((), ())), preferred_element_type=F32)


def _dot3(m, a):
    hi = _bf(a)
    r = a - hi.astype(F32)
    mid = _bf(r)
    lo = _bf(r - mid.astype(F32))
    return _dot(m, hi) + _dot(m, mid) + _dot(m, lo)


def _rms(x, w):
    ms = jnp.mean(x * x, axis=-1, keepdims=True)
    return x * lax.rsqrt(ms + EPS) * w


def _silu(x):
    return x / (1.0 + jnp.exp(-x))


def _softplus(x):
    e = jnp.exp(-jnp.abs(x))
    u = 1.0 + e
    log1p_e = jnp.where(u == 1.0, e, jnp.log(u) * (e / (u - 1.0)))
    return jnp.maximum(x, 0.0) + log1p_e


def _log_sigmoid(x):
    return jnp.minimum(x, 0.0) - jnp.log(1.0 + jnp.exp(-jnp.abs(x)))


def _seg_masks(rows, seq_len):
    t = lax.broadcasted_iota(jnp.int32, (rows, rows), 0)
    u = lax.broadcasted_iota(jnp.int32, (rows, rows), 1)
    if seq_len == rows:
        same = None
        tril = u <= t
    else:
        shift = seq_len.bit_length() - 1
        assert 1 << shift == seq_len
        same = (t >> shift) == (u >> shift)
        tril = jnp.logical_and(same, u <= t)
    return same, tril


def _mask_to_bf16(mask, rows):
    if mask is None:
        return jnp.ones((rows, rows), BF16)
    return jnp.where(mask, 1.0, 0.0).astype(BF16)


def _row_in_seq(rows, seq_len, s, width):
    r = lax.broadcasted_iota(jnp.int32, (rows, width), 0)
    return jnp.logical_and(r >= s * seq_len, r < (s + 1) * seq_len)


def _interleave(chains, offset=0):
    waiting = list(enumerate(chains))
    active = []
    rnd = 0
    while waiting or active:
        while waiting and waiting[0][0] * offset <= rnd:
            active.append(waiting.pop(0)[1])
        alive = []
        for c in active:
            try:
                next(c)
                alive.append(c)
            except StopIteration:
                pass
        active = alive
        rnd += 1


def _run(phases):
    for _ in phases:
        pass


def _pad_rows(x, rows):
    if x.shape[0] >= rows:
        return x
    return jnp.concatenate([x, jnp.zeros((rows - x.shape[0], x.shape[1]), x.dtype)], axis=0)


def _gla_phases(q, k, v, lf, get_state, put_state, seq_len, outs):
    rows = q.shape[0]
    n_seq = rows // seq_len
    same, tril = _seg_masks(rows, seq_len)
    tril_b = _mask_to_bf16(tril, rows)
    same_b = _mask_to_bf16(same, rows)

    b = _dot3(tril_b, lf)
    if n_seq == 1:
        bl = jnp.broadcast_to(b[rows - 1:rows, :], b.shape)
    else:
        bl = _dot3(same_b, lf)
    qt = q * jnp.exp(b)
    kt = k * jnp.exp(-b)
    kd = k * jnp.exp(bl - b)
    bl_t = _pad_rows(bl, 128).T
    yield

    lane = lax.broadcasted_iota(jnp.int32, (rows, GLA_KW), 1)
    lhs = jnp.concatenate(
        [jnp.where((lane >= h * GLA_DK) & (lane < (h + 1) * GLA_DK), qt, 0.0)
         for h in range(GLA_HEADS)], axis=0).astype(BF16)
    att = _dot_nt(lhs, kt.astype(BF16))
    t4 = lax.broadcasted_iota(jnp.int32, (GLA_HEADS * rows, rows), 0) & (rows - 1)
    u4 = lax.broadcasted_iota(jnp.int32, (GLA_HEADS * rows, rows), 1)
    causal4 = u4 <= t4
    if n_seq > 1:
        shift = seq_len.bit_length() - 1
        causal4 = jnp.logical_and(causal4, (t4 >> shift) == (u4 >> shift))
    att = jnp.where(causal4, att, 0.0).astype(BF16)
    vb = v.astype(BF16)
    kdb = kd.astype(BF16)
    yield

    o_inter = None
    r4 = lax.broadcasted_iota(jnp.int32, (GLA_HEADS * rows, GLA_DV), 0) & (rows - 1)
    for s in range(n_seq):
        st = get_state(s)
        oi = _dot(lhs, st.astype(BF16))
        if n_seq == 1:
            o_inter = oi
            kd_s = kdb
        else:
            m4 = jnp.logical_and(r4 >= s * seq_len, r4 < (s + 1) * seq_len)
            oi = jnp.where(m4, oi, 0.0)
            o_inter = oi if o_inter is None else o_inter + oi
            kd_s = jnp.where(_row_in_seq(rows, seq_len, s, GLA_KW), kdb, jnp.zeros_like(kdb))
        blocks = []
        for hp in range(GLA_HEADS // 2):
            upd = _dot_tn(kd_s[:, 2 * hp * GLA_DK:2 * (hp + 1) * GLA_DK],
                          vb[:, 2 * hp * GLA_DV:2 * (hp + 1) * GLA_DV])
            blocks += [upd[:GLA_DK, :GLA_DV], upd[GLA_DK:, GLA_DV:]]
        upd_d = jnp.concatenate(blocks, axis=0)
        c0 = s * seq_len
        decay = jnp.exp(jnp.broadcast_to(bl_t[:, c0:c0 + 1], (GLA_KW, GLA_DV)))
        put_state(s, decay * st + upd_d)
        if n_seq > 1 and s % 4 == 3:
            yield
    yield

    for h in range(GLA_HEADS):
        o_h = _dot(att[h * rows:(h + 1) * rows], vb[:, h * GLA_DV:(h + 1) * GLA_DV])
        outs.append(o_h + o_inter[h * rows:(h + 1) * rows])


def _ssd_phases(xs, bm, cm, dt_raw_blk, dtb_blk, a_blk, d_row, get_state, put_state, seq_len, ys):
    rows = xs.shape[0]
    n_seq = rows // seq_len
    same, tril = _seg_masks(rows, seq_len)
    tril_b = _mask_to_bf16(tril, rows)
    same_b = _mask_to_bf16(same, rows)

    dt = _softplus(dt_raw_blk + dtb_blk)
    dta = dt * a_blk
    lc = _dot3(tril_b, dta)
    if n_seq == 1:
        ll = jnp.broadcast_to(lc[rows - 1:rows, :], lc.shape)
    else:
        ll = _dot3(same_b, dta)
    elc = jnp.exp(lc)
    w = jnp.exp(ll - lc) * dt
    ell = jnp.exp(ll)
    lc_t = _pad_rows(lc, 128).T[:, :rows]
    yield

    bmb = bm.astype(BF16)
    cmb = cm.astype(BF16)
    cb = [_dot_nt(cmb[:, g * SSD_STATE:(g + 1) * SSD_STATE],
                  bmb[:, g * SSD_STATE:(g + 1) * SSD_STATE]) for g in range(SSD_GROUPS)]
    yield

    lane128 = lax.broadcasted_iota(jnp.int32, (rows, 128), 1)
    lo128 = lane128 < SSD_HEAD_DIM
    sub128 = lax.broadcasted_iota(jnp.int32, (128, 128), 0) < SSD_HEAD_DIM

    def col(x, h, width):
        return jnp.broadcast_to(x[:, DT_LANE + h:DT_LANE + h + 1], (x.shape[0], width))

    heads_per_group = SSD_HEADS // SSD_GROUPS
    for j in range(SSD_HEADS // 2):
        g = (2 * j) // heads_per_group
        xs_p = xs[:, 128 * j:128 * (j + 1)]
        dt_p = jnp.where(lo128, col(dt, 2 * j, 128), col(dt, 2 * j + 1, 128))
        xdt = xs_p * dt_p
        x_lo = jnp.where(lo128, xdt, 0.0).astype(BF16)
        x_hi = jnp.where(lo128, 0.0, xdt).astype(BF16)
        y_p = None
        for h, x_half in ((2 * j, x_lo), (2 * j + 1, x_hi)):
            seg = col(lc, h, rows) - jnp.broadcast_to(lc_t[DT_LANE + h:DT_LANE + h + 1, :], (rows, rows))
            w_h = jnp.where(tril, cb[g] * jnp.exp(seg), 0.0).astype(BF16)
            y_h = _dot(w_h, x_half)
            y_p = y_h if y_p is None else y_p + y_h
            yield

        elc_p = jnp.where(lo128, col(elc, 2 * j, 128), col(elc, 2 * j + 1, 128))
        w_p = jnp.where(lo128, col(w, 2 * j, 128), col(w, 2 * j + 1, 128))
        xw = (xs_p * w_p).astype(BF16)
        bm_g = bmb[:, g * SSD_STATE:(g + 1) * SSD_STATE]
        cm_g = cmb[:, g * SSD_STATE:(g + 1) * SSD_STATE]
        y_inter = None
        for s in range(n_seq):
            st = get_state(s, j)
            ci = _dot_nt(cm_g, st.astype(BF16))
            if n_seq == 1:
                y_inter = ci
                xw_s = xw
            else:
                msk = _row_in_seq(rows, seq_len, s, 128)
                ci = jnp.where(msk, ci, 0.0)
                y_inter = ci if y_inter is None else y_inter + ci
                xw_s = jnp.where(msk, xw, jnp.zeros_like(xw))
            upd = _dot_tn(xw_s, bm_g)
            r0 = s * seq_len
            e0 = jnp.broadcast_to(ell[r0:r0 + 1, DT_LANE + 2 * j:DT_LANE + 2 * j + 1], (128, 128))
            e1 = jnp.broadcast_to(ell[r0:r0 + 1, DT_LANE + 2 * j + 1:DT_LANE + 2 * j + 2], (128, 128))
            put_state(s, j, jnp.where(sub128, e0, e1) * st + upd)
        ys.append(y_p + y_inter * elc_p + d_row[:, 128 * j:128 * (j + 1)] * xs_p)
        yield


def _gate_logf(small_blk, gw2_ref, gb_ref):
    logits = _dot(_bf(small_blk), gw2_ref[...]) + gb_ref[...]
    return _log_sigmoid(logits) * (1.0 / GLA_GATE_TAU)


def _gla_finish(o_h, g_h, gnw):
    return _rms(o_h, gnw) * _silu(g_h)


def _ssd_finish(y, z, snw):
    y = y * _silu(z)
    gs = SSD_INNER // SSD_GROUPS
    parts = [_rms(y[:, g * gs:(g + 1) * gs], snw[:, g * gs:(g + 1) * gs]) for g in range(SSD_GROUPS)]
    return jnp.concatenate(parts, axis=1)


def _memkv_kernel(mem_ref, nw_ref, wk_ref, wv_ref, k_ref, v_ref, kb_ref, vb_ref):
    mn = _bf(_rms(mem_ref[0], nw_ref[...]))
    mk = _dot(mn, wk_ref[...])
    mv = _dot(mn, wv_ref[...])
    for h in range(XA_HEADS):
        k_ref[0, 0, :, h, :] = mk[:, h * XA_HEAD_DIM:(h + 1) * XA_HEAD_DIM]
        v_ref[0, 0, :, h, :] = mv[:, h * XA_HEAD_DIM:(h + 1) * XA_HEAD_DIM]
    kb_ref[0] = _bf(mk)
    vb_ref[0] = _bf(mv)


def _const_spec(shape):
    nd = len(shape)
    return pl.BlockSpec(shape, lambda *_: (0,) * nd, pipeline_mode=pl.Buffered(1))


def _memkv(mem, nw, wk, wv):
    b, m, d = mem.shape
    blk = pl.BlockSpec((1, m, d), lambda i: (i, 0, 0))
    cache_blk = pl.BlockSpec((1, 1, m, XA_HEADS, XA_HEAD_DIM), lambda i: (0, i, 0, 0, 0))
    cache_shape = jax.ShapeDtypeStruct((1, b, m, XA_HEADS, XA_HEAD_DIM), F32)
    return pl.pallas_call(
        _memkv_kernel,
        grid=(b,),
        in_specs=[blk, _const_spec((1, d)), _const_spec((d, d)), _const_spec((d, d))],
        out_specs=[cache_blk, cache_blk, blk, blk],
        out_shape=[cache_shape] * 2 + [jax.ShapeDtypeStruct((b, m, d), BF16)] * 2,
        compiler_params=pltpu.CompilerParams(dimension_semantics=("arbitrary",),
                                             vmem_limit_bytes=VMEM_LIMIT),
        name="mem_kv",
    )(mem, nw, wk, wv)


PROJ_CHUNK = 512


def _mixer_prompt_chain(x_ref, h_ref, proj_s, xpad_s, mixed_s, sg_s, ss_s,
                        lnpre_ref, win_ref, gw2_ref, gb_ref, gnw_ref, cw_ref, cb_ref,
                        dtb_ref, a_ref, d_ref, snw_ref, wout_ref, lnpost_ref):
    rows = TILE
    x = x_ref[...]
    hn = _rms(x, lnpre_ref[...]).astype(BF16)
    yield

    def proj_cols(c0, width):
        proj_s[:, c0:c0 + width] = _dot(hn, win_ref[:, c0:c0 + width])

    def late_proj():
        for c0 in (COL_Z, COL_Q, COL_V, COL_G):
            proj_cols(c0, PROJ_CHUNK)
            yield

    side = [late_proj()]

    def tick():
        if side[0] is not None:
            try:
                next(side[0])
            except StopIteration:
                side[0] = None

    for c in range(SSD_CONV_CH // PROJ_CHUNK):
        proj_cols(COL_XBC + c * PROJ_CHUNK, PROJ_CHUNK)
        yield
    proj_cols(COL_SMALL, 128)

    xa = []
    for c in range(SSD_CONV_CH // PROJ_CHUNK):
        cs = slice(c * PROJ_CHUNK, (c + 1) * PROJ_CHUNK)
        xpad_s[8:8 + rows, cs] = proj_s[:, COL_XBC + c * PROJ_CHUNK:COL_XBC + (c + 1) * PROJ_CHUNK]
        conv = cb_ref[:, cs]
        for j in range(SSD_CONV):
            conv = conv + xpad_s[5 + j:5 + j + rows, cs] * cw_ref[j:j + 1, cs]
        xpad_s[5:8, cs] = xpad_s[5 + rows:8 + rows, cs]
        xa.append(_silu(conv))
        tick()
        yield

    small = proj_s[:, COL_SMALL:COL_SMALL + 128]
    logf = _gate_logf(small, gw2_ref, gb_ref)

    def get_ss(s, j):
        return ss_s[128 * j:128 * (j + 1), :]

    def put_ss(s, j, val):
        ss_s[128 * j:128 * (j + 1), :] = val

    ys = []
    ssd = _ssd_phases(xa[0], xa[1][:, :SSD_GROUPS * SSD_STATE], xa[1][:, SSD_GROUPS * SSD_STATE:],
                      small, dtb_ref[...], a_ref[...], d_ref[...], get_ss, put_ss, rows, ys)
    for i, _ in enumerate(ssd):
        if i % 2 == 0:
            tick()
        yield
    while side[0] is not None:
        tick()
        yield
    mixed_s[:, GLA_VW:] = _ssd_finish(jnp.concatenate(ys, axis=1), proj_s[:, COL_Z:COL_Z + SSD_INNER],
                                      snw_ref[...])
    yield

    def get_sg(s):
        return sg_s[...]

    def put_sg(s, val):
        sg_s[...] = val

    for c in range(rows // GLA_CHUNK):
        r0 = c * GLA_CHUNK
        sl = slice(r0, r0 + GLA_CHUNK)
        outs = []
        yield from _gla_phases(proj_s[sl, COL_Q:COL_Q + GLA_KW], proj_s[sl, COL_K:COL_K + GLA_KW],
                               proj_s[sl, COL_V:COL_V + GLA_VW], logf[sl], get_sg, put_sg, GLA_CHUNK, outs)
        for h in range(GLA_HEADS):
            g_h = proj_s[sl, COL_G + h * GLA_DV:COL_G + (h + 1) * GLA_DV]
            mixed_s[sl, h * GLA_DV:(h + 1) * GLA_DV] = _gla_finish(outs[h], g_h, gnw_ref[...])
        yield

    mixed = mixed_s[...].astype(BF16)
    ms = []
    for c in range(D_MODEL // PROJ_CHUNK):
        ms.append(_dot(mixed, wout_ref[:, c * PROJ_CHUNK:(c + 1) * PROJ_CHUNK]))
        yield
    h_ref[...] = x + _rms(jnp.concatenate(ms, axis=1), lnpost_ref[...])


N_MIXER_WEIGHTS = 13
N_MIXER_SCRATCH = 5
MIX_OFFSET = 3


def _mixer_prompt_kernel(x_ref, *rest):
    consts = rest[:N_MIXER_WEIGHTS]
    h_ref, sg_out, ss_out, conv_out = rest[N_MIXER_WEIGHTS:N_MIXER_WEIGHTS + 4]
    scratch = rest[N_MIXER_WEIGHTS + 4:]
    per_seq = [scratch[i * N_MIXER_SCRATCH:(i + 1) * N_MIXER_SCRATCH] for i in range(MIX_SEQS)]
    t = pl.program_id(1)

    @pl.when(t == 0)
    def _():
        for _, xpad_s, _, sg_s, ss_s in per_seq:
            sg_s[...] = jnp.zeros_like(sg_s)
            ss_s[...] = jnp.zeros_like(ss_s)
            xpad_s[0:8, :] = jnp.zeros((8, SSD_CONV_CH), F32)

    _interleave([_mixer_prompt_chain(x_ref.at[i], h_ref.at[i], *per_seq[i], *consts) for i in range(MIX_SEQS)],
                offset=MIX_OFFSET)

    @pl.when(t == pl.num_programs(1) - 1)
    def _():
        for i, (_, xpad_s, _, sg_s, ss_s) in enumerate(per_seq):
            sg_out[i] = sg_s[...].reshape(GLA_HEADS, GLA_DK, GLA_DV)
            ss_out[i] = ss_s[...].reshape(SSD_HEADS, SSD_HEAD_DIM, SSD_STATE)
            conv_out[i] = xpad_s[5:8, :]


def _mixer_prompt(x, p):
    b, t, d = x.shape
    nt = t // TILE
    consts = [p["ln_mix_pre"], p["w_in"], p["gw2"], p["gb"], p["gnw"], p["conv_w"], p["conv_b"],
              p["dtb"], p["a_blk"], p["d_row"], p["snw"], p["w_out"], p["ln_mix_post"]]
    assert len(consts) == N_MIXER_WEIGHTS
    tok = pl.BlockSpec((MIX_SEQS, TILE, d), lambda i, j: (i, j, 0))
    return pl.pallas_call(
        _mixer_prompt_kernel,
        grid=(b // MIX_SEQS, nt),
        in_specs=[tok] + [_const_spec(c.shape) for c in consts],
        out_specs=[tok,
                   pl.BlockSpec((MIX_SEQS, GLA_HEADS, GLA_DK, GLA_DV), lambda i, j: (i, 0, 0, 0)),
                   pl.BlockSpec((MIX_SEQS, SSD_HEADS, SSD_HEAD_DIM, SSD_STATE), lambda i, j: (i, 0, 0, 0)),
                   pl.BlockSpec((MIX_SEQS, SSD_CONV - 1, SSD_CONV_CH), lambda i, j: (i, 0, 0))],
        out_shape=[jax.ShapeDtypeStruct((b, t, d), F32),
                   jax.ShapeDtypeStruct((b, GLA_HEADS, GLA_DK, GLA_DV), F32),
                   jax.ShapeDtypeStruct((b, SSD_HEADS, SSD_HEAD_DIM, SSD_STATE), F32),
                   jax.ShapeDtypeStruct((b, SSD_CONV - 1, SSD_CONV_CH), F32)],
        scratch_shapes=[pltpu.VMEM((TILE, PROJ_COLS), F32),
                        pltpu.VMEM((TILE + 8, SSD_CONV_CH), F32),
                        pltpu.VMEM((TILE, D_MODEL), F32),
                        pltpu.VMEM((GLA_KW, GLA_DV), F32),
                        pltpu.VMEM((SSD_INNER, SSD_STATE), F32)] * MIX_SEQS,
        compiler_params=pltpu.CompilerParams(dimension_semantics=("arbitrary", "arbitrary"),
                                             vmem_limit_bytes=VMEM_LIMIT),
        name="mixer_prompt",
    )(x, *consts)


def _softmax_rows(s):
    e = jnp.exp(s - jnp.max(s, axis=-1, keepdims=True))
    return e / jnp.sum(e, axis=-1, keepdims=True)


def _ffn(h, lnpre, lnpost, wg_ref, wu_ref, wd_ref):
    hf = _bf(_rms(h, lnpre))
    gt = _dot(hf, wg_ref[...])
    up = _dot(hf, wu_ref[...])
    f = _dot(_bf(_silu(gt) * up), wd_ref[...])
    return h + _rms(f, lnpost)


FFN_CHUNK = 256
FFN_OFFSET = 2


def _attn_ffn_chain(h_ref, y_ref, mk_ref, mv_ref, lnxa_ref, wq_ref, wo_ref, lnxap_ref,
                    lnf_ref, lnfp_ref, wg_ref, wu_ref, wd_ref):
    h = h_ref[...]
    hn = _rms(h, lnxa_ref[...]).astype(BF16)
    yield
    heads_per_chunk = PROJ_CHUNK // XA_HEAD_DIM
    outs = []
    for c in range(D_MODEL // PROJ_CHUNK):
        q = _dot(hn, wq_ref[:, c * PROJ_CHUNK:(c + 1) * PROJ_CHUNK]).astype(BF16)
        yield
        for hh in range(heads_per_chunk):
            hd = c * heads_per_chunk + hh
            sl = slice(hd * XA_HEAD_DIM, (hd + 1) * XA_HEAD_DIM)
            s = _dot_nt(q[:, hh * XA_HEAD_DIM:(hh + 1) * XA_HEAD_DIM], mk_ref[0, :, sl])
            yield
            pr = _softmax_rows(s).astype(BF16)
            yield
            outs.append(_dot(pr, mv_ref[0, :, sl]).astype(BF16))
    a = jnp.concatenate(outs, axis=1)
    yield
    parts = []
    for c in range(D_MODEL // PROJ_CHUNK):
        parts.append(_dot(a, wo_ref[:, c * PROJ_CHUNK:(c + 1) * PROJ_CHUNK]))
        yield
    h = h + _rms(jnp.concatenate(parts, axis=1), lnxap_ref[...])
    hf = _rms(h, lnf_ref[...]).astype(BF16)
    yield
    acts = []
    for c in range(D_FF // FFN_CHUNK):
        cs = slice(c * FFN_CHUNK, (c + 1) * FFN_CHUNK)
        gt = _dot(hf, wg_ref[:, cs])
        up = _dot(hf, wu_ref[:, cs])
        yield
        acts.append((_silu(gt) * up).astype(BF16))
    act = jnp.concatenate(acts, axis=1)
    yield
    parts = []
    for c in range(D_MODEL // PROJ_CHUNK):
        parts.append(_dot(act, wd_ref[:, c * PROJ_CHUNK:(c + 1) * PROJ_CHUNK]))
        yield
    y_ref[...] = h + _rms(jnp.concatenate(parts, axis=1), lnfp_ref[...])


def _attn_ffn_prompt_kernel(h_ref, mk_ref, mv_ref, *rest):
    consts, y_ref = rest[:-1], rest[-1]
    chains = []
    for sub in range(FFN_SUBTILES):
        rs = pl.ds(sub * TILE, TILE)
        chains.append(_attn_ffn_chain(h_ref.at[0, rs], y_ref.at[0, rs], mk_ref, mv_ref, *consts))
    _interleave(chains, offset=FFN_OFFSET)


def _attn_ffn_prompt(h, mkb, mvb, p):
    b, t, d = h.shape
    rows = FFN_SUBTILES * TILE
    consts = [p["ln_xa_pre"], p["w_xq"], p["w_xo"], p["ln_xa_post"], p["ln_ffn_pre"], p["ln_ffn_post"],
              p["w_gate"], p["w_up"], p["w_down"]]
    tok = pl.BlockSpec((1, rows, d), lambda i, j: (i, j, 0))
    mem = pl.BlockSpec((1, mkb.shape[1], d), lambda i, j: (i, 0, 0))
    return pl.pallas_call(
        _attn_ffn_prompt_kernel,
        grid=(b, t // rows),
        in_specs=[tok, mem, mem] + [_const_spec(c.shape) for c in consts],
        out_specs=tok,
        out_shape=jax.ShapeDtypeStruct((b, t, d), F32),
        compiler_params=pltpu.CompilerParams(dimension_semantics=("arbitrary", "arbitrary"),
                                             vmem_limit_bytes=VMEM_LIMIT),
        name="attn_ffn_prompt",
    )(h, mkb, mvb, *consts)


def _proj_sample_kernel(x_ref, lnpre_ref, win_ref, proj_ref):
    proj_ref[...] = _dot(_bf(_rms(x_ref[...], lnpre_ref[...])), win_ref[...])


def _proj_sample(x, p):
    n, d = x.shape
    return pl.pallas_call(
        _proj_sample_kernel,
        grid=(1,),
        in_specs=[_const_spec((n, d)), _const_spec((1, d)), _const_spec(p["w_in"].shape)],
        out_specs=pl.BlockSpec((n, PROJ_COLS), lambda i: (0, 0)),
        out_shape=jax.ShapeDtypeStruct((n, PROJ_COLS), F32),
        compiler_params=pltpu.CompilerParams(dimension_semantics=("arbitrary",),
                                             vmem_limit_bytes=VMEM_LIMIT),
        name="proj_sample",
    )(x, p["ln_mix_pre"], p["w_in"])


def _mixer_sample_kernel(seq_len, proj_ref, tail_ref, sg_ref, ss_ref, gw2_ref, gb_ref, gnw_ref,
                         cw_ref, cb_ref, dtb_ref, a_ref, d_ref, snw_ref,
                         mixed_ref, sg_out, ss_out, xpad_s, tpad_s):
    rows = proj_ref.shape[0]
    n_seq = rows // seq_len
    small = proj_ref[:, COL_SMALL:COL_SMALL + 128]
    logf = _gate_logf(small, gw2_ref, gb_ref)

    xbc = proj_ref[:, COL_XBC:COL_XBC + SSD_CONV_CH]
    xpad_s[0:8, :] = jnp.zeros((8, SSD_CONV_CH), F32)
    xpad_s[8:8 + rows, :] = xbc
    tpad_s[0:rows, :] = tail_ref[...]
    tpad_s[rows:rows + 8, :] = jnp.zeros((8, SSD_CONV_CH), F32)
    pos = lax.broadcasted_iota(jnp.int32, (rows, SSD_CONV_CH), 0) & (seq_len - 1)
    conv = cb_ref[...]
    for j in range(SSD_CONV - 1):
        back = SSD_CONV - 1 - j
        prev = jnp.where(pos >= back, xpad_s[8 - back:8 - back + rows, :],
                         tpad_s[seq_len - back:seq_len - back + rows, :])
        conv = conv + prev * cw_ref[j:j + 1, :]
    conv = conv + xbc * cw_ref[SSD_CONV - 1:SSD_CONV, :]
    xa = _silu(conv)

    def get_ss(s, j):
        return ss_ref[s, 2 * j:2 * j + 2].reshape(2 * SSD_HEAD_DIM, SSD_STATE)

    def put_ss(s, j, val):
        ss_out[s, 2 * j:2 * j + 2] = val.reshape(2, SSD_HEAD_DIM, SSD_STATE)

    ys = []
    _run(_ssd_phases(xa[:, :SSD_INNER], xa[:, SSD_INNER:SSD_INNER + 256], xa[:, SSD_INNER + 256:],
                     small, dtb_ref[...], a_ref[...], d_ref[...], get_ss, put_ss, seq_len, ys))
    mixed_ref[:, GLA_VW:] = _ssd_finish(jnp.concatenate(ys, axis=1), proj_ref[:, COL_Z:COL_Z + SSD_INNER],
                                        snw_ref[...])

    def get_sg(s):
        return sg_ref[s].reshape(GLA_KW, GLA_DV)

    def put_sg(s, val):
        sg_out[s] = val.reshape(GLA_HEADS, GLA_DK, GLA_DV)

    outs = []
    _run(_gla_phases(proj_ref[:, COL_Q:COL_Q + GLA_KW], proj_ref[:, COL_K:COL_K + GLA_KW],
                     proj_ref[:, COL_V:COL_V + GLA_VW], logf, get_sg, put_sg, seq_len, outs))
    for h in range(GLA_HEADS):
        g_h = proj_ref[:, COL_G + h * GLA_DV:COL_G + (h + 1) * GLA_DV]
        mixed_ref[:, h * GLA_DV:(h + 1) * GLA_DV] = _gla_finish(outs[h], g_h, gnw_ref[...])


def _mixer_sample(proj, tail, sg, ss, p, seq_len):
    n = proj.shape[0]
    rows = SAMPLE_SEQS * seq_len
    nb = sg.shape[0]
    consts = [p["gw2"], p["gb"], p["gnw"], p["conv_w"], p["conv_b"], p["dtb"], p["a_blk"], p["d_row"],
              p["snw"]]
    sg_spec = pl.BlockSpec((SAMPLE_SEQS, GLA_HEADS, GLA_DK, GLA_DV), lambda i: (i, 0, 0, 0))
    ss_spec = pl.BlockSpec((SAMPLE_SEQS, SSD_HEADS, SSD_HEAD_DIM, SSD_STATE), lambda i: (i, 0, 0, 0))
    return pl.pallas_call(
        functools.partial(_mixer_sample_kernel, seq_len),
        grid=(nb // SAMPLE_SEQS,),
        in_specs=[pl.BlockSpec((rows, PROJ_COLS), lambda i: (i, 0)),
                  pl.BlockSpec((rows, SSD_CONV_CH), lambda i: (i, 0)),
                  sg_spec, ss_spec] + [_const_spec(c.shape) for c in consts],
        out_specs=[pl.BlockSpec((rows, D_MODEL), lambda i: (i, 0)), sg_spec, ss_spec],
        out_shape=[jax.ShapeDtypeStruct((n, D_MODEL), F32),
                   jax.ShapeDtypeStruct(sg.shape, F32), jax.ShapeDtypeStruct(ss.shape, F32)],
        scratch_shapes=[pltpu.VMEM((rows + 8, SSD_CONV_CH), F32),
                        pltpu.VMEM((rows + 8, SSD_CONV_CH), F32)],
        compiler_params=pltpu.CompilerParams(dimension_semantics=("arbitrary",),
                                             vmem_limit_bytes=VMEM_LIMIT),
        name="mixer_sample",
    )(proj, tail, sg, ss, *consts)


def _post_mix_sample_kernel(x_ref, mixed_ref, wout_ref, lnpost_ref, lnxa_ref, wq_ref, h_ref, q_ref):
    m = _dot(_bf(mixed_ref[...]), wout_ref[...])
    h = x_ref[...] + _rms(m, lnpost_ref[...])
    h_ref[...] = h
    q_ref[...] = _bf(_dot(_bf(_rms(h, lnxa_ref[...])), wq_ref[...]))


def _post_mix_sample(x, mixed, p):
    n, d = x.shape
    full = pl.BlockSpec((n, d), lambda i: (0, 0))
    consts = [p["w_out"], p["ln_mix_post"], p["ln_xa_pre"], p["w_xq"]]
    return pl.pallas_call(
        _post_mix_sample_kernel,
        grid=(1,),
        in_specs=[full, full] + [_const_spec(c.shape) for c in consts],
        out_specs=[full, full],
        out_shape=[jax.ShapeDtypeStruct((n, d), F32), jax.ShapeDtypeStruct((n, d), BF16)],
        compiler_params=pltpu.CompilerParams(dimension_semantics=("arbitrary",),
                                             vmem_limit_bytes=VMEM_LIMIT),
        name="post_mix_sample",
    )(x, mixed, *consts)


def _attn_sample_kernel(seq_len, q_ref, k_hbm, v_hbm, a_ref, kbuf, vbuf, sem):
    step = pl.program_id(0)
    rows = q_ref.shape[0]
    n_seq = rows // seq_len

    def copies(at_step, slot):
        out = []
        for s in range(n_seq):
            for hd in range(XA_HEADS):
                b = at_step * n_seq + s
                out.append(pltpu.make_async_copy(k_hbm.at[0, b, :, hd, :], kbuf.at[slot, s, hd], sem.at[0, slot]))
                out.append(pltpu.make_async_copy(v_hbm.at[0, b, :, hd, :], vbuf.at[slot, s, hd], sem.at[1, slot]))
        return out

    @pl.when(step == 0)
    def _():
        for cp in copies(0, 0):
            cp.start()

    @pl.when(step + 1 < pl.num_programs(0))
    def _():
        for cp in copies(step + 1, (step + 1) % 2):
            cp.start()

    slot = step % 2
    for cp in copies(step, slot):
        cp.wait()

    q = q_ref[...]
    results = {}

    def one(s, hd):
        sc = _dot_nt(q[:, hd * XA_HEAD_DIM:(hd + 1) * XA_HEAD_DIM], _bf(kbuf[slot, s, hd]))
        yield
        e = jnp.exp(sc - jnp.max(sc, axis=-1, keepdims=True))
        pr = _bf(e / jnp.sum(e, axis=-1, keepdims=True))
        yield
        results[s, hd] = _dot(pr, _bf(vbuf[slot, s, hd]))

    _interleave([one(s, hd) for s in range(n_seq) for hd in range(XA_HEADS)])
    for hd in range(XA_HEADS):
        sl = slice(hd * XA_HEAD_DIM, (hd + 1) * XA_HEAD_DIM)
        out = results[0, hd]
        for s in range(1, n_seq):
            out = jnp.where(_row_in_seq(rows, seq_len, s, XA_HEAD_DIM), results[s, hd], out)
        a_ref[:, sl] = out


def _attn_sample(q, ck, cv, seq_len):
    n, d = q.shape
    _, nb, m, nh, hd = ck.shape
    rows = ATTN_SEQS * seq_len
    tok = pl.BlockSpec((rows, d), lambda i: (i, 0))
    hbm = pl.BlockSpec(memory_space=pl.ANY)
    return pl.pallas_call(
        functools.partial(_attn_sample_kernel, seq_len),
        grid=(nb // ATTN_SEQS,),
        in_specs=[tok, hbm, hbm],
        out_specs=tok,
        out_shape=jax.ShapeDtypeStruct((n, d), F32),
        scratch_shapes=[pltpu.VMEM((2, ATTN_SEQS, nh, m, hd), F32),
                        pltpu.VMEM((2, ATTN_SEQS, nh, m, hd), F32),
                        pltpu.SemaphoreType.DMA((2, 2))],
        compiler_params=pltpu.CompilerParams(dimension_semantics=("arbitrary",),
                                             vmem_limit_bytes=VMEM_LIMIT),
        name="attn_sample",
    )(q, ck, cv)


def _post_attn_sample_kernel(h_ref, a_ref, wo_ref, lnxap_ref, lnf_ref, lnfp_ref, wg_ref, wu_ref, wd_ref,
                             y_ref):
    a = _dot(_bf(a_ref[...]), wo_ref[...])
    h = h_ref[...] + _rms(a, lnxap_ref[...])
    y_ref[...] = _ffn(h, lnf_ref[...], lnfp_ref[...], wg_ref, wu_ref, wd_ref)


def _post_attn_sample(h, a, p):
    n, d = h.shape
    full = pl.BlockSpec((n, d), lambda i: (0, 0))
    consts = [p["w_xo"], p["ln_xa_post"], p["ln_ffn_pre"], p["ln_ffn_post"], p["w_gate"], p["w_up"],
              p["w_down"]]
    return pl.pallas_call(
        _post_attn_sample_kernel,
        grid=(1,),
        in_specs=[full, full] + [_const_spec(c.shape) for c in consts],
        out_specs=full,
        out_shape=jax.ShapeDtypeStruct((n, d), F32),
        compiler_params=pltpu.CompilerParams(dimension_semantics=("arbitrary",),
                                             vmem_limit_bytes=VMEM_LIMIT),
        name="post_attn_sample",
    )(h, a, *consts)


def _pack_params(ln_mix_pre, ln_mix_post, w_in, gla_gate_w2, gla_gate_b, gla_norm_w, ssd_conv_w,
                 ssd_conv_b, ssd_dt_bias, ssd_A_log, ssd_D, ssd_norm_w, w_out, ln_xa_pre, ln_xa_post,
                 mem_norm_w, w_xq, w_xk, w_xv, w_xo, ln_ffn_pre, ln_ffn_post, w_gate, w_up, w_down):
    sizes = (GLA_KW, GLA_KW, GLA_VW, GLA_VW, GLA_GATE_RANK, SSD_INNER, SSD_CONV_CH, SSD_HEADS)
    offs = [0]
    for sz in sizes:
        offs.append(offs[-1] + sz)
    wq, wk, wv, wg, wglr, wz, wxbc, wdt = [w_in[:, offs[i]:offs[i + 1]] for i in range(len(sizes))]
    pad = jnp.zeros((D_MODEL, PROJ_COLS - COL_SMALL - GLA_GATE_RANK - SSD_HEADS), F32)
    w_in_p = jnp.concatenate([wq * (GLA_DK ** -0.5), wk, wv, wg, wz, wxbc, wglr, wdt, pad], axis=1)

    def small_row(v):
        return jnp.zeros((1, 128), F32).at[0, DT_LANE:DT_LANE + SSD_HEADS].set(v)

    row = lambda v: v.reshape(1, -1)
    return {
        "ln_mix_pre": row(ln_mix_pre), "ln_mix_post": row(ln_mix_post),
        "w_in": _bf(w_in_p),
        "gw2": _bf(jnp.zeros((128, GLA_KW), F32).at[:GLA_GATE_RANK].set(gla_gate_w2)),
        "gb": row(gla_gate_b), "gnw": row(gla_norm_w),
        "conv_w": ssd_conv_w, "conv_b": row(ssd_conv_b),
        "dtb": small_row(ssd_dt_bias),
        "a_blk": small_row(-jnp.exp(ssd_A_log)),
        "d_row": jnp.repeat(ssd_D, SSD_HEAD_DIM).reshape(1, SSD_INNER),
        "snw": row(ssd_norm_w),
        "w_out": _bf(w_out),
        "ln_xa_pre": row(ln_xa_pre), "ln_xa_post": row(ln_xa_post), "mem_norm_w": row(mem_norm_w),
        "w_xq": _bf(w_xq * (XA_HEAD_DIM ** -0.5)), "w_xk": _bf(w_xk), "w_xv": _bf(w_xv), "w_xo": _bf(w_xo),
        "ln_ffn_pre": row(ln_ffn_pre), "ln_ffn_post": row(ln_ffn_post),
        "w_gate": _bf(w_gate), "w_up": _bf(w_up), "w_down": _bf(w_down),
    }


def kernel(x_prompt, x_sample, mem_prompt, state_gla, state_ssm, state_conv, cache_mem_k, cache_mem_v,
           ln_mix_pre, ln_mix_post, w_in, gla_gate_w2, gla_gate_b, gla_norm_w, ssd_conv_w, ssd_conv_b,
           ssd_dt_bias, ssd_A_log, ssd_D, ssd_norm_w, w_out, ln_xa_pre, ln_xa_post, mem_norm_w,
           w_xq, w_xk, w_xv, w_xo, ln_ffn_pre, ln_ffn_post, w_gate, w_up, w_down):
    assert w_in.shape[0] == 1, "single-layer kernel"
    layer = [a[0] for a in (ln_mix_pre, ln_mix_post, w_in, gla_gate_w2, gla_gate_b, gla_norm_w, ssd_conv_w,
                            ssd_conv_b, ssd_dt_bias, ssd_A_log, ssd_D, ssd_norm_w, w_out, ln_xa_pre,
                            ln_xa_post, mem_norm_w, w_xq, w_xk, w_xv, w_xo, ln_ffn_pre, ln_ffn_post,
                            w_gate, w_up, w_down)]
    p = _pack_params(*layer)
    bp, mem_len, _ = mem_prompt.shape
    bs, ts, _ = x_sample.shape
    assert ts == SSD_CONV, "the padded conv-tail layout assumes one new row per conv tap"

    mk, mv, mkb, mvb = _memkv(mem_prompt, p["mem_norm_w"], p["w_xk"], p["w_xv"])
    h_p, gla_p, ssm_p, conv_p = _mixer_prompt(x_prompt, p)
    y_p = _attn_ffn_prompt(h_p, mkb, mvb, p)

    xs = x_sample.reshape(bs * ts, D_MODEL)
    proj = _proj_sample(xs, p)
    tail = jnp.pad(state_conv[0], ((0, 0), (1, 0), (0, 0))).reshape(bs * ts, SSD_CONV_CH)
    mixed, gla_s, ssm_s = _mixer_sample(proj, tail, state_gla[0], state_ssm[0], p, ts)
    h_s, q_s = _post_mix_sample(xs, mixed, p)
    a_s = _attn_sample(q_s, cache_mem_k, cache_mem_v, ts)
    y_s = _post_attn_sample(h_s, a_s, p)
    conv_s = proj[:, COL_XBC:COL_XBC + SSD_CONV_CH].reshape(bs, ts, SSD_CONV_CH)[:, ts - (SSD_CONV - 1):]

    return (y_p, y_s.reshape(bs, ts, D_MODEL), gla_p[None], ssm_p[None], conv_p[None],
            mk, mv, gla_s[None], ssm_s[None], conv_s[None])
```

```python
import functools

import jax
import jax.numpy as jnp
from jax import lax
from jax.experimental import pallas as pl
from jax.experimental.pallas import tpu as pltpu

F32 = jnp.float32
BF16 = jnp.bfloat16

D_MODEL = 1024
GLA_HEADS = 4
GLA_DK = 64
GLA_DV = 128
GLA_KW = GLA_HEADS * GLA_DK
GLA_VW = GLA_HEADS * GLA_DV
GLA_GATE_RANK = 16
GLA_GATE_TAU = 16.0
SSD_INNER = 512
SSD_HEAD_DIM = 64
SSD_HEADS = 8
SSD_GROUPS = 2
SSD_STATE = 128
SSD_CONV = 4
SSD_CONV_CH = 1024
XA_HEADS = 4
XA_HEAD_DIM = 256
D_FF = 2816
EPS = 1e-6

COL_Q = 0
COL_K = 256
COL_V = 512
COL_G = 1024
COL_Z = 1536
COL_XBC = 2048
COL_SMALL = 3072
DT_LANE = GLA_GATE_RANK
PROJ_COLS = 3200

GLA_CHUNK = 64
TILE = 256
FFN_SUBTILES = 4
MIX_SEQS = 4
SAMPLE_SEQS = 16
ATTN_SEQS = 4
VMEM_LIMIT = 56 * 1024 * 1024


def _bf(x):
    return x.astype(BF16)


def _dot(a, b):
    return jnp.dot(a, b, preferred_element_type=F32)


def _dot_nt(a, b):
    return lax.dot_general(a, b, (((1,), (1,)), ((), ())), preferred_element_type=F32)


def _dot_tn(a, b):
    return lax.dot_general(a, b, (((0,), (0,)), ((), ())), preferred_element_type=F32)


def _dot3(m, a):
    hi = _bf(a)
    r = a - hi.astype(F32)
    mid = _bf(r)
    lo = _bf(r - mid.astype(F32))
    return _dot(m, hi) + _dot(m, mid) + _dot(m, lo)


def _rms(x, w):
    ms = jnp.mean(x * x, axis=-1, keepdims=True)
    return x * lax.rsqrt(ms + EPS) * w


def _silu(x):
    return x / (1.0 + jnp.exp(-x))


def _softplus(x):
    e = jnp.exp(-jnp.abs(x))
    u = 1.0 + e
    log1p_e = jnp.where(u == 1.0, e, jnp.log(u) * (e / (u - 1.0)))
    return jnp.maximum(x, 0.0) + log1p_e


def _log_sigmoid(x):
    return jnp.minimum(x, 0.0) - jnp.log(1.0 + jnp.exp(-jnp.abs(x)))


def _seg_masks(rows, seq_len):
    t = lax.broadcasted_iota(jnp.int32, (rows, rows), 0)
    u = lax.broadcasted_iota(jnp.int32, (rows, rows), 1)
    if seq_len == rows:
        same = None
        tril = u <= t
    else:
        shift = seq_len.bit_length() - 1
        assert 1 << shift == seq_len
        same = (t >> shift) == (u >> shift)
        tril = jnp.logical_and(same, u <= t)
    return same, tril


def _mask_to_bf16(mask, rows):
    if mask is None:
        return jnp.ones((rows, rows), BF16)
    return jnp.where(mask, 1.0, 0.0).astype(BF16)


def _row_in_seq(rows, seq_len, s, width):
    r = lax.broadcasted_iota(jnp.int32, (rows, width), 0)
    return jnp.logical_and(r >= s * seq_len, r < (s + 1) * seq_len)


def _interleave(chains, offset=0):
    waiting = list(enumerate(chains))
    active = []
    rnd = 0
    while waiting or active:
        while waiting and waiting[0][0] * offset <= rnd:
            active.append(waiting.pop(0)[1])
        alive = []
        for c in active:
            try:
                next(c)
                alive.append(c)
            except StopIteration:
                pass
        active = alive
        rnd += 1


def _run(phases):
    for _ in phases:
        pass


def _pad_rows(x, rows):
    if x.shape[0] >= rows:
        return x
    return jnp.concatenate([x, jnp.zeros((rows - x.shape[0], x.shape[1]), x.dtype)], axis=0)


def _gla_phases(q, k, v, lf, get_state, put_state, seq_len, outs):
    rows = q.shape[0]
    n_seq = rows // seq_len
    same, tril = _seg_masks(rows, seq_len)
    tril_b = _mask_to_bf16(tril, rows)
    same_b = _mask_to_bf16(same, rows)

    b = _dot3(tril_b, lf)
    if n_seq == 1:
        bl = jnp.broadcast_to(b[rows - 1:rows, :], b.shape)
    else:
        bl = _dot3(same_b, lf)
    qt = q * jnp.exp(b)
    kt = k * jnp.exp(-b)
    kd = k * jnp.exp(bl - b)
    bl_t = _pad_rows(bl, 128).T
    yield

    lane = lax.broadcasted_iota(jnp.int32, (rows, GLA_KW), 1)
    lhs = jnp.concatenate(
        [jnp.where((lane >= h * GLA_DK) & (lane < (h + 1) * GLA_DK), qt, 0.0)
         for h in range(GLA_HEADS)], axis=0).astype(BF16)
    att = _dot_nt(lhs, kt.astype(BF16))
    t4 = lax.broadcasted_iota(jnp.int32, (GLA_HEADS * rows, rows), 0) & (rows - 1)
    u4 = lax.broadcasted_iota(jnp.int32, (GLA_HEADS * rows, rows), 1)
    causal4 = u4 <= t4
    if n_seq > 1:
        shift = seq_len.bit_length() - 1
        causal4 = jnp.logical_and(causal4, (t4 >> shift) == (u4 >> shift))
    att = jnp.where(causal4, att, 0.0).astype(BF16)
    vb = v.astype(BF16)
    kdb = kd.astype(BF16)
    yield

    o_inter = None
    r4 = lax.broadcasted_iota(jnp.int32, (GLA_HEADS * rows, GLA_DV), 0) & (rows - 1)
    for s in range(n_seq):
        st = get_state(s)
        oi = _dot(lhs, st.astype(BF16))
        if n_seq == 1:
            o_inter = oi
            kd_s = kdb
        else:
            m4 = jnp.logical_and(r4 >= s * seq_len, r4 < (s + 1) * seq_len)
            oi = jnp.where(m4, oi, 0.0)
            o_inter = oi if o_inter is None else o_inter + oi
            kd_s = jnp.where(_row_in_seq(rows, seq_len, s, GLA_KW), kdb, jnp.zeros_like(kdb))
        blocks = []
        for hp in range(GLA_HEADS // 2):
            upd = _dot_tn(kd_s[:, 2 * hp * GLA_DK:2 * (hp + 1) * GLA_DK],
                          vb[:, 2 * hp * GLA_DV:2 * (hp + 1) * GLA_DV])
            blocks += [upd[:GLA_DK, :GLA_DV], upd[GLA_DK:, GLA_DV:]]
        upd_d = jnp.concatenate(blocks, axis=0)
        c0 = s * seq_len
        decay = jnp.exp(jnp.broadcast_to(bl_t[:, c0:c0 + 1], (GLA_KW, GLA_DV)))
        put_state(s, decay * st + upd_d)
        if n_seq > 1 and s % 4 == 3:
            yield
    yield

    for h in range(GLA_HEADS):
        o_h = _dot(att[h * rows:(h + 1) * rows], vb[:, h * GLA_DV:(h + 1) * GLA_DV])
        outs.append(o_h + o_inter[h * rows:(h + 1) * rows])


def _ssd_phases(xs, bm, cm, dt_raw_blk, dtb_blk, a_blk, d_row, get_state, put_state, seq_len, ys):
    rows = xs.shape[0]
    n_seq = rows // seq_len
    same, tril = _seg_masks(rows, seq_len)
    tril_b = _mask_to_bf16(tril, rows)
    same_b = _mask_to_bf16(same, rows)

    dt = _softplus(dt_raw_blk + dtb_blk)
    dta = dt * a_blk
    lc = _dot3(tril_b, dta)
    if n_seq == 1:
        ll = jnp.broadcast_to(lc[rows - 1:rows, :], lc.shape)
    else:
        ll = _dot3(same_b, dta)
    elc = jnp.exp(lc)
    w = jnp.exp(ll - lc) * dt
    ell = jnp.exp(ll)
    lc_t = _pad_rows(lc, 128).T[:, :rows]
    yield

    bmb = bm.astype(BF16)
    cmb = cm.astype(BF16)
    cb = [_dot_nt(cmb[:, g * SSD_STATE:(g + 1) * SSD_STATE],
                  bmb[:, g * SSD_STATE:(g + 1) * SSD_STATE]) for g in range(SSD_GROUPS)]
    yield

    lane128 = lax.broadcasted_iota(jnp.int32, (rows, 128), 1)
    lo128 = lane128 < SSD_HEAD_DIM
    sub128 = lax.broadcasted_iota(jnp.int32, (128, 128), 0) < SSD_HEAD_DIM

    def col(x, h, width):
        return jnp.broadcast_to(x[:, DT_LANE + h:DT_LANE + h + 1], (x.shape[0], width))

    heads_per_group = SSD_HEADS // SSD_GROUPS
    for j in range(SSD_HEADS // 2):
        g = (2 * j) // heads_per_group
        xs_p = xs[:, 128 * j:128 * (j + 1)]
        dt_p = jnp.where(lo128, col(dt, 2 * j, 128), col(dt, 2 * j + 1, 128))
        xdt = xs_p * dt_p
        x_lo = jnp.where(lo128, xdt, 0.0).astype(BF16)
        x_hi = jnp.where(lo128, 0.0, xdt).astype(BF16)
        y_p = None
        for h, x_half in ((2 * j, x_lo), (2 * j + 1, x_hi)):
            seg = col(lc, h, rows) - jnp.broadcast_to(lc_t[DT_LANE + h:DT_LANE + h + 1, :], (rows, rows))
            w_h = jnp.where(tril, cb[g] * jnp.exp(seg), 0.0).astype(BF16)
            y_h = _dot(w_h, x_half)
            y_p = y_h if y_p is None else y_p + y_h
            yield

        elc_p = jnp.where(lo128, col(elc, 2 * j, 128), col(elc, 2 * j + 1, 128))
        w_p = jnp.where(lo128, col(w, 2 * j, 128), col(w, 2 * j + 1, 128))
        xw = (xs_p * w_p).astype(BF16)
        bm_g = bmb[:, g * SSD_STATE:(g + 1) * SSD_STATE]
        cm_g = cmb[:, g * SSD_STATE:(g + 1) * SSD_STATE]
        y_inter = None
        for s in range(n_seq):
            st = get_state(s, j)
            ci = _dot_nt(cm_g, st.astype(BF16))
            if n_seq == 1:
                y_inter = ci
                xw_s = xw
            else:
                msk = _row_in_seq(rows, seq_len, s, 128)
                ci = jnp.where(msk, ci, 0.0)
                y_inter = ci if y_inter is None else y_inter + ci
                xw_s = jnp.where(msk, xw, jnp.zeros_like(xw))
            upd = _dot_tn(xw_s, bm_g)
            r0 = s * seq_len
            e0 = jnp.broadcast_to(ell[r0:r0 + 1, DT_LANE + 2 * j:DT_LANE + 2 * j + 1], (128, 128))
            e1 = jnp.broadcast_to(ell[r0:r0 + 1, DT_LANE + 2 * j + 1:DT_LANE + 2 * j + 2], (128, 128))
            put_state(s, j, jnp.where(sub128, e0, e1) * st + upd)
        ys.append(y_p + y_inter * elc_p + d_row[:, 128 * j:128 * (j + 1)] * xs_p)
        yield


def _gate_logf(small_blk, gw2_ref, gb_ref):
    logits = _dot(_bf(small_blk), gw2_ref[...]) + gb_ref[...]
    return _log_sigmoid(logits) * (1.0 / GLA_GATE_TAU)


def _gla_finish(o_h, g_h, gnw):
    return _rms(o_h, gnw) * _silu(g_h)


def _ssd_finish(y, z, snw):
    y = y * _silu(z)
    gs = SSD_INNER // SSD_GROUPS
    parts = [_rms(y[:, g * gs:(g + 1) * gs], snw[:, g * gs:(g + 1) * gs]) for g in range(SSD_GROUPS)]
    return jnp.concatenate(parts, axis=1)


def _memkv_kernel(mem_ref, nw_ref, wk_ref, wv_ref, k_ref, v_ref, kb_ref, vb_ref):
    mn = _bf(_rms(mem_ref[0], nw_ref[...]))
    mk = _dot(mn, wk_ref[...])
    mv = _dot(mn, wv_ref[...])
    for h in range(XA_HEADS):
        k_ref[0, 0, :, h, :] = mk[:, h * XA_HEAD_DIM:(h + 1) * XA_HEAD_DIM]
        v_ref[0, 0, :, h, :] = mv[:, h * XA_HEAD_DIM:(h + 1) * XA_HEAD_DIM]
    kb_ref[0] = _bf(mk)
    vb_ref[0] = _bf(mv)


def _const_spec(shape):
    nd = len(shape)
    return pl.BlockSpec(shape, lambda *_: (0,) * nd, pipeline_mode=pl.Buffered(1))


def _memkv(mem, nw, wk, wv):
    b, m, d = mem.shape
    blk = pl.BlockSpec((1, m, d), lambda i: (i, 0, 0))
    cache_blk = pl.BlockSpec((1, 1, m, XA_HEADS, XA_HEAD_DIM), lambda i: (0, i, 0, 0, 0))
    cache_shape = jax.ShapeDtypeStruct((1, b, m, XA_HEADS, XA_HEAD_DIM), F32)
    return pl.pallas_call(
        _memkv_kernel,
        grid=(b,),
        in_specs=[blk, _const_spec((1, d)), _const_spec((d, d)), _const_spec((d, d))],
        out_specs=[cache_blk, cache_blk, blk, blk],
        out_shape=[cache_shape] * 2 + [jax.ShapeDtypeStruct((b, m, d), BF16)] * 2,
        compiler_params=pltpu.CompilerParams(dimension_semantics=("arbitrary",),
                                             vmem_limit_bytes=VMEM_LIMIT),
        name="mem_kv",
    )(mem, nw, wk, wv)


PROJ_CHUNK = 512


def _mixer_prompt_chain(x_ref, h_ref, proj_s, xpad_s, mixed_s, sg_s, ss_s,
                        lnpre_ref, win_ref, gw2_ref, gb_ref, gnw_ref, cw_ref, cb_ref,
                        dtb_ref, a_ref, d_ref, snw_ref, wout_ref, lnpost_ref):
    rows = TILE
    x = x_ref[...]
    hn = _rms(x, lnpre_ref[...]).astype(BF16)
    yield

    def proj_cols(c0, width):
        proj_s[:, c0:c0 + width] = _dot(hn, win_ref[:, c0:c0 + width])

    def late_proj():
        for c0 in (COL_Z, COL_Q, COL_V, COL_G):
            proj_cols(c0, PROJ_CHUNK)
            yield

    side = [late_proj()]

    def tick():
        if side[0] is not None:
            try:
                next(side[0])
            except StopIteration:
                side[0] = None

    for c in range(SSD_CONV_CH // PROJ_CHUNK):
        proj_cols(COL_XBC + c * PROJ_CHUNK, PROJ_CHUNK)
        yield
    proj_cols(COL_SMALL, 128)

    xa = []
    for c in range(SSD_CONV_CH // PROJ_CHUNK):
        cs = slice(c * PROJ_CHUNK, (c + 1) * PROJ_CHUNK)
        xpad_s[8:8 + rows, cs] = proj_s[:, COL_XBC + c * PROJ_CHUNK:COL_XBC + (c + 1) * PROJ_CHUNK]
        conv = cb_ref[:, cs]
        for j in range(SSD_CONV):
            conv = conv + xpad_s[5 + j:5 + j + rows, cs] * cw_ref[j:j + 1, cs]
        xpad_s[5:8, cs] = xpad_s[5 + rows:8 + rows, cs]
        xa.append(_silu(conv))
        tick()
        yield

    small = proj_s[:, COL_SMALL:COL_SMALL + 128]
    logf = _gate_logf(small, gw2_ref, gb_ref)

    def get_ss(s, j):
        return ss_s[128 * j:128 * (j + 1), :]

    def put_ss(s, j, val):
        ss_s[128 * j:128 * (j + 1), :] = val

    ys = []
    ssd = _ssd_phases(xa[0], xa[1][:, :SSD_GROUPS * SSD_STATE], xa[1][:, SSD_GROUPS * SSD_STATE:],
                      small, dtb_ref[...], a_ref[...], d_ref[...], get_ss, put_ss, rows, ys)
    for i, _ in enumerate(ssd):
        if i % 2 == 0:
            tick()
        yield
    while side[0] is not None:
        tick()
        yield
    mixed_s[:, GLA_VW:] = _ssd_finish(jnp.concatenate(ys, axis=1), proj_s[:, COL_Z:COL_Z + SSD_INNER],
                                      snw_ref[...]).astype(BF16)
    yield

    def get_sg(s):
        return sg_s[...]

    def put_sg(s, val):
        sg_s[...] = val

    for c in range(rows // GLA_CHUNK):
        r0 = c * GLA_CHUNK
        sl = slice(r0, r0 + GLA_CHUNK)
        outs = []
        yield from _gla_phases(proj_s[sl, COL_Q:COL_Q + GLA_KW], proj_s[sl, COL_K:COL_K + GLA_KW],
                               proj_s[sl, COL_V:COL_V + GLA_VW], logf[sl], get_sg, put_sg, GLA_CHUNK, outs)
        for h in range(GLA_HEADS):
            g_h = proj_s[sl, COL_G + h * GLA_DV:COL_G + (h + 1) * GLA_DV]
            mixed_s[sl, h * GLA_DV:(h + 1) * GLA_DV] = _gla_finish(outs[h], g_h, gnw_ref[...]).astype(BF16)
        yield

    mixed = mixed_s[...]
    ms = []
    for c in range(D_MODEL // PROJ_CHUNK):
        ms.append(_dot(mixed, wout_ref[:, c * PROJ_CHUNK:(c + 1) * PROJ_CHUNK]))
        yield
    h_ref[...] = x + _rms(jnp.concatenate(ms, axis=1), lnpost_ref[...])


N_MIXER_WEIGHTS = 13
N_MIXER_SCRATCH = 5
MIX_OFFSET = 3


def _mixer_prompt_kernel(x_ref, *rest):
    consts = rest[:N_MIXER_WEIGHTS]
    h_ref, sg_out, ss_out, conv_out = rest[N_MIXER_WEIGHTS:N_MIXER_WEIGHTS + 4]
    scratch = rest[N_MIXER_WEIGHTS + 4:]
    per_seq = [scratch[i * N_MIXER_SCRATCH:(i + 1) * N_MIXER_SCRATCH] for i in range(MIX_SEQS)]
    t = pl.program_id(1)

    @pl.when(t == 0)
    def _():
        for _, xpad_s, _, sg_s, ss_s in per_seq:
            sg_s[...] = jnp.zeros_like(sg_s)
            ss_s[...] = jnp.zeros_like(ss_s)
            xpad_s[0:8, :] = jnp.zeros((8, SSD_CONV_CH), F32)

    _interleave([_mixer_prompt_chain(x_ref.at[i], h_ref.at[i], *per_seq[i], *consts) for i in range(MIX_SEQS)],
                offset=MIX_OFFSET)

    @pl.when(t == pl.num_programs(1) - 1)
    def _():
        for i, (_, xpad_s, _, sg_s, ss_s) in enumerate(per_seq):
            sg_out[i] = sg_s[...].reshape(GLA_HEADS, GLA_DK, GLA_DV)
            ss_out[i] = ss_s[...].reshape(SSD_HEADS, SSD_HEAD_DIM, SSD_STATE)
            conv_out[i] = xpad_s[5:8, :]


def _mixer_prompt(x, p):
    b, t, d = x.shape
    nt = t // TILE
    consts = [p["ln_mix_pre"], p["w_in"], p["gw2"], p["gb"], p["gnw"], p["conv_w"], p["conv_b"],
              p["dtb"], p["a_blk"], p["d_row"], p["snw"], p["w_out"], p["ln_mix_post"]]
    assert len(consts) == N_MIXER_WEIGHTS
    tok = pl.BlockSpec((MIX_SEQS, TILE, d), lambda i, j: (i, j, 0))
    return pl.pallas_call(
        _mixer_prompt_kernel,
        grid=(b // MIX_SEQS, nt),
        in_specs=[tok] + [_const_spec(c.shape) for c in consts],
        out_specs=[tok,
                   pl.BlockSpec((MIX_SEQS, GLA_HEADS, GLA_DK, GLA_DV), lambda i, j: (i, 0, 0, 0)),
                   pl.BlockSpec((MIX_SEQS, SSD_HEADS, SSD_HEAD_DIM, SSD_STATE), lambda i, j: (i, 0, 0, 0)),
                   pl.BlockSpec((MIX_SEQS, SSD_CONV - 1, SSD_CONV_CH), lambda i, j: (i, 0, 0))],
        out_shape=[jax.ShapeDtypeStruct((b, t, d), F32),
                   jax.ShapeDtypeStruct((b, GLA_HEADS, GLA_DK, GLA_DV), F32),
                   jax.ShapeDtypeStruct((b, SSD_HEADS, SSD_HEAD_DIM, SSD_STATE), F32),
                   jax.ShapeDtypeStruct((b, SSD_CONV - 1, SSD_CONV_CH), F32)],
        scratch_shapes=[pltpu.VMEM((TILE, PROJ_COLS), F32),
                        pltpu.VMEM((TILE + 8, SSD_CONV_CH), F32),
                        pltpu.VMEM((TILE, D_MODEL), BF16),
                        pltpu.VMEM((GLA_KW, GLA_DV), F32),
                        pltpu.VMEM((SSD_INNER, SSD_STATE), F32)] * MIX_SEQS,
        compiler_params=pltpu.CompilerParams(dimension_semantics=("arbitrary", "arbitrary"),
                                             vmem_limit_bytes=VMEM_LIMIT),
        name="mixer_prompt",
    )(x, *consts)


def _softmax_rows(s):
    e = jnp.exp(s - jnp.max(s, axis=-1, keepdims=True))
    return e / jnp.sum(e, axis=-1, keepdims=True)


def _ffn(h, lnpre, lnpost, wg_ref, wu_ref, wd_ref):
    hf = _bf(_rms(h, lnpre))
    gt = _dot(hf, wg_ref[...])
    up = _dot(hf, wu_ref[...])
    f = _dot(_bf(_silu(gt) * up), wd_ref[...])
    return h + _rms(f, lnpost)


FFN_CHUNK = 256
FFN_OFFSET = 1


def _attn_ffn_chain(h_ref, y_ref, mk_ref, mv_ref, lnxa_ref, wq_ref, wo_ref, lnxap_ref,
                    lnf_ref, lnfp_ref, wg_ref, wu_ref, wd_ref):
    h = h_ref[...]
    hn = _rms(h, lnxa_ref[...]).astype(BF16)
    yield
    heads_per_chunk = PROJ_CHUNK // XA_HEAD_DIM
    outs = []
    for c in range(D_MODEL // PROJ_CHUNK):
        q = _dot(hn, wq_ref[:, c * PROJ_CHUNK:(c + 1) * PROJ_CHUNK]).astype(BF16)
        yield
        for hh in range(heads_per_chunk):
            hd = c * heads_per_chunk + hh
            sl = slice(hd * XA_HEAD_DIM, (hd + 1) * XA_HEAD_DIM)
            s = _dot_nt(q[:, hh * XA_HEAD_DIM:(hh + 1) * XA_HEAD_DIM], mk_ref[0, :, sl])
            yield
            pr = _softmax_rows(s).astype(BF16)
            yield
            outs.append(_dot(pr, mv_ref[0, :, sl]).astype(BF16))
    a = jnp.concatenate(outs, axis=1)
    yield
    parts = []
    for c in range(D_MODEL // PROJ_CHUNK):
        parts.append(_dot(a, wo_ref[:, c * PROJ_CHUNK:(c + 1) * PROJ_CHUNK]))
        yield
    h = h + _rms(jnp.concatenate(parts, axis=1), lnxap_ref[...])
    hf = _rms(h, lnf_ref[...]).astype(BF16)
    yield
    acts = []
    for c in range(D_FF // FFN_CHUNK):
        cs = slice(c * FFN_CHUNK, (c + 1) * FFN_CHUNK)
        gt = _dot(hf, wg_ref[:, cs])
        up = _dot(hf, wu_ref[:, cs])
        yield
        acts.append((_silu(gt) * up).astype(BF16))
    act = jnp.concatenate(acts, axis=1)
    yield
    parts = []
    for c in range(D_MODEL // PROJ_CHUNK):
        parts.append(_dot(act, wd_ref[:, c * PROJ_CHUNK:(c + 1) * PROJ_CHUNK]))
        yield
    y_ref[...] = h + _rms(jnp.concatenate(parts, axis=1), lnfp_ref[...])


def _attn_ffn_prompt_kernel(h_ref, mk_ref, mv_ref, *rest):
    consts, y_ref = rest[:-1], rest[-1]
    chains = []
    for sub in range(FFN_SUBTILES):
        rs = pl.ds(sub * TILE, TILE)
        chains.append(_attn_ffn_chain(h_ref.at[0, rs], y_ref.at[0, rs], mk_ref, mv_ref, *consts))
    _interleave(chains, offset=FFN_OFFSET)


def _attn_ffn_prompt(h, mkb, mvb, p):
    b, t, d = h.shape
    rows = FFN_SUBTILES * TILE
    consts = [p["ln_xa_pre"], p["w_xq"], p["w_xo"], p["ln_xa_post"], p["ln_ffn_pre"], p["ln_ffn_post"],
              p["w_gate"], p["w_up"], p["w_down"]]
    tok = pl.BlockSpec((1, rows, d), lambda i, j: (i, j, 0))
    mem = pl.BlockSpec((1, mkb.shape[1], d), lambda i, j: (i, 0, 0))
    return pl.pallas_call(
        _attn_ffn_prompt_kernel,
        grid=(b, t // rows),
        in_specs=[tok, mem, mem] + [_const_spec(c.shape) for c in consts],
        out_specs=tok,
        out_shape=jax.ShapeDtypeStruct((b, t, d), F32),
        compiler_params=pltpu.CompilerParams(dimension_semantics=("arbitrary", "arbitrary"),
                                             vmem_limit_bytes=VMEM_LIMIT),
        name="attn_ffn_prompt",
    )(h, mkb, mvb, *consts)


def _proj_sample_kernel(x_ref, lnpre_ref, win_ref, proj_ref):
    proj_ref[...] = _dot(_bf(_rms(x_ref[...], lnpre_ref[...])), win_ref[...])


def _proj_sample(x, p):
    n, d = x.shape
    return pl.pallas_call(
        _proj_sample_kernel,
        grid=(1,),
        in_specs=[_const_spec((n, d)), _const_spec((1, d)), _const_spec(p["w_in"].shape)],
        out_specs=pl.BlockSpec((n, PROJ_COLS), lambda i: (0, 0)),
        out_shape=jax.ShapeDtypeStruct((n, PROJ_COLS), F32),
        compiler_params=pltpu.CompilerParams(dimension_semantics=("arbitrary",),
                                             vmem_limit_bytes=VMEM_LIMIT),
        name="proj_sample",
    )(x, p["ln_mix_pre"], p["w_in"])


def _mixer_sample_kernel(seq_len, proj_ref, tail_ref, sg_ref, ss_ref, gw2_ref, gb_ref, gnw_ref,
                         cw_ref, cb_ref, dtb_ref, a_ref, d_ref, snw_ref,
                         mixed_ref, sg_out, ss_out, xpad_s, tpad_s):
    rows = proj_ref.shape[0]
    n_seq = rows // seq_len
    small = proj_ref[:, COL_SMALL:COL_SMALL + 128]
    logf = _gate_logf(small, gw2_ref, gb_ref)

    xbc = proj_ref[:, COL_XBC:COL_XBC + SSD_CONV_CH]
    xpad_s[0:8, :] = jnp.zeros((8, SSD_CONV_CH), F32)
    xpad_s[8:8 + rows, :] = xbc
    tpad_s[0:rows, :] = tail_ref[...]
    tpad_s[rows:rows + 8, :] = jnp.zeros((8, SSD_CONV_CH), F32)
    pos = lax.broadcasted_iota(jnp.int32, (rows, SSD_CONV_CH), 0) & (seq_len - 1)
    conv = cb_ref[...]
    for j in range(SSD_CONV - 1):
        back = SSD_CONV - 1 - j
        prev = jnp.where(pos >= back, xpad_s[8 - back:8 - back + rows, :],
                         tpad_s[seq_len - back:seq_len - back + rows, :])
        conv = conv + prev * cw_ref[j:j + 1, :]
    conv = conv + xbc * cw_ref[SSD_CONV - 1:SSD_CONV, :]
    xa = _silu(conv)

    def get_ss(s, j):
        return ss_ref[s, 2 * j:2 * j + 2].reshape(2 * SSD_HEAD_DIM, SSD_STATE)

    def put_ss(s, j, val):
        ss_out[s, 2 * j:2 * j + 2] = val.reshape(2, SSD_HEAD_DIM, SSD_STATE)

    ys = []
    _run(_ssd_phases(xa[:, :SSD_INNER], xa[:, SSD_INNER:SSD_INNER + 256], xa[:, SSD_INNER + 256:],
                     small, dtb_ref[...], a_ref[...], d_ref[...], get_ss, put_ss, seq_len, ys))
    mixed_ref[:, GLA_VW:] = _ssd_finish(jnp.concatenate(ys, axis=1), proj_ref[:, COL_Z:COL_Z + SSD_INNER],
                                        snw_ref[...])

    def get_sg(s):
        return sg_ref[s].reshape(GLA_KW, GLA_DV)

    def put_sg(s, val):
        sg_out[s] = val.reshape(GLA_HEADS, GLA_DK, GLA_DV)

    outs = []
    _run(_gla_phases(proj_ref[:, COL_Q:COL_Q + GLA_KW], proj_ref[:, COL_K:COL_K + GLA_KW],
                     proj_ref[:, COL_V:COL_V + GLA_VW], logf, get_sg, put_sg, seq_len, outs))
    for h in range(GLA_HEADS):
        g_h = proj_ref[:, COL_G + h * GLA_DV:COL_G + (h + 1) * GLA_DV]
        mixed_ref[:, h * GLA_DV:(h + 1) * GLA_DV] = _gla_finish(outs[h], g_h, gnw_ref[...])


def _mixer_sample(proj, tail, sg, ss, p, seq_len):
    n = proj.shape[0]
    rows = SAMPLE_SEQS * seq_len
    nb = sg.shape[0]
    consts = [p["gw2"], p["gb"], p["gnw"], p["conv_w"], p["conv_b"], p["dtb"], p["a_blk"], p["d_row"],
              p["snw"]]
    sg_spec = pl.BlockSpec((SAMPLE_SEQS, GLA_HEADS, GLA_DK, GLA_DV), lambda i: (i, 0, 0, 0))
    ss_spec = pl.BlockSpec((SAMPLE_SEQS, SSD_HEADS, SSD_HEAD_DIM, SSD_STATE), lambda i: (i, 0, 0, 0))
    return pl.pallas_call(
        functools.partial(_mixer_sample_kernel, seq_len),
        grid=(nb // SAMPLE_SEQS,),
        in_specs=[pl.BlockSpec((rows, PROJ_COLS), lambda i: (i, 0)),
                  pl.BlockSpec((rows, SSD_CONV_CH), lambda i: (i, 0)),
                  sg_spec, ss_spec] + [_const_spec(c.shape) for c in consts],
        out_specs=[pl.BlockSpec((rows, D_MODEL), lambda i: (i, 0)), sg_spec, ss_spec],
        out_shape=[jax.ShapeDtypeStruct((n, D_MODEL), F32),
                   jax.ShapeDtypeStruct(sg.shape, F32), jax.ShapeDtypeStruct(ss.shape, F32)],
        scratch_shapes=[pltpu.VMEM((rows + 8, SSD_CONV_CH), F32),
                        pltpu.VMEM((rows + 8, SSD_CONV_CH), F32)],
        compiler_params=pltpu.CompilerParams(dimension_semantics=("arbitrary",),
                                             vmem_limit_bytes=VMEM_LIMIT),
        name="mixer_sample",
    )(proj, tail, sg, ss, *consts)


def _post_mix_sample_kernel(x_ref, mixed_ref, wout_ref, lnpost_ref, lnxa_ref, wq_ref, h_ref, q_ref):
    m = _dot(_bf(mixed_ref[...]), wout_ref[...])
    h = x_ref[...] + _rms(m, lnpost_ref[...])
    h_ref[...] = h
    q_ref[...] = _bf(_dot(_bf(_rms(h, lnxa_ref[...])), wq_ref[...]))


def _post_mix_sample(x, mixed, p):
    n, d = x.shape
    full = pl.BlockSpec((n, d), lambda i: (0, 0))
    consts = [p["w_out"], p["ln_mix_post"], p["ln_xa_pre"], p["w_xq"]]
    return pl.pallas_call(
        _post_mix_sample_kernel,
        grid=(1,),
        in_specs=[full, full] + [_const_spec(c.shape) for c in consts],
        out_specs=[full, full],
        out_shape=[jax.ShapeDtypeStruct((n, d), F32), jax.ShapeDtypeStruct((n, d), BF16)],
        compiler_params=pltpu.CompilerParams(dimension_semantics=("arbitrary",),
                                             vmem_limit_bytes=VMEM_LIMIT),
        name="post_mix_sample",
    )(x, mixed, *consts)


def _attn_sample_kernel(seq_len, q_ref, k_hbm, v_hbm, a_ref, kbuf, vbuf, sem):
    step = pl.program_id(0)
    rows = q_ref.shape[0]
    n_seq = rows // seq_len

    def copies(at_step, slot):
        out = []
        for s in range(n_seq):
            for hd in range(XA_HEADS):
                b = at_step * n_seq + s
                out.append(pltpu.make_async_copy(k_hbm.at[0, b, :, hd, :], kbuf.at[slot, s, hd], sem.at[0, slot]))
                out.append(pltpu.make_async_copy(v_hbm.at[0, b, :, hd, :], vbuf.at[slot, s, hd], sem.at[1, slot]))
        return out

    @pl.when(step == 0)
    def _():
        for cp in copies(0, 0):
            cp.start()

    @pl.when(step + 1 < pl.num_programs(0))
    def _():
        for cp in copies(step + 1, (step + 1) % 2):
            cp.start()

    slot = step % 2
    for cp in copies(step, slot):
        cp.wait()

    q = q_ref[...]
    results = {}

    def one(s, hd):
        sc = _dot_nt(q[:, hd * XA_HEAD_DIM:(hd + 1) * XA_HEAD_DIM], _bf(kbuf[slot, s, hd]))
        yield
        e = jnp.exp(sc - jnp.max(sc, axis=-1, keepdims=True))
        pr = _bf(e / jnp.sum(e, axis=-1, keepdims=True))
        yield
        results[s, hd] = _dot(pr, _bf(vbuf[slot, s, hd]))

    _interleave([one(s, hd) for s in range(n_seq) for hd in range(XA_HEADS)])
    for hd in range(XA_HEADS):
        sl = slice(hd * XA_HEAD_DIM, (hd + 1) * XA_HEAD_DIM)
        out = results[0, hd]
        for s in range(1, n_seq):
            out = jnp.where(_row_in_seq(rows, seq_len, s, XA_HEAD_DIM), results[s, hd], out)
        a_ref[:, sl] = out


def _attn_sample(q, ck, cv, seq_len):
    n, d = q.shape
    _, nb, m, nh, hd = ck.shape
    rows = ATTN_SEQS * seq_len
    tok = pl.BlockSpec((rows, d), lambda i: (i, 0))
    hbm = pl.BlockSpec(memory_space=pl.ANY)
    return pl.pallas_call(
        functools.partial(_attn_sample_kernel, seq_len),
        grid=(nb // ATTN_SEQS,),
        in_specs=[tok, hbm, hbm],
        out_specs=tok,
        out_shape=jax.ShapeDtypeStruct((n, d), F32),
        scratch_shapes=[pltpu.VMEM((2, ATTN_SEQS, nh, m, hd), F32),
                        pltpu.VMEM((2, ATTN_SEQS, nh, m, hd), F32),
                        pltpu.SemaphoreType.DMA((2, 2))],
        compiler_params=pltpu.CompilerParams(dimension_semantics=("arbitrary",),
                                             vmem_limit_bytes=VMEM_LIMIT),
        name="attn_sample",
    )(q, ck, cv)


def _post_attn_sample_kernel(h_ref, a_ref, wo_ref, lnxap_ref, lnf_ref, lnfp_ref, wg_ref, wu_ref, wd_ref,
                             y_ref):
    a = _dot(_bf(a_ref[...]), wo_ref[...])
    h = h_ref[...] + _rms(a, lnxap_ref[...])
    y_ref[...] = _ffn(h, lnf_ref[...], lnfp_ref[...], wg_ref, wu_ref, wd_ref)


def _post_attn_sample(h, a, p):
    n, d = h.shape
    full = pl.BlockSpec((n, d), lambda i: (0, 0))
    consts = [p["w_xo"], p["ln_xa_post"], p["ln_ffn_pre"], p["ln_ffn_post"], p["w_gate"], p["w_up"],
              p["w_down"]]
    return pl.pallas_call(
        _post_attn_sample_kernel,
        grid=(1,),
        in_specs=[full, full] + [_const_spec(c.shape) for c in consts],
        out_specs=full,
        out_shape=jax.ShapeDtypeStruct((n, d), F32),
        compiler_params=pltpu.CompilerParams(dimension_semantics=("arbitrary",),
                                             vmem_limit_bytes=VMEM_LIMIT),
        name="post_attn_sample",
    )(h, a, *consts)


def _pack_params(ln_mix_pre, ln_mix_post, w_in, gla_gate_w2, gla_gate_b, gla_norm_w, ssd_conv_w,
                 ssd_conv_b, ssd_dt_bias, ssd_A_log, ssd_D, ssd_norm_w, w_out, ln_xa_pre, ln_xa_post,
                 mem_norm_w, w_xq, w_xk, w_xv, w_xo, ln_ffn_pre, ln_ffn_post, w_gate, w_up, w_down):
    sizes = (GLA_KW, GLA_KW, GLA_VW, GLA_VW, GLA_GATE_RANK, SSD_INNER, SSD_CONV_CH, SSD_HEADS)
    offs = [0]
    for sz in sizes:
        offs.append(offs[-1] + sz)
    wq, wk, wv, wg, wglr, wz, wxbc, wdt = [w_in[:, offs[i]:offs[i + 1]] for i in range(len(sizes))]
    pad = jnp.zeros((D_MODEL, PROJ_COLS - COL_SMALL - GLA_GATE_RANK - SSD_HEADS), F32)
    w_in_p = jnp.concatenate([wq * (GLA_DK ** -0.5), wk, wv, wg, wz, wxbc, wglr, wdt, pad], axis=1)

    def small_row(v):
        return jnp.zeros((1, 128), F32).at[0, DT_LANE:DT_LANE + SSD_HEADS].set(v)

    row = lambda v: v.reshape(1, -1)
    return {
        "ln_mix_pre": row(ln_mix_pre), "ln_mix_post": row(ln_mix_post),
        "w_in": _bf(w_in_p),
        "gw2": _bf(jnp.zeros((128, GLA_KW), F32).at[:GLA_GATE_RANK].set(gla_gate_w2)),
        "gb": row(gla_gate_b), "gnw": row(gla_norm_w),
        "conv_w": ssd_conv_w, "conv_b": row(ssd_conv_b),
        "dtb": small_row(ssd_dt_bias),
        "a_blk": small_row(-jnp.exp(ssd_A_log)),
        "d_row": jnp.repeat(ssd_D, SSD_HEAD_DIM).reshape(1, SSD_INNER),
        "snw": row(ssd_norm_w),
        "w_out": _bf(w_out),
        "ln_xa_pre": row(ln_xa_pre), "ln_xa_post": row(ln_xa_post), "mem_norm_w": row(mem_norm_w),
        "w_xq": _bf(w_xq * (XA_HEAD_DIM ** -0.5)), "w_xk": _bf(w_xk), "w_xv": _bf(w_xv), "w_xo": _bf(w_xo),
        "ln_ffn_pre": row(ln_ffn_pre), "ln_ffn_post": row(ln_ffn_post),
        "w_gate": _bf(w_gate), "w_up": _bf(w_up), "w_down": _bf(w_down),
    }


def kernel(x_prompt, x_sample, mem_prompt, state_gla, state_ssm, state_conv, cache_mem_k, cache_mem_v,
           ln_mix_pre, ln_mix_post, w_in, gla_gate_w2, gla_gate_b, gla_norm_w, ssd_conv_w, ssd_conv_b,
           ssd_dt_bias, ssd_A_log, ssd_D, ssd_norm_w, w_out, ln_xa_pre, ln_xa_post, mem_norm_w,
           w_xq, w_xk, w_xv, w_xo, ln_ffn_pre, ln_ffn_post, w_gate, w_up, w_down):
    assert w_in.shape[0] == 1, "single-layer kernel"
    layer = [a[0] for a in (ln_mix_pre, ln_mix_post, w_in, gla_gate_w2, gla_gate_b, gla_norm_w, ssd_conv_w,
                            ssd_conv_b, ssd_dt_bias, ssd_A_log, ssd_D, ssd_norm_w, w_out, ln_xa_pre,
                            ln_xa_post, mem_norm_w, w_xq, w_xk, w_xv, w_xo, ln_ffn_pre, ln_ffn_post,
                            w_gate, w_up, w_down)]
    p = _pack_params(*layer)
    bp, mem_len, _ = mem_prompt.shape
    bs, ts, _ = x_sample.shape
    assert ts == SSD_CONV, "the padded conv-tail layout assumes one new row per conv tap"

    mk, mv, mkb, mvb = _memkv(mem_prompt, p["mem_norm_w"], p["w_xk"], p["w_xv"])
    h_p, gla_p, ssm_p, conv_p = _mixer_prompt(x_prompt, p)
    y_p = _attn_ffn_prompt(h_p, mkb, mvb, p)

    xs = x_sample.reshape(bs * ts, D_MODEL)
    proj = _proj_sample(xs, p)
    tail = jnp.pad(state_conv[0], ((0, 0), (1, 0), (0, 0))).reshape(bs * ts, SSD_CONV_CH)
    mixed, gla_s, ssm_s = _mixer_sample(proj, tail, state_gla[0], state_ssm[0], p, ts)
    h_s, q_s = _post_mix_sample(xs, mixed, p)
    a_s = _attn_sample(q_s, cache_mem_k, cache_mem_v, ts)
    y_s = _post_attn_sample(h_s, a_s, p)
    conv_s = proj[:, COL_XBC:COL_XBC + SSD_CONV_CH].reshape(bs, ts, SSD_CONV_CH)[:, ts - (SSD_CONV - 1):]

    return (y_p, y_s.reshape(bs, ts, D_MODEL), gla_p[None], ssm_p[None], conv_p[None],
            mk, mv, gla_s[None], ssm_s[None], conv_s[None])
```

```python
import functools

import jax
import jax.numpy as jnp
from jax import lax
from jax.experimental import pallas as pl
from jax.experimental.pallas import tpu as pltpu

F32 = jnp.float32
BF16 = jnp.bfloat16

D_MODEL = 1024
GLA_HEADS = 4
GLA_DK = 64
GLA_DV = 128
GLA_KW = GLA_HEADS * GLA_DK
GLA_VW = GLA_HEADS * GLA_DV
GLA_GATE_RANK = 16
GLA_GATE_TAU = 16.0
SSD_INNER = 512
SSD_HEAD_DIM = 64
SSD_HEADS = 8
SSD_GROUPS = 2
SSD_STATE = 128
SSD_CONV = 4
SSD_CONV_CH = 1024
XA_HEADS = 4
XA_HEAD_DIM = 256
D_FF = 2816
EPS = 1e-6

COL_Q = 0
COL_K = 256
COL_V = 512
COL_G = 1024
COL_Z = 1536
COL_XBC = 2048
COL_SMALL = 3072
DT_LANE = GLA_GATE_RANK
PROJ_COLS = 3200

GLA_CHUNK = 64
TILE = 256
FFN_SUBTILES = 4
MIX_SEQS = 4
SAMPLE_SEQS = 16
ATTN_SEQS = 4
VMEM_LIMIT = 56 * 1024 * 1024


def _bf(x):
    return x.astype(BF16)


def _dot(a, b):
    return jnp.dot(a, b, preferred_element_type=F32)


def _dot_nt(a, b):
    return lax.dot_general(a, b, (((1,), (1,)), ((), ())), preferred_element_type=F32)


def _dot_tn(a, b):
    return lax.dot_general(a, b, (((0,), (0,)), ((), ())), preferred_element_type=F32)


def _dot3(m, a):
    hi = _bf(a)
    r = a - hi.astype(F32)
    mid = _bf(r)
    lo = _bf(r - mid.astype(F32))
    return _dot(m, hi) + _dot(m, mid) + _dot(m, lo)


def _rms(x, w):
    ms = jnp.mean(x * x, axis=-1, keepdims=True)
    return x * lax.rsqrt(ms + EPS) * w


def _silu(x):
    return x / (1.0 + jnp.exp(-x))


def _softplus(x):
    e = jnp.exp(-jnp.abs(x))
    u = 1.0 + e
    log1p_e = jnp.where(u == 1.0, e, jnp.log(u) * (e / (u - 1.0)))
    return jnp.maximum(x, 0.0) + log1p_e


def _log_sigmoid(x):
    return jnp.minimum(x, 0.0) - jnp.log(1.0 + jnp.exp(-jnp.abs(x)))


def _seg_masks(rows, seq_len):
    t = lax.broadcasted_iota(jnp.int32, (rows, rows), 0)
    u = lax.broadcasted_iota(jnp.int32, (rows, rows), 1)
    if seq_len == rows:
        same = None
        tril = u <= t
    else:
        shift = seq_len.bit_length() - 1
        assert 1 << shift == seq_len
        same = (t >> shift) == (u >> shift)
        tril = jnp.logical_and(same, u <= t)
    return same, tril


def _mask_to_bf16(mask, rows):
    if mask is None:
        return jnp.ones((rows, rows), BF16)
    return jnp.where(mask, 1.0, 0.0).astype(BF16)


def _row_in_seq(rows, seq_len, s, width):
    r = lax.broadcasted_iota(jnp.int32, (rows, width), 0)
    return jnp.logical_and(r >= s * seq_len, r < (s + 1) * seq_len)


def _interleave(chains, offset=0):
    waiting = list(enumerate(chains))
    active = []
    rnd = 0
    while waiting or active:
        while waiting and waiting[0][0] * offset <= rnd:
            active.append(waiting.pop(0)[1])
        alive = []
        for c in active:
            try:
                next(c)
                alive.append(c)
            except StopIteration:
                pass
        active = alive
        rnd += 1


def _run(phases):
    for _ in phases:
        pass


def _pad_rows(x, rows):
    if x.shape[0] >= rows:
        return x
    return jnp.concatenate([x, jnp.zeros((rows - x.shape[0], x.shape[1]), x.dtype)], axis=0)


def _gla_phases(q, k, v, lf, get_state, put_state, seq_len, outs):
    rows = q.shape[0]
    n_seq = rows // seq_len
    same, tril = _seg_masks(rows, seq_len)
    tril_b = _mask_to_bf16(tril, rows)
    same_b = _mask_to_bf16(same, rows)

    b = _dot3(tril_b, lf)
    if n_seq == 1:
        bl = jnp.broadcast_to(b[rows - 1:rows, :], b.shape)
    else:
        bl = _dot3(same_b, lf)
    qt = q * jnp.exp(b)
    kt = k * jnp.exp(-b)
    kd = k * jnp.exp(bl - b)
    bl_t = _pad_rows(bl, 128).T
    yield

    lane = lax.broadcasted_iota(jnp.int32, (rows, GLA_KW), 1)
    lhs = jnp.concatenate(
        [jnp.where((lane >= h * GLA_DK) & (lane < (h + 1) * GLA_DK), qt, 0.0)
         for h in range(GLA_HEADS)], axis=0).astype(BF16)
    att = _dot_nt(lhs, kt.astype(BF16))
    t4 = lax.broadcasted_iota(jnp.int32, (GLA_HEADS * rows, rows), 0) & (rows - 1)
    u4 = lax.broadcasted_iota(jnp.int32, (GLA_HEADS * rows, rows), 1)
    causal4 = u4 <= t4
    if n_seq > 1:
        shift = seq_len.bit_length() - 1
        causal4 = jnp.logical_and(causal4, (t4 >> shift) == (u4 >> shift))
    att = jnp.where(causal4, att, 0.0).astype(BF16)
    vb = v.astype(BF16)
    kdb = kd.astype(BF16)
    yield

    o_inter = None
    r4 = lax.broadcasted_iota(jnp.int32, (GLA_HEADS * rows, GLA_DV), 0) & (rows - 1)
    for s in range(n_seq):
        st = get_state(s)
        oi = _dot(lhs, st.astype(BF16))
        if n_seq == 1:
            o_inter = oi
            kd_s = kdb
        else:
            m4 = jnp.logical_and(r4 >= s * seq_len, r4 < (s + 1) * seq_len)
            oi = jnp.where(m4, oi, 0.0)
            o_inter = oi if o_inter is None else o_inter + oi
            kd_s = jnp.where(_row_in_seq(rows, seq_len, s, GLA_KW), kdb, jnp.zeros_like(kdb))
        blocks = []
        for hp in range(GLA_HEADS // 2):
            upd = _dot_tn(kd_s[:, 2 * hp * GLA_DK:2 * (hp + 1) * GLA_DK],
                          vb[:, 2 * hp * GLA_DV:2 * (hp + 1) * GLA_DV])
            blocks += [upd[:GLA_DK, :GLA_DV], upd[GLA_DK:, GLA_DV:]]
        upd_d = jnp.concatenate(blocks, axis=0)
        c0 = s * seq_len
        decay = jnp.exp(jnp.broadcast_to(bl_t[:, c0:c0 + 1], (GLA_KW, GLA_DV)))
        put_state(s, decay * st + upd_d)
        if n_seq > 1 and s % 4 == 3:
            yield
    yield

    for h in range(GLA_HEADS):
        o_h = _dot(att[h * rows:(h + 1) * rows], vb[:, h * GLA_DV:(h + 1) * GLA_DV])
        outs.append(o_h + o_inter[h * rows:(h + 1) * rows])


def _ssd_phases(xs, bm, cm, dt_raw_blk, dtb_blk, a_blk, d_row, get_state, put_state, seq_len, ys):
    rows = xs.shape[0]
    n_seq = rows // seq_len
    same, tril = _seg_masks(rows, seq_len)
    tril_b = _mask_to_bf16(tril, rows)
    same_b = _mask_to_bf16(same, rows)

    dt = _softplus(dt_raw_blk + dtb_blk)
    dta = dt * a_blk
    lc = _dot3(tril_b, dta)
    if n_seq == 1:
        ll = jnp.broadcast_to(lc[rows - 1:rows, :], lc.shape)
    else:
        ll = _dot3(same_b, dta)
    elc = jnp.exp(lc)
    w = jnp.exp(ll - lc) * dt
    ell = jnp.exp(ll)
    lc_t = _pad_rows(lc, 128).T[:, :rows]
    yield

    bmb = bm.astype(BF16)
    cmb = cm.astype(BF16)
    cb = [_dot_nt(cmb[:, g * SSD_STATE:(g + 1) * SSD_STATE],
                  bmb[:, g * SSD_STATE:(g + 1) * SSD_STATE]) for g in range(SSD_GROUPS)]
    yield

    lane128 = lax.broadcasted_iota(jnp.int32, (rows, 128), 1)
    lo128 = lane128 < SSD_HEAD_DIM
    sub128 = lax.broadcasted_iota(jnp.int32, (128, 128), 0) < SSD_HEAD_DIM

    def col(x, h, width):
        return jnp.broadcast_to(x[:, DT_LANE + h:DT_LANE + h + 1], (x.shape[0], width))

    heads_per_group = SSD_HEADS // SSD_GROUPS
    for j in range(SSD_HEADS // 2):
        g = (2 * j) // heads_per_group
        xs_p = xs[:, 128 * j:128 * (j + 1)]
        dt_p = jnp.where(lo128, col(dt, 2 * j, 128), col(dt, 2 * j + 1, 128))
        xdt = xs_p * dt_p
        x_lo = jnp.where(lo128, xdt, 0.0).astype(BF16)
        x_hi = jnp.where(lo128, 0.0, xdt).astype(BF16)
        y_p = None
        for h, x_half in ((2 * j, x_lo), (2 * j + 1, x_hi)):
            seg = col(lc, h, rows) - jnp.broadcast_to(lc_t[DT_LANE + h:DT_LANE + h + 1, :], (rows, rows))
            w_h = jnp.where(tril, cb[g] * jnp.exp(seg), 0.0).astype(BF16)
            y_h = _dot(w_h, x_half)
            y_p = y_h if y_p is None else y_p + y_h
            yield

        elc_p = jnp.where(lo128, col(elc, 2 * j, 128), col(elc, 2 * j + 1, 128))
        w_p = jnp.where(lo128, col(w, 2 * j, 128), col(w, 2 * j + 1, 128))
        xw = (xs_p * w_p).astype(BF16)
        bm_g = bmb[:, g * SSD_STATE:(g + 1) * SSD_STATE]
        cm_g = cmb[:, g * SSD_STATE:(g + 1) * SSD_STATE]
        y_inter = None
        for s in range(n_seq):
            st = get_state(s, j)
            ci = _dot_nt(cm_g, st.astype(BF16))
            if n_seq == 1:
                y_inter = ci
                xw_s = xw
            else:
                msk = _row_in_seq(rows, seq_len, s, 128)
                ci = jnp.where(msk, ci, 0.0)
                y_inter = ci if y_inter is None else y_inter + ci
                xw_s = jnp.where(msk, xw, jnp.zeros_like(xw))
            upd = _dot_tn(xw_s, bm_g)
            r0 = s * seq_len
            e0 = jnp.broadcast_to(ell[r0:r0 + 1, DT_LANE + 2 * j:DT_LANE + 2 * j + 1], (128, 128))
            e1 = jnp.broadcast_to(ell[r0:r0 + 1, DT_LANE + 2 * j + 1:DT_LANE + 2 * j + 2], (128, 128))
            put_state(s, j, jnp.where(sub128, e0, e1) * st + upd)
        ys.append(y_p + y_inter * elc_p + d_row[:, 128 * j:128 * (j + 1)] * xs_p)
        yield


def _gate_logf(small_blk, gw2_ref, gb_ref):
    logits = _dot(_bf(small_blk), gw2_ref[...]) + gb_ref[...]
    return _log_sigmoid(logits) * (1.0 / GLA_GATE_TAU)


def _gla_finish(o_h, g_h, gnw):
    return _rms(o_h, gnw) * _silu(g_h)


def _ssd_finish(y, z, snw):
    y = y * _silu(z)
    gs = SSD_INNER // SSD_GROUPS
    parts = [_rms(y[:, g * gs:(g + 1) * gs], snw[:, g * gs:(g + 1) * gs]) for g in range(SSD_GROUPS)]
    return jnp.concatenate(parts, axis=1)


def _memkv_kernel(mem_ref, nw_ref, wk_ref, wv_ref, k_ref, v_ref, kb_ref, vb_ref):
    mn = _bf(_rms(mem_ref[0], nw_ref[...]))
    mk = _dot(mn, wk_ref[...])
    mv = _dot(mn, wv_ref[...])
    for h in range(XA_HEADS):
        k_ref[0, 0, :, h, :] = mk[:, h * XA_HEAD_DIM:(h + 1) * XA_HEAD_DIM]
        v_ref[0, 0, :, h, :] = mv[:, h * XA_HEAD_DIM:(h + 1) * XA_HEAD_DIM]
    kb_ref[0] = _bf(mk)
    vb_ref[0] = _bf(mv)


def _const_spec(shape):
    nd = len(shape)
    return pl.BlockSpec(shape, lambda *_: (0,) * nd, pipeline_mode=pl.Buffered(1))


def _memkv(mem, nw, wk, wv):
    b, m, d = mem.shape
    blk = pl.BlockSpec((1, m, d), lambda i: (i, 0, 0))
    cache_blk = pl.BlockSpec((1, 1, m, XA_HEADS, XA_HEAD_DIM), lambda i: (0, i, 0, 0, 0))
    cache_shape = jax.ShapeDtypeStruct((1, b, m, XA_HEADS, XA_HEAD_DIM), F32)
    return pl.pallas_call(
        _memkv_kernel,
        grid=(b,),
        in_specs=[blk, _const_spec((1, d)), _const_spec((d, d)), _const_spec((d, d))],
        out_specs=[cache_blk, cache_blk, blk, blk],
        out_shape=[cache_shape] * 2 + [jax.ShapeDtypeStruct((b, m, d), BF16)] * 2,
        compiler_params=pltpu.CompilerParams(dimension_semantics=("arbitrary",),
                                             vmem_limit_bytes=VMEM_LIMIT),
        name="mem_kv",
    )(mem, nw, wk, wv)


PROJ_CHUNK = 512


def _mixer_prompt_chain(x_ref, h_ref, proj_s, xpad_s, mixed_s, sg_s, ss_s,
                        lnpre_ref, win_ref, gw2_ref, gb_ref, gnw_ref, cw_ref, cb_ref,
                        dtb_ref, a_ref, d_ref, snw_ref, wout_ref, lnpost_ref):
    rows = TILE
    x = x_ref[...]
    hn = _rms(x, lnpre_ref[...]).astype(BF16)
    yield

    def proj_cols(c0, width):
        proj_s[:, c0:c0 + width] = _dot(hn, win_ref[:, c0:c0 + width])

    def late_proj():
        for c0 in (COL_Z, COL_Q, COL_V, COL_G):
            proj_cols(c0, PROJ_CHUNK)
            yield

    side = [late_proj()]

    def tick():
        if side[0] is not None:
            try:
                next(side[0])
            except StopIteration:
                side[0] = None

    for c in range(SSD_CONV_CH // PROJ_CHUNK):
        proj_cols(COL_XBC + c * PROJ_CHUNK, PROJ_CHUNK)
        yield
    proj_cols(COL_SMALL, 128)

    xa = []
    for c in range(SSD_CONV_CH // PROJ_CHUNK):
        cs = slice(c * PROJ_CHUNK, (c + 1) * PROJ_CHUNK)
        xpad_s[8:8 + rows, cs] = proj_s[:, COL_XBC + c * PROJ_CHUNK:COL_XBC + (c + 1) * PROJ_CHUNK]
        conv = cb_ref[:, cs]
        for j in range(SSD_CONV):
            conv = conv + xpad_s[5 + j:5 + j + rows, cs] * cw_ref[j:j + 1, cs]
        xpad_s[5:8, cs] = xpad_s[5 + rows:8 + rows, cs]
        xa.append(_silu(conv))
        tick()
        yield

    small = proj_s[:, COL_SMALL:COL_SMALL + 128]
    logf = _gate_logf(small, gw2_ref, gb_ref)

    def get_ss(s, j):
        return ss_s[128 * j:128 * (j + 1), :]

    def put_ss(s, j, val):
        ss_s[128 * j:128 * (j + 1), :] = val

    ys = []
    ssd = _ssd_phases(xa[0], xa[1][:, :SSD_GROUPS * SSD_STATE], xa[1][:, SSD_GROUPS * SSD_STATE:],
                      small, dtb_ref[...], a_ref[...], d_ref[...], get_ss, put_ss, rows, ys)
    for i, _ in enumerate(ssd):
        if i % 2 == 0:
            tick()
        yield
    while side[0] is not None:
        tick()
        yield
    mixed_s[:, GLA_VW:] = _ssd_finish(jnp.concatenate(ys, axis=1), proj_s[:, COL_Z:COL_Z + SSD_INNER],
                                      snw_ref[...]).astype(BF16)
    yield

    def get_sg(s):
        return sg_s[...]

    def put_sg(s, val):
        sg_s[...] = val

    for c in range(rows // GLA_CHUNK):
        r0 = c * GLA_CHUNK
        sl = slice(r0, r0 + GLA_CHUNK)
        outs = []
        yield from _gla_phases(proj_s[sl, COL_Q:COL_Q + GLA_KW], proj_s[sl, COL_K:COL_K + GLA_KW],
                               proj_s[sl, COL_V:COL_V + GLA_VW], logf[sl], get_sg, put_sg, GLA_CHUNK, outs)
        for h in range(GLA_HEADS):
            g_h = proj_s[sl, COL_G + h * GLA_DV:COL_G + (h + 1) * GLA_DV]
            mixed_s[sl, h * GLA_DV:(h + 1) * GLA_DV] = _gla_finish(outs[h], g_h, gnw_ref[...]).astype(BF16)
        yield

    mixed = mixed_s[...]
    ms = []
    for c in range(D_MODEL // PROJ_CHUNK):
        ms.append(_dot(mixed, wout_ref[:, c * PROJ_CHUNK:(c + 1) * PROJ_CHUNK]))
        yield
    h_ref[...] = x + _rms(jnp.concatenate(ms, axis=1), lnpost_ref[...])


N_MIXER_WEIGHTS = 13
N_MIXER_SCRATCH = 5
MIX_OFFSET = 3


def _mixer_prompt_kernel(x_ref, *rest):
    consts = rest[:N_MIXER_WEIGHTS]
    h_ref, sg_out, ss_out, conv_out = rest[N_MIXER_WEIGHTS:N_MIXER_WEIGHTS + 4]
    scratch = rest[N_MIXER_WEIGHTS + 4:]
    per_seq = [scratch[i * N_MIXER_SCRATCH:(i + 1) * N_MIXER_SCRATCH] for i in range(MIX_SEQS)]
    t = pl.program_id(1)

    @pl.when(t == 0)
    def _():
        for _, xpad_s, _, sg_s, ss_s in per_seq:
            sg_s[...] = jnp.zeros_like(sg_s)
            ss_s[...] = jnp.zeros_like(ss_s)
            xpad_s[0:8, :] = jnp.zeros((8, SSD_CONV_CH), F32)

    _interleave([_mixer_prompt_chain(x_ref.at[i], h_ref.at[i], *per_seq[i], *consts) for i in range(MIX_SEQS)],
                offset=MIX_OFFSET)

    @pl.when(t == pl.num_programs(1) - 1)
    def _():
        for i, (_, xpad_s, _, sg_s, ss_s) in enumerate(per_seq):
            sg_out[i] = sg_s[...].reshape(GLA_HEADS, GLA_DK, GLA_DV)
            ss_out[i] = ss_s[...].reshape(SSD_HEADS, SSD_HEAD_DIM, SSD_STATE)
            conv_out[i] = xpad_s[5:8, :]


def _mixer_prompt(x, p):
    b, t, d = x.shape
    nt = t // TILE
    consts = [p["ln_mix_pre"], p["w_in"], p["gw2"], p["gb"], p["gnw"], p["conv_w"], p["conv_b"],
              p["dtb"], p["a_blk"], p["d_row"], p["snw"], p["w_out"], p["ln_mix_post"]]
    assert len(consts) == N_MIXER_WEIGHTS
    tok = pl.BlockSpec((MIX_SEQS, TILE, d), lambda i, j: (i, j, 0))
    return pl.pallas_call(
        _mixer_prompt_kernel,
        grid=(b // MIX_SEQS, nt),
        in_specs=[tok] + [_const_spec(c.shape) for c in consts],
        out_specs=[tok,
                   pl.BlockSpec((MIX_SEQS, GLA_HEADS, GLA_DK, GLA_DV), lambda i, j: (i, 0, 0, 0)),
                   pl.BlockSpec((MIX_SEQS, SSD_HEADS, SSD_HEAD_DIM, SSD_STATE), lambda i, j: (i, 0, 0, 0)),
                   pl.BlockSpec((MIX_SEQS, SSD_CONV - 1, SSD_CONV_CH), lambda i, j: (i, 0, 0))],
        out_shape=[jax.ShapeDtypeStruct((b, t, d), F32),
                   jax.ShapeDtypeStruct((b, GLA_HEADS, GLA_DK, GLA_DV), F32),
                   jax.ShapeDtypeStruct((b, SSD_HEADS, SSD_HEAD_DIM, SSD_STATE), F32),
                   jax.ShapeDtypeStruct((b, SSD_CONV - 1, SSD_CONV_CH), F32)],
        scratch_shapes=[pltpu.VMEM((TILE, PROJ_COLS), F32),
                        pltpu.VMEM((TILE + 8, SSD_CONV_CH), F32),
                        pltpu.VMEM((TILE, D_MODEL), BF16),
                        pltpu.VMEM((GLA_KW, GLA_DV), F32),
                        pltpu.VMEM((SSD_INNER, SSD_STATE), F32)] * MIX_SEQS,
        compiler_params=pltpu.CompilerParams(dimension_semantics=("arbitrary", "arbitrary"),
                                             vmem_limit_bytes=VMEM_LIMIT),
        name="mixer_prompt",
    )(x, *consts)


def _softmax_rows(s):
    e = jnp.exp(s - jnp.max(s, axis=-1, keepdims=True))
    return e / jnp.sum(e, axis=-1, keepdims=True)


def _ffn(h, lnpre, lnpost, wg_ref, wu_ref, wd_ref):
    hf = _bf(_rms(h, lnpre))
    gt = _dot(hf, wg_ref[...])
    up = _dot(hf, wu_ref[...])
    f = _dot(_bf(_silu(gt) * up), wd_ref[...])
    return h + _rms(f, lnpost)


FFN_CHUNK = 256
FFN_OFFSET = 1


def _attn_ffn_chain(h_ref, y_ref, mk_ref, mv_ref, lnxa_ref, wq_ref, wo_ref, lnxap_ref,
                    lnf_ref, lnfp_ref, wg_ref, wu_ref, wd_ref):
    h = h_ref[...]
    hn = _rms(h, lnxa_ref[...]).astype(BF16)
    yield
    heads_per_chunk = PROJ_CHUNK // XA_HEAD_DIM
    outs = []
    for c in range(D_MODEL // PROJ_CHUNK):
        q = _dot(hn, wq_ref[:, c * PROJ_CHUNK:(c + 1) * PROJ_CHUNK]).astype(BF16)
        yield
        for hh in range(heads_per_chunk):
            hd = c * heads_per_chunk + hh
            sl = slice(hd * XA_HEAD_DIM, (hd + 1) * XA_HEAD_DIM)
            s = _dot_nt(q[:, hh * XA_HEAD_DIM:(hh + 1) * XA_HEAD_DIM], mk_ref[0, :, sl])
            yield
            pr = _softmax_rows(s).astype(BF16)
            yield
            outs.append(_dot(pr, mv_ref[0, :, sl]).astype(BF16))
    a = jnp.concatenate(outs, axis=1)
    yield
    parts = []
    for c in range(D_MODEL // PROJ_CHUNK):
        parts.append(_dot(a, wo_ref[:, c * PROJ_CHUNK:(c + 1) * PROJ_CHUNK]))
        yield
    h = h + _rms(jnp.concatenate(parts, axis=1), lnxap_ref[...])
    hf = _rms(h, lnf_ref[...]).astype(BF16)
    yield
    acts = []
    for c in range(D_FF // FFN_CHUNK):
        cs = slice(c * FFN_CHUNK, (c + 1) * FFN_CHUNK)
        gt = _dot(hf, wg_ref[:, cs])
        up = _dot(hf, wu_ref[:, cs])
        yield
        acts.append((_silu(gt) * up).astype(BF16))
    act = jnp.concatenate(acts, axis=1)
    yield
    parts = []
    for c in range(D_MODEL // PROJ_CHUNK):
        parts.append(_dot(act, wd_ref[:, c * PROJ_CHUNK:(c + 1) * PROJ_CHUNK]))
        yield
    y_ref[...] = h + _rms(jnp.concatenate(parts, axis=1), lnfp_ref[...])


def _attn_ffn_prompt_kernel(h_ref, mk_ref, mv_ref, *rest):
    consts, y_ref = rest[:-1], rest[-1]
    chains = []
    for sub in range(FFN_SUBTILES):
        rs = pl.ds(sub * TILE, TILE)
        chains.append(_attn_ffn_chain(h_ref.at[0, rs], y_ref.at[0, rs], mk_ref, mv_ref, *consts))
    _interleave(chains, offset=FFN_OFFSET)


def _attn_ffn_prompt(h, mkb, mvb, p):
    b, t, d = h.shape
    rows = FFN_SUBTILES * TILE
    consts = [p["ln_xa_pre"], p["w_xq"], p["w_xo"], p["ln_xa_post"], p["ln_ffn_pre"], p["ln_ffn_post"],
              p["w_gate"], p["w_up"], p["w_down"]]
    tok = pl.BlockSpec((1, rows, d), lambda i, j: (i, j, 0))
    mem = pl.BlockSpec((1, mkb.shape[1], d), lambda i, j: (i, 0, 0))
    return pl.pallas_call(
        _attn_ffn_prompt_kernel,
        grid=(b, t // rows),
        in_specs=[tok, mem, mem] + [_const_spec(c.shape) for c in consts],
        out_specs=tok,
        out_shape=jax.ShapeDtypeStruct((b, t, d), F32),
        compiler_params=pltpu.CompilerParams(dimension_semantics=("arbitrary", "arbitrary"),
                                             vmem_limit_bytes=VMEM_LIMIT),
        name="attn_ffn_prompt",
    )(h, mkb, mvb, *consts)


def _proj_sample_kernel(x_ref, lnpre_ref, win_ref, proj_ref):
    proj_ref[...] = _dot(_bf(_rms(x_ref[...], lnpre_ref[...])), win_ref[...])


def _proj_sample(x, p):
    n, d = x.shape
    return pl.pallas_call(
        _proj_sample_kernel,
        grid=(1,),
        in_specs=[_const_spec((n, d)), _const_spec((1, d)), _const_spec(p["w_in"].shape)],
        out_specs=pl.BlockSpec((n, PROJ_COLS), lambda i: (0, 0)),
        out_shape=jax.ShapeDtypeStruct((n, PROJ_COLS), F32),
        compiler_params=pltpu.CompilerParams(dimension_semantics=("arbitrary",),
                                             vmem_limit_bytes=VMEM_LIMIT),
        name="proj_sample",
    )(x, p["ln_mix_pre"], p["w_in"])


def _mixer_sample_kernel(seq_len, proj_ref, tail_ref, sg_ref, ss_ref, gw2_ref, gb_ref, gnw_ref,
                         cw_ref, cb_ref, dtb_ref, a_ref, d_ref, snw_ref,
                         mixed_ref, sg_out, ss_out, xpad_s, tpad_s):
    rows = proj_ref.shape[0]
    n_seq = rows // seq_len
    small = proj_ref[:, COL_SMALL:COL_SMALL + 128]
    logf = _gate_logf(small, gw2_ref, gb_ref)

    xbc = proj_ref[:, COL_XBC:COL_XBC + SSD_CONV_CH]
    xpad_s[0:8, :] = jnp.zeros((8, SSD_CONV_CH), F32)
    xpad_s[8:8 + rows, :] = xbc
    tpad_s[0:rows, :] = tail_ref[...]
    tpad_s[rows:rows + 8, :] = jnp.zeros((8, SSD_CONV_CH), F32)
    pos = lax.broadcasted_iota(jnp.int32, (rows, SSD_CONV_CH), 0) & (seq_len - 1)
    conv = cb_ref[...]
    for j in range(SSD_CONV - 1):
        back = SSD_CONV - 1 - j
        prev = jnp.where(pos >= back, xpad_s[8 - back:8 - back + rows, :],
                         tpad_s[seq_len - back:seq_len - back + rows, :])
        conv = conv + prev * cw_ref[j:j + 1, :]
    conv = conv + xbc * cw_ref[SSD_CONV - 1:SSD_CONV, :]
    xa = _silu(conv)

    def get_ss(s, j):
        return ss_ref[s, 2 * j:2 * j + 2].reshape(2 * SSD_HEAD_DIM, SSD_STATE)

    def put_ss(s, j, val):
        ss_out[s, 2 * j:2 * j + 2] = val.reshape(2, SSD_HEAD_DIM, SSD_STATE)

    ys = []
    _run(_ssd_phases(xa[:, :SSD_INNER], xa[:, SSD_INNER:SSD_INNER + 256], xa[:, SSD_INNER + 256:],
                     small, dtb_ref[...], a_ref[...], d_ref[...], get_ss, put_ss, seq_len, ys))
    mixed_ref[:, GLA_VW:] = _ssd_finish(jnp.concatenate(ys, axis=1), proj_ref[:, COL_Z:COL_Z + SSD_INNER],
                                        snw_ref[...])

    def get_sg(s):
        return sg_ref[s].reshape(GLA_KW, GLA_DV)

    def put_sg(s, val):
        sg_out[s] = val.reshape(GLA_HEADS, GLA_DK, GLA_DV)

    outs = []
    _run(_gla_phases(proj_ref[:, COL_Q:COL_Q + GLA_KW], proj_ref[:, COL_K:COL_K + GLA_KW],
                     proj_ref[:, COL_V:COL_V + GLA_VW], logf, get_sg, put_sg, seq_len, outs))
    for h in range(GLA_HEADS):
        g_h = proj_ref[:, COL_G + h * GLA_DV:COL_G + (h + 1) * GLA_DV]
        mixed_ref[:, h * GLA_DV:(h + 1) * GLA_DV] = _gla_finish(outs[h], g_h, gnw_ref[...])


def _mixer_sample(proj, tail, sg, ss, p, seq_len):
    n = proj.shape[0]
    rows = SAMPLE_SEQS * seq_len
    nb = sg.shape[0]
    consts = [p["gw2"], p["gb"], p["gnw"], p["conv_w"], p["conv_b"], p["dtb"], p["a_blk"], p["d_row"],
              p["snw"]]
    sg_spec = pl.BlockSpec((SAMPLE_SEQS, GLA_HEADS, GLA_DK, GLA_DV), lambda i: (i, 0, 0, 0))
    ss_spec = pl.BlockSpec((SAMPLE_SEQS, SSD_HEADS, SSD_HEAD_DIM, SSD_STATE), lambda i: (i, 0, 0, 0))
    return pl.pallas_call(
        functools.partial(_mixer_sample_kernel, seq_len),
        grid=(nb // SAMPLE_SEQS,),
        in_specs=[pl.BlockSpec((rows, PROJ_COLS), lambda i: (i, 0)),
                  pl.BlockSpec((rows, SSD_CONV_CH), lambda i: (i, 0)),
                  sg_spec, ss_spec] + [_const_spec(c.shape) for c in consts],
        out_specs=[pl.BlockSpec((rows, D_MODEL), lambda i: (i, 0)), sg_spec, ss_spec],
        out_shape=[jax.ShapeDtypeStruct((n, D_MODEL), F32),
                   jax.ShapeDtypeStruct(sg.shape, F32), jax.ShapeDtypeStruct(ss.shape, F32)],
        scratch_shapes=[pltpu.VMEM((rows + 8, SSD_CONV_CH), F32),
                        pltpu.VMEM((rows + 8, SSD_CONV_CH), F32)],
        compiler_params=pltpu.CompilerParams(dimension_semantics=("arbitrary",),
                                             vmem_limit_bytes=VMEM_LIMIT),
        name="mixer_sample",
    )(proj, tail, sg, ss, *consts)


def _post_mix_sample_kernel(x_ref, mixed_ref, wout_ref, lnpost_ref, lnxa_ref, wq_ref, h_ref, q_ref):
    m = _dot(_bf(mixed_ref[...]), wout_ref[...])
    h = x_ref[...] + _rms(m, lnpost_ref[...])
    h_ref[...] = h
    q_ref[...] = _bf(_dot(_bf(_rms(h, lnxa_ref[...])), wq_ref[...]))


def _post_mix_sample(x, mixed, p):
    n, d = x.shape
    full = pl.BlockSpec((n, d), lambda i: (0, 0))
    consts = [p["w_out"], p["ln_mix_post"], p["ln_xa_pre"], p["w_xq"]]
    return pl.pallas_call(
        _post_mix_sample_kernel,
        grid=(1,),
        in_specs=[full, full] + [_const_spec(c.shape) for c in consts],
        out_specs=[full, full],
        out_shape=[jax.ShapeDtypeStruct((n, d), F32), jax.ShapeDtypeStruct((n, d), BF16)],
        compiler_params=pltpu.CompilerParams(dimension_semantics=("arbitrary",),
                                             vmem_limit_bytes=VMEM_LIMIT),
        name="post_mix_sample",
    )(x, mixed, *consts)


def _attn_sample_kernel(seq_len, q_ref, k_hbm, v_hbm, a_ref, kbuf, vbuf, sem):
    step = pl.program_id(0)
    rows = q_ref.shape[0]
    n_seq = rows // seq_len

    def copies(at_step, slot):
        out = []
        for s in range(n_seq):
            for hd in range(XA_HEADS):
                b = at_step * n_seq + s
                out.append(pltpu.make_async_copy(k_hbm.at[0, b, :, hd, :], kbuf.at[slot, s, hd], sem.at[0, slot]))
                out.append(pltpu.make_async_copy(v_hbm.at[0, b, :, hd, :], vbuf.at[slot, s, hd], sem.at[1, slot]))
        return out

    @pl.when(step == 0)
    def _():
        for cp in copies(0, 0):
            cp.start()

    @pl.when(step + 1 < pl.num_programs(0))
    def _():
        for cp in copies(step + 1, (step + 1) % 2):
            cp.start()

    slot = step % 2
    for cp in copies(step, slot):
        cp.wait()

    q = q_ref[...]
    results = {}

    def one(s, hd):
        sc = _dot_nt(q[:, hd * XA_HEAD_DIM:(hd + 1) * XA_HEAD_DIM], _bf(kbuf[slot, s, hd]))
        yield
        e = jnp.exp(sc - jnp.max(sc, axis=-1, keepdims=True))
        pr = _bf(e / jnp.sum(e, axis=-1, keepdims=True))
        yield
        results[s, hd] = _dot(pr, _bf(vbuf[slot, s, hd]))

    _interleave([one(s, hd) for s in range(n_seq) for hd in range(XA_HEADS)])
    for hd in range(XA_HEADS):
        sl = slice(hd * XA_HEAD_DIM, (hd + 1) * XA_HEAD_DIM)
        out = results[0, hd]
        for s in range(1, n_seq):
            out = jnp.where(_row_in_seq(rows, seq_len, s, XA_HEAD_DIM), results[s, hd], out)
        a_ref[:, sl] = out


def _attn_sample(q, ck, cv, seq_len):
    n, d = q.shape
    _, nb, m, nh, hd = ck.shape
    rows = ATTN_SEQS * seq_len
    tok = pl.BlockSpec((rows, d), lambda i: (i, 0))
    hbm = pl.BlockSpec(memory_space=pl.ANY)
    return pl.pallas_call(
        functools.partial(_attn_sample_kernel, seq_len),
        grid=(nb // ATTN_SEQS,),
        in_specs=[tok, hbm, hbm],
        out_specs=tok,
        out_shape=jax.ShapeDtypeStruct((n, d), F32),
        scratch_shapes=[pltpu.VMEM((2, ATTN_SEQS, nh, m, hd), F32),
                        pltpu.VMEM((2, ATTN_SEQS, nh, m, hd), F32),
                        pltpu.SemaphoreType.DMA((2, 2))],
        compiler_params=pltpu.CompilerParams(dimension_semantics=("arbitrary",),
                                             vmem_limit_bytes=VMEM_LIMIT),
        name="attn_sample",
    )(q, ck, cv)


def _post_attn_sample_kernel(h_ref, a_ref, wo_ref, lnxap_ref, lnf_ref, lnfp_ref, wg_ref, wu_ref, wd_ref,
                             y_ref):
    a = _dot(_bf(a_ref[...]), wo_ref[...])
    h = h_ref[...] + _rms(a, lnxap_ref[...])
    y_ref[...] = _ffn(h, lnf_ref[...], lnfp_ref[...], wg_ref, wu_ref, wd_ref)


def _post_attn_sample(h, a, p):
    n, d = h.shape
    full = pl.BlockSpec((n, d), lambda i: (0, 0))
    consts = [p["w_xo"], p["ln_xa_post"], p["ln_ffn_pre"], p["ln_ffn_post"], p["w_gate"], p["w_up"],
              p["w_down"]]
    return pl.pallas_call(
        _post_attn_sample_kernel,
        grid=(1,),
        in_specs=[full, full] + [_const_spec(c.shape) for c in consts],
        out_specs=full,
        out_shape=jax.ShapeDtypeStruct((n, d), F32),
        compiler_params=pltpu.CompilerParams(dimension_semantics=("arbitrary",),
                                             vmem_limit_bytes=VMEM_LIMIT),
        name="post_attn_sample",
    )(h, a, *consts)


SRC_GLR = GLA_KW + GLA_KW + GLA_VW + GLA_VW
SRC_Z = SRC_GLR + GLA_GATE_RANK
SRC_XBC = SRC_Z + SSD_INNER
SRC_DT = SRC_XBC + SSD_CONV_CH
IN_COLS = SRC_DT + SSD_HEADS


def _pack_w_in_kernel(wt_hbm, o_ref, buf, sem):
    pieces = [(0, COL_Q, SRC_GLR),
              (SRC_Z, COL_Z, SSD_INNER),
              (SRC_XBC, COL_XBC, SSD_CONV_CH),
              (SRC_GLR, COL_SMALL, GLA_GATE_RANK),
              (SRC_DT, COL_SMALL + DT_LANE, SSD_HEADS)]
    copies = [pltpu.make_async_copy(wt_hbm.at[0, pl.ds(src, n), :], buf.at[pl.ds(dst, n), :], sem.at[0])
              for src, dst, n in pieces]
    for cp in copies:
        cp.start()
    used = COL_SMALL + DT_LANE + SSD_HEADS
    buf[used:PROJ_COLS, :] = jnp.zeros((PROJ_COLS - used, D_MODEL), F32)
    for cp in copies:
        cp.wait()
    for j in range(PROJ_COLS // 128):
        blk = buf[128 * j:128 * (j + 1), :]
        if 128 * j < COL_K:
            blk = blk * (GLA_DK ** -0.5)
        o_ref[:, 128 * j:128 * (j + 1)] = blk.T.astype(BF16)


def _pack_w_in(w_in):
    wt = jnp.swapaxes(w_in, 1, 2)
    return pl.pallas_call(
        _pack_w_in_kernel,
        grid=(1,),
        in_specs=[pl.BlockSpec(memory_space=pl.ANY)],
        out_specs=pl.BlockSpec((D_MODEL, PROJ_COLS), lambda i: (0, 0)),
        out_shape=jax.ShapeDtypeStruct((D_MODEL, PROJ_COLS), BF16),
        scratch_shapes=[pltpu.VMEM((PROJ_COLS, D_MODEL), F32), pltpu.SemaphoreType.DMA((1,))],
        compiler_params=pltpu.CompilerParams(dimension_semantics=("arbitrary",),
                                             vmem_limit_bytes=VMEM_LIMIT),
        name="pack_w_in",
    )(wt)


def _pack_params(ln_mix_pre, ln_mix_post, w_in_packed, gla_gate_w2, gla_gate_b, gla_norm_w, ssd_conv_w,
                 ssd_conv_b, ssd_dt_bias, ssd_A_log, ssd_D, ssd_norm_w, w_out, ln_xa_pre, ln_xa_post,
                 mem_norm_w, w_xq, w_xk, w_xv, w_xo, ln_ffn_pre, ln_ffn_post, w_gate, w_up, w_down):

    def small_row(v):
        return jnp.zeros((1, 128), F32).at[0, DT_LANE:DT_LANE + SSD_HEADS].set(v)

    row = lambda v: v.reshape(1, -1)
    return {
        "ln_mix_pre": row(ln_mix_pre), "ln_mix_post": row(ln_mix_post),
        "w_in": w_in_packed,
        "gw2": _bf(jnp.zeros((128, GLA_KW), F32).at[:GLA_GATE_RANK].set(gla_gate_w2)),
        "gb": row(gla_gate_b), "gnw": row(gla_norm_w),
        "conv_w": ssd_conv_w, "conv_b": row(ssd_conv_b),
        "dtb": small_row(ssd_dt_bias),
        "a_blk": small_row(-jnp.exp(ssd_A_log)),
        "d_row": jnp.repeat(ssd_D, SSD_HEAD_DIM).reshape(1, SSD_INNER),
        "snw": row(ssd_norm_w),
        "w_out": _bf(w_out),
        "ln_xa_pre": row(ln_xa_pre), "ln_xa_post": row(ln_xa_post), "mem_norm_w": row(mem_norm_w),
        "w_xq": _bf(w_xq * (XA_HEAD_DIM ** -0.5)), "w_xk": _bf(w_xk), "w_xv": _bf(w_xv), "w_xo": _bf(w_xo),
        "ln_ffn_pre": row(ln_ffn_pre), "ln_ffn_post": row(ln_ffn_post),
        "w_gate": _bf(w_gate), "w_up": _bf(w_up), "w_down": _bf(w_down),
    }


def kernel(x_prompt, x_sample, mem_prompt, state_gla, state_ssm, state_conv, cache_mem_k, cache_mem_v,
           ln_mix_pre, ln_mix_post, w_in, gla_gate_w2, gla_gate_b, gla_norm_w, ssd_conv_w, ssd_conv_b,
           ssd_dt_bias, ssd_A_log, ssd_D, ssd_norm_w, w_out, ln_xa_pre, ln_xa_post, mem_norm_w,
           w_xq, w_xk, w_xv, w_xo, ln_ffn_pre, ln_ffn_post, w_gate, w_up, w_down):
    assert w_in.shape[0] == 1, "single-layer kernel"
    layer = [a[0] for a in (ln_mix_pre, ln_mix_post, w_in, gla_gate_w2, gla_gate_b, gla_norm_w, ssd_conv_w,
                            ssd_conv_b, ssd_dt_bias, ssd_A_log, ssd_D, ssd_norm_w, w_out, ln_xa_pre,
                            ln_xa_post, mem_norm_w, w_xq, w_xk, w_xv, w_xo, ln_ffn_pre, ln_ffn_post,
                            w_gate, w_up, w_down)]
    layer[2] = _pack_w_in(w_in)
    p = _pack_params(*layer)
    bp, mem_len, _ = mem_prompt.shape
    bs, ts, _ = x_sample.shape
    assert ts == SSD_CONV, "the padded conv-tail layout assumes one new row per conv tap"

    mk, mv, mkb, mvb = _memkv(mem_prompt, p["mem_norm_w"], p["w_xk"], p["w_xv"])
    h_p, gla_p, ssm_p, conv_p = _mixer_prompt(x_prompt, p)
    y_p = _attn_ffn_prompt(h_p, mkb, mvb, p)

    xs = x_sample.reshape(bs * ts, D_MODEL)
    proj = _proj_sample(xs, p)
    tail = jnp.pad(state_conv[0], ((0, 0), (1, 0), (0, 0))).reshape(bs * ts, SSD_CONV_CH)
    mixed, gla_s, ssm_s = _mixer_sample(proj, tail, state_gla[0], state_ssm[0], p, ts)
    h_s, q_s = _post_mix_sample(xs, mixed, p)
    a_s = _attn_sample(q_s, cache_mem_k, cache_mem_v, ts)
    y_s = _post_attn_sample(h_s, a_s, p)
    conv_s = proj[:, COL_XBC:COL_XBC + SSD_CONV_CH].reshape(bs, ts, SSD_CONV_CH)[:, ts - (SSD_CONV - 1):]

    return (y_p, y_s.reshape(bs, ts, D_MODEL), gla_p[None], ssm_p[None], conv_p[None],
            mk, mv, gla_s[None], ssm_s[None], conv_s[None])
```

```python
import functools

import jax
import jax.numpy as jnp
from jax import lax
from jax.experimental import pallas as pl
from jax.experimental.pallas import tpu as pltpu

F32 = jnp.float32
BF16 = jnp.bfloat16

D_MODEL = 1024
GLA_HEADS = 4
GLA_DK = 64
GLA_DV = 128
GLA_KW = GLA_HEADS * GLA_DK
GLA_VW = GLA_HEADS * GLA_DV
GLA_GATE_RANK = 16
GLA_GATE_TAU = 16.0
SSD_INNER = 512
SSD_HEAD_DIM = 64
SSD_HEADS = 8
SSD_GROUPS = 2
SSD_STATE = 128
SSD_CONV = 4
SSD_CONV_CH = 1024
XA_HEADS = 4
XA_HEAD_DIM = 256
D_FF = 2816
EPS = 1e-6

COL_Q = 0
COL_K = 256
COL_V = 512
COL_G = 1024
COL_Z = 1536
COL_XBC = 2048
COL_SMALL = 3072
DT_LANE = GLA_GATE_RANK
PROJ_COLS = 3200

GLA_CHUNK = 64
TILE = 256
FFN_SUBTILES = 4
MIX_SEQS = 4
SAMPLE_SEQS = 16
ATTN_SEQS = 4
VMEM_LIMIT = 56 * 1024 * 1024


def _bf(x):
    return x.astype(BF16)


def _dot(a, b):
    return jnp.dot(a, b, preferred_element_type=F32)


def _dot_nt(a, b):
    return lax.dot_general(a, b, (((1,), (1,)), ((), ())), preferred_element_type=F32)


def _dot_tn(a, b):
    return lax.dot_general(a, b, (((0,), (0,)), ((), ())), preferred_element_type=F32)


def _dot3(m, a):
    hi = _bf(a)
    r = a - hi.astype(F32)
    mid = _bf(r)
    lo = _bf(r - mid.astype(F32))
    return _dot(m, hi) + _dot(m, mid) + _dot(m, lo)


def _rms(x, w):
    ms = jnp.mean(x * x, axis=-1, keepdims=True)
    return x * lax.rsqrt(ms + EPS) * w


def _silu(x):
    return x / (1.0 + jnp.exp(-x))


def _softplus(x):
    e = jnp.exp(-jnp.abs(x))
    u = 1.0 + e
    log1p_e = jnp.where(u == 1.0, e, jnp.log(u) * (e / (u - 1.0)))
    return jnp.maximum(x, 0.0) + log1p_e


def _log_sigmoid(x):
    return jnp.minimum(x, 0.0) - jnp.log(1.0 + jnp.exp(-jnp.abs(x)))


def _seg_masks(rows, seq_len):
    t = lax.broadcasted_iota(jnp.int32, (rows, rows), 0)
    u = lax.broadcasted_iota(jnp.int32, (rows, rows), 1)
    if seq_len == rows:
        same = None
        tril = u <= t
    else:
        shift = seq_len.bit_length() - 1
        assert 1 << shift == seq_len
        same = (t >> shift) == (u >> shift)
        tril = jnp.logical_and(same, u <= t)
    return same, tril


def _mask_to_bf16(mask, rows):
    if mask is None:
        return jnp.ones((rows, rows), BF16)
    return jnp.where(mask, 1.0, 0.0).astype(BF16)


def _row_in_seq(rows, seq_len, s, width):
    r = lax.broadcasted_iota(jnp.int32, (rows, width), 0)
    return jnp.logical_and(r >= s * seq_len, r < (s + 1) * seq_len)


def _interleave(chains, offset=0):
    waiting = list(enumerate(chains))
    active = []
    rnd = 0
    while waiting or active:
        while waiting and waiting[0][0] * offset <= rnd:
            active.append(waiting.pop(0)[1])
        alive = []
        for c in active:
            try:
                next(c)
                alive.append(c)
            except StopIteration:
                pass
        active = alive
        rnd += 1


def _run(phases):
    for _ in phases:
        pass


def _pad_rows(x, rows):
    if x.shape[0] >= rows:
        return x
    return jnp.concatenate([x, jnp.zeros((rows - x.shape[0], x.shape[1]), x.dtype)], axis=0)


def _gla_phases(q, k, v, lf, get_state, put_state, seq_len, outs):
    rows = q.shape[0]
    n_seq = rows // seq_len
    same, tril = _seg_masks(rows, seq_len)
    tril_b = _mask_to_bf16(tril, rows)
    same_b = _mask_to_bf16(same, rows)

    b = _dot3(tril_b, lf)
    if n_seq == 1:
        bl = jnp.broadcast_to(b[rows - 1:rows, :], b.shape)
    else:
        bl = _dot3(same_b, lf)
    qt = q * jnp.exp(b)
    kt = k * jnp.exp(-b)
    kd = k * jnp.exp(bl - b)
    bl_t = _pad_rows(bl, 128).T
    yield

    lane = lax.broadcasted_iota(jnp.int32, (rows, GLA_KW), 1)
    lhs = jnp.concatenate(
        [jnp.where((lane >= h * GLA_DK) & (lane < (h + 1) * GLA_DK), qt, 0.0)
         for h in range(GLA_HEADS)], axis=0).astype(BF16)
    att = _dot_nt(lhs, kt.astype(BF16))
    t4 = lax.broadcasted_iota(jnp.int32, (GLA_HEADS * rows, rows), 0) & (rows - 1)
    u4 = lax.broadcasted_iota(jnp.int32, (GLA_HEADS * rows, rows), 1)
    causal4 = u4 <= t4
    if n_seq > 1:
        shift = seq_len.bit_length() - 1
        causal4 = jnp.logical_and(causal4, (t4 >> shift) == (u4 >> shift))
    att = jnp.where(causal4, att, 0.0).astype(BF16)
    vb = v.astype(BF16)
    kdb = kd.astype(BF16)
    yield

    o_inter = None
    r4 = lax.broadcasted_iota(jnp.int32, (GLA_HEADS * rows, GLA_DV), 0) & (rows - 1)
    for s in range(n_seq):
        st = get_state(s)
        oi = _dot(lhs, st.astype(BF16))
        if n_seq == 1:
            o_inter = oi
            kd_s = kdb
        else:
            m4 = jnp.logical_and(r4 >= s * seq_len, r4 < (s + 1) * seq_len)
            oi = jnp.where(m4, oi, 0.0)
            o_inter = oi if o_inter is None else o_inter + oi
            kd_s = jnp.where(_row_in_seq(rows, seq_len, s, GLA_KW), kdb, jnp.zeros_like(kdb))
        blocks = []
        for hp in range(GLA_HEADS // 2):
            upd = _dot_tn(kd_s[:, 2 * hp * GLA_DK:2 * (hp + 1) * GLA_DK],
                          vb[:, 2 * hp * GLA_DV:2 * (hp + 1) * GLA_DV])
            blocks += [upd[:GLA_DK, :GLA_DV], upd[GLA_DK:, GLA_DV:]]
        upd_d = jnp.concatenate(blocks, axis=0)
        c0 = s * seq_len
        decay = jnp.exp(jnp.broadcast_to(bl_t[:, c0:c0 + 1], (GLA_KW, GLA_DV)))
        put_state(s, decay * st + upd_d)
        if n_seq > 1 and s % 4 == 3:
            yield
    yield

    for h in range(GLA_HEADS):
        o_h = _dot(att[h * rows:(h + 1) * rows], vb[:, h * GLA_DV:(h + 1) * GLA_DV])
        outs.append(o_h + o_inter[h * rows:(h + 1) * rows])


def _ssd_phases(xs, bm, cm, dt_raw_blk, dtb_blk, a_blk, d_row, get_state, put_state, seq_len, ys):
    rows = xs.shape[0]
    n_seq = rows // seq_len
    same, tril = _seg_masks(rows, seq_len)
    tril_b = _mask_to_bf16(tril, rows)
    same_b = _mask_to_bf16(same, rows)

    dt = _softplus(dt_raw_blk + dtb_blk)
    dta = dt * a_blk
    lc = _dot3(tril_b, dta)
    if n_seq == 1:
        ll = jnp.broadcast_to(lc[rows - 1:rows, :], lc.shape)
    else:
        ll = _dot3(same_b, dta)
    elc = jnp.exp(lc)
    w = jnp.exp(ll - lc) * dt
    ell = jnp.exp(ll)
    lc_t = _pad_rows(lc, 128).T[:, :rows]
    yield

    bmb = bm.astype(BF16)
    cmb = cm.astype(BF16)
    cb = [_dot_nt(cmb[:, g * SSD_STATE:(g + 1) * SSD_STATE],
                  bmb[:, g * SSD_STATE:(g + 1) * SSD_STATE]) for g in range(SSD_GROUPS)]
    yield

    lane128 = lax.broadcasted_iota(jnp.int32, (rows, 128), 1)
    lo128 = lane128 < SSD_HEAD_DIM
    sub128 = lax.broadcasted_iota(jnp.int32, (128, 128), 0) < SSD_HEAD_DIM

    def col(x, h, width):
        return jnp.broadcast_to(x[:, DT_LANE + h:DT_LANE + h + 1], (x.shape[0], width))

    heads_per_group = SSD_HEADS // SSD_GROUPS
    for j in range(SSD_HEADS // 2):
        g = (2 * j) // heads_per_group
        xs_p = xs[:, 128 * j:128 * (j + 1)]
        dt_p = jnp.where(lo128, col(dt, 2 * j, 128), col(dt, 2 * j + 1, 128))
        xdt = xs_p * dt_p
        x_lo = jnp.where(lo128, xdt, 0.0).astype(BF16)
        x_hi = jnp.where(lo128, 0.0, xdt).astype(BF16)
        y_p = None
        for h, x_half in ((2 * j, x_lo), (2 * j + 1, x_hi)):
            seg = col(lc, h, rows) - jnp.broadcast_to(lc_t[DT_LANE + h:DT_LANE + h + 1, :], (rows, rows))
            w_h = jnp.where(tril, cb[g] * jnp.exp(seg), 0.0).astype(BF16)
            y_h = _dot(w_h, x_half)
            y_p = y_h if y_p is None else y_p + y_h
            yield

        elc_p = jnp.where(lo128, col(elc, 2 * j, 128), col(elc, 2 * j + 1, 128))
        w_p = jnp.where(lo128, col(w, 2 * j, 128), col(w, 2 * j + 1, 128))
        xw = (xs_p * w_p).astype(BF16)
        bm_g = bmb[:, g * SSD_STATE:(g + 1) * SSD_STATE]
        cm_g = cmb[:, g * SSD_STATE:(g + 1) * SSD_STATE]
        y_inter = None
        for s in range(n_seq):
            st = get_state(s, j)
            ci = _dot_nt(cm_g, st.astype(BF16))
            if n_seq == 1:
                y_inter = ci
                xw_s = xw
            else:
                msk = _row_in_seq(rows, seq_len, s, 128)
                ci = jnp.where(msk, ci, 0.0)
                y_inter = ci if y_inter is None else y_inter + ci
                xw_s = jnp.where(msk, xw, jnp.zeros_like(xw))
            upd = _dot_tn(xw_s, bm_g)
            r0 = s * seq_len
            e0 = jnp.broadcast_to(ell[r0:r0 + 1, DT_LANE + 2 * j:DT_LANE + 2 * j + 1], (128, 128))
            e1 = jnp.broadcast_to(ell[r0:r0 + 1, DT_LANE + 2 * j + 1:DT_LANE + 2 * j + 2], (128, 128))
            put_state(s, j, jnp.where(sub128, e0, e1) * st + upd)
        ys.append(y_p + y_inter * elc_p + d_row[:, 128 * j:128 * (j + 1)] * xs_p)
        yield


def _ssd_phases_seq(xs, bm, cm, dt_raw_blk, dtb_blk, a_blk, d_row, get_state, put_state, ys):
    rows = xs.shape[0]
    _, tril = _seg_masks(rows, rows)
    tril_b = _mask_to_bf16(tril, rows)
    src = lax.broadcasted_iota(jnp.int32, (rows, rows), 0)
    dst = lax.broadcasted_iota(jnp.int32, (rows, rows), 1)
    causal_t = src <= dst

    dt = _softplus(dt_raw_blk + dtb_blk)
    lc = _dot3(tril_b, dt * a_blk)
    lc_t = lc.T
    dt_t = dt.T
    elc_t = jnp.exp(lc_t)
    w_t = jnp.exp(lc_t[:, rows - 1:rows] - lc_t) * dt_t
    ell = jnp.exp(lc[rows - 1:rows, :])
    yield

    xs_t = xs.T
    bmb = bm.astype(BF16)
    cmb = cm.astype(BF16)
    cb_t = [_dot_nt(bmb[:, g * SSD_STATE:(g + 1) * SSD_STATE],
                    cmb[:, g * SSD_STATE:(g + 1) * SSD_STATE]) for g in range(SSD_GROUPS)]
    yield

    heads_per_group = SSD_HEADS // SSD_GROUPS
    y_t = []
    for h in range(SSD_HEADS):
        g = h // heads_per_group
        lane = DT_LANE + h
        xs_h = xs_t[h * SSD_HEAD_DIM:(h + 1) * SSD_HEAD_DIM, :]
        seg_t = lc_t[lane:lane + 1, :] - jnp.broadcast_to(lc[:, lane:lane + 1], (rows, rows))
        w_h = jnp.where(causal_t, cb_t[g] * jnp.exp(seg_t), 0.0).astype(BF16)
        y_h = _dot((xs_h * dt_t[lane:lane + 1, :]).astype(BF16), w_h)
        st = get_state(h)
        ci = _dot_nt(st.astype(BF16), cmb[:, g * SSD_STATE:(g + 1) * SSD_STATE])
        y_t.append(y_h + ci * elc_t[lane:lane + 1, :])
        upd = _dot((xs_h * w_t[lane:lane + 1, :]).astype(BF16), bmb[:, g * SSD_STATE:(g + 1) * SSD_STATE])
        put_state(h, jnp.broadcast_to(ell[:, lane:lane + 1], st.shape) * st + upd)
        yield
    ys.append(jnp.concatenate(y_t, axis=0).T + d_row * xs)


def _gate_logf(small_blk, gw2_ref, gb_ref):
    logits = _dot(_bf(small_blk), gw2_ref[...]) + gb_ref[...]
    return _log_sigmoid(logits) * (1.0 / GLA_GATE_TAU)


def _gla_finish(o_h, g_h, gnw):
    return _rms(o_h, gnw) * _silu(g_h)


def _ssd_finish(y, z, snw):
    y = y * _silu(z)
    gs = SSD_INNER // SSD_GROUPS
    parts = [_rms(y[:, g * gs:(g + 1) * gs], snw[:, g * gs:(g + 1) * gs]) for g in range(SSD_GROUPS)]
    return jnp.concatenate(parts, axis=1)


def _memkv_kernel(mem_ref, nw_ref, wk_ref, wv_ref, k_ref, v_ref, kb_ref, vb_ref):
    mn = _bf(_rms(mem_ref[0], nw_ref[...]))
    mk = _dot(mn, wk_ref[...])
    mv = _dot(mn, wv_ref[...])
    for h in range(XA_HEADS):
        k_ref[0, 0, :, h, :] = mk[:, h * XA_HEAD_DIM:(h + 1) * XA_HEAD_DIM]
        v_ref[0, 0, :, h, :] = mv[:, h * XA_HEAD_DIM:(h + 1) * XA_HEAD_DIM]
    kb_ref[0] = _bf(mk)
    vb_ref[0] = _bf(mv)


def _const_spec(shape):
    nd = len(shape)
    return pl.BlockSpec(shape, lambda *_: (0,) * nd, pipeline_mode=pl.Buffered(1))


def _memkv(mem, nw, wk, wv):
    b, m, d = mem.shape
    blk = pl.BlockSpec((1, m, d), lambda i: (i, 0, 0))
    cache_blk = pl.BlockSpec((1, 1, m, XA_HEADS, XA_HEAD_DIM), lambda i: (0, i, 0, 0, 0))
    cache_shape = jax.ShapeDtypeStruct((1, b, m, XA_HEADS, XA_HEAD_DIM), F32)
    return pl.pallas_call(
        _memkv_kernel,
        grid=(b,),
        in_specs=[blk, _const_spec((1, d)), _const_spec((d, d)), _const_spec((d, d))],
        out_specs=[cache_blk, cache_blk, blk, blk],
        out_shape=[cache_shape] * 2 + [jax.ShapeDtypeStruct((b, m, d), BF16)] * 2,
        compiler_params=pltpu.CompilerParams(dimension_semantics=("arbitrary",),
                                             vmem_limit_bytes=VMEM_LIMIT),
        name="mem_kv",
    )(mem, nw, wk, wv)


PROJ_CHUNK = 512


def _mixer_prompt_chain(x_ref, h_ref, proj_s, xpad_s, mixed_s, sg_s, ss_s,
                        lnpre_ref, win_ref, gw2_ref, gb_ref, gnw_ref, cw_ref, cb_ref,
                        dtb_ref, a_ref, d_ref, snw_ref, wout_ref, lnpost_ref):
    rows = TILE
    x = x_ref[...]
    hn = _rms(x, lnpre_ref[...]).astype(BF16)
    yield

    def proj_cols(c0, width):
        proj_s[:, c0:c0 + width] = _dot(hn, win_ref[:, c0:c0 + width])

    def late_proj():
        for c0 in (COL_Z, COL_Q, COL_V, COL_G):
            proj_cols(c0, PROJ_CHUNK)
            yield

    side = [late_proj()]

    def tick():
        if side[0] is not None:
            try:
                next(side[0])
            except StopIteration:
                side[0] = None

    for c in range(SSD_CONV_CH // PROJ_CHUNK):
        proj_cols(COL_XBC + c * PROJ_CHUNK, PROJ_CHUNK)
        yield
    proj_cols(COL_SMALL, 128)

    xa = []
    for c in range(SSD_CONV_CH // PROJ_CHUNK):
        cs = slice(c * PROJ_CHUNK, (c + 1) * PROJ_CHUNK)
        xpad_s[8:8 + rows, cs] = proj_s[:, COL_XBC + c * PROJ_CHUNK:COL_XBC + (c + 1) * PROJ_CHUNK]
        conv = cb_ref[:, cs]
        for j in range(SSD_CONV):
            conv = conv + xpad_s[5 + j:5 + j + rows, cs] * cw_ref[j:j + 1, cs]
        xpad_s[5:8, cs] = xpad_s[5 + rows:8 + rows, cs]
        xa.append(_silu(conv))
        tick()
        yield

    small = proj_s[:, COL_SMALL:COL_SMALL + 128]
    logf = _gate_logf(small, gw2_ref, gb_ref)

    def get_ss(h):
        return ss_s[SSD_HEAD_DIM * h:SSD_HEAD_DIM * (h + 1), :]

    def put_ss(h, val):
        ss_s[SSD_HEAD_DIM * h:SSD_HEAD_DIM * (h + 1), :] = val

    ys = []
    ssd = _ssd_phases_seq(xa[0], xa[1][:, :SSD_GROUPS * SSD_STATE], xa[1][:, SSD_GROUPS * SSD_STATE:],
                          small, dtb_ref[...], a_ref[...], d_ref[...], get_ss, put_ss, ys)
    for i, _ in enumerate(ssd):
        if i % 2 == 0:
            tick()
        yield
    while side[0] is not None:
        tick()
        yield
    mixed_s[:, GLA_VW:] = _ssd_finish(jnp.concatenate(ys, axis=1), proj_s[:, COL_Z:COL_Z + SSD_INNER],
                                      snw_ref[...]).astype(BF16)
    yield

    def get_sg(s):
        return sg_s[...]

    def put_sg(s, val):
        sg_s[...] = val

    for c in range(rows // GLA_CHUNK):
        r0 = c * GLA_CHUNK
        sl = slice(r0, r0 + GLA_CHUNK)
        outs = []
        yield from _gla_phases(proj_s[sl, COL_Q:COL_Q + GLA_KW], proj_s[sl, COL_K:COL_K + GLA_KW],
                               proj_s[sl, COL_V:COL_V + GLA_VW], logf[sl], get_sg, put_sg, GLA_CHUNK, outs)
        for h in range(GLA_HEADS):
            g_h = proj_s[sl, COL_G + h * GLA_DV:COL_G + (h + 1) * GLA_DV]
            mixed_s[sl, h * GLA_DV:(h + 1) * GLA_DV] = _gla_finish(outs[h], g_h, gnw_ref[...]).astype(BF16)
        yield

    mixed = mixed_s[...]
    ms = []
    for c in range(D_MODEL // PROJ_CHUNK):
        ms.append(_dot(mixed, wout_ref[:, c * PROJ_CHUNK:(c + 1) * PROJ_CHUNK]))
        yield
    h_ref[...] = x + _rms(jnp.concatenate(ms, axis=1), lnpost_ref[...])


N_MIXER_WEIGHTS = 13
N_MIXER_SCRATCH = 5
MIX_OFFSET = 3


def _mixer_prompt_kernel(x_ref, *rest):
    consts = rest[:N_MIXER_WEIGHTS]
    h_ref, sg_out, ss_out, conv_out = rest[N_MIXER_WEIGHTS:N_MIXER_WEIGHTS + 4]
    scratch = rest[N_MIXER_WEIGHTS + 4:]
    per_seq = [scratch[i * N_MIXER_SCRATCH:(i + 1) * N_MIXER_SCRATCH] for i in range(MIX_SEQS)]
    t = pl.program_id(1)

    @pl.when(t == 0)
    def _():
        for _, xpad_s, _, sg_s, ss_s in per_seq:
            sg_s[...] = jnp.zeros_like(sg_s)
            ss_s[...] = jnp.zeros_like(ss_s)
            xpad_s[0:8, :] = jnp.zeros((8, SSD_CONV_CH), F32)

    _interleave([_mixer_prompt_chain(x_ref.at[i], h_ref.at[i], *per_seq[i], *consts) for i in range(MIX_SEQS)],
                offset=MIX_OFFSET)

    @pl.when(t == pl.num_programs(1) - 1)
    def _():
        for i, (_, xpad_s, _, sg_s, ss_s) in enumerate(per_seq):
            sg_out[i] = sg_s[...].reshape(GLA_HEADS, GLA_DK, GLA_DV)
            ss_out[i] = ss_s[...].reshape(SSD_HEADS, SSD_HEAD_DIM, SSD_STATE)
            conv_out[i] = xpad_s[5:8, :]


def _mixer_prompt(x, p):
    b, t, d = x.shape
    nt = t // TILE
    consts = [p["ln_mix_pre"], p["w_in"], p["gw2"], p["gb"], p["gnw"], p["conv_w"], p["conv_b"],
              p["dtb"], p["a_blk"], p["d_row"], p["snw"], p["w_out"], p["ln_mix_post"]]
    assert len(consts) == N_MIXER_WEIGHTS
    tok = pl.BlockSpec((MIX_SEQS, TILE, d), lambda i, j: (i, j, 0))
    return pl.pallas_call(
        _mixer_prompt_kernel,
        grid=(b // MIX_SEQS, nt),
        in_specs=[tok] + [_const_spec(c.shape) for c in consts],
        out_specs=[tok,
                   pl.BlockSpec((MIX_SEQS, GLA_HEADS, GLA_DK, GLA_DV), lambda i, j: (i, 0, 0, 0)),
                   pl.BlockSpec((MIX_SEQS, SSD_HEADS, SSD_HEAD_DIM, SSD_STATE), lambda i, j: (i, 0, 0, 0)),
                   pl.BlockSpec((MIX_SEQS, SSD_CONV - 1, SSD_CONV_CH), lambda i, j: (i, 0, 0))],
        out_shape=[jax.ShapeDtypeStruct((b, t, d), F32),
                   jax.ShapeDtypeStruct((b, GLA_HEADS, GLA_DK, GLA_DV), F32),
                   jax.ShapeDtypeStruct((b, SSD_HEADS, SSD_HEAD_DIM, SSD_STATE), F32),
                   jax.ShapeDtypeStruct((b, SSD_CONV - 1, SSD_CONV_CH), F32)],
        scratch_shapes=[pltpu.VMEM((TILE, PROJ_COLS), F32),
                        pltpu.VMEM((TILE + 8, SSD_CONV_CH), F32),
                        pltpu.VMEM((TILE, D_MODEL), BF16),
                        pltpu.VMEM((GLA_KW, GLA_DV), F32),
                        pltpu.VMEM((SSD_INNER, SSD_STATE), F32)] * MIX_SEQS,
        compiler_params=pltpu.CompilerParams(dimension_semantics=("arbitrary", "arbitrary"),
                                             vmem_limit_bytes=VMEM_LIMIT),
        name="mixer_prompt",
    )(x, *consts)


def _softmax_rows(s):
    e = jnp.exp(s - jnp.max(s, axis=-1, keepdims=True))
    return e / jnp.sum(e, axis=-1, keepdims=True)


def _ffn(h, lnpre, lnpost, wg_ref, wu_ref, wd_ref):
    hf = _bf(_rms(h, lnpre))
    gt = _dot(hf, wg_ref[...])
    up = _dot(hf, wu_ref[...])
    f = _dot(_bf(_silu(gt) * up), wd_ref[...])
    return h + _rms(f, lnpost)


FFN_CHUNK = 256
FFN_OFFSET = 1


def _attn_ffn_chain(h_ref, y_ref, mk_ref, mv_ref, lnxa_ref, wq_ref, wo_ref, lnxap_ref,
                    lnf_ref, lnfp_ref, wg_ref, wu_ref, wd_ref):
    h = h_ref[...]
    hn = _rms(h, lnxa_ref[...]).astype(BF16)
    yield
    heads_per_chunk = PROJ_CHUNK // XA_HEAD_DIM
    outs = []
    for c in range(D_MODEL // PROJ_CHUNK):
        q = _dot(hn, wq_ref[:, c * PROJ_CHUNK:(c + 1) * PROJ_CHUNK]).astype(BF16)
        yield
        for hh in range(heads_per_chunk):
            hd = c * heads_per_chunk + hh
            sl = slice(hd * XA_HEAD_DIM, (hd + 1) * XA_HEAD_DIM)
            s = _dot_nt(q[:, hh * XA_HEAD_DIM:(hh + 1) * XA_HEAD_DIM], mk_ref[0, :, sl])
            yield
            pr = _softmax_rows(s).astype(BF16)
            yield
            outs.append(_dot(pr, mv_ref[0, :, sl]).astype(BF16))
    a = jnp.concatenate(outs, axis=1)
    yield
    parts = []
    for c in range(D_MODEL // PROJ_CHUNK):
        parts.append(_dot(a, wo_ref[:, c * PROJ_CHUNK:(c + 1) * PROJ_CHUNK]))
        yield
    h = h + _rms(jnp.concatenate(parts, axis=1), lnxap_ref[...])
    hf = _rms(h, lnf_ref[...]).astype(BF16)
    yield
    acts = []
    for c in range(D_FF // FFN_CHUNK):
        cs = slice(c * FFN_CHUNK, (c + 1) * FFN_CHUNK)
        gt = _dot(hf, wg_ref[:, cs])
        up = _dot(hf, wu_ref[:, cs])
        yield
        acts.append((_silu(gt) * up).astype(BF16))
    act = jnp.concatenate(acts, axis=1)
    yield
    parts = []
    for c in range(D_MODEL // PROJ_CHUNK):
        parts.append(_dot(act, wd_ref[:, c * PROJ_CHUNK:(c + 1) * PROJ_CHUNK]))
        yield
    y_ref[...] = h + _rms(jnp.concatenate(parts, axis=1), lnfp_ref[...])


def _attn_ffn_prompt_kernel(h_ref, mk_ref, mv_ref, *rest):
    consts, y_ref = rest[:-1], rest[-1]
    chains = []
    for sub in range(FFN_SUBTILES):
        rs = pl.ds(sub * TILE, TILE)
        chains.append(_attn_ffn_chain(h_ref.at[0, rs], y_ref.at[0, rs], mk_ref, mv_ref, *consts))
    _interleave(chains, offset=FFN_OFFSET)


def _attn_ffn_prompt(h, mkb, mvb, p):
    b, t, d = h.shape
    rows = FFN_SUBTILES * TILE
    consts = [p["ln_xa_pre"], p["w_xq"], p["w_xo"], p["ln_xa_post"], p["ln_ffn_pre"], p["ln_ffn_post"],
              p["w_gate"], p["w_up"], p["w_down"]]
    tok = pl.BlockSpec((1, rows, d), lambda i, j: (i, j, 0))
    mem = pl.BlockSpec((1, mkb.shape[1], d), lambda i, j: (i, 0, 0))
    return pl.pallas_call(
        _attn_ffn_prompt_kernel,
        grid=(b, t // rows),
        in_specs=[tok, mem, mem] + [_const_spec(c.shape) for c in consts],
        out_specs=tok,
        out_shape=jax.ShapeDtypeStruct((b, t, d), F32),
        compiler_params=pltpu.CompilerParams(dimension_semantics=("arbitrary", "arbitrary"),
                                             vmem_limit_bytes=VMEM_LIMIT),
        name="attn_ffn_prompt",
    )(h, mkb, mvb, *consts)


def _proj_sample_kernel(x_ref, lnpre_ref, win_ref, proj_ref):
    proj_ref[...] = _dot(_bf(_rms(x_ref[...], lnpre_ref[...])), win_ref[...])


def _proj_sample(x, p):
    n, d = x.shape
    return pl.pallas_call(
        _proj_sample_kernel,
        grid=(1,),
        in_specs=[_const_spec((n, d)), _const_spec((1, d)), _const_spec(p["w_in"].shape)],
        out_specs=pl.BlockSpec((n, PROJ_COLS), lambda i: (0, 0)),
        out_shape=jax.ShapeDtypeStruct((n, PROJ_COLS), F32),
        compiler_params=pltpu.CompilerParams(dimension_semantics=("arbitrary",),
                                             vmem_limit_bytes=VMEM_LIMIT),
        name="proj_sample",
    )(x, p["ln_mix_pre"], p["w_in"])


def _mixer_sample_kernel(seq_len, proj_ref, tail_ref, sg_ref, ss_ref, gw2_ref, gb_ref, gnw_ref,
                         cw_ref, cb_ref, dtb_ref, a_ref, d_ref, snw_ref,
                         mixed_ref, sg_out, ss_out, xpad_s, tpad_s):
    rows = proj_ref.shape[0]
    n_seq = rows // seq_len
    small = proj_ref[:, COL_SMALL:COL_SMALL + 128]
    logf = _gate_logf(small, gw2_ref, gb_ref)

    xbc = proj_ref[:, COL_XBC:COL_XBC + SSD_CONV_CH]
    xpad_s[0:8, :] = jnp.zeros((8, SSD_CONV_CH), F32)
    xpad_s[8:8 + rows, :] = xbc
    tpad_s[0:rows, :] = tail_ref[...]
    tpad_s[rows:rows + 8, :] = jnp.zeros((8, SSD_CONV_CH), F32)
    pos = lax.broadcasted_iota(jnp.int32, (rows, SSD_CONV_CH), 0) & (seq_len - 1)
    conv = cb_ref[...]
    for j in range(SSD_CONV - 1):
        back = SSD_CONV - 1 - j
        prev = jnp.where(pos >= back, xpad_s[8 - back:8 - back + rows, :],
                         tpad_s[seq_len - back:seq_len - back + rows, :])
        conv = conv + prev * cw_ref[j:j + 1, :]
    conv = conv + xbc * cw_ref[SSD_CONV - 1:SSD_CONV, :]
    xa = _silu(conv)

    def get_ss(s, j):
        return ss_ref[s, 2 * j:2 * j + 2].reshape(2 * SSD_HEAD_DIM, SSD_STATE)

    def put_ss(s, j, val):
        ss_out[s, 2 * j:2 * j + 2] = val.reshape(2, SSD_HEAD_DIM, SSD_STATE)

    ys = []
    _run(_ssd_phases(xa[:, :SSD_INNER], xa[:, SSD_INNER:SSD_INNER + 256], xa[:, SSD_INNER + 256:],
                     small, dtb_ref[...], a_ref[...], d_ref[...], get_ss, put_ss, seq_len, ys))
    mixed_ref[:, GLA_VW:] = _ssd_finish(jnp.concatenate(ys, axis=1), proj_ref[:, COL_Z:COL_Z + SSD_INNER],
                                        snw_ref[...])

    def get_sg(s):
        return sg_ref[s].reshape(GLA_KW, GLA_DV)

    def put_sg(s, val):
        sg_out[s] = val.reshape(GLA_HEADS, GLA_DK, GLA_DV)

    outs = []
    _run(_gla_phases(proj_ref[:, COL_Q:COL_Q + GLA_KW], proj_ref[:, COL_K:COL_K + GLA_KW],
                     proj_ref[:, COL_V:COL_V + GLA_VW], logf, get_sg, put_sg, seq_len, outs))
    for h in range(GLA_HEADS):
        g_h = proj_ref[:, COL_G + h * GLA_DV:COL_G + (h + 1) * GLA_DV]
        mixed_ref[:, h * GLA_DV:(h + 1) * GLA_DV] = _gla_finish(outs[h], g_h, gnw_ref[...])


def _mixer_sample(proj, tail, sg, ss, p, seq_len):
    n = proj.shape[0]
    rows = SAMPLE_SEQS * seq_len
    nb = sg.shape[0]
    consts = [p["gw2"], p["gb"], p["gnw"], p["conv_w"], p["conv_b"], p["dtb"], p["a_blk"], p["d_row"],
              p["snw"]]
    sg_spec = pl.BlockSpec((SAMPLE_SEQS, GLA_HEADS, GLA_DK, GLA_DV), lambda i: (i, 0, 0, 0))
    ss_spec = pl.BlockSpec((SAMPLE_SEQS, SSD_HEADS, SSD_HEAD_DIM, SSD_STATE), lambda i: (i, 0, 0, 0))
    return pl.pallas_call(
        functools.partial(_mixer_sample_kernel, seq_len),
        grid=(nb // SAMPLE_SEQS,),
        in_specs=[pl.BlockSpec((rows, PROJ_COLS), lambda i: (i, 0)),
                  pl.BlockSpec((rows, SSD_CONV_CH), lambda i: (i, 0)),
                  sg_spec, ss_spec] + [_const_spec(c.shape) for c in consts],
        out_specs=[pl.BlockSpec((rows, D_MODEL), lambda i: (i, 0)), sg_spec, ss_spec],
        out_shape=[jax.ShapeDtypeStruct((n, D_MODEL), F32),
                   jax.ShapeDtypeStruct(sg.shape, F32), jax.ShapeDtypeStruct(ss.shape, F32)],
        scratch_shapes=[pltpu.VMEM((rows + 8, SSD_CONV_CH), F32),
                        pltpu.VMEM((rows + 8, SSD_CONV_CH), F32)],
        compiler_params=pltpu.CompilerParams(dimension_semantics=("arbitrary",),
                                             vmem_limit_bytes=VMEM_LIMIT),
        name="mixer_sample",
    )(proj, tail, sg, ss, *consts)


def _post_mix_sample_kernel(x_ref, mixed_ref, wout_ref, lnpost_ref, lnxa_ref, wq_ref, h_ref, q_ref):
    m = _dot(_bf(mixed_ref[...]), wout_ref[...])
    h = x_ref[...] + _rms(m, lnpost_ref[...])
    h_ref[...] = h
    q_ref[...] = _bf(_dot(_bf(_rms(h, lnxa_ref[...])), wq_ref[...]))


def _post_mix_sample(x, mixed, p):
    n, d = x.shape
    full = pl.BlockSpec((n, d), lambda i: (0, 0))
    consts = [p["w_out"], p["ln_mix_post"], p["ln_xa_pre"], p["w_xq"]]
    return pl.pallas_call(
        _post_mix_sample_kernel,
        grid=(1,),
        in_specs=[full, full] + [_const_spec(c.shape) for c in consts],
        out_specs=[full, full],
        out_shape=[jax.ShapeDtypeStruct((n, d), F32), jax.ShapeDtypeStruct((n, d), BF16)],
        compiler_params=pltpu.CompilerParams(dimension_semantics=("arbitrary",),
                                             vmem_limit_bytes=VMEM_LIMIT),
        name="post_mix_sample",
    )(x, mixed, *consts)


def _attn_sample_kernel(seq_len, q_ref, k_hbm, v_hbm, a_ref, kbuf, vbuf, sem):
    step = pl.program_id(0)
    rows = q_ref.shape[0]
    n_seq = rows // seq_len

    def copies(at_step, slot):
        out = []
        for s in range(n_seq):
            for hd in range(XA_HEADS):
                b = at_step * n_seq + s
                out.append(pltpu.make_async_copy(k_hbm.at[0, b, :, hd, :], kbuf.at[slot, s, hd], sem.at[0, slot]))
                out.append(pltpu.make_async_copy(v_hbm.at[0, b, :, hd, :], vbuf.at[slot, s, hd], sem.at[1, slot]))
        return out

    @pl.when(step == 0)
    def _():
        for cp in copies(0, 0):
            cp.start()

    @pl.when(step + 1 < pl.num_programs(0))
    def _():
        for cp in copies(step + 1, (step + 1) % 2):
            cp.start()

    slot = step % 2
    for cp in copies(step, slot):
        cp.wait()

    q = q_ref[...]
    results = {}

    def one(s, hd):
        sc = _dot_nt(q[:, hd * XA_HEAD_DIM:(hd + 1) * XA_HEAD_DIM], _bf(kbuf[slot, s, hd]))
        yield
        e = jnp.exp(sc - jnp.max(sc, axis=-1, keepdims=True))
        pr = _bf(e / jnp.sum(e, axis=-1, keepdims=True))
        yield
        results[s, hd] = _dot(pr, _bf(vbuf[slot, s, hd]))

    _interleave([one(s, hd) for s in range(n_seq) for hd in range(XA_HEADS)])
    for hd in range(XA_HEADS):
        sl = slice(hd * XA_HEAD_DIM, (hd + 1) * XA_HEAD_DIM)
        out = results[0, hd]
        for s in range(1, n_seq):
            out = jnp.where(_row_in_seq(rows, seq_len, s, XA_HEAD_DIM), results[s, hd], out)
        a_ref[:, sl] = out


def _attn_sample(q, ck, cv, seq_len):
    n, d = q.shape
    _, nb, m, nh, hd = ck.shape
    rows = ATTN_SEQS * seq_len
    tok = pl.BlockSpec((rows, d), lambda i: (i, 0))
    hbm = pl.BlockSpec(memory_space=pl.ANY)
    return pl.pallas_call(
        functools.partial(_attn_sample_kernel, seq_len),
        grid=(nb // ATTN_SEQS,),
        in_specs=[tok, hbm, hbm],
        out_specs=tok,
        out_shape=jax.ShapeDtypeStruct((n, d), F32),
        scratch_shapes=[pltpu.VMEM((2, ATTN_SEQS, nh, m, hd), F32),
                        pltpu.VMEM((2, ATTN_SEQS, nh, m, hd), F32),
                        pltpu.SemaphoreType.DMA((2, 2))],
        compiler_params=pltpu.CompilerParams(dimension_semantics=("arbitrary",),
                                             vmem_limit_bytes=VMEM_LIMIT),
        name="attn_sample",
    )(q, ck, cv)


def _post_attn_sample_kernel(h_ref, a_ref, wo_ref, lnxap_ref, lnf_ref, lnfp_ref, wg_ref, wu_ref, wd_ref,
                             y_ref):
    a = _dot(_bf(a_ref[...]), wo_ref[...])
    h = h_ref[...] + _rms(a, lnxap_ref[...])
    y_ref[...] = _ffn(h, lnf_ref[...], lnfp_ref[...], wg_ref, wu_ref, wd_ref)


def _post_attn_sample(h, a, p):
    n, d = h.shape
    full = pl.BlockSpec((n, d), lambda i: (0, 0))
    consts = [p["w_xo"], p["ln_xa_post"], p["ln_ffn_pre"], p["ln_ffn_post"], p["w_gate"], p["w_up"],
              p["w_down"]]
    return pl.pallas_call(
        _post_attn_sample_kernel,
        grid=(1,),
        in_specs=[full, full] + [_const_spec(c.shape) for c in consts],
        out_specs=full,
        out_shape=jax.ShapeDtypeStruct((n, d), F32),
        compiler_params=pltpu.CompilerParams(dimension_semantics=("arbitrary",),
                                             vmem_limit_bytes=VMEM_LIMIT),
        name="post_attn_sample",
    )(h, a, *consts)


SRC_GLR = GLA_KW + GLA_KW + GLA_VW + GLA_VW
SRC_Z = SRC_GLR + GLA_GATE_RANK
SRC_XBC = SRC_Z + SSD_INNER
SRC_DT = SRC_XBC + SSD_CONV_CH
IN_COLS = SRC_DT + SSD_HEADS


def _pack_w_in_kernel(wt_hbm, o_ref, buf, sem):
    pieces = [(0, COL_Q, SRC_GLR),
              (SRC_Z, COL_Z, SSD_INNER),
              (SRC_XBC, COL_XBC, SSD_CONV_CH),
              (SRC_GLR, COL_SMALL, GLA_GATE_RANK),
              (SRC_DT, COL_SMALL + DT_LANE, SSD_HEADS)]
    copies = [pltpu.make_async_copy(wt_hbm.at[0, pl.ds(src, n), :], buf.at[pl.ds(dst, n), :], sem.at[0])
              for src, dst, n in pieces]
    for cp in copies:
        cp.start()
    used = COL_SMALL + DT_LANE + SSD_HEADS
    buf[used:PROJ_COLS, :] = jnp.zeros((PROJ_COLS - used, D_MODEL), F32)
    for cp in copies:
        cp.wait()
    for j in range(PROJ_COLS // 128):
        blk = buf[128 * j:128 * (j + 1), :]
        if 128 * j < COL_K:
            blk = blk * (GLA_DK ** -0.5)
        o_ref[:, 128 * j:128 * (j + 1)] = blk.T.astype(BF16)


def _pack_w_in(w_in):
    wt = jnp.swapaxes(w_in, 1, 2)
    return pl.pallas_call(
        _pack_w_in_kernel,
        grid=(1,),
        in_specs=[pl.BlockSpec(memory_space=pl.ANY)],
        out_specs=pl.BlockSpec((D_MODEL, PROJ_COLS), lambda i: (0, 0)),
        out_shape=jax.ShapeDtypeStruct((D_MODEL, PROJ_COLS), BF16),
        scratch_shapes=[pltpu.VMEM((PROJ_COLS, D_MODEL), F32), pltpu.SemaphoreType.DMA((1,))],
        compiler_params=pltpu.CompilerParams(dimension_semantics=("arbitrary",),
                                             vmem_limit_bytes=VMEM_LIMIT),
        name="pack_w_in",
    )(wt)


def _pack_params(ln_mix_pre, ln_mix_post, w_in_packed, gla_gate_w2, gla_gate_b, gla_norm_w, ssd_conv_w,
                 ssd_conv_b, ssd_dt_bias, ssd_A_log, ssd_D, ssd_norm_w, w_out, ln_xa_pre, ln_xa_post,
                 mem_norm_w, w_xq, w_xk, w_xv, w_xo, ln_ffn_pre, ln_ffn_post, w_gate, w_up, w_down):

    def small_row(v):
        return jnp.zeros((1, 128), F32).at[0, DT_LANE:DT_LANE + SSD_HEADS].set(v)

    row = lambda v: v.reshape(1, -1)
    return {
        "ln_mix_pre": row(ln_mix_pre), "ln_mix_post": row(ln_mix_post),
        "w_in": w_in_packed,
        "gw2": _bf(jnp.zeros((128, GLA_KW), F32).at[:GLA_GATE_RANK].set(gla_gate_w2)),
        "gb": row(gla_gate_b), "gnw": row(gla_norm_w),
        "conv_w": ssd_conv_w, "conv_b": row(ssd_conv_b),
        "dtb": small_row(ssd_dt_bias),
        "a_blk": small_row(-jnp.exp(ssd_A_log)),
        "d_row": jnp.repeat(ssd_D, SSD_HEAD_DIM).reshape(1, SSD_INNER),
        "snw": row(ssd_norm_w),
        "w_out": _bf(w_out),
        "ln_xa_pre": row(ln_xa_pre), "ln_xa_post": row(ln_xa_post), "mem_norm_w": row(mem_norm_w),
        "w_xq": _bf(w_xq * (XA_HEAD_DIM ** -0.5)), "w_xk": _bf(w_xk), "w_xv": _bf(w_xv), "w_xo": _bf(w_xo),
        "ln_ffn_pre": row(ln_ffn_pre), "ln_ffn_post": row(ln_ffn_post),
        "w_gate": _bf(w_gate), "w_up": _bf(w_up), "w_down": _bf(w_down),
    }


def kernel(x_prompt, x_sample, mem_prompt, state_gla, state_ssm, state_conv, cache_mem_k, cache_mem_v,
           ln_mix_pre, ln_mix_post, w_in, gla_gate_w2, gla_gate_b, gla_norm_w, ssd_conv_w, ssd_conv_b,
           ssd_dt_bias, ssd_A_log, ssd_D, ssd_norm_w, w_out, ln_xa_pre, ln_xa_post, mem_norm_w,
           w_xq, w_xk, w_xv, w_xo, ln_ffn_pre, ln_ffn_post, w_gate, w_up, w_down):
    assert w_in.shape[0] == 1, "single-layer kernel"
    layer = [a[0] for a in (ln_mix_pre, ln_mix_post, w_in, gla_gate_w2, gla_gate_b, gla_norm_w, ssd_conv_w,
                            ssd_conv_b, ssd_dt_bias, ssd_A_log, ssd_D, ssd_norm_w, w_out, ln_xa_pre,
                            ln_xa_post, mem_norm_w, w_xq, w_xk, w_xv, w_xo, ln_ffn_pre, ln_ffn_post,
                            w_gate, w_up, w_down)]
    layer[2] = _pack_w_in(w_in)
    p = _pack_params(*layer)
    bp, mem_len, _ = mem_prompt.shape
    bs, ts, _ = x_sample.shape
    assert ts == SSD_CONV, "the padded conv-tail layout assumes one new row per conv tap"

    mk, mv, mkb, mvb = _memkv(mem_prompt, p["mem_norm_w"], p["w_xk"], p["w_xv"])
    h_p, gla_p, ssm_p, conv_p = _mixer_prompt(x_prompt, p)
    y_p = _attn_ffn_prompt(h_p, mkb, mvb, p)

    xs = x_sample.reshape(bs * ts, D_MODEL)
    proj = _proj_sample(xs, p)
    tail = jnp.pad(state_conv[0], ((0, 0), (1, 0), (0, 0))).reshape(bs * ts, SSD_CONV_CH)
    mixed, gla_s, ssm_s = _mixer_sample(proj, tail, state_gla[0], state_ssm[0], p, ts)
    h_s, q_s = _post_mix_sample(xs, mixed, p)
    a_s = _attn_sample(q_s, cache_mem_k, cache_mem_v, ts)
    y_s = _post_attn_sample(h_s, a_s, p)
    conv_s = proj[:, COL_XBC:COL_XBC + SSD_CONV_CH].reshape(bs, ts, SSD_CONV_CH)[:, ts - (SSD_CONV - 1):]

    return (y_p, y_s.reshape(bs, ts, D_MODEL), gla_p[None], ssm_p[None], conv_p[None],
            mk, mv, gla_s[None], ssm_s[None], conv_s[None])
```

```python
import functools

import jax
import jax.numpy as jnp
from jax import lax
from jax.experimental import pallas as pl
from jax.experimental.pallas import tpu as pltpu

F32 = jnp.float32
BF16 = jnp.bfloat16

D_MODEL = 1024
GLA_HEADS = 4
GLA_DK = 64
GLA_DV = 128
GLA_KW = GLA_HEADS * GLA_DK
GLA_VW = GLA_HEADS * GLA_DV
GLA_GATE_RANK = 16
GLA_GATE_TAU = 16.0
SSD_INNER = 512
SSD_HEAD_DIM = 64
SSD_HEADS = 8
SSD_GROUPS = 2
SSD_STATE = 128
SSD_CONV = 4
SSD_CONV_CH = 1024
XA_HEADS = 4
XA_HEAD_DIM = 256
D_FF = 2816
EPS = 1e-6

COL_Q = 0
COL_K = 256
COL_V = 512
COL_G = 1024
COL_Z = 1536
COL_XBC = 2048
COL_SMALL = 3072
DT_LANE = GLA_GATE_RANK
PROJ_COLS = 3200

GLA_CHUNK = 64
TILE = 256
FFN_SUBTILES = 4
MIX_SEQS = 4
SAMPLE_SEQS = 8
ATTN_SEQS = 4
VMEM_LIMIT = 56 * 1024 * 1024


def _bf(x):
    return x.astype(BF16)


def _dot(a, b):
    return jnp.dot(a, b, preferred_element_type=F32)


def _dot_nt(a, b):
    return lax.dot_general(a, b, (((1,), (1,)), ((), ())), preferred_element_type=F32)


def _dot_tn(a, b):
    return lax.dot_general(a, b, (((0,), (0,)), ((), ())), preferred_element_type=F32)


def _dot3(m, a):
    hi = _bf(a)
    r = a - hi.astype(F32)
    mid = _bf(r)
    lo = _bf(r - mid.astype(F32))
    return _dot(m, hi) + _dot(m, mid) + _dot(m, lo)


def _rms(x, w):
    ms = jnp.mean(x * x, axis=-1, keepdims=True)
    return x * lax.rsqrt(ms + EPS) * w


def _silu(x):
    return x / (1.0 + jnp.exp(-x))


def _softplus(x):
    e = jnp.exp(-jnp.abs(x))
    u = 1.0 + e
    log1p_e = jnp.where(u == 1.0, e, jnp.log(u) * (e / (u - 1.0)))
    return jnp.maximum(x, 0.0) + log1p_e


def _log_sigmoid(x):
    return jnp.minimum(x, 0.0) - jnp.log(1.0 + jnp.exp(-jnp.abs(x)))


def _seg_masks(rows, seq_len):
    t = lax.broadcasted_iota(jnp.int32, (rows, rows), 0)
    u = lax.broadcasted_iota(jnp.int32, (rows, rows), 1)
    if seq_len == rows:
        same = None
        tril = u <= t
    else:
        shift = seq_len.bit_length() - 1
        assert 1 << shift == seq_len
        same = (t >> shift) == (u >> shift)
        tril = jnp.logical_and(same, u <= t)
    return same, tril


def _mask_to_bf16(mask, rows):
    if mask is None:
        return jnp.ones((rows, rows), BF16)
    return jnp.where(mask, 1.0, 0.0).astype(BF16)


def _row_in_seq(rows, seq_len, s, width):
    r = lax.broadcasted_iota(jnp.int32, (rows, width), 0)
    return jnp.logical_and(r >= s * seq_len, r < (s + 1) * seq_len)


def _interleave(chains, offset=0):
    waiting = list(enumerate(chains))
    active = []
    rnd = 0
    while waiting or active:
        while waiting and waiting[0][0] * offset <= rnd:
            active.append(waiting.pop(0)[1])
        alive = []
        for c in active:
            try:
                next(c)
                alive.append(c)
            except StopIteration:
                pass
        active = alive
        rnd += 1


def _run(phases):
    for _ in phases:
        pass


def _pad_rows(x, rows):
    if x.shape[0] >= rows:
        return x
    return jnp.concatenate([x, jnp.zeros((rows - x.shape[0], x.shape[1]), x.dtype)], axis=0)


def _gla_phases(q, k, v, lf, get_state, put_state, seq_len, outs):
    rows = q.shape[0]
    n_seq = rows // seq_len
    same, tril = _seg_masks(rows, seq_len)
    tril_b = _mask_to_bf16(tril, rows)
    same_b = _mask_to_bf16(same, rows)

    b = _dot3(tril_b, lf)
    if n_seq == 1:
        bl = jnp.broadcast_to(b[rows - 1:rows, :], b.shape)
    else:
        bl = _dot3(same_b, lf)
    qt = q * jnp.exp(b)
    kt = k * jnp.exp(-b)
    kd = k * jnp.exp(bl - b)
    bl_t = _pad_rows(bl, 128).T
    yield

    lane = lax.broadcasted_iota(jnp.int32, (rows, GLA_KW), 1)
    lhs = jnp.concatenate(
        [jnp.where((lane >= h * GLA_DK) & (lane < (h + 1) * GLA_DK), qt, 0.0)
         for h in range(GLA_HEADS)], axis=0).astype(BF16)
    att = _dot_nt(lhs, kt.astype(BF16))
    t4 = lax.broadcasted_iota(jnp.int32, (GLA_HEADS * rows, rows), 0) & (rows - 1)
    u4 = lax.broadcasted_iota(jnp.int32, (GLA_HEADS * rows, rows), 1)
    causal4 = u4 <= t4
    if n_seq > 1:
        shift = seq_len.bit_length() - 1
        causal4 = jnp.logical_and(causal4, (t4 >> shift) == (u4 >> shift))
    att = jnp.where(causal4, att, 0.0).astype(BF16)
    vb = v.astype(BF16)
    kdb = kd.astype(BF16)
    yield

    o_inter = None
    r4 = lax.broadcasted_iota(jnp.int32, (GLA_HEADS * rows, GLA_DV), 0) & (rows - 1)
    for s in range(n_seq):
        st = get_state(s)
        oi = _dot(lhs, st.astype(BF16))
        if n_seq == 1:
            o_inter = oi
            kd_s = kdb
        else:
            m4 = jnp.logical_and(r4 >= s * seq_len, r4 < (s + 1) * seq_len)
            oi = jnp.where(m4, oi, 0.0)
            o_inter = oi if o_inter is None else o_inter + oi
            kd_s = jnp.where(_row_in_seq(rows, seq_len, s, GLA_KW), kdb, jnp.zeros_like(kdb))
        blocks = []
        for hp in range(GLA_HEADS // 2):
            upd = _dot_tn(kd_s[:, 2 * hp * GLA_DK:2 * (hp + 1) * GLA_DK],
                          vb[:, 2 * hp * GLA_DV:2 * (hp + 1) * GLA_DV])
            blocks += [upd[:GLA_DK, :GLA_DV], upd[GLA_DK:, GLA_DV:]]
        upd_d = jnp.concatenate(blocks, axis=0)
        c0 = s * seq_len
        decay = jnp.exp(jnp.broadcast_to(bl_t[:, c0:c0 + 1], (GLA_KW, GLA_DV)))
        put_state(s, decay * st + upd_d)
        if n_seq > 1 and s % 4 == 3:
            yield
    yield

    for h in range(GLA_HEADS):
        o_h = _dot(att[h * rows:(h + 1) * rows], vb[:, h * GLA_DV:(h + 1) * GLA_DV])
        outs.append(o_h + o_inter[h * rows:(h + 1) * rows])


def _ssd_phases(xs, bm, cm, dt_raw_blk, dtb_blk, a_blk, d_row, get_state, put_state, seq_len, ys):
    rows = xs.shape[0]
    n_seq = rows // seq_len
    same, tril = _seg_masks(rows, seq_len)
    tril_b = _mask_to_bf16(tril, rows)
    same_b = _mask_to_bf16(same, rows)

    dt = _softplus(dt_raw_blk + dtb_blk)
    dta = dt * a_blk
    lc = _dot3(tril_b, dta)
    if n_seq == 1:
        ll = jnp.broadcast_to(lc[rows - 1:rows, :], lc.shape)
    else:
        ll = _dot3(same_b, dta)
    elc = jnp.exp(lc)
    w = jnp.exp(ll - lc) * dt
    ell = jnp.exp(ll)
    lc_t = _pad_rows(lc, 128).T[:, :rows]
    yield

    bmb = bm.astype(BF16)
    cmb = cm.astype(BF16)
    cb = [_dot_nt(cmb[:, g * SSD_STATE:(g + 1) * SSD_STATE],
                  bmb[:, g * SSD_STATE:(g + 1) * SSD_STATE]) for g in range(SSD_GROUPS)]
    yield

    lane128 = lax.broadcasted_iota(jnp.int32, (rows, 128), 1)
    lo128 = lane128 < SSD_HEAD_DIM
    sub128 = lax.broadcasted_iota(jnp.int32, (128, 128), 0) < SSD_HEAD_DIM

    def col(x, h, width):
        return jnp.broadcast_to(x[:, DT_LANE + h:DT_LANE + h + 1], (x.shape[0], width))

    heads_per_group = SSD_HEADS // SSD_GROUPS
    for j in range(SSD_HEADS // 2):
        g = (2 * j) // heads_per_group
        xs_p = xs[:, 128 * j:128 * (j + 1)]
        dt_p = jnp.where(lo128, col(dt, 2 * j, 128), col(dt, 2 * j + 1, 128))
        xdt = xs_p * dt_p
        x_lo = jnp.where(lo128, xdt, 0.0).astype(BF16)
        x_hi = jnp.where(lo128, 0.0, xdt).astype(BF16)
        y_p = None
        for h, x_half in ((2 * j, x_lo), (2 * j + 1, x_hi)):
            seg = col(lc, h, rows) - jnp.broadcast_to(lc_t[DT_LANE + h:DT_LANE + h + 1, :], (rows, rows))
            w_h = jnp.where(tril, cb[g] * jnp.exp(seg), 0.0).astype(BF16)
            y_h = _dot(w_h, x_half)
            y_p = y_h if y_p is None else y_p + y_h
            yield

        elc_p = jnp.where(lo128, col(elc, 2 * j, 128), col(elc, 2 * j + 1, 128))
        w_p = jnp.where(lo128, col(w, 2 * j, 128), col(w, 2 * j + 1, 128))
        xw = (xs_p * w_p).astype(BF16)
        bm_g = bmb[:, g * SSD_STATE:(g + 1) * SSD_STATE]
        cm_g = cmb[:, g * SSD_STATE:(g + 1) * SSD_STATE]
        y_inter = None
        for s in range(n_seq):
            st = get_state(s, j)
            ci = _dot_nt(cm_g, st.astype(BF16))
            if n_seq == 1:
                y_inter = ci
                xw_s = xw
            else:
                msk = _row_in_seq(rows, seq_len, s, 128)
                ci = jnp.where(msk, ci, 0.0)
                y_inter = ci if y_inter is None else y_inter + ci
                xw_s = jnp.where(msk, xw, jnp.zeros_like(xw))
            upd = _dot_tn(xw_s, bm_g)
            r0 = s * seq_len
            e0 = jnp.broadcast_to(ell[r0:r0 + 1, DT_LANE + 2 * j:DT_LANE + 2 * j + 1], (128, 128))
            e1 = jnp.broadcast_to(ell[r0:r0 + 1, DT_LANE + 2 * j + 1:DT_LANE + 2 * j + 2], (128, 128))
            put_state(s, j, jnp.where(sub128, e0, e1) * st + upd)
        ys.append(y_p + y_inter * elc_p + d_row[:, 128 * j:128 * (j + 1)] * xs_p)
        yield


def _ssd_phases_seq(xs, bm, cm, dt_raw_blk, dtb_blk, a_blk, d_row, get_state, put_state, ys):
    rows = xs.shape[0]
    _, tril = _seg_masks(rows, rows)
    tril_b = _mask_to_bf16(tril, rows)
    src = lax.broadcasted_iota(jnp.int32, (rows, rows), 0)
    dst = lax.broadcasted_iota(jnp.int32, (rows, rows), 1)
    causal_t = src <= dst

    dt = _softplus(dt_raw_blk + dtb_blk)
    lc = _dot3(tril_b, dt * a_blk)
    lc_t = lc.T
    dt_t = dt.T
    elc_t = jnp.exp(lc_t)
    w_t = jnp.exp(lc_t[:, rows - 1:rows] - lc_t) * dt_t
    ell = jnp.exp(lc[rows - 1:rows, :])
    yield

    xs_t = xs.T
    bmb = bm.astype(BF16)
    cmb = cm.astype(BF16)
    cb_t = [_dot_nt(bmb[:, g * SSD_STATE:(g + 1) * SSD_STATE],
                    cmb[:, g * SSD_STATE:(g + 1) * SSD_STATE]) for g in range(SSD_GROUPS)]
    yield

    heads_per_group = SSD_HEADS // SSD_GROUPS
    y_t = []
    for h in range(SSD_HEADS):
        g = h // heads_per_group
        lane = DT_LANE + h
        xs_h = xs_t[h * SSD_HEAD_DIM:(h + 1) * SSD_HEAD_DIM, :]
        seg_t = lc_t[lane:lane + 1, :] - jnp.broadcast_to(lc[:, lane:lane + 1], (rows, rows))
        w_h = jnp.where(causal_t, cb_t[g] * jnp.exp(seg_t), 0.0).astype(BF16)
        y_h = _dot((xs_h * dt_t[lane:lane + 1, :]).astype(BF16), w_h)
        st = get_state(h)
        ci = _dot_nt(st.astype(BF16), cmb[:, g * SSD_STATE:(g + 1) * SSD_STATE])
        y_t.append(y_h + ci * elc_t[lane:lane + 1, :])
        upd = _dot((xs_h * w_t[lane:lane + 1, :]).astype(BF16), bmb[:, g * SSD_STATE:(g + 1) * SSD_STATE])
        put_state(h, jnp.broadcast_to(ell[:, lane:lane + 1], st.shape) * st + upd)
        yield
    ys.append(jnp.concatenate(y_t, axis=0).T + d_row * xs)


def _gate_logf(small_blk, gw2_ref, gb_ref):
    logits = _dot(_bf(small_blk), gw2_ref[...]) + gb_ref[...]
    return _log_sigmoid(logits) * (1.0 / GLA_GATE_TAU)


def _gla_finish(o_h, g_h, gnw):
    return _rms(o_h, gnw) * _silu(g_h)


def _ssd_finish(y, z, snw):
    y = y * _silu(z)
    gs = SSD_INNER // SSD_GROUPS
    parts = [_rms(y[:, g * gs:(g + 1) * gs], snw[:, g * gs:(g + 1) * gs]) for g in range(SSD_GROUPS)]
    return jnp.concatenate(parts, axis=1)


def _memkv_kernel(mem_ref, nw_ref, wk_ref, wv_ref, k_ref, v_ref, kb_ref, vb_ref):
    mn = _bf(_rms(mem_ref[0], nw_ref[...]))
    mk = _dot(mn, wk_ref[...])
    mv = _dot(mn, wv_ref[...])
    for h in range(XA_HEADS):
        k_ref[0, 0, :, h, :] = mk[:, h * XA_HEAD_DIM:(h + 1) * XA_HEAD_DIM]
        v_ref[0, 0, :, h, :] = mv[:, h * XA_HEAD_DIM:(h + 1) * XA_HEAD_DIM]
    kb_ref[0] = _bf(mk)
    vb_ref[0] = _bf(mv)


def _const_spec(shape):
    nd = len(shape)
    return pl.BlockSpec(shape, lambda *_: (0,) * nd, pipeline_mode=pl.Buffered(1))


def _memkv(mem, nw, wk, wv):
    b, m, d = mem.shape
    blk = pl.BlockSpec((1, m, d), lambda i: (i, 0, 0))
    cache_blk = pl.BlockSpec((1, 1, m, XA_HEADS, XA_HEAD_DIM), lambda i: (0, i, 0, 0, 0))
    cache_shape = jax.ShapeDtypeStruct((1, b, m, XA_HEADS, XA_HEAD_DIM), F32)
    return pl.pallas_call(
        _memkv_kernel,
        grid=(b,),
        in_specs=[blk, _const_spec((1, d)), _const_spec((d, d)), _const_spec((d, d))],
        out_specs=[cache_blk, cache_blk, blk, blk],
        out_shape=[cache_shape] * 2 + [jax.ShapeDtypeStruct((b, m, d), BF16)] * 2,
        compiler_params=pltpu.CompilerParams(dimension_semantics=("arbitrary",),
                                             vmem_limit_bytes=VMEM_LIMIT),
        name="mem_kv",
    )(mem, nw, wk, wv)


PROJ_CHUNK = 512


def _mixer_prompt_chain(x_ref, h_ref, proj_s, xpad_s, mixed_s, sg_s, ss_s,
                        lnpre_ref, win_ref, gw2_ref, gb_ref, gnw_ref, cw_ref, cb_ref,
                        dtb_ref, a_ref, d_ref, snw_ref, wout_ref, lnpost_ref):
    rows = TILE
    x = x_ref[...]
    hn = _rms(x, lnpre_ref[...]).astype(BF16)
    yield

    def proj_cols(c0, width):
        proj_s[:, c0:c0 + width] = _dot(hn, win_ref[:, c0:c0 + width])

    def late_proj():
        for c0 in (COL_Q, COL_V, COL_G, COL_Z):
            proj_cols(c0, PROJ_CHUNK)
            yield

    side = [late_proj()]

    def tick():
        if side[0] is not None:
            try:
                next(side[0])
            except StopIteration:
                side[0] = None

    for c in range(SSD_CONV_CH // PROJ_CHUNK):
        proj_cols(COL_XBC + c * PROJ_CHUNK, PROJ_CHUNK)
        yield
    proj_cols(COL_SMALL, 128)

    xa = []
    for c in range(SSD_CONV_CH // PROJ_CHUNK):
        cs = slice(c * PROJ_CHUNK, (c + 1) * PROJ_CHUNK)
        xpad_s[8:8 + rows, cs] = proj_s[:, COL_XBC + c * PROJ_CHUNK:COL_XBC + (c + 1) * PROJ_CHUNK]
        conv = cb_ref[:, cs]
        for j in range(SSD_CONV):
            conv = conv + xpad_s[5 + j:5 + j + rows, cs] * cw_ref[j:j + 1, cs]
        xpad_s[5:8, cs] = xpad_s[5 + rows:8 + rows, cs]
        xa.append(_silu(conv))
        tick()
        yield

    small = proj_s[:, COL_SMALL:COL_SMALL + 128]
    logf = _gate_logf(small, gw2_ref, gb_ref)

    def get_ss(h):
        return ss_s[SSD_HEAD_DIM * h:SSD_HEAD_DIM * (h + 1), :]

    def put_ss(h, val):
        ss_s[SSD_HEAD_DIM * h:SSD_HEAD_DIM * (h + 1), :] = val

    ys = []
    ssd = _ssd_phases_seq(xa[0], xa[1][:, :SSD_GROUPS * SSD_STATE], xa[1][:, SSD_GROUPS * SSD_STATE:],
                          small, dtb_ref[...], a_ref[...], d_ref[...], get_ss, put_ss, ys)

    def get_sg(s):
        return sg_s[...]

    def put_sg(s, val):
        sg_s[...] = val

    def gla_chunks():
        for c in range(rows // GLA_CHUNK):
            r0 = c * GLA_CHUNK
            sl = slice(r0, r0 + GLA_CHUNK)
            outs = []
            yield from _gla_phases(proj_s[sl, COL_Q:COL_Q + GLA_KW], proj_s[sl, COL_K:COL_K + GLA_KW],
                                   proj_s[sl, COL_V:COL_V + GLA_VW], logf[sl], get_sg, put_sg, GLA_CHUNK, outs)
            for h in range(GLA_HEADS):
                g_h = proj_s[sl, COL_G + h * GLA_DV:COL_G + (h + 1) * GLA_DV]
                mixed_s[sl, h * GLA_DV:(h + 1) * GLA_DV] = _gla_finish(outs[h], g_h, gnw_ref[...]).astype(BF16)
            yield

    while side[0] is not None:
        next(ssd)
        tick()
        yield
    alive = [ssd, gla_chunks()]
    while alive:
        for phases in list(alive):
            try:
                next(phases)
            except StopIteration:
                alive.remove(phases)
                continue
            yield
    mixed_s[:, GLA_VW:] = _ssd_finish(jnp.concatenate(ys, axis=1), proj_s[:, COL_Z:COL_Z + SSD_INNER],
                                      snw_ref[...]).astype(BF16)
    yield

    mixed = mixed_s[...]
    ms = []
    for c in range(D_MODEL // PROJ_CHUNK):
        ms.append(_dot(mixed, wout_ref[:, c * PROJ_CHUNK:(c + 1) * PROJ_CHUNK]))
        yield
    h_ref[...] = x + _rms(jnp.concatenate(ms, axis=1), lnpost_ref[...])


N_MIXER_WEIGHTS = 13
N_MIXER_SCRATCH = 5
MIX_OFFSET = 3


def _mixer_prompt_kernel(x_ref, *rest):
    consts = rest[:N_MIXER_WEIGHTS]
    h_ref, sg_out, ss_out, conv_out = rest[N_MIXER_WEIGHTS:N_MIXER_WEIGHTS + 4]
    scratch = rest[N_MIXER_WEIGHTS + 4:]
    per_seq = [scratch[i * N_MIXER_SCRATCH:(i + 1) * N_MIXER_SCRATCH] for i in range(MIX_SEQS)]
    t = pl.program_id(1)

    @pl.when(t == 0)
    def _():
        for _, xpad_s, _, sg_s, ss_s in per_seq:
            sg_s[...] = jnp.zeros_like(sg_s)
            ss_s[...] = jnp.zeros_like(ss_s)
            xpad_s[0:8, :] = jnp.zeros((8, SSD_CONV_CH), F32)

    _interleave([_mixer_prompt_chain(x_ref.at[i], h_ref.at[i], *per_seq[i], *consts) for i in range(MIX_SEQS)],
                offset=MIX_OFFSET)

    @pl.when(t == pl.num_programs(1) - 1)
    def _():
        for i, (_, xpad_s, _, sg_s, ss_s) in enumerate(per_seq):
            sg_out[i] = sg_s[...].reshape(GLA_HEADS, GLA_DK, GLA_DV)
            ss_out[i] = ss_s[...].reshape(SSD_HEADS, SSD_HEAD_DIM, SSD_STATE)
            conv_out[i] = xpad_s[5:8, :]


def _mixer_prompt(x, p):
    b, t, d = x.shape
    nt = t // TILE
    consts = [p["ln_mix_pre"], p["w_in"], p["gw2"], p["gb"], p["gnw"], p["conv_w"], p["conv_b"],
              p["dtb"], p["a_blk"], p["d_row"], p["snw"], p["w_out"], p["ln_mix_post"]]
    assert len(consts) == N_MIXER_WEIGHTS
    tok = pl.BlockSpec((MIX_SEQS, TILE, d), lambda i, j: (i, j, 0))
    return pl.pallas_call(
        _mixer_prompt_kernel,
        grid=(b // MIX_SEQS, nt),
        in_specs=[tok] + [_const_spec(c.shape) for c in consts],
        out_specs=[tok,
                   pl.BlockSpec((MIX_SEQS, GLA_HEADS, GLA_DK, GLA_DV), lambda i, j: (i, 0, 0, 0)),
                   pl.BlockSpec((MIX_SEQS, SSD_HEADS, SSD_HEAD_DIM, SSD_STATE), lambda i, j: (i, 0, 0, 0)),
                   pl.BlockSpec((MIX_SEQS, SSD_CONV - 1, SSD_CONV_CH), lambda i, j: (i, 0, 0))],
        out_shape=[jax.ShapeDtypeStruct((b, t, d), F32),
                   jax.ShapeDtypeStruct((b, GLA_HEADS, GLA_DK, GLA_DV), F32),
                   jax.ShapeDtypeStruct((b, SSD_HEADS, SSD_HEAD_DIM, SSD_STATE), F32),
                   jax.ShapeDtypeStruct((b, SSD_CONV - 1, SSD_CONV_CH), F32)],
        scratch_shapes=[pltpu.VMEM((TILE, PROJ_COLS), F32),
                        pltpu.VMEM((TILE + 8, SSD_CONV_CH), F32),
                        pltpu.VMEM((TILE, D_MODEL), BF16),
                        pltpu.VMEM((GLA_KW, GLA_DV), F32),
                        pltpu.VMEM((SSD_INNER, SSD_STATE), F32)] * MIX_SEQS,
        compiler_params=pltpu.CompilerParams(dimension_semantics=("arbitrary", "arbitrary"),
                                             vmem_limit_bytes=VMEM_LIMIT),
        name="mixer_prompt",
    )(x, *consts)


def _softmax_rows(s):
    e = jnp.exp(s - jnp.max(s, axis=-1, keepdims=True))
    return e / jnp.sum(e, axis=-1, keepdims=True)


def _ffn(h, lnpre, lnpost, wg_ref, wu_ref, wd_ref):
    hf = _bf(_rms(h, lnpre))
    gt = _dot(hf, wg_ref[...])
    up = _dot(hf, wu_ref[...])
    f = _dot(_bf(_silu(gt) * up), wd_ref[...])
    return h + _rms(f, lnpost)


FFN_CHUNK = 256
FFN_OFFSET = 1


def _attn_ffn_chain(h_ref, y_ref, mk_ref, mv_ref, lnxa_ref, wq_ref, wo_ref, lnxap_ref,
                    lnf_ref, lnfp_ref, wg_ref, wu_ref, wd_ref):
    h = h_ref[...]
    hn = _rms(h, lnxa_ref[...]).astype(BF16)
    yield
    heads_per_chunk = PROJ_CHUNK // XA_HEAD_DIM
    outs = []
    for c in range(D_MODEL // PROJ_CHUNK):
        q = _dot(hn, wq_ref[:, c * PROJ_CHUNK:(c + 1) * PROJ_CHUNK]).astype(BF16)
        yield
        for hh in range(heads_per_chunk):
            hd = c * heads_per_chunk + hh
            sl = slice(hd * XA_HEAD_DIM, (hd + 1) * XA_HEAD_DIM)
            s = _dot_nt(q[:, hh * XA_HEAD_DIM:(hh + 1) * XA_HEAD_DIM], mk_ref[0, :, sl])
            yield
            pr = _softmax_rows(s).astype(BF16)
            yield
            outs.append(_dot(pr, mv_ref[0, :, sl]).astype(BF16))
    a = jnp.concatenate(outs, axis=1)
    yield
    parts = []
    for c in range(D_MODEL // PROJ_CHUNK):
        parts.append(_dot(a, wo_ref[:, c * PROJ_CHUNK:(c + 1) * PROJ_CHUNK]))
        yield
    h = h + _rms(jnp.concatenate(parts, axis=1), lnxap_ref[...])
    hf = _rms(h, lnf_ref[...]).astype(BF16)
    yield
    acts = []
    for c in range(D_FF // FFN_CHUNK):
        cs = slice(c * FFN_CHUNK, (c + 1) * FFN_CHUNK)
        gt = _dot(hf, wg_ref[:, cs])
        up = _dot(hf, wu_ref[:, cs])
        yield
        acts.append((_silu(gt) * up).astype(BF16))
    act = jnp.concatenate(acts, axis=1)
    yield
    parts = []
    for c in range(D_MODEL // PROJ_CHUNK):
        parts.append(_dot(act, wd_ref[:, c * PROJ_CHUNK:(c + 1) * PROJ_CHUNK]))
        yield
    y_ref[...] = h + _rms(jnp.concatenate(parts, axis=1), lnfp_ref[...])


def _attn_ffn_prompt_kernel(h_ref, mk_ref, mv_ref, *rest):
    consts, y_ref = rest[:-1], rest[-1]
    chains = []
    for sub in range(FFN_SUBTILES):
        rs = pl.ds(sub * TILE, TILE)
        chains.append(_attn_ffn_chain(h_ref.at[0, rs], y_ref.at[0, rs], mk_ref, mv_ref, *consts))
    _interleave(chains, offset=FFN_OFFSET)


def _attn_ffn_prompt(h, mkb, mvb, p):
    b, t, d = h.shape
    rows = FFN_SUBTILES * TILE
    consts = [p["ln_xa_pre"], p["w_xq"], p["w_xo"], p["ln_xa_post"], p["ln_ffn_pre"], p["ln_ffn_post"],
              p["w_gate"], p["w_up"], p["w_down"]]
    tok = pl.BlockSpec((1, rows, d), lambda i, j: (i, j, 0))
    mem = pl.BlockSpec((1, mkb.shape[1], d), lambda i, j: (i, 0, 0))
    return pl.pallas_call(
        _attn_ffn_prompt_kernel,
        grid=(b, t // rows),
        in_specs=[tok, mem, mem] + [_const_spec(c.shape) for c in consts],
        out_specs=tok,
        out_shape=jax.ShapeDtypeStruct((b, t, d), F32),
        compiler_params=pltpu.CompilerParams(dimension_semantics=("arbitrary", "arbitrary"),
                                             vmem_limit_bytes=VMEM_LIMIT),
        name="attn_ffn_prompt",
    )(h, mkb, mvb, *consts)


def _proj_sample_kernel(x_ref, lnpre_ref, win_ref, proj_ref):
    proj_ref[...] = _dot(_bf(_rms(x_ref[...], lnpre_ref[...])), win_ref[...])


def _proj_sample(x, p):
    n, d = x.shape
    return pl.pallas_call(
        _proj_sample_kernel,
        grid=(1,),
        in_specs=[_const_spec((n, d)), _const_spec((1, d)), _const_spec(p["w_in"].shape)],
        out_specs=pl.BlockSpec((n, PROJ_COLS), lambda i: (0, 0)),
        out_shape=jax.ShapeDtypeStruct((n, PROJ_COLS), F32),
        compiler_params=pltpu.CompilerParams(dimension_semantics=("arbitrary",),
                                             vmem_limit_bytes=VMEM_LIMIT),
        name="proj_sample",
    )(x, p["ln_mix_pre"], p["w_in"])


def _mixer_sample_kernel(seq_len, proj_ref, tail_ref, sg_ref, ss_ref, gw2_ref, gb_ref, gnw_ref,
                         cw_ref, cb_ref, dtb_ref, a_ref, d_ref, snw_ref,
                         mixed_ref, sg_out, ss_out, xpad_s, tpad_s):
    rows = proj_ref.shape[0]
    n_seq = rows // seq_len
    small = proj_ref[:, COL_SMALL:COL_SMALL + 128]
    logf = _gate_logf(small, gw2_ref, gb_ref)

    xbc = proj_ref[:, COL_XBC:COL_XBC + SSD_CONV_CH]
    xpad_s[0:8, :] = jnp.zeros((8, SSD_CONV_CH), F32)
    xpad_s[8:8 + rows, :] = xbc
    tpad_s[0:rows, :] = tail_ref[...]
    tpad_s[rows:rows + 8, :] = jnp.zeros((8, SSD_CONV_CH), F32)
    pos = lax.broadcasted_iota(jnp.int32, (rows, SSD_CONV_CH), 0) & (seq_len - 1)
    conv = cb_ref[...]
    for j in range(SSD_CONV - 1):
        back = SSD_CONV - 1 - j
        prev = jnp.where(pos >= back, xpad_s[8 - back:8 - back + rows, :],
                         tpad_s[seq_len - back:seq_len - back + rows, :])
        conv = conv + prev * cw_ref[j:j + 1, :]
    conv = conv + xbc * cw_ref[SSD_CONV - 1:SSD_CONV, :]
    xa = _silu(conv)

    def get_ss(s, j):
        return ss_ref[s, 2 * j:2 * j + 2].reshape(2 * SSD_HEAD_DIM, SSD_STATE)

    def put_ss(s, j, val):
        ss_out[s, 2 * j:2 * j + 2] = val.reshape(2, SSD_HEAD_DIM, SSD_STATE)

    ys = []
    _run(_ssd_phases(xa[:, :SSD_INNER], xa[:, SSD_INNER:SSD_INNER + 256], xa[:, SSD_INNER + 256:],
                     small, dtb_ref[...], a_ref[...], d_ref[...], get_ss, put_ss, seq_len, ys))
    mixed_ref[:, GLA_VW:] = _ssd_finish(jnp.concatenate(ys, axis=1), proj_ref[:, COL_Z:COL_Z + SSD_INNER],
                                        snw_ref[...])

    def get_sg(s):
        return sg_ref[s].reshape(GLA_KW, GLA_DV)

    def put_sg(s, val):
        sg_out[s] = val.reshape(GLA_HEADS, GLA_DK, GLA_DV)

    outs = []
    _run(_gla_phases(proj_ref[:, COL_Q:COL_Q + GLA_KW], proj_ref[:, COL_K:COL_K + GLA_KW],
                     proj_ref[:, COL_V:COL_V + GLA_VW], logf, get_sg, put_sg, seq_len, outs))
    for h in range(GLA_HEADS):
        g_h = proj_ref[:, COL_G + h * GLA_DV:COL_G + (h + 1) * GLA_DV]
        mixed_ref[:, h * GLA_DV:(h + 1) * GLA_DV] = _gla_finish(outs[h], g_h, gnw_ref[...])


def _mixer_sample(proj, tail, sg, ss, p, seq_len):
    n = proj.shape[0]
    rows = SAMPLE_SEQS * seq_len
    nb = sg.shape[0]
    consts = [p["gw2"], p["gb"], p["gnw"], p["conv_w"], p["conv_b"], p["dtb"], p["a_blk"], p["d_row"],
              p["snw"]]
    sg_spec = pl.BlockSpec((SAMPLE_SEQS, GLA_HEADS, GLA_DK, GLA_DV), lambda i: (i, 0, 0, 0))
    ss_spec = pl.BlockSpec((SAMPLE_SEQS, SSD_HEADS, SSD_HEAD_DIM, SSD_STATE), lambda i: (i, 0, 0, 0))
    return pl.pallas_call(
        functools.partial(_mixer_sample_kernel, seq_len),
        grid=(nb // SAMPLE_SEQS,),
        in_specs=[pl.BlockSpec((rows, PROJ_COLS), lambda i: (i, 0)),
                  pl.BlockSpec((rows, SSD_CONV_CH), lambda i: (i, 0)),
                  sg_spec, ss_spec] + [_const_spec(c.shape) for c in consts],
        out_specs=[pl.BlockSpec((rows, D_MODEL), lambda i: (i, 0)), sg_spec, ss_spec],
        out_shape=[jax.ShapeDtypeStruct((n, D_MODEL), F32),
                   jax.ShapeDtypeStruct(sg.shape, F32), jax.ShapeDtypeStruct(ss.shape, F32)],
        scratch_shapes=[pltpu.VMEM((rows + 8, SSD_CONV_CH), F32),
                        pltpu.VMEM((rows + 8, SSD_CONV_CH), F32)],
        compiler_params=pltpu.CompilerParams(dimension_semantics=("arbitrary",),
                                             vmem_limit_bytes=VMEM_LIMIT),
        name="mixer_sample",
    )(proj, tail, sg, ss, *consts)


def _post_mix_sample_kernel(x_ref, mixed_ref, wout_ref, lnpost_ref, lnxa_ref, wq_ref, h_ref, q_ref):
    m = _dot(_bf(mixed_ref[...]), wout_ref[...])
    h = x_ref[...] + _rms(m, lnpost_ref[...])
    h_ref[...] = h
    q_ref[...] = _bf(_dot(_bf(_rms(h, lnxa_ref[...])), wq_ref[...]))


def _post_mix_sample(x, mixed, p):
    n, d = x.shape
    full = pl.BlockSpec((n, d), lambda i: (0, 0))
    consts = [p["w_out"], p["ln_mix_post"], p["ln_xa_pre"], p["w_xq"]]
    return pl.pallas_call(
        _post_mix_sample_kernel,
        grid=(1,),
        in_specs=[full, full] + [_const_spec(c.shape) for c in consts],
        out_specs=[full, full],
        out_shape=[jax.ShapeDtypeStruct((n, d), F32), jax.ShapeDtypeStruct((n, d), BF16)],
        compiler_params=pltpu.CompilerParams(dimension_semantics=("arbitrary",),
                                             vmem_limit_bytes=VMEM_LIMIT),
        name="post_mix_sample",
    )(x, mixed, *consts)


def _attn_sample_kernel(seq_len, q_ref, k_hbm, v_hbm, a_ref, kbuf, vbuf, sem):
    step = pl.program_id(0)
    rows = q_ref.shape[0]
    n_seq = rows // seq_len

    def copies(at_step, slot):
        out = []
        for s in range(n_seq):
            for hd in range(XA_HEADS):
                b = at_step * n_seq + s
                out.append(pltpu.make_async_copy(k_hbm.at[0, b, :, hd, :], kbuf.at[slot, s, hd], sem.at[0, slot]))
                out.append(pltpu.make_async_copy(v_hbm.at[0, b, :, hd, :], vbuf.at[slot, s, hd], sem.at[1, slot]))
        return out

    @pl.when(step == 0)
    def _():
        for cp in copies(0, 0):
            cp.start()

    @pl.when(step + 1 < pl.num_programs(0))
    def _():
        for cp in copies(step + 1, (step + 1) % 2):
            cp.start()

    slot = step % 2
    for cp in copies(step, slot):
        cp.wait()

    q = q_ref[...]
    results = {}

    def one(s, hd):
        sc = _dot_nt(q[:, hd * XA_HEAD_DIM:(hd + 1) * XA_HEAD_DIM], _bf(kbuf[slot, s, hd]))
        yield
        e = jnp.exp(sc - jnp.max(sc, axis=-1, keepdims=True))
        pr = _bf(e / jnp.sum(e, axis=-1, keepdims=True))
        yield
        results[s, hd] = _dot(pr, _bf(vbuf[slot, s, hd]))

    _interleave([one(s, hd) for s in range(n_seq) for hd in range(XA_HEADS)])
    for hd in range(XA_HEADS):
        sl = slice(hd * XA_HEAD_DIM, (hd + 1) * XA_HEAD_DIM)
        out = results[0, hd]
        for s in range(1, n_seq):
            out = jnp.where(_row_in_seq(rows, seq_len, s, XA_HEAD_DIM), results[s, hd], out)
        a_ref[:, sl] = out


def _attn_sample(q, ck, cv, seq_len):
    n, d = q.shape
    _, nb, m, nh, hd = ck.shape
    rows = ATTN_SEQS * seq_len
    tok = pl.BlockSpec((rows, d), lambda i: (i, 0))
    hbm = pl.BlockSpec(memory_space=pl.ANY)
    return pl.pallas_call(
        functools.partial(_attn_sample_kernel, seq_len),
        grid=(nb // ATTN_SEQS,),
        in_specs=[tok, hbm, hbm],
        out_specs=tok,
        out_shape=jax.ShapeDtypeStruct((n, d), F32),
        scratch_shapes=[pltpu.VMEM((2, ATTN_SEQS, nh, m, hd), F32),
                        pltpu.VMEM((2, ATTN_SEQS, nh, m, hd), F32),
                        pltpu.SemaphoreType.DMA((2, 2))],
        compiler_params=pltpu.CompilerParams(dimension_semantics=("arbitrary",),
                                             vmem_limit_bytes=VMEM_LIMIT),
        name="attn_sample",
    )(q, ck, cv)


def _post_attn_sample_kernel(h_ref, a_ref, wo_ref, lnxap_ref, lnf_ref, lnfp_ref, wg_ref, wu_ref, wd_ref,
                             y_ref):
    a = _dot(_bf(a_ref[...]), wo_ref[...])
    h = h_ref[...] + _rms(a, lnxap_ref[...])
    y_ref[...] = _ffn(h, lnf_ref[...], lnfp_ref[...], wg_ref, wu_ref, wd_ref)


def _post_attn_sample(h, a, p):
    n, d = h.shape
    full = pl.BlockSpec((n, d), lambda i: (0, 0))
    consts = [p["w_xo"], p["ln_xa_post"], p["ln_ffn_pre"], p["ln_ffn_post"], p["w_gate"], p["w_up"],
              p["w_down"]]
    return pl.pallas_call(
        _post_attn_sample_kernel,
        grid=(1,),
        in_specs=[full, full] + [_const_spec(c.shape) for c in consts],
        out_specs=full,
        out_shape=jax.ShapeDtypeStruct((n, d), F32),
        compiler_params=pltpu.CompilerParams(dimension_semantics=("arbitrary",),
                                             vmem_limit_bytes=VMEM_LIMIT),
        name="post_attn_sample",
    )(h, a, *consts)


SRC_GLR = GLA_KW + GLA_KW + GLA_VW + GLA_VW
SRC_Z = SRC_GLR + GLA_GATE_RANK
SRC_XBC = SRC_Z + SSD_INNER
SRC_DT = SRC_XBC + SSD_CONV_CH
IN_COLS = SRC_DT + SSD_HEADS


def _pack_w_in_kernel(wt_hbm, o_ref, buf, sem):
    pieces = [(0, COL_Q, SRC_GLR),
              (SRC_Z, COL_Z, SSD_INNER),
              (SRC_XBC, COL_XBC, SSD_CONV_CH),
              (SRC_GLR, COL_SMALL, GLA_GATE_RANK),
              (SRC_DT, COL_SMALL + DT_LANE, SSD_HEADS)]
    copies = [pltpu.make_async_copy(wt_hbm.at[0, pl.ds(src, n), :], buf.at[pl.ds(dst, n), :], sem.at[0])
              for src, dst, n in pieces]
    for cp in copies:
        cp.start()
    used = COL_SMALL + DT_LANE + SSD_HEADS
    buf[used:PROJ_COLS, :] = jnp.zeros((PROJ_COLS - used, D_MODEL), F32)
    for cp in copies:
        cp.wait()
    for j in range(PROJ_COLS // 128):
        blk = buf[128 * j:128 * (j + 1), :]
        if 128 * j < COL_K:
            blk = blk * (GLA_DK ** -0.5)
        o_ref[:, 128 * j:128 * (j + 1)] = blk.T.astype(BF16)


def _pack_w_in(w_in):
    wt = jnp.swapaxes(w_in, 1, 2)
    return pl.pallas_call(
        _pack_w_in_kernel,
        grid=(1,),
        in_specs=[pl.BlockSpec(memory_space=pl.ANY)],
        out_specs=pl.BlockSpec((D_MODEL, PROJ_COLS), lambda i: (0, 0)),
        out_shape=jax.ShapeDtypeStruct((D_MODEL, PROJ_COLS), BF16),
        scratch_shapes=[pltpu.VMEM((PROJ_COLS, D_MODEL), F32), pltpu.SemaphoreType.DMA((1,))],
        compiler_params=pltpu.CompilerParams(dimension_semantics=("arbitrary",),
                                             vmem_limit_bytes=VMEM_LIMIT),
        name="pack_w_in",
    )(wt)


def _pack_params(ln_mix_pre, ln_mix_post, w_in_packed, gla_gate_w2, gla_gate_b, gla_norm_w, ssd_conv_w,
                 ssd_conv_b, ssd_dt_bias, ssd_A_log, ssd_D, ssd_norm_w, w_out, ln_xa_pre, ln_xa_post,
                 mem_norm_w, w_xq, w_xk, w_xv, w_xo, ln_ffn_pre, ln_ffn_post, w_gate, w_up, w_down):

    def small_row(v):
        return jnp.zeros((1, 128), F32).at[0, DT_LANE:DT_LANE + SSD_HEADS].set(v)

    row = lambda v: v.reshape(1, -1)
    return {
        "ln_mix_pre": row(ln_mix_pre), "ln_mix_post": row(ln_mix_post),
        "w_in": w_in_packed,
        "gw2": _bf(jnp.zeros((128, GLA_KW), F32).at[:GLA_GATE_RANK].set(gla_gate_w2)),
        "gb": row(gla_gate_b), "gnw": row(gla_norm_w),
        "conv_w": ssd_conv_w, "conv_b": row(ssd_conv_b),
        "dtb": small_row(ssd_dt_bias),
        "a_blk": small_row(-jnp.exp(ssd_A_log)),
        "d_row": jnp.repeat(ssd_D, SSD_HEAD_DIM).reshape(1, SSD_INNER),
        "snw": row(ssd_norm_w),
        "w_out": _bf(w_out),
        "ln_xa_pre": row(ln_xa_pre), "ln_xa_post": row(ln_xa_post), "mem_norm_w": row(mem_norm_w),
        "w_xq": _bf(w_xq * (XA_HEAD_DIM ** -0.5)), "w_xk": _bf(w_xk), "w_xv": _bf(w_xv), "w_xo": _bf(w_xo),
        "ln_ffn_pre": row(ln_ffn_pre), "ln_ffn_post": row(ln_ffn_post),
        "w_gate": _bf(w_gate), "w_up": _bf(w_up), "w_down": _bf(w_down),
    }


def kernel(x_prompt, x_sample, mem_prompt, state_gla, state_ssm, state_conv, cache_mem_k, cache_mem_v,
           ln_mix_pre, ln_mix_post, w_in, gla_gate_w2, gla_gate_b, gla_norm_w, ssd_conv_w, ssd_conv_b,
           ssd_dt_bias, ssd_A_log, ssd_D, ssd_norm_w, w_out, ln_xa_pre, ln_xa_post, mem_norm_w,
           w_xq, w_xk, w_xv, w_xo, ln_ffn_pre, ln_ffn_post, w_gate, w_up, w_down):
    assert w_in.shape[0] == 1, "single-layer kernel"
    layer = [a[0] for a in (ln_mix_pre, ln_mix_post, w_in, gla_gate_w2, gla_gate_b, gla_norm_w, ssd_conv_w,
                            ssd_conv_b, ssd_dt_bias, ssd_A_log, ssd_D, ssd_norm_w, w_out, ln_xa_pre,
                            ln_xa_post, mem_norm_w, w_xq, w_xk, w_xv, w_xo, ln_ffn_pre, ln_ffn_post,
                            w_gate, w_up, w_down)]
    layer[2] = _pack_w_in(w_in)
    p = _pack_params(*layer)
    bp, mem_len, _ = mem_prompt.shape
    bs, ts, _ = x_sample.shape
    assert ts == SSD_CONV, "the padded conv-tail layout assumes one new row per conv tap"

    mk, mv, mkb, mvb = _memkv(mem_prompt, p["mem_norm_w"], p["w_xk"], p["w_xv"])
    h_p, gla_p, ssm_p, conv_p = _mixer_prompt(x_prompt, p)
    y_p = _attn_ffn_prompt(h_p, mkb, mvb, p)

    xs = x_sample.reshape(bs * ts, D_MODEL)
    proj = _proj_sample(xs, p)
    tail = jnp.pad(state_conv[0], ((0, 0), (1, 0), (0, 0))).reshape(bs * ts, SSD_CONV_CH)
    mixed, gla_s, ssm_s = _mixer_sample(proj, tail, state_gla[0], state_ssm[0], p, ts)
    h_s, q_s = _post_mix_sample(xs, mixed, p)
    a_s = _attn_sample(q_s, cache_mem_k, cache_mem_v, ts)
    y_s = _post_attn_sample(h_s, a_s, p)
    conv_s = proj[:, COL_XBC:COL_XBC + SSD_CONV_CH].reshape(bs, ts, SSD_CONV_CH)[:, ts - (SSD_CONV - 1):]

    return (y_p, y_s.reshape(bs, ts, D_MODEL), gla_p[None], ssm_p[None], conv_p[None],
            mk, mv, gla_s[None], ssm_s[None], conv_s[None])
```

```python
import functools

import jax
import jax.numpy as jnp
from jax import lax
from jax.experimental import pallas as pl
from jax.experimental.pallas import tpu as pltpu

F32 = jnp.float32
BF16 = jnp.bfloat16

D_MODEL = 1024
GLA_HEADS = 4
GLA_DK = 64
GLA_DV = 128
GLA_KW = GLA_HEADS * GLA_DK
GLA_VW = GLA_HEADS * GLA_DV
GLA_GATE_RANK = 16
GLA_GATE_TAU = 16.0
SSD_INNER = 512
SSD_HEAD_DIM = 64
SSD_HEADS = 8
SSD_GROUPS = 2
SSD_STATE = 128
SSD_CONV = 4
SSD_CONV_CH = 1024
XA_HEADS = 4
XA_HEAD_DIM = 256
D_FF = 2816
EPS = 1e-6

COL_Q = 0
COL_K = 256
COL_V = 512
COL_G = 1024
COL_Z = 1536
COL_XBC = 2048
COL_SMALL = 3072
DT_LANE = GLA_GATE_RANK
PROJ_COLS = 3200

GLA_CHUNK = 64
TILE = 256
FFN_SUBTILES = 4
MIX_SEQS = 4
SAMPLE_SEQS = 16
ATTN_SEQS = 4
VMEM_LIMIT = 56 * 1024 * 1024


def _bf(x):
    return x.astype(BF16)


def _dot(a, b):
    return jnp.dot(a, b, preferred_element_type=F32)


def _dot_nt(a, b):
    return lax.dot_general(a, b, (((1,), (1,)), ((), ())), preferred_element_type=F32)


def _dot_tn(a, b):
    return lax.dot_general(a, b, (((0,), (0,)), ((), ())), preferred_element_type=F32)


def _dot3(m, a):
    hi = _bf(a)
    r = a - hi.astype(F32)
    mid = _bf(r)
    lo = _bf(r - mid.astype(F32))
    return _dot(m, hi) + _dot(m, mid) + _dot(m, lo)


def _rms(x, w):
    ms = jnp.mean(x * x, axis=-1, keepdims=True)
    return x * lax.rsqrt(ms + EPS) * w


LOG2E = 1.4426950408889634


def _exp_neg(x):
    return jnp.exp2(x * (-LOG2E))


def _silu(x):
    return x / (1.0 + _exp_neg(x))


def _softplus(x):
    e = _exp_neg(jnp.abs(x))
    u = 1.0 + e
    log1p_e = jnp.where(u == 1.0, e, jnp.log(u) * (e / (u - 1.0)))
    return jnp.maximum(x, 0.0) + log1p_e


def _log_sigmoid(x):
    return jnp.minimum(x, 0.0) - jnp.log(1.0 + _exp_neg(jnp.abs(x)))


def _seg_masks(rows, seq_len):
    t = lax.broadcasted_iota(jnp.int32, (rows, rows), 0)
    u = lax.broadcasted_iota(jnp.int32, (rows, rows), 1)
    if seq_len == rows:
        same = None
        tril = u <= t
    else:
        shift = seq_len.bit_length() - 1
        assert 1 << shift == seq_len
        same = (t >> shift) == (u >> shift)
        tril = jnp.logical_and(same, u <= t)
    return same, tril


def _mask_to_bf16(mask, rows):
    if mask is None:
        return jnp.ones((rows, rows), BF16)
    return jnp.where(mask, 1.0, 0.0).astype(BF16)


def _row_in_seq(rows, seq_len, s, width):
    r = lax.broadcasted_iota(jnp.int32, (rows, width), 0)
    return jnp.logical_and(r >= s * seq_len, r < (s + 1) * seq_len)


def _interleave(chains, offset=0):
    waiting = list(enumerate(chains))
    active = []
    rnd = 0
    while waiting or active:
        while waiting and waiting[0][0] * offset <= rnd:
            active.append(waiting.pop(0)[1])
        alive = []
        for c in active:
            try:
                next(c)
                alive.append(c)
            except StopIteration:
                pass
        active = alive
        rnd += 1


def _run(phases):
    for _ in phases:
        pass


def _pad_rows(x, rows):
    if x.shape[0] >= rows:
        return x
    return jnp.concatenate([x, jnp.zeros((rows - x.shape[0], x.shape[1]), x.dtype)], axis=0)


def _gla_phases(q, k, v, lf, get_state, put_state, seq_len, outs):
    rows = q.shape[0]
    n_seq = rows // seq_len
    same, tril = _seg_masks(rows, seq_len)
    tril_b = _mask_to_bf16(tril, rows)
    same_b = _mask_to_bf16(same, rows)

    b = _dot3(tril_b, lf)
    if n_seq == 1:
        bl = jnp.broadcast_to(b[rows - 1:rows, :], b.shape)
    else:
        bl = _dot3(same_b, lf)
    qt = q * jnp.exp(b)
    kt = k * jnp.exp(-b)
    kd = k * jnp.exp(bl - b)
    bl_t = _pad_rows(bl, 128).T
    yield

    lane = lax.broadcasted_iota(jnp.int32, (rows, GLA_KW), 1)
    lhs = jnp.concatenate(
        [jnp.where((lane >= h * GLA_DK) & (lane < (h + 1) * GLA_DK), qt, 0.0)
         for h in range(GLA_HEADS)], axis=0).astype(BF16)
    att = _dot_nt(lhs, kt.astype(BF16))
    t4 = lax.broadcasted_iota(jnp.int32, (GLA_HEADS * rows, rows), 0) & (rows - 1)
    u4 = lax.broadcasted_iota(jnp.int32, (GLA_HEADS * rows, rows), 1)
    causal4 = u4 <= t4
    if n_seq > 1:
        shift = seq_len.bit_length() - 1
        causal4 = jnp.logical_and(causal4, (t4 >> shift) == (u4 >> shift))
    att = jnp.where(causal4, att, 0.0).astype(BF16)
    vb = v.astype(BF16)
    kdb = kd.astype(BF16)
    yield

    o_inter = None
    r4 = lax.broadcasted_iota(jnp.int32, (GLA_HEADS * rows, GLA_DV), 0) & (rows - 1)
    for s in range(n_seq):
        st = get_state(s)
        oi = _dot(lhs, st.astype(BF16))
        if n_seq == 1:
            o_inter = oi
            kd_s = kdb
        else:
            m4 = jnp.logical_and(r4 >= s * seq_len, r4 < (s + 1) * seq_len)
            oi = jnp.where(m4, oi, 0.0)
            o_inter = oi if o_inter is None else o_inter + oi
            kd_s = jnp.where(_row_in_seq(rows, seq_len, s, GLA_KW), kdb, jnp.zeros_like(kdb))
        blocks = []
        for hp in range(GLA_HEADS // 2):
            upd = _dot_tn(kd_s[:, 2 * hp * GLA_DK:2 * (hp + 1) * GLA_DK],
                          vb[:, 2 * hp * GLA_DV:2 * (hp + 1) * GLA_DV])
            blocks += [upd[:GLA_DK, :GLA_DV], upd[GLA_DK:, GLA_DV:]]
        upd_d = jnp.concatenate(blocks, axis=0)
        c0 = s * seq_len
        decay = jnp.exp(jnp.broadcast_to(bl_t[:, c0:c0 + 1], (GLA_KW, GLA_DV)))
        put_state(s, decay * st + upd_d)
        if n_seq > 1 and s % 4 == 3:
            yield
    yield

    for h in range(GLA_HEADS):
        o_h = _dot(att[h * rows:(h + 1) * rows], vb[:, h * GLA_DV:(h + 1) * GLA_DV])
        outs.append(o_h + o_inter[h * rows:(h + 1) * rows])


def _ssd_phases(xs, bm, cm, dt_raw_blk, dtb_blk, a_blk, d_row, get_state, put_state, seq_len, ys):
    rows = xs.shape[0]
    n_seq = rows // seq_len
    same, tril = _seg_masks(rows, seq_len)
    tril_b = _mask_to_bf16(tril, rows)
    same_b = _mask_to_bf16(same, rows)

    dt = _softplus(dt_raw_blk + dtb_blk)
    dta = dt * a_blk
    lc = _dot3(tril_b, dta)
    if n_seq == 1:
        ll = jnp.broadcast_to(lc[rows - 1:rows, :], lc.shape)
    else:
        ll = _dot3(same_b, dta)
    elc = jnp.exp(lc)
    w = jnp.exp(ll - lc) * dt
    ell = jnp.exp(ll)
    lc_t = _pad_rows(lc, 128).T[:, :rows]
    yield

    bmb = bm.astype(BF16)
    cmb = cm.astype(BF16)
    cb = [_dot_nt(cmb[:, g * SSD_STATE:(g + 1) * SSD_STATE],
                  bmb[:, g * SSD_STATE:(g + 1) * SSD_STATE]) for g in range(SSD_GROUPS)]
    yield

    lane128 = lax.broadcasted_iota(jnp.int32, (rows, 128), 1)
    lo128 = lane128 < SSD_HEAD_DIM
    sub128 = lax.broadcasted_iota(jnp.int32, (128, 128), 0) < SSD_HEAD_DIM

    def col(x, h, width):
        return jnp.broadcast_to(x[:, DT_LANE + h:DT_LANE + h + 1], (x.shape[0], width))

    heads_per_group = SSD_HEADS // SSD_GROUPS
    for j in range(SSD_HEADS // 2):
        g = (2 * j) // heads_per_group
        xs_p = xs[:, 128 * j:128 * (j + 1)]
        dt_p = jnp.where(lo128, col(dt, 2 * j, 128), col(dt, 2 * j + 1, 128))
        xdt = xs_p * dt_p
        x_lo = jnp.where(lo128, xdt, 0.0).astype(BF16)
        x_hi = jnp.where(lo128, 0.0, xdt).astype(BF16)
        y_p = None
        for h, x_half in ((2 * j, x_lo), (2 * j + 1, x_hi)):
            seg = col(lc, h, rows) - jnp.broadcast_to(lc_t[DT_LANE + h:DT_LANE + h + 1, :], (rows, rows))
            w_h = jnp.where(tril, cb[g] * jnp.exp(seg), 0.0).astype(BF16)
            y_h = _dot(w_h, x_half)
            y_p = y_h if y_p is None else y_p + y_h
            yield

        elc_p = jnp.where(lo128, col(elc, 2 * j, 128), col(elc, 2 * j + 1, 128))
        w_p = jnp.where(lo128, col(w, 2 * j, 128), col(w, 2 * j + 1, 128))
        xw = (xs_p * w_p).astype(BF16)
        bm_g = bmb[:, g * SSD_STATE:(g + 1) * SSD_STATE]
        cm_g = cmb[:, g * SSD_STATE:(g + 1) * SSD_STATE]
        y_inter = None
        for s in range(n_seq):
            st = get_state(s, j)
            ci = _dot_nt(cm_g, st.astype(BF16))
            if n_seq == 1:
                y_inter = ci
                xw_s = xw
            else:
                msk = _row_in_seq(rows, seq_len, s, 128)
                ci = jnp.where(msk, ci, 0.0)
                y_inter = ci if y_inter is None else y_inter + ci
                xw_s = jnp.where(msk, xw, jnp.zeros_like(xw))
            upd = _dot_tn(xw_s, bm_g)
            r0 = s * seq_len
            e0 = jnp.broadcast_to(ell[r0:r0 + 1, DT_LANE + 2 * j:DT_LANE + 2 * j + 1], (128, 128))
            e1 = jnp.broadcast_to(ell[r0:r0 + 1, DT_LANE + 2 * j + 1:DT_LANE + 2 * j + 2], (128, 128))
            put_state(s, j, jnp.where(sub128, e0, e1) * st + upd)
        ys.append(y_p + y_inter * elc_p + d_row[:, 128 * j:128 * (j + 1)] * xs_p)
        yield


def _ssd_phases_seq(xs, bm, cm, dt_raw_blk, dtb_blk, a_blk, d_row, get_state, put_state, ys):
    rows = xs.shape[0]
    _, tril = _seg_masks(rows, rows)
    tril_b = _mask_to_bf16(tril, rows)
    src = lax.broadcasted_iota(jnp.int32, (rows, rows), 0)
    dst = lax.broadcasted_iota(jnp.int32, (rows, rows), 1)
    causal_t = src <= dst

    dt = _softplus(dt_raw_blk + dtb_blk)
    lc = _dot3(tril_b, dt * a_blk)
    lc_t = lc.T
    dt_t = dt.T
    elc_t = jnp.exp(lc_t)
    w_t = jnp.exp(lc_t[:, rows - 1:rows] - lc_t) * dt_t
    ell = jnp.exp(lc[rows - 1:rows, :])
    lc2 = lc * LOG2E
    lc2_t = lc_t * LOG2E
    yield

    xs_t = xs.T
    bmb = bm.astype(BF16)
    cmb = cm.astype(BF16)
    cb_t = [_dot_nt(bmb[:, g * SSD_STATE:(g + 1) * SSD_STATE],
                    cmb[:, g * SSD_STATE:(g + 1) * SSD_STATE]) for g in range(SSD_GROUPS)]
    yield

    heads_per_group = SSD_HEADS // SSD_GROUPS
    y_t = []
    for h in range(SSD_HEADS):
        g = h // heads_per_group
        lane = DT_LANE + h
        xs_h = xs_t[h * SSD_HEAD_DIM:(h + 1) * SSD_HEAD_DIM, :]
        seg_t = lc2_t[lane:lane + 1, :] - jnp.broadcast_to(lc2[:, lane:lane + 1], (rows, rows))
        w_h = jnp.where(causal_t, cb_t[g] * jnp.exp2(seg_t), 0.0).astype(BF16)
        y_h = _dot((xs_h * dt_t[lane:lane + 1, :]).astype(BF16), w_h)
        st = get_state(h)
        ci = _dot_nt(st.astype(BF16), cmb[:, g * SSD_STATE:(g + 1) * SSD_STATE])
        y_t.append(y_h + ci * elc_t[lane:lane + 1, :])
        upd = _dot((xs_h * w_t[lane:lane + 1, :]).astype(BF16), bmb[:, g * SSD_STATE:(g + 1) * SSD_STATE])
        put_state(h, jnp.broadcast_to(ell[:, lane:lane + 1], st.shape) * st + upd)
        yield
    ys.append(jnp.concatenate(y_t, axis=0).T + d_row * xs)


def _gate_logf(small_blk, gw2_ref, gb_ref):
    logits = _dot(_bf(small_blk), gw2_ref[...]) + gb_ref[...]
    return _log_sigmoid(logits) * (1.0 / GLA_GATE_TAU)


def _gla_finish(o_h, g_h, gnw):
    return _rms(o_h, gnw) * _silu(g_h)


def _ssd_finish(y, z, snw):
    y = y * _silu(z)
    gs = SSD_INNER // SSD_GROUPS
    parts = [_rms(y[:, g * gs:(g + 1) * gs], snw[:, g * gs:(g + 1) * gs]) for g in range(SSD_GROUPS)]
    return jnp.concatenate(parts, axis=1)


def _memkv_kernel(mem_ref, nw_ref, wk_ref, wv_ref, k_ref, v_ref, kb_ref, vb_ref):
    mn = _bf(_rms(mem_ref[0], nw_ref[...]))
    mk = _dot(mn, wk_ref[...])
    mv = _dot(mn, wv_ref[...])
    for h in range(XA_HEADS):
        k_ref[0, 0, :, h, :] = mk[:, h * XA_HEAD_DIM:(h + 1) * XA_HEAD_DIM]
        v_ref[0, 0, :, h, :] = mv[:, h * XA_HEAD_DIM:(h + 1) * XA_HEAD_DIM]
    kb_ref[0] = _bf(mk)
    vb_ref[0] = _bf(mv)


def _const_spec(shape):
    nd = len(shape)
    return pl.BlockSpec(shape, lambda *_: (0,) * nd, pipeline_mode=pl.Buffered(1))


def _memkv(mem, nw, wk, wv):
    b, m, d = mem.shape
    blk = pl.BlockSpec((1, m, d), lambda i: (i, 0, 0))
    cache_blk = pl.BlockSpec((1, 1, m, XA_HEADS, XA_HEAD_DIM), lambda i: (0, i, 0, 0, 0))
    cache_shape = jax.ShapeDtypeStruct((1, b, m, XA_HEADS, XA_HEAD_DIM), F32)
    return pl.pallas_call(
        _memkv_kernel,
        grid=(b,),
        in_specs=[blk, _const_spec((1, d)), _const_spec((d, d)), _const_spec((d, d))],
        out_specs=[cache_blk, cache_blk, blk, blk],
        out_shape=[cache_shape] * 2 + [jax.ShapeDtypeStruct((b, m, d), BF16)] * 2,
        compiler_params=pltpu.CompilerParams(dimension_semantics=("arbitrary",),
                                             vmem_limit_bytes=VMEM_LIMIT),
        name="mem_kv",
    )(mem, nw, wk, wv)


PROJ_CHUNK = 512


def _mixer_prompt_chain(x_ref, h_ref, proj_s, xpad_s, mixed_s, sg_s, ss_s,
                        lnpre_ref, win_ref, gw2_ref, gb_ref, gnw_ref, cw_ref, cb_ref,
                        dtb_ref, a_ref, d_ref, snw_ref, wout_ref, lnpost_ref):
    rows = TILE
    x = x_ref[...]
    hn = _rms(x, lnpre_ref[...]).astype(BF16)
    yield

    def proj_cols(c0, width):
        proj_s[:, c0:c0 + width] = _dot(hn, win_ref[:, c0:c0 + width])

    def late_proj():
        for c0 in (COL_Q, COL_V, COL_G, COL_Z):
            proj_cols(c0, PROJ_CHUNK)
            yield

    side = [late_proj()]

    def tick():
        if side[0] is not None:
            try:
                next(side[0])
            except StopIteration:
                side[0] = None

    for c in range(SSD_CONV_CH // PROJ_CHUNK):
        proj_cols(COL_XBC + c * PROJ_CHUNK, PROJ_CHUNK)
        yield
    proj_cols(COL_SMALL, 128)

    xa = []
    for c in range(SSD_CONV_CH // PROJ_CHUNK):
        cs = slice(c * PROJ_CHUNK, (c + 1) * PROJ_CHUNK)
        xpad_s[8:8 + rows, cs] = proj_s[:, COL_XBC + c * PROJ_CHUNK:COL_XBC + (c + 1) * PROJ_CHUNK]
        conv = cb_ref[:, cs]
        for j in range(SSD_CONV):
            conv = conv + xpad_s[5 + j:5 + j + rows, cs] * cw_ref[j:j + 1, cs]
        xpad_s[5:8, cs] = xpad_s[5 + rows:8 + rows, cs]
        xa.append(_silu(conv))
        tick()
        yield

    small = proj_s[:, COL_SMALL:COL_SMALL + 128]
    logf = _gate_logf(small, gw2_ref, gb_ref)

    def get_ss(h):
        return ss_s[SSD_HEAD_DIM * h:SSD_HEAD_DIM * (h + 1), :]

    def put_ss(h, val):
        ss_s[SSD_HEAD_DIM * h:SSD_HEAD_DIM * (h + 1), :] = val

    ys = []
    ssd = _ssd_phases_seq(xa[0], xa[1][:, :SSD_GROUPS * SSD_STATE], xa[1][:, SSD_GROUPS * SSD_STATE:],
                          small, dtb_ref[...], a_ref[...], d_ref[...], get_ss, put_ss, ys)

    def get_sg(s):
        return sg_s[...]

    def put_sg(s, val):
        sg_s[...] = val

    def gla_chunks():
        for c in range(rows // GLA_CHUNK):
            r0 = c * GLA_CHUNK
            sl = slice(r0, r0 + GLA_CHUNK)
            outs = []
            yield from _gla_phases(proj_s[sl, COL_Q:COL_Q + GLA_KW], proj_s[sl, COL_K:COL_K + GLA_KW],
                                   proj_s[sl, COL_V:COL_V + GLA_VW], logf[sl], get_sg, put_sg, GLA_CHUNK, outs)
            for h in range(GLA_HEADS):
                g_h = proj_s[sl, COL_G + h * GLA_DV:COL_G + (h + 1) * GLA_DV]
                mixed_s[sl, h * GLA_DV:(h + 1) * GLA_DV] = _gla_finish(outs[h], g_h, gnw_ref[...]).astype(BF16)
            yield

    while side[0] is not None:
        next(ssd)
        tick()
        yield
    alive = [ssd, gla_chunks()]
    while alive:
        for phases in list(alive):
            try:
                next(phases)
            except StopIteration:
                alive.remove(phases)
                continue
            yield
    mixed_s[:, GLA_VW:] = _ssd_finish(jnp.concatenate(ys, axis=1), proj_s[:, COL_Z:COL_Z + SSD_INNER],
                                      snw_ref[...]).astype(BF16)
    yield

    mixed = mixed_s[...]
    ms = []
    for c in range(D_MODEL // PROJ_CHUNK):
        ms.append(_dot(mixed, wout_ref[:, c * PROJ_CHUNK:(c + 1) * PROJ_CHUNK]))
        yield
    h_ref[...] = x + _rms(jnp.concatenate(ms, axis=1), lnpost_ref[...])


N_MIXER_WEIGHTS = 13
N_MIXER_SCRATCH = 5
MIX_OFFSET = 3


def _mixer_prompt_kernel(x_ref, *rest):
    consts = rest[:N_MIXER_WEIGHTS]
    h_ref, sg_out, ss_out, conv_out = rest[N_MIXER_WEIGHTS:N_MIXER_WEIGHTS + 4]
    scratch = rest[N_MIXER_WEIGHTS + 4:]
    per_seq = [scratch[i * N_MIXER_SCRATCH:(i + 1) * N_MIXER_SCRATCH] for i in range(MIX_SEQS)]
    t = pl.program_id(1)

    @pl.when(t == 0)
    def _():
        for _, xpad_s, _, sg_s, ss_s in per_seq:
            sg_s[...] = jnp.zeros_like(sg_s)
            ss_s[...] = jnp.zeros_like(ss_s)
            xpad_s[0:8, :] = jnp.zeros((8, SSD_CONV_CH), F32)

    _interleave([_mixer_prompt_chain(x_ref.at[i], h_ref.at[i], *per_seq[i], *consts) for i in range(MIX_SEQS)],
                offset=MIX_OFFSET)

    @pl.when(t == pl.num_programs(1) - 1)
    def _():
        for i, (_, xpad_s, _, sg_s, ss_s) in enumerate(per_seq):
            sg_out[i] = sg_s[...].reshape(GLA_HEADS, GLA_DK, GLA_DV)
            ss_out[i] = ss_s[...].reshape(SSD_HEADS, SSD_HEAD_DIM, SSD_STATE)
            conv_out[i] = xpad_s[5:8, :]


def _mixer_prompt(x, p):
    b, t, d = x.shape
    nt = t // TILE
    consts = [p["ln_mix_pre"], p["w_in"], p["gw2"], p["gb"], p["gnw"], p["conv_w"], p["conv_b"],
              p["dtb"], p["a_blk"], p["d_row"], p["snw"], p["w_out"], p["ln_mix_post"]]
    assert len(consts) == N_MIXER_WEIGHTS
    tok = pl.BlockSpec((MIX_SEQS, TILE, d), lambda i, j: (i, j, 0))
    return pl.pallas_call(
        _mixer_prompt_kernel,
        grid=(b // MIX_SEQS, nt),
        in_specs=[tok] + [_const_spec(c.shape) for c in consts],
        out_specs=[tok,
                   pl.BlockSpec((MIX_SEQS, GLA_HEADS, GLA_DK, GLA_DV), lambda i, j: (i, 0, 0, 0)),
                   pl.BlockSpec((MIX_SEQS, SSD_HEADS, SSD_HEAD_DIM, SSD_STATE), lambda i, j: (i, 0, 0, 0)),
                   pl.BlockSpec((MIX_SEQS, SSD_CONV - 1, SSD_CONV_CH), lambda i, j: (i, 0, 0))],
        out_shape=[jax.ShapeDtypeStruct((b, t, d), F32),
                   jax.ShapeDtypeStruct((b, GLA_HEADS, GLA_DK, GLA_DV), F32),
                   jax.ShapeDtypeStruct((b, SSD_HEADS, SSD_HEAD_DIM, SSD_STATE), F32),
                   jax.ShapeDtypeStruct((b, SSD_CONV - 1, SSD_CONV_CH), F32)],
        scratch_shapes=[pltpu.VMEM((TILE, PROJ_COLS), F32),
                        pltpu.VMEM((TILE + 8, SSD_CONV_CH), F32),
                        pltpu.VMEM((TILE, D_MODEL), BF16),
                        pltpu.VMEM((GLA_KW, GLA_DV), F32),
                        pltpu.VMEM((SSD_INNER, SSD_STATE), F32)] * MIX_SEQS,
        compiler_params=pltpu.CompilerParams(dimension_semantics=("arbitrary", "arbitrary"),
                                             vmem_limit_bytes=VMEM_LIMIT),
        name="mixer_prompt",
    )(x, *consts)


def _softmax_rows(s):
    e = jnp.exp(s - jnp.max(s, axis=-1, keepdims=True))
    return e / jnp.sum(e, axis=-1, keepdims=True)


FFN_CHUNK = 256
FFN_OFFSET = 1


def _attn_ffn_chain(h_ref, y_ref, mk_ref, mv_ref, lnxa_ref, wq_ref, wo_ref, lnxap_ref,
                    lnf_ref, lnfp_ref, wg_ref, wu_ref, wd_ref):
    h = h_ref[...]
    hn = _rms(h, lnxa_ref[...]).astype(BF16)
    yield
    heads_per_chunk = PROJ_CHUNK // XA_HEAD_DIM
    outs = []
    for c in range(D_MODEL // PROJ_CHUNK):
        q = _dot(hn, wq_ref[:, c * PROJ_CHUNK:(c + 1) * PROJ_CHUNK]).astype(BF16)
        yield
        for hh in range(heads_per_chunk):
            hd = c * heads_per_chunk + hh
            sl = slice(hd * XA_HEAD_DIM, (hd + 1) * XA_HEAD_DIM)
            s = _dot_nt(q[:, hh * XA_HEAD_DIM:(hh + 1) * XA_HEAD_DIM], mk_ref[0, :, sl])
            yield
            pr = _softmax_rows(s).astype(BF16)
            yield
            outs.append(_dot(pr, mv_ref[0, :, sl]).astype(BF16))
    a = jnp.concatenate(outs, axis=1)
    yield
    parts = []
    for c in range(D_MODEL // PROJ_CHUNK):
        parts.append(_dot(a, wo_ref[:, c * PROJ_CHUNK:(c + 1) * PROJ_CHUNK]))
        yield
    h = h + _rms(jnp.concatenate(parts, axis=1), lnxap_ref[...])
    hf = _rms(h, lnf_ref[...]).astype(BF16)
    yield
    acts = []
    for c in range(D_FF // FFN_CHUNK):
        cs = slice(c * FFN_CHUNK, (c + 1) * FFN_CHUNK)
        gt = _dot(hf, wg_ref[:, cs])
        up = _dot(hf, wu_ref[:, cs])
        yield
        acts.append((_silu(gt) * up).astype(BF16))
    act = jnp.concatenate(acts, axis=1)
    yield
    parts = []
    for c in range(D_MODEL // PROJ_CHUNK):
        parts.append(_dot(act, wd_ref[:, c * PROJ_CHUNK:(c + 1) * PROJ_CHUNK]))
        yield
    y_ref[...] = h + _rms(jnp.concatenate(parts, axis=1), lnfp_ref[...])


def _attn_ffn_prompt_kernel(h_ref, mk_ref, mv_ref, *rest):
    consts, y_ref = rest[:-1], rest[-1]
    chains = []
    for sub in range(FFN_SUBTILES):
        rs = pl.ds(sub * TILE, TILE)
        chains.append(_attn_ffn_chain(h_ref.at[0, rs], y_ref.at[0, rs], mk_ref, mv_ref, *consts))
    _interleave(chains, offset=FFN_OFFSET)


def _attn_ffn_prompt(h, mkb, mvb, p):
    b, t, d = h.shape
    rows = FFN_SUBTILES * TILE
    consts = [p["ln_xa_pre"], p["w_xq"], p["w_xo"], p["ln_xa_post"], p["ln_ffn_pre"], p["ln_ffn_post"],
              p["w_gate"], p["w_up"], p["w_down"]]
    tok = pl.BlockSpec((1, rows, d), lambda i, j: (i, j, 0))
    mem = pl.BlockSpec((1, mkb.shape[1], d), lambda i, j: (i, 0, 0))
    return pl.pallas_call(
        _attn_ffn_prompt_kernel,
        grid=(b, t // rows),
        in_specs=[tok, mem, mem] + [_const_spec(c.shape) for c in consts],
        out_specs=tok,
        out_shape=jax.ShapeDtypeStruct((b, t, d), F32),
        compiler_params=pltpu.CompilerParams(dimension_semantics=("arbitrary", "arbitrary"),
                                             vmem_limit_bytes=VMEM_LIMIT),
        name="attn_ffn_prompt",
    )(h, mkb, mvb, *consts)


SAMPLE_PROJ_CHUNK = 640


def _proj_sample_kernel(x_ref, lnpre_ref, win_ref, proj_ref, hn_s):
    @pl.when(pl.program_id(0) == 0)
    def _():
        hn_s[...] = _bf(_rms(x_ref[...], lnpre_ref[...]))

    proj_ref[...] = _dot(hn_s[...], win_ref[...])


def _proj_sample(x, p):
    n, d = x.shape
    return pl.pallas_call(
        _proj_sample_kernel,
        grid=(PROJ_COLS // SAMPLE_PROJ_CHUNK,),
        in_specs=[_const_spec((n, d)), _const_spec((1, d)),
                  pl.BlockSpec((d, SAMPLE_PROJ_CHUNK), lambda j: (0, j))],
        out_specs=pl.BlockSpec((n, SAMPLE_PROJ_CHUNK), lambda j: (0, j)),
        out_shape=jax.ShapeDtypeStruct((n, PROJ_COLS), F32),
        scratch_shapes=[pltpu.VMEM((n, d), BF16)],
        compiler_params=pltpu.CompilerParams(dimension_semantics=("arbitrary",),
                                             vmem_limit_bytes=VMEM_LIMIT),
        name="proj_sample",
    )(x, p["ln_mix_pre"], p["w_in"])


def _mixer_sample_kernel(seq_len, proj_ref, tail_ref, sg_ref, ss_ref, gw2_ref, gb_ref, gnw_ref,
                         cw_ref, cb_ref, dtb_ref, a_ref, d_ref, snw_ref,
                         mixed_ref, sg_out, ss_out, xpad_s, tpad_s):
    rows = proj_ref.shape[0]
    n_seq = rows // seq_len
    small = proj_ref[:, COL_SMALL:COL_SMALL + 128]
    logf = _gate_logf(small, gw2_ref, gb_ref)

    xbc = proj_ref[:, COL_XBC:COL_XBC + SSD_CONV_CH]
    xpad_s[0:8, :] = jnp.zeros((8, SSD_CONV_CH), F32)
    xpad_s[8:8 + rows, :] = xbc
    tpad_s[0:rows, :] = tail_ref[...]
    tpad_s[rows:rows + 8, :] = jnp.zeros((8, SSD_CONV_CH), F32)
    pos = lax.broadcasted_iota(jnp.int32, (rows, SSD_CONV_CH), 0) & (seq_len - 1)
    conv = cb_ref[...]
    for j in range(SSD_CONV - 1):
        back = SSD_CONV - 1 - j
        prev = jnp.where(pos >= back, xpad_s[8 - back:8 - back + rows, :],
                         tpad_s[seq_len - back:seq_len - back + rows, :])
        conv = conv + prev * cw_ref[j:j + 1, :]
    conv = conv + xbc * cw_ref[SSD_CONV - 1:SSD_CONV, :]
    xa = _silu(conv)

    def get_ss(s, j):
        return ss_ref[s, 2 * j:2 * j + 2].reshape(2 * SSD_HEAD_DIM, SSD_STATE)

    def put_ss(s, j, val):
        ss_out[s, 2 * j:2 * j + 2] = val.reshape(2, SSD_HEAD_DIM, SSD_STATE)

    ys = []
    _run(_ssd_phases(xa[:, :SSD_INNER], xa[:, SSD_INNER:SSD_INNER + 256], xa[:, SSD_INNER + 256:],
                     small, dtb_ref[...], a_ref[...], d_ref[...], get_ss, put_ss, seq_len, ys))
    mixed_ref[:, GLA_VW:] = _ssd_finish(jnp.concatenate(ys, axis=1), proj_ref[:, COL_Z:COL_Z + SSD_INNER],
                                        snw_ref[...])

    def get_sg(s):
        return sg_ref[s].reshape(GLA_KW, GLA_DV)

    def put_sg(s, val):
        sg_out[s] = val.reshape(GLA_HEADS, GLA_DK, GLA_DV)

    outs = []
    _run(_gla_phases(proj_ref[:, COL_Q:COL_Q + GLA_KW], proj_ref[:, COL_K:COL_K + GLA_KW],
                     proj_ref[:, COL_V:COL_V + GLA_VW], logf, get_sg, put_sg, seq_len, outs))
    for h in range(GLA_HEADS):
        g_h = proj_ref[:, COL_G + h * GLA_DV:COL_G + (h + 1) * GLA_DV]
        mixed_ref[:, h * GLA_DV:(h + 1) * GLA_DV] = _gla_finish(outs[h], g_h, gnw_ref[...])


def _mixer_sample(proj, tail, sg, ss, p, seq_len):
    n = proj.shape[0]
    rows = SAMPLE_SEQS * seq_len
    nb = sg.shape[0]
    consts = [p["gw2"], p["gb"], p["gnw"], p["conv_w"], p["conv_b"], p["dtb"], p["a_blk"], p["d_row"],
              p["snw"]]
    sg_spec = pl.BlockSpec((SAMPLE_SEQS, GLA_HEADS, GLA_DK, GLA_DV), lambda i: (i, 0, 0, 0))
    ss_spec = pl.BlockSpec((SAMPLE_SEQS, SSD_HEADS, SSD_HEAD_DIM, SSD_STATE), lambda i: (i, 0, 0, 0))
    return pl.pallas_call(
        functools.partial(_mixer_sample_kernel, seq_len),
        grid=(nb // SAMPLE_SEQS,),
        in_specs=[pl.BlockSpec((rows, PROJ_COLS), lambda i: (i, 0)),
                  pl.BlockSpec((rows, SSD_CONV_CH), lambda i: (i, 0)),
                  sg_spec, ss_spec] + [_const_spec(c.shape) for c in consts],
        out_specs=[pl.BlockSpec((rows, D_MODEL), lambda i: (i, 0)), sg_spec, ss_spec],
        out_shape=[jax.ShapeDtypeStruct((n, D_MODEL), F32),
                   jax.ShapeDtypeStruct(sg.shape, F32), jax.ShapeDtypeStruct(ss.shape, F32)],
        scratch_shapes=[pltpu.VMEM((rows + 8, SSD_CONV_CH), F32),
                        pltpu.VMEM((rows + 8, SSD_CONV_CH), F32)],
        compiler_params=pltpu.CompilerParams(dimension_semantics=("arbitrary",),
                                             vmem_limit_bytes=VMEM_LIMIT),
        name="mixer_sample",
    )(proj, tail, sg, ss, *consts)


def _post_mix_sample_kernel(x_ref, mixed_ref, wout_ref, lnpost_ref, lnxa_ref, wq_ref, h_ref, q_ref):
    m = _dot(_bf(mixed_ref[...]), wout_ref[...])
    h = x_ref[...] + _rms(m, lnpost_ref[...])
    h_ref[...] = h
    q_ref[...] = _bf(_dot(_bf(_rms(h, lnxa_ref[...])), wq_ref[...]))


def _post_mix_sample(x, mixed, p):
    n, d = x.shape
    full = pl.BlockSpec((n, d), lambda i: (0, 0))
    consts = [p["w_out"], p["ln_mix_post"], p["ln_xa_pre"], p["w_xq"]]
    return pl.pallas_call(
        _post_mix_sample_kernel,
        grid=(1,),
        in_specs=[full, full] + [_const_spec(c.shape) for c in consts],
        out_specs=[full, full],
        out_shape=[jax.ShapeDtypeStruct((n, d), F32), jax.ShapeDtypeStruct((n, d), BF16)],
        compiler_params=pltpu.CompilerParams(dimension_semantics=("arbitrary",),
                                             vmem_limit_bytes=VMEM_LIMIT),
        name="post_mix_sample",
    )(x, mixed, *consts)


def _attn_sample_kernel(seq_len, q_ref, k_hbm, v_hbm, a_ref, kbuf, vbuf, sem):
    step = pl.program_id(0)
    rows = q_ref.shape[0]
    n_seq = rows // seq_len

    def copies(at_step, slot):
        out = []
        for s in range(n_seq):
            for hd in range(XA_HEADS):
                b = at_step * n_seq + s
                out.append(pltpu.make_async_copy(k_hbm.at[0, b, :, hd, :], kbuf.at[slot, s, hd], sem.at[0, slot]))
                out.append(pltpu.make_async_copy(v_hbm.at[0, b, :, hd, :], vbuf.at[slot, s, hd], sem.at[1, slot]))
        return out

    @pl.when(step == 0)
    def _():
        for cp in copies(0, 0):
            cp.start()

    @pl.when(step + 1 < pl.num_programs(0))
    def _():
        for cp in copies(step + 1, (step + 1) % 2):
            cp.start()

    slot = step % 2
    for cp in copies(step, slot):
        cp.wait()

    q = q_ref[...]
    results = {}

    def one(s, hd):
        sc = _dot_nt(q[:, hd * XA_HEAD_DIM:(hd + 1) * XA_HEAD_DIM], _bf(kbuf[slot, s, hd]))
        yield
        e = jnp.exp(sc - jnp.max(sc, axis=-1, keepdims=True))
        pr = _bf(e / jnp.sum(e, axis=-1, keepdims=True))
        yield
        results[s, hd] = _dot(pr, _bf(vbuf[slot, s, hd]))

    _interleave([one(s, hd) for s in range(n_seq) for hd in range(XA_HEADS)])
    for hd in range(XA_HEADS):
        sl = slice(hd * XA_HEAD_DIM, (hd + 1) * XA_HEAD_DIM)
        out = results[0, hd]
        for s in range(1, n_seq):
            out = jnp.where(_row_in_seq(rows, seq_len, s, XA_HEAD_DIM), results[s, hd], out)
        a_ref[:, sl] = out


def _attn_sample(q, ck, cv, seq_len):
    n, d = q.shape
    _, nb, m, nh, hd = ck.shape
    rows = ATTN_SEQS * seq_len
    tok = pl.BlockSpec((rows, d), lambda i: (i, 0))
    hbm = pl.BlockSpec(memory_space=pl.ANY)
    return pl.pallas_call(
        functools.partial(_attn_sample_kernel, seq_len),
        grid=(nb // ATTN_SEQS,),
        in_specs=[tok, hbm, hbm],
        out_specs=tok,
        out_shape=jax.ShapeDtypeStruct((n, d), F32),
        scratch_shapes=[pltpu.VMEM((2, ATTN_SEQS, nh, m, hd), F32),
                        pltpu.VMEM((2, ATTN_SEQS, nh, m, hd), F32),
                        pltpu.SemaphoreType.DMA((2, 2))],
        compiler_params=pltpu.CompilerParams(dimension_semantics=("arbitrary",),
                                             vmem_limit_bytes=VMEM_LIMIT),
        name="attn_sample",
    )(q, ck, cv)


def _post_attn_sample_kernel(h_ref, a_ref, wo_ref, lnxap_ref, lnf_ref, lnfp_ref, wg_ref, wu_ref, wd_ref,
                             y_ref, h_s, hf_s, acc_s):
    c = pl.program_id(0)

    @pl.when(c == 0)
    def _():
        a = _dot(_bf(a_ref[...]), wo_ref[...])
        h = h_ref[...] + _rms(a, lnxap_ref[...])
        h_s[...] = h
        hf_s[...] = _bf(_rms(h, lnf_ref[...]))
        acc_s[...] = jnp.zeros_like(acc_s)

    hf = hf_s[...]
    act = _bf(_silu(_dot(hf, wg_ref[...])) * _dot(hf, wu_ref[...]))
    acc_s[...] += _dot(act, wd_ref[...])

    @pl.when(c == pl.num_programs(0) - 1)
    def _():
        y_ref[...] = h_s[...] + _rms(acc_s[...], lnfp_ref[...])


def _post_attn_sample(h, a, p):
    n, d = h.shape
    full = pl.BlockSpec((n, d), lambda c: (0, 0))
    consts = [p["w_xo"], p["ln_xa_post"], p["ln_ffn_pre"], p["ln_ffn_post"]]
    col_chunk = pl.BlockSpec((d, FFN_CHUNK), lambda c: (0, c))
    row_chunk = pl.BlockSpec((FFN_CHUNK, d), lambda c: (c, 0))
    return pl.pallas_call(
        _post_attn_sample_kernel,
        grid=(D_FF // FFN_CHUNK,),
        in_specs=[full, full] + [_const_spec(c.shape) for c in consts] + [col_chunk, col_chunk, row_chunk],
        out_specs=full,
        out_shape=jax.ShapeDtypeStruct((n, d), F32),
        scratch_shapes=[pltpu.VMEM((n, d), F32), pltpu.VMEM((n, d), BF16), pltpu.VMEM((n, d), F32)],
        compiler_params=pltpu.CompilerParams(dimension_semantics=("arbitrary",),
                                             vmem_limit_bytes=VMEM_LIMIT),
        name="post_attn_sample",
    )(h, a, *consts, p["w_gate"], p["w_up"], p["w_down"])


SRC_GLR = GLA_KW + GLA_KW + GLA_VW + GLA_VW
SRC_Z = SRC_GLR + GLA_GATE_RANK
SRC_XBC = SRC_Z + SSD_INNER
SRC_DT = SRC_XBC + SSD_CONV_CH
IN_COLS = SRC_DT + SSD_HEADS


def _pack_w_in_kernel(wt_hbm, o_ref, buf, sem):
    pieces = [(0, COL_Q, SRC_GLR),
              (SRC_Z, COL_Z, SSD_INNER),
              (SRC_XBC, COL_XBC, SSD_CONV_CH),
              (SRC_GLR, COL_SMALL, GLA_GATE_RANK),
              (SRC_DT, COL_SMALL + DT_LANE, SSD_HEADS)]
    copies = [pltpu.make_async_copy(wt_hbm.at[0, pl.ds(src, n), :], buf.at[pl.ds(dst, n), :], sem.at[0])
              for src, dst, n in pieces]
    for cp in copies:
        cp.start()
    used = COL_SMALL + DT_LANE + SSD_HEADS
    buf[used:PROJ_COLS, :] = jnp.zeros((PROJ_COLS - used, D_MODEL), F32)
    for cp in copies:
        cp.wait()
    for j in range(PROJ_COLS // 128):
        blk = buf[128 * j:128 * (j + 1), :]
        if 128 * j < COL_K:
            blk = blk * (GLA_DK ** -0.5)
        o_ref[:, 128 * j:128 * (j + 1)] = blk.T.astype(BF16)


def _pack_w_in(w_in):
    wt = jnp.swapaxes(w_in, 1, 2)
    return pl.pallas_call(
        _pack_w_in_kernel,
        grid=(1,),
        in_specs=[pl.BlockSpec(memory_space=pl.ANY)],
        out_specs=pl.BlockSpec((D_MODEL, PROJ_COLS), lambda i: (0, 0)),
        out_shape=jax.ShapeDtypeStruct((D_MODEL, PROJ_COLS), BF16),
        scratch_shapes=[pltpu.VMEM((PROJ_COLS, D_MODEL), F32), pltpu.SemaphoreType.DMA((1,))],
        compiler_params=pltpu.CompilerParams(dimension_semantics=("arbitrary",),
                                             vmem_limit_bytes=VMEM_LIMIT),
        name="pack_w_in",
    )(wt)


def _pack_params(ln_mix_pre, ln_mix_post, w_in_packed, gla_gate_w2, gla_gate_b, gla_norm_w, ssd_conv_w,
                 ssd_conv_b, ssd_dt_bias, ssd_A_log, ssd_D, ssd_norm_w, w_out, ln_xa_pre, ln_xa_post,
                 mem_norm_w, w_xq, w_xk, w_xv, w_xo, ln_ffn_pre, ln_ffn_post, w_gate, w_up, w_down):

    def small_row(v):
        return jnp.zeros((1, 128), F32).at[0, DT_LANE:DT_LANE + SSD_HEADS].set(v)

    row = lambda v: v.reshape(1, -1)
    return {
        "ln_mix_pre": row(ln_mix_pre), "ln_mix_post": row(ln_mix_post),
        "w_in": w_in_packed,
        "gw2": _bf(jnp.zeros((128, GLA_KW), F32).at[:GLA_GATE_RANK].set(gla_gate_w2)),
        "gb": row(gla_gate_b), "gnw": row(gla_norm_w),
        "conv_w": ssd_conv_w, "conv_b": row(ssd_conv_b),
        "dtb": small_row(ssd_dt_bias),
        "a_blk": small_row(-jnp.exp(ssd_A_log)),
        "d_row": jnp.repeat(ssd_D, SSD_HEAD_DIM).reshape(1, SSD_INNER),
        "snw": row(ssd_norm_w),
        "w_out": _bf(w_out),
        "ln_xa_pre": row(ln_xa_pre), "ln_xa_post": row(ln_xa_post), "mem_norm_w": row(mem_norm_w),
        "w_xq": _bf(w_xq * (XA_HEAD_DIM ** -0.5)), "w_xk": _bf(w_xk), "w_xv": _bf(w_xv), "w_xo": _bf(w_xo),
        "ln_ffn_pre": row(ln_ffn_pre), "ln_ffn_post": row(ln_ffn_post),
        "w_gate": _bf(w_gate), "w_up": _bf(w_up), "w_down": _bf(w_down),
    }


def kernel(x_prompt, x_sample, mem_prompt, state_gla, state_ssm, state_conv, cache_mem_k, cache_mem_v,
           ln_mix_pre, ln_mix_post, w_in, gla_gate_w2, gla_gate_b, gla_norm_w, ssd_conv_w, ssd_conv_b,
           ssd_dt_bias, ssd_A_log, ssd_D, ssd_norm_w, w_out, ln_xa_pre, ln_xa_post, mem_norm_w,
           w_xq, w_xk, w_xv, w_xo, ln_ffn_pre, ln_ffn_post, w_gate, w_up, w_down):
    assert w_in.shape[0] == 1, "single-layer kernel"
    layer = [a[0] for a in (ln_mix_pre, ln_mix_post, w_in, gla_gate_w2, gla_gate_b, gla_norm_w, ssd_conv_w,
                            ssd_conv_b, ssd_dt_bias, ssd_A_log, ssd_D, ssd_norm_w, w_out, ln_xa_pre,
                            ln_xa_post, mem_norm_w, w_xq, w_xk, w_xv, w_xo, ln_ffn_pre, ln_ffn_post,
                            w_gate, w_up, w_down)]
    layer[2] = _pack_w_in(w_in)
    p = _pack_params(*layer)
    bp, mem_len, _ = mem_prompt.shape
    bs, ts, _ = x_sample.shape
    assert ts == SSD_CONV, "the padded conv-tail layout assumes one new row per conv tap"

    mk, mv, mkb, mvb = _memkv(mem_prompt, p["mem_norm_w"], p["w_xk"], p["w_xv"])
    h_p, gla_p, ssm_p, conv_p = _mixer_prompt(x_prompt, p)
    y_p = _attn_ffn_prompt(h_p, mkb, mvb, p)

    xs = x_sample.reshape(bs * ts, D_MODEL)
    proj = _proj_sample(xs, p)
    tail = jnp.pad(state_conv[0], ((0, 0), (1, 0), (0, 0))).reshape(bs * ts, SSD_CONV_CH)
    mixed, gla_s, ssm_s = _mixer_sample(proj, tail, state_gla[0], state_ssm[0], p, ts)
    h_s, q_s = _post_mix_sample(xs, mixed, p)
    a_s = _attn_sample(q_s, cache_mem_k, cache_mem_v, ts)
    y_s = _post_attn_sample(h_s, a_s, p)
    conv_s = proj[:, COL_XBC:COL_XBC + SSD_CONV_CH].reshape(bs, ts, SSD_CONV_CH)[:, ts - (SSD_CONV - 1):]

    return (y_p, y_s.reshape(bs, ts, D_MODEL), gla_p[None], ssm_p[None], conv_p[None],
            mk, mv, gla_s[None], ssm_s[None], conv_s[None])
```

```python
import functools

import jax
import jax.numpy as jnp
from jax import lax
from jax.experimental import pallas as pl
from jax.experimental.pallas import tpu as pltpu

F32 = jnp.float32
BF16 = jnp.bfloat16

D_MODEL = 1024
GLA_HEADS = 4
GLA_DK = 64
GLA_DV = 128
GLA_KW = GLA_HEADS * GLA_DK
GLA_VW = GLA_HEADS * GLA_DV
GLA_GATE_RANK = 16
GLA_GATE_TAU = 16.0
SSD_INNER = 512
SSD_HEAD_DIM = 64
SSD_HEADS = 8
SSD_GROUPS = 2
SSD_STATE = 128
SSD_CONV = 4
SSD_CONV_CH = 1024
XA_HEADS = 4
XA_HEAD_DIM = 256
D_FF = 2816
EPS = 1e-6

COL_Q = 0
COL_K = 256
COL_V = 512
COL_G = 1024
COL_Z = 1536
COL_XBC = 2048
COL_SMALL = 3072
DT_LANE = GLA_GATE_RANK
PROJ_COLS = 3200

GLA_CHUNK = 64
TILE = 256
FFN_SUBTILES = 4
MIX_SEQS = 4
SAMPLE_SEQS = 16
ATTN_SEQS = 4
VMEM_LIMIT = 56 * 1024 * 1024


def _bf(x):
    return x.astype(BF16)


def _dot(a, b):
    return jnp.dot(a, b, preferred_element_type=F32)


def _dot_nt(a, b):
    return lax.dot_general(a, b, (((1,), (1,)), ((), ())), preferred_element_type=F32)


def _dot_tn(a, b):
    return lax.dot_general(a, b, (((0,), (0,)), ((), ())), preferred_element_type=F32)


def _dot3(m, a):
    hi = _bf(a)
    r = a - hi.astype(F32)
    mid = _bf(r)
    lo = _bf(r - mid.astype(F32))
    return _dot(m, hi) + _dot(m, mid) + _dot(m, lo)


def _rms(x, w):
    ms = jnp.mean(x * x, axis=-1, keepdims=True)
    return x * lax.rsqrt(ms + EPS) * w


LOG2E = 1.4426950408889634


def _exp_neg(x):
    return jnp.exp2(x * (-LOG2E))


def _silu(x):
    return x / (1.0 + _exp_neg(x))


def _softplus(x):
    e = _exp_neg(jnp.abs(x))
    u = 1.0 + e
    log1p_e = jnp.where(u == 1.0, e, jnp.log(u) * (e / (u - 1.0)))
    return jnp.maximum(x, 0.0) + log1p_e


def _log_sigmoid(x):
    return jnp.minimum(x, 0.0) - jnp.log(1.0 + _exp_neg(jnp.abs(x)))


def _seg_masks(rows, seq_len):
    t = lax.broadcasted_iota(jnp.int32, (rows, rows), 0)
    u = lax.broadcasted_iota(jnp.int32, (rows, rows), 1)
    if seq_len == rows:
        same = None
        tril = u <= t
    else:
        shift = seq_len.bit_length() - 1
        assert 1 << shift == seq_len
        same = (t >> shift) == (u >> shift)
        tril = jnp.logical_and(same, u <= t)
    return same, tril


def _mask_to_bf16(mask, rows):
    if mask is None:
        return jnp.ones((rows, rows), BF16)
    return jnp.where(mask, 1.0, 0.0).astype(BF16)


def _row_in_seq(rows, seq_len, s, width):
    r = lax.broadcasted_iota(jnp.int32, (rows, width), 0)
    return jnp.logical_and(r >= s * seq_len, r < (s + 1) * seq_len)


def _interleave(chains, offset=0):
    waiting = list(enumerate(chains))
    active = []
    rnd = 0
    while waiting or active:
        while waiting and waiting[0][0] * offset <= rnd:
            active.append(waiting.pop(0)[1])
        alive = []
        for c in active:
            try:
                next(c)
                alive.append(c)
            except StopIteration:
                pass
        active = alive
        rnd += 1


def _run(phases):
    for _ in phases:
        pass


def _pad_rows(x, rows):
    if x.shape[0] >= rows:
        return x
    return jnp.concatenate([x, jnp.zeros((rows - x.shape[0], x.shape[1]), x.dtype)], axis=0)


def _gla_phases(q, k, v, lf, get_state, put_state, seq_len, outs):
    rows = q.shape[0]
    n_seq = rows // seq_len
    same, tril = _seg_masks(rows, seq_len)
    tril_b = _mask_to_bf16(tril, rows)
    same_b = _mask_to_bf16(same, rows)

    b = _dot3(tril_b, lf)
    if n_seq == 1:
        bl = jnp.broadcast_to(b[rows - 1:rows, :], b.shape)
    else:
        bl = _dot3(same_b, lf)
    qt = q * jnp.exp(b)
    kt = k * jnp.exp(-b)
    kd = k * jnp.exp(bl - b)
    bl_t = _pad_rows(bl, 128).T
    yield

    lane = lax.broadcasted_iota(jnp.int32, (rows, GLA_KW), 1)
    lhs = jnp.concatenate(
        [jnp.where((lane >= h * GLA_DK) & (lane < (h + 1) * GLA_DK), qt, 0.0)
         for h in range(GLA_HEADS)], axis=0).astype(BF16)
    att = _dot_nt(lhs, kt.astype(BF16))
    t4 = lax.broadcasted_iota(jnp.int32, (GLA_HEADS * rows, rows), 0) & (rows - 1)
    u4 = lax.broadcasted_iota(jnp.int32, (GLA_HEADS * rows, rows), 1)
    causal4 = u4 <= t4
    if n_seq > 1:
        shift = seq_len.bit_length() - 1
        causal4 = jnp.logical_and(causal4, (t4 >> shift) == (u4 >> shift))
    att = jnp.where(causal4, att, 0.0).astype(BF16)
    vb = v.astype(BF16)
    kdb = kd.astype(BF16)
    yield

    o_inter = None
    r4 = lax.broadcasted_iota(jnp.int32, (GLA_HEADS * rows, GLA_DV), 0) & (rows - 1)
    for s in range(n_seq):
        st = get_state(s)
        oi = _dot(lhs, st.astype(BF16))
        if n_seq == 1:
            o_inter = oi
            kd_s = kdb
        else:
            m4 = jnp.logical_and(r4 >= s * seq_len, r4 < (s + 1) * seq_len)
            oi = jnp.where(m4, oi, 0.0)
            o_inter = oi if o_inter is None else o_inter + oi
            kd_s = jnp.where(_row_in_seq(rows, seq_len, s, GLA_KW), kdb, jnp.zeros_like(kdb))
        blocks = []
        for hp in range(GLA_HEADS // 2):
            upd = _dot_tn(kd_s[:, 2 * hp * GLA_DK:2 * (hp + 1) * GLA_DK],
                          vb[:, 2 * hp * GLA_DV:2 * (hp + 1) * GLA_DV])
            blocks += [upd[:GLA_DK, :GLA_DV], upd[GLA_DK:, GLA_DV:]]
        upd_d = jnp.concatenate(blocks, axis=0)
        c0 = s * seq_len
        decay = jnp.exp(jnp.broadcast_to(bl_t[:, c0:c0 + 1], (GLA_KW, GLA_DV)))
        put_state(s, decay * st + upd_d)
        if n_seq > 1 and s % 4 == 3:
            yield
    yield

    for h in range(GLA_HEADS):
        o_h = _dot(att[h * rows:(h + 1) * rows], vb[:, h * GLA_DV:(h + 1) * GLA_DV])
        outs.append(o_h + o_inter[h * rows:(h + 1) * rows])


def _ssd_phases(xs, bm, cm, dt_raw_blk, dtb_blk, a_blk, d_row, get_state, put_state, seq_len, ys):
    rows = xs.shape[0]
    n_seq = rows // seq_len
    same, tril = _seg_masks(rows, seq_len)
    tril_b = _mask_to_bf16(tril, rows)
    same_b = _mask_to_bf16(same, rows)

    dt = _softplus(dt_raw_blk + dtb_blk)
    dta = dt * a_blk
    lc = _dot3(tril_b, dta)
    if n_seq == 1:
        ll = jnp.broadcast_to(lc[rows - 1:rows, :], lc.shape)
    else:
        ll = _dot3(same_b, dta)
    elc = jnp.exp(lc)
    w = jnp.exp(ll - lc) * dt
    ell = jnp.exp(ll)
    lc_t = _pad_rows(lc, 128).T[:, :rows]
    yield

    bmb = bm.astype(BF16)
    cmb = cm.astype(BF16)
    cb = [_dot_nt(cmb[:, g * SSD_STATE:(g + 1) * SSD_STATE],
                  bmb[:, g * SSD_STATE:(g + 1) * SSD_STATE]) for g in range(SSD_GROUPS)]
    yield

    lane128 = lax.broadcasted_iota(jnp.int32, (rows, 128), 1)
    lo128 = lane128 < SSD_HEAD_DIM
    sub128 = lax.broadcasted_iota(jnp.int32, (128, 128), 0) < SSD_HEAD_DIM

    def col(x, h, width):
        return jnp.broadcast_to(x[:, DT_LANE + h:DT_LANE + h + 1], (x.shape[0], width))

    heads_per_group = SSD_HEADS // SSD_GROUPS
    for j in range(SSD_HEADS // 2):
        g = (2 * j) // heads_per_group
        xs_p = xs[:, 128 * j:128 * (j + 1)]
        dt_p = jnp.where(lo128, col(dt, 2 * j, 128), col(dt, 2 * j + 1, 128))
        xdt = xs_p * dt_p
        x_lo = jnp.where(lo128, xdt, 0.0).astype(BF16)
        x_hi = jnp.where(lo128, 0.0, xdt).astype(BF16)
        y_p = None
        for h, x_half in ((2 * j, x_lo), (2 * j + 1, x_hi)):
            seg = col(lc, h, rows) - jnp.broadcast_to(lc_t[DT_LANE + h:DT_LANE + h + 1, :], (rows, rows))
            w_h = jnp.where(tril, cb[g] * jnp.exp(seg), 0.0).astype(BF16)
            y_h = _dot(w_h, x_half)
            y_p = y_h if y_p is None else y_p + y_h
            yield

        elc_p = jnp.where(lo128, col(elc, 2 * j, 128), col(elc, 2 * j + 1, 128))
        w_p = jnp.where(lo128, col(w, 2 * j, 128), col(w, 2 * j + 1, 128))
        xw = (xs_p * w_p).astype(BF16)
        bm_g = bmb[:, g * SSD_STATE:(g + 1) * SSD_STATE]
        cm_g = cmb[:, g * SSD_STATE:(g + 1) * SSD_STATE]
        y_inter = None
        for s in range(n_seq):
            st = get_state(s, j)
            ci = _dot_nt(cm_g, st.astype(BF16))
            if n_seq == 1:
                y_inter = ci
                xw_s = xw
            else:
                msk = _row_in_seq(rows, seq_len, s, 128)
                ci = jnp.where(msk, ci, 0.0)
                y_inter = ci if y_inter is None else y_inter + ci
                xw_s = jnp.where(msk, xw, jnp.zeros_like(xw))
            upd = _dot_tn(xw_s, bm_g)
            r0 = s * seq_len
            e0 = jnp.broadcast_to(ell[r0:r0 + 1, DT_LANE + 2 * j:DT_LANE + 2 * j + 1], (128, 128))
            e1 = jnp.broadcast_to(ell[r0:r0 + 1, DT_LANE + 2 * j + 1:DT_LANE + 2 * j + 2], (128, 128))
            put_state(s, j, jnp.where(sub128, e0, e1) * st + upd)
        ys.append(y_p + y_inter * elc_p + d_row[:, 128 * j:128 * (j + 1)] * xs_p)
        yield


def _ssd_phases_seq(xs, bm, cm, dt_raw_blk, dtb_blk, a_blk, d_row, get_state, put_state, ys):
    rows = xs.shape[0]
    _, tril = _seg_masks(rows, rows)
    tril_b = _mask_to_bf16(tril, rows)
    src = lax.broadcasted_iota(jnp.int32, (rows, rows), 0)
    dst = lax.broadcasted_iota(jnp.int32, (rows, rows), 1)
    causal_t = src <= dst

    dt = _softplus(dt_raw_blk + dtb_blk)
    lc = _dot3(tril_b, dt * a_blk)
    lc_t = lc.T
    dt_t = dt.T
    elc_t = jnp.exp(lc_t)
    w_t = jnp.exp(lc_t[:, rows - 1:rows] - lc_t) * dt_t
    ell = jnp.exp(lc[rows - 1:rows, :])
    lc2 = lc * LOG2E
    lc2_t = lc_t * LOG2E
    yield

    xs_t = xs.T
    bmb = bm.astype(BF16)
    cmb = cm.astype(BF16)
    cb_t = [_dot_nt(bmb[:, g * SSD_STATE:(g + 1) * SSD_STATE],
                    cmb[:, g * SSD_STATE:(g + 1) * SSD_STATE]) for g in range(SSD_GROUPS)]
    yield

    heads_per_group = SSD_HEADS // SSD_GROUPS
    y_t = []
    for h in range(SSD_HEADS):
        g = h // heads_per_group
        lane = DT_LANE + h
        xs_h = xs_t[h * SSD_HEAD_DIM:(h + 1) * SSD_HEAD_DIM, :]
        seg_t = lc2_t[lane:lane + 1, :] - jnp.broadcast_to(lc2[:, lane:lane + 1], (rows, rows))
        w_h = jnp.where(causal_t, cb_t[g] * jnp.exp2(seg_t), 0.0).astype(BF16)
        y_h = _dot((xs_h * dt_t[lane:lane + 1, :]).astype(BF16), w_h)
        st = get_state(h)
        ci = _dot_nt(st.astype(BF16), cmb[:, g * SSD_STATE:(g + 1) * SSD_STATE])
        y_t.append(y_h + ci * elc_t[lane:lane + 1, :])
        upd = _dot((xs_h * w_t[lane:lane + 1, :]).astype(BF16), bmb[:, g * SSD_STATE:(g + 1) * SSD_STATE])
        put_state(h, jnp.broadcast_to(ell[:, lane:lane + 1], st.shape) * st + upd)
        yield
    ys.append(jnp.concatenate(y_t, axis=0).T + d_row * xs)


def _gate_logf(small_blk, gw2_ref, gb_ref):
    logits = _dot(_bf(small_blk), gw2_ref[...]) + gb_ref[...]
    return _log_sigmoid(logits) * (1.0 / GLA_GATE_TAU)


def _gla_finish(o_h, g_h, gnw):
    return _rms(o_h, gnw) * _silu(g_h)


def _ssd_finish(y, z, snw):
    y = y * _silu(z)
    gs = SSD_INNER // SSD_GROUPS
    parts = [_rms(y[:, g * gs:(g + 1) * gs], snw[:, g * gs:(g + 1) * gs]) for g in range(SSD_GROUPS)]
    return jnp.concatenate(parts, axis=1)


def _memkv_kernel(mem_ref, nw_ref, wk_ref, wv_ref, k_ref, v_ref, kb_ref, vb_ref):
    mn = _bf(_rms(mem_ref[0], nw_ref[...]))
    mk = _dot(mn, wk_ref[...])
    mv = _dot(mn, wv_ref[...])
    for h in range(XA_HEADS):
        k_ref[0, 0, :, h, :] = mk[:, h * XA_HEAD_DIM:(h + 1) * XA_HEAD_DIM]
        v_ref[0, 0, :, h, :] = mv[:, h * XA_HEAD_DIM:(h + 1) * XA_HEAD_DIM]
    kb_ref[0] = _bf(mk)
    vb_ref[0] = _bf(mv)


def _const_spec(shape):
    nd = len(shape)
    return pl.BlockSpec(shape, lambda *_: (0,) * nd, pipeline_mode=pl.Buffered(1))


def _memkv(mem, nw, wk, wv):
    b, m, d = mem.shape
    blk = pl.BlockSpec((1, m, d), lambda i: (i, 0, 0))
    cache_blk = pl.BlockSpec((1, 1, m, XA_HEADS, XA_HEAD_DIM), lambda i: (0, i, 0, 0, 0))
    cache_shape = jax.ShapeDtypeStruct((1, b, m, XA_HEADS, XA_HEAD_DIM), F32)
    return pl.pallas_call(
        _memkv_kernel,
        grid=(b,),
        in_specs=[blk, _const_spec((1, d)), _const_spec((d, d)), _const_spec((d, d))],
        out_specs=[cache_blk, cache_blk, blk, blk],
        out_shape=[cache_shape] * 2 + [jax.ShapeDtypeStruct((b, m, d), BF16)] * 2,
        compiler_params=pltpu.CompilerParams(dimension_semantics=("arbitrary",),
                                             vmem_limit_bytes=VMEM_LIMIT),
        name="mem_kv",
    )(mem, nw, wk, wv)


PROJ_CHUNK = 512


def _mixer_prompt_chain(x_ref, h_ref, proj_s, xpad_s, mixed_s, sg_s, ss_s,
                        lnpre_ref, win_ref, gw2_ref, gb_ref, gnw_ref, cw_ref, cb_ref,
                        dtb_ref, a_ref, d_ref, snw_ref, wout_ref, lnpost_ref):
    rows = TILE
    x = x_ref[...]
    hn = _rms(x, lnpre_ref[...]).astype(BF16)
    yield

    def proj_cols(c0, width):
        proj_s[:, c0:c0 + width] = _dot(hn, win_ref[:, c0:c0 + width])

    def late_proj():
        for c0 in (COL_Q, COL_V, COL_G, COL_Z):
            proj_cols(c0, PROJ_CHUNK)
            yield

    side = [late_proj()]

    def tick():
        if side[0] is not None:
            try:
                next(side[0])
            except StopIteration:
                side[0] = None

    for c in range(SSD_CONV_CH // PROJ_CHUNK):
        proj_cols(COL_XBC + c * PROJ_CHUNK, PROJ_CHUNK)
        yield
    proj_cols(COL_SMALL, 128)

    xa = []
    for c in range(SSD_CONV_CH // PROJ_CHUNK):
        cs = slice(c * PROJ_CHUNK, (c + 1) * PROJ_CHUNK)
        xpad_s[8:8 + rows, cs] = proj_s[:, COL_XBC + c * PROJ_CHUNK:COL_XBC + (c + 1) * PROJ_CHUNK]
        conv = cb_ref[:, cs]
        for j in range(SSD_CONV):
            conv = conv + xpad_s[5 + j:5 + j + rows, cs] * cw_ref[j:j + 1, cs]
        xpad_s[5:8, cs] = xpad_s[5 + rows:8 + rows, cs]
        xa.append(_silu(conv))
        tick()
        yield

    small = proj_s[:, COL_SMALL:COL_SMALL + 128]
    logf = _gate_logf(small, gw2_ref, gb_ref)

    def get_ss(h):
        return ss_s[SSD_HEAD_DIM * h:SSD_HEAD_DIM * (h + 1), :]

    def put_ss(h, val):
        ss_s[SSD_HEAD_DIM * h:SSD_HEAD_DIM * (h + 1), :] = val

    ys = []
    ssd = _ssd_phases_seq(xa[0], xa[1][:, :SSD_GROUPS * SSD_STATE], xa[1][:, SSD_GROUPS * SSD_STATE:],
                          small, dtb_ref[...], a_ref[...], d_ref[...], get_ss, put_ss, ys)

    def get_sg(s):
        return sg_s[...]

    def put_sg(s, val):
        sg_s[...] = val

    def gla_chunks():
        for c in range(rows // GLA_CHUNK):
            r0 = c * GLA_CHUNK
            sl = slice(r0, r0 + GLA_CHUNK)
            outs = []
            yield from _gla_phases(proj_s[sl, COL_Q:COL_Q + GLA_KW], proj_s[sl, COL_K:COL_K + GLA_KW],
                                   proj_s[sl, COL_V:COL_V + GLA_VW], logf[sl], get_sg, put_sg, GLA_CHUNK, outs)
            for h in range(GLA_HEADS):
                g_h = proj_s[sl, COL_G + h * GLA_DV:COL_G + (h + 1) * GLA_DV]
                mixed_s[sl, h * GLA_DV:(h + 1) * GLA_DV] = _gla_finish(outs[h], g_h, gnw_ref[...]).astype(BF16)
            yield

    while side[0] is not None:
        next(ssd)
        tick()
        yield
    alive = [ssd, gla_chunks()]
    while alive:
        for phases in list(alive):
            try:
                next(phases)
            except StopIteration:
                alive.remove(phases)
                continue
            yield
    mixed_s[:, GLA_VW:] = _ssd_finish(jnp.concatenate(ys, axis=1), proj_s[:, COL_Z:COL_Z + SSD_INNER],
                                      snw_ref[...]).astype(BF16)
    yield

    mixed = mixed_s[...]
    ms = []
    for c in range(D_MODEL // PROJ_CHUNK):
        ms.append(_dot(mixed, wout_ref[:, c * PROJ_CHUNK:(c + 1) * PROJ_CHUNK]))
        yield
    h_ref[...] = x + _rms(jnp.concatenate(ms, axis=1), lnpost_ref[...])


N_MIXER_WEIGHTS = 13
N_MIXER_SCRATCH = 5
MIX_OFFSET = 4


def _mixer_prompt_kernel(x_ref, *rest):
    consts = rest[:N_MIXER_WEIGHTS]
    h_ref, sg_out, ss_out, conv_out = rest[N_MIXER_WEIGHTS:N_MIXER_WEIGHTS + 4]
    scratch = rest[N_MIXER_WEIGHTS + 4:]
    per_seq = [scratch[i * N_MIXER_SCRATCH:(i + 1) * N_MIXER_SCRATCH] for i in range(MIX_SEQS)]
    t = pl.program_id(1)

    @pl.when(t == 0)
    def _():
        for _, xpad_s, _, sg_s, ss_s in per_seq:
            sg_s[...] = jnp.zeros_like(sg_s)
            ss_s[...] = jnp.zeros_like(ss_s)
            xpad_s[0:8, :] = jnp.zeros((8, SSD_CONV_CH), F32)

    _interleave([_mixer_prompt_chain(x_ref.at[i], h_ref.at[i], *per_seq[i], *consts) for i in range(MIX_SEQS)],
                offset=MIX_OFFSET)

    @pl.when(t == pl.num_programs(1) - 1)
    def _():
        for i, (_, xpad_s, _, sg_s, ss_s) in enumerate(per_seq):
            sg_out[i] = sg_s[...].reshape(GLA_HEADS, GLA_DK, GLA_DV)
            ss_out[i] = ss_s[...].reshape(SSD_HEADS, SSD_HEAD_DIM, SSD_STATE)
            conv_out[i] = xpad_s[5:8, :]


def _mixer_prompt(x, p):
    b, t, d = x.shape
    nt = t // TILE
    consts = [p["ln_mix_pre"], p["w_in"], p["gw2"], p["gb"], p["gnw"], p["conv_w"], p["conv_b"],
              p["dtb"], p["a_blk"], p["d_row"], p["snw"], p["w_out"], p["ln_mix_post"]]
    assert len(consts) == N_MIXER_WEIGHTS
    tok = pl.BlockSpec((MIX_SEQS, TILE, d), lambda i, j: (i, j, 0))
    return pl.pallas_call(
        _mixer_prompt_kernel,
        grid=(b // MIX_SEQS, nt),
        in_specs=[tok] + [_const_spec(c.shape) for c in consts],
        out_specs=[tok,
                   pl.BlockSpec((MIX_SEQS, GLA_HEADS, GLA_DK, GLA_DV), lambda i, j: (i, 0, 0, 0)),
                   pl.BlockSpec((MIX_SEQS, SSD_HEADS, SSD_HEAD_DIM, SSD_STATE), lambda i, j: (i, 0, 0, 0)),
                   pl.BlockSpec((MIX_SEQS, SSD_CONV - 1, SSD_CONV_CH), lambda i, j: (i, 0, 0))],
        out_shape=[jax.ShapeDtypeStruct((b, t, d), F32),
                   jax.ShapeDtypeStruct((b, GLA_HEADS, GLA_DK, GLA_DV), F32),
                   jax.ShapeDtypeStruct((b, SSD_HEADS, SSD_HEAD_DIM, SSD_STATE), F32),
                   jax.ShapeDtypeStruct((b, SSD_CONV - 1, SSD_CONV_CH), F32)],
        scratch_shapes=[pltpu.VMEM((TILE, PROJ_COLS), F32),
                        pltpu.VMEM((TILE + 8, SSD_CONV_CH), F32),
                        pltpu.VMEM((TILE, D_MODEL), BF16),
                        pltpu.VMEM((GLA_KW, GLA_DV), F32),
                        pltpu.VMEM((SSD_INNER, SSD_STATE), F32)] * MIX_SEQS,
        compiler_params=pltpu.CompilerParams(dimension_semantics=("arbitrary", "arbitrary"),
                                             vmem_limit_bytes=VMEM_LIMIT),
        name="mixer_prompt",
    )(x, *consts)


def _softmax_rows(s):
    e = jnp.exp(s - jnp.max(s, axis=-1, keepdims=True))
    return e / jnp.sum(e, axis=-1, keepdims=True)


FFN_CHUNK = 256
FFN_OFFSET = 1


def _attn_ffn_chain(h_ref, y_ref, mk_ref, mv_ref, lnxa_ref, wq_ref, wo_ref, lnxap_ref,
                    lnf_ref, lnfp_ref, wg_ref, wu_ref, wd_ref):
    h = h_ref[...]
    hn = _rms(h, lnxa_ref[...]).astype(BF16)
    yield
    heads_per_chunk = PROJ_CHUNK // XA_HEAD_DIM
    outs = []
    for c in range(D_MODEL // PROJ_CHUNK):
        q = _dot(hn, wq_ref[:, c * PROJ_CHUNK:(c + 1) * PROJ_CHUNK]).astype(BF16)
        yield
        for hh in range(heads_per_chunk):
            hd = c * heads_per_chunk + hh
            sl = slice(hd * XA_HEAD_DIM, (hd + 1) * XA_HEAD_DIM)
            s = _dot_nt(q[:, hh * XA_HEAD_DIM:(hh + 1) * XA_HEAD_DIM], mk_ref[0, :, sl])
            yield
            pr = _softmax_rows(s).astype(BF16)
            yield
            outs.append(_dot(pr, mv_ref[0, :, sl]).astype(BF16))
    a = jnp.concatenate(outs, axis=1)
    yield
    yield from _wo_ffn_phases(h, a, y_ref, wo_ref, lnxap_ref, lnf_ref, lnfp_ref, wg_ref, wu_ref, wd_ref)


def _wo_ffn_phases(h, a, y_ref, wo_ref, lnxap_ref, lnf_ref, lnfp_ref, wg_ref, wu_ref, wd_ref):
    parts = []
    for c in range(D_MODEL // PROJ_CHUNK):
        parts.append(_dot(a, wo_ref[:, c * PROJ_CHUNK:(c + 1) * PROJ_CHUNK]))
        yield
    h = h + _rms(jnp.concatenate(parts, axis=1), lnxap_ref[...])
    hf = _rms(h, lnf_ref[...]).astype(BF16)
    yield
    acts = []
    for c in range(D_FF // FFN_CHUNK):
        cs = slice(c * FFN_CHUNK, (c + 1) * FFN_CHUNK)
        gt = _dot(hf, wg_ref[:, cs])
        up = _dot(hf, wu_ref[:, cs])
        yield
        acts.append((_silu(gt) * up).astype(BF16))
    act = jnp.concatenate(acts, axis=1)
    yield
    parts = []
    for c in range(D_MODEL // PROJ_CHUNK):
        parts.append(_dot(act, wd_ref[:, c * PROJ_CHUNK:(c + 1) * PROJ_CHUNK]))
        yield
    y_ref[...] = h + _rms(jnp.concatenate(parts, axis=1), lnfp_ref[...])


def _attn_ffn_prompt_kernel(h_ref, mk_ref, mv_ref, *rest):
    consts, y_ref = rest[:-1], rest[-1]
    chains = []
    for sub in range(FFN_SUBTILES):
        rs = pl.ds(sub * TILE, TILE)
        chains.append(_attn_ffn_chain(h_ref.at[0, rs], y_ref.at[0, rs], mk_ref, mv_ref, *consts))
    _interleave(chains, offset=FFN_OFFSET)


def _attn_ffn_prompt(h, mkb, mvb, p):
    b, t, d = h.shape
    rows = FFN_SUBTILES * TILE
    consts = [p["ln_xa_pre"], p["w_xq"], p["w_xo"], p["ln_xa_post"], p["ln_ffn_pre"], p["ln_ffn_post"],
              p["w_gate"], p["w_up"], p["w_down"]]
    tok = pl.BlockSpec((1, rows, d), lambda i, j: (i, j, 0))
    mem = pl.BlockSpec((1, mkb.shape[1], d), lambda i, j: (i, 0, 0))
    return pl.pallas_call(
        _attn_ffn_prompt_kernel,
        grid=(b, t // rows),
        in_specs=[tok, mem, mem] + [_const_spec(c.shape) for c in consts],
        out_specs=tok,
        out_shape=jax.ShapeDtypeStruct((b, t, d), F32),
        compiler_params=pltpu.CompilerParams(dimension_semantics=("arbitrary", "arbitrary"),
                                             vmem_limit_bytes=VMEM_LIMIT),
        name="attn_ffn_prompt",
    )(h, mkb, mvb, *consts)


def _proj_sample_kernel(x_ref, lnpre_ref, win_ref, proj_ref):
    proj_ref[...] = _dot(_bf(_rms(x_ref[...], lnpre_ref[...])), win_ref[...])


def _proj_sample(x, p):
    n, d = x.shape
    return pl.pallas_call(
        _proj_sample_kernel,
        grid=(1,),
        in_specs=[_const_spec((n, d)), _const_spec((1, d)), _const_spec(p["w_in"].shape)],
        out_specs=pl.BlockSpec((n, PROJ_COLS), lambda i: (0, 0)),
        out_shape=jax.ShapeDtypeStruct((n, PROJ_COLS), F32),
        compiler_params=pltpu.CompilerParams(dimension_semantics=("arbitrary",),
                                             vmem_limit_bytes=VMEM_LIMIT),
        name="proj_sample",
    )(x, p["ln_mix_pre"], p["w_in"])


def _mixer_sample_kernel(seq_len, proj_ref, tail_ref, sg_ref, ss_ref, gw2_ref, gb_ref, gnw_ref,
                         cw_ref, cb_ref, dtb_ref, a_ref, d_ref, snw_ref,
                         mixed_ref, sg_out, ss_out, xpad_s, tpad_s):
    rows = proj_ref.shape[0]
    n_seq = rows // seq_len
    small = proj_ref[:, COL_SMALL:COL_SMALL + 128]
    logf = _gate_logf(small, gw2_ref, gb_ref)

    xbc = proj_ref[:, COL_XBC:COL_XBC + SSD_CONV_CH]
    xpad_s[0:8, :] = jnp.zeros((8, SSD_CONV_CH), F32)
    xpad_s[8:8 + rows, :] = xbc
    tpad_s[0:rows, :] = tail_ref[...]
    tpad_s[rows:rows + 8, :] = jnp.zeros((8, SSD_CONV_CH), F32)
    pos = lax.broadcasted_iota(jnp.int32, (rows, SSD_CONV_CH), 0) & (seq_len - 1)
    conv = cb_ref[...]
    for j in range(SSD_CONV - 1):
        back = SSD_CONV - 1 - j
        prev = jnp.where(pos >= back, xpad_s[8 - back:8 - back + rows, :],
                         tpad_s[seq_len - back:seq_len - back + rows, :])
        conv = conv + prev * cw_ref[j:j + 1, :]
    conv = conv + xbc * cw_ref[SSD_CONV - 1:SSD_CONV, :]
    xa = _silu(conv)

    def get_ss(s, j):
        return ss_ref[s, 2 * j:2 * j + 2].reshape(2 * SSD_HEAD_DIM, SSD_STATE)

    def put_ss(s, j, val):
        ss_out[s, 2 * j:2 * j + 2] = val.reshape(2, SSD_HEAD_DIM, SSD_STATE)

    ys = []
    _run(_ssd_phases(xa[:, :SSD_INNER], xa[:, SSD_INNER:SSD_INNER + 256], xa[:, SSD_INNER + 256:],
                     small, dtb_ref[...], a_ref[...], d_ref[...], get_ss, put_ss, seq_len, ys))
    mixed_ref[:, GLA_VW:] = _ssd_finish(jnp.concatenate(ys, axis=1), proj_ref[:, COL_Z:COL_Z + SSD_INNER],
                                        snw_ref[...])

    def get_sg(s):
        return sg_ref[s].reshape(GLA_KW, GLA_DV)

    def put_sg(s, val):
        sg_out[s] = val.reshape(GLA_HEADS, GLA_DK, GLA_DV)

    outs = []
    _run(_gla_phases(proj_ref[:, COL_Q:COL_Q + GLA_KW], proj_ref[:, COL_K:COL_K + GLA_KW],
                     proj_ref[:, COL_V:COL_V + GLA_VW], logf, get_sg, put_sg, seq_len, outs))
    for h in range(GLA_HEADS):
        g_h = proj_ref[:, COL_G + h * GLA_DV:COL_G + (h + 1) * GLA_DV]
        mixed_ref[:, h * GLA_DV:(h + 1) * GLA_DV] = _gla_finish(outs[h], g_h, gnw_ref[...])


def _mixer_sample(proj, tail, sg, ss, p, seq_len):
    n = proj.shape[0]
    rows = SAMPLE_SEQS * seq_len
    nb = sg.shape[0]
    consts = [p["gw2"], p["gb"], p["gnw"], p["conv_w"], p["conv_b"], p["dtb"], p["a_blk"], p["d_row"],
              p["snw"]]
    sg_spec = pl.BlockSpec((SAMPLE_SEQS, GLA_HEADS, GLA_DK, GLA_DV), lambda i: (i, 0, 0, 0))
    ss_spec = pl.BlockSpec((SAMPLE_SEQS, SSD_HEADS, SSD_HEAD_DIM, SSD_STATE), lambda i: (i, 0, 0, 0))
    return pl.pallas_call(
        functools.partial(_mixer_sample_kernel, seq_len),
        grid=(nb // SAMPLE_SEQS,),
        in_specs=[pl.BlockSpec((rows, PROJ_COLS), lambda i: (i, 0)),
                  pl.BlockSpec((rows, SSD_CONV_CH), lambda i: (i, 0)),
                  sg_spec, ss_spec] + [_const_spec(c.shape) for c in consts],
        out_specs=[pl.BlockSpec((rows, D_MODEL), lambda i: (i, 0)), sg_spec, ss_spec],
        out_shape=[jax.ShapeDtypeStruct((n, D_MODEL), F32),
                   jax.ShapeDtypeStruct(sg.shape, F32), jax.ShapeDtypeStruct(ss.shape, F32)],
        scratch_shapes=[pltpu.VMEM((rows + 8, SSD_CONV_CH), F32),
                        pltpu.VMEM((rows + 8, SSD_CONV_CH), F32)],
        compiler_params=pltpu.CompilerParams(dimension_semantics=("arbitrary",),
                                             vmem_limit_bytes=VMEM_LIMIT),
        name="mixer_sample",
    )(proj, tail, sg, ss, *consts)


def _post_mix_sample_kernel(x_ref, mixed_ref, wout_ref, lnpost_ref, lnxa_ref, wq_ref, h_ref, q_ref):
    m = _dot(_bf(mixed_ref[...]), wout_ref[...])
    h = x_ref[...] + _rms(m, lnpost_ref[...])
    h_ref[...] = h
    q_ref[...] = _bf(_dot(_bf(_rms(h, lnxa_ref[...])), wq_ref[...]))


def _post_mix_sample(x, mixed, p):
    n, d = x.shape
    full = pl.BlockSpec((n, d), lambda i: (0, 0))
    consts = [p["w_out"], p["ln_mix_post"], p["ln_xa_pre"], p["w_xq"]]
    return pl.pallas_call(
        _post_mix_sample_kernel,
        grid=(1,),
        in_specs=[full, full] + [_const_spec(c.shape) for c in consts],
        out_specs=[full, full],
        out_shape=[jax.ShapeDtypeStruct((n, d), F32), jax.ShapeDtypeStruct((n, d), BF16)],
        compiler_params=pltpu.CompilerParams(dimension_semantics=("arbitrary",),
                                             vmem_limit_bytes=VMEM_LIMIT),
        name="post_mix_sample",
    )(x, mixed, *consts)


def _attn_sample_kernel(seq_len, q_ref, k_hbm, v_hbm, a_ref, kbuf, vbuf, sem):
    step = pl.program_id(0)
    rows = q_ref.shape[0]
    n_seq = rows // seq_len

    def copies(at_step, slot):
        out = []
        for s in range(n_seq):
            for hd in range(XA_HEADS):
                b = at_step * n_seq + s
                out.append(pltpu.make_async_copy(k_hbm.at[0, b, :, hd, :], kbuf.at[slot, s, hd], sem.at[0, slot]))
                out.append(pltpu.make_async_copy(v_hbm.at[0, b, :, hd, :], vbuf.at[slot, s, hd], sem.at[1, slot]))
        return out

    @pl.when(step == 0)
    def _():
        for cp in copies(0, 0):
            cp.start()

    @pl.when(step + 1 < pl.num_programs(0))
    def _():
        for cp in copies(step + 1, (step + 1) % 2):
            cp.start()

    slot = step % 2
    for cp in copies(step, slot):
        cp.wait()

    q = q_ref[...]
    results = {}

    def one(s, hd):
        sc = _dot_nt(q[:, hd * XA_HEAD_DIM:(hd + 1) * XA_HEAD_DIM], _bf(kbuf[slot, s, hd]))
        yield
        e = jnp.exp(sc - jnp.max(sc, axis=-1, keepdims=True))
        pr = _bf(e / jnp.sum(e, axis=-1, keepdims=True))
        yield
        results[s, hd] = _dot(pr, _bf(vbuf[slot, s, hd]))

    _interleave([one(s, hd) for s in range(n_seq) for hd in range(XA_HEADS)])
    for hd in range(XA_HEADS):
        sl = slice(hd * XA_HEAD_DIM, (hd + 1) * XA_HEAD_DIM)
        out = results[0, hd]
        for s in range(1, n_seq):
            out = jnp.where(_row_in_seq(rows, seq_len, s, XA_HEAD_DIM), results[s, hd], out)
        a_ref[:, sl] = out


def _attn_sample(q, ck, cv, seq_len):
    n, d = q.shape
    _, nb, m, nh, hd = ck.shape
    rows = ATTN_SEQS * seq_len
    tok = pl.BlockSpec((rows, d), lambda i: (i, 0))
    hbm = pl.BlockSpec(memory_space=pl.ANY)
    return pl.pallas_call(
        functools.partial(_attn_sample_kernel, seq_len),
        grid=(nb // ATTN_SEQS,),
        in_specs=[tok, hbm, hbm],
        out_specs=tok,
        out_shape=jax.ShapeDtypeStruct((n, d), F32),
        scratch_shapes=[pltpu.VMEM((2, ATTN_SEQS, nh, m, hd), F32),
                        pltpu.VMEM((2, ATTN_SEQS, nh, m, hd), F32),
                        pltpu.SemaphoreType.DMA((2, 2))],
        compiler_params=pltpu.CompilerParams(dimension_semantics=("arbitrary",),
                                             vmem_limit_bytes=VMEM_LIMIT),
        name="attn_sample",
    )(q, ck, cv)


def _post_attn_sample_kernel(h_ref, a_ref, *rest):
    consts, y_ref = rest[:-1], rest[-1]
    chains = []
    for sub in range(h_ref.shape[0] // TILE):
        rs = pl.ds(sub * TILE, TILE)
        chains.append(_wo_ffn_phases(h_ref[rs, :], _bf(a_ref[rs, :]), y_ref.at[rs], *consts))
    _interleave(chains, offset=FFN_OFFSET)


def _post_attn_sample(h, a, p):
    n, d = h.shape
    full = pl.BlockSpec((n, d), lambda i: (0, 0))
    consts = [p["w_xo"], p["ln_xa_post"], p["ln_ffn_pre"], p["ln_ffn_post"], p["w_gate"], p["w_up"],
              p["w_down"]]
    return pl.pallas_call(
        _post_attn_sample_kernel,
        grid=(1,),
        in_specs=[full, full] + [_const_spec(c.shape) for c in consts],
        out_specs=full,
        out_shape=jax.ShapeDtypeStruct((n, d), F32),
        compiler_params=pltpu.CompilerParams(dimension_semantics=("arbitrary",),
                                             vmem_limit_bytes=VMEM_LIMIT),
        name="post_attn_sample",
    )(h, a, *consts)


SRC_GLR = GLA_KW + GLA_KW + GLA_VW + GLA_VW
SRC_Z = SRC_GLR + GLA_GATE_RANK
SRC_XBC = SRC_Z + SSD_INNER
SRC_DT = SRC_XBC + SSD_CONV_CH
IN_COLS = SRC_DT + SSD_HEADS


def _pack_w_in_kernel(wt_hbm, o_ref, buf, sem):
    pieces = [(0, COL_Q, SRC_GLR),
              (SRC_Z, COL_Z, SSD_INNER),
              (SRC_XBC, COL_XBC, SSD_CONV_CH),
              (SRC_GLR, COL_SMALL, GLA_GATE_RANK),
              (SRC_DT, COL_SMALL + DT_LANE, SSD_HEADS)]
    copies = [pltpu.make_async_copy(wt_hbm.at[0, pl.ds(src, n), :], buf.at[pl.ds(dst, n), :], sem.at[0])
              for src, dst, n in pieces]
    for cp in copies:
        cp.start()
    used = COL_SMALL + DT_LANE + SSD_HEADS
    buf[used:PROJ_COLS, :] = jnp.zeros((PROJ_COLS - used, D_MODEL), F32)
    for cp in copies:
        cp.wait()
    for j in range(PROJ_COLS // 128):
        blk = buf[128 * j:128 * (j + 1), :]
        if 128 * j < COL_K:
            blk = blk * (GLA_DK ** -0.5)
        o_ref[:, 128 * j:128 * (j + 1)] = blk.T.astype(BF16)


def _pack_w_in(w_in):
    wt = jnp.swapaxes(w_in, 1, 2)
    return pl.pallas_call(
        _pack_w_in_kernel,
        grid=(1,),
        in_specs=[pl.BlockSpec(memory_space=pl.ANY)],
        out_specs=pl.BlockSpec((D_MODEL, PROJ_COLS), lambda i: (0, 0)),
        out_shape=jax.ShapeDtypeStruct((D_MODEL, PROJ_COLS), BF16),
        scratch_shapes=[pltpu.VMEM((PROJ_COLS, D_MODEL), F32), pltpu.SemaphoreType.DMA((1,))],
        compiler_params=pltpu.CompilerParams(dimension_semantics=("arbitrary",),
                                             vmem_limit_bytes=VMEM_LIMIT),
        name="pack_w_in",
    )(wt)


def _pack_params(ln_mix_pre, ln_mix_post, w_in_packed, gla_gate_w2, gla_gate_b, gla_norm_w, ssd_conv_w,
                 ssd_conv_b, ssd_dt_bias, ssd_A_log, ssd_D, ssd_norm_w, w_out, ln_xa_pre, ln_xa_post,
                 mem_norm_w, w_xq, w_xk, w_xv, w_xo, ln_ffn_pre, ln_ffn_post, w_gate, w_up, w_down):

    def small_row(v):
        return jnp.zeros((1, 128), F32).at[0, DT_LANE:DT_LANE + SSD_HEADS].set(v)

    row = lambda v: v.reshape(1, -1)
    return {
        "ln_mix_pre": row(ln_mix_pre), "ln_mix_post": row(ln_mix_post),
        "w_in": w_in_packed,
        "gw2": _bf(jnp.zeros((128, GLA_KW), F32).at[:GLA_GATE_RANK].set(gla_gate_w2)),
        "gb": row(gla_gate_b), "gnw": row(gla_norm_w),
        "conv_w": ssd_conv_w, "conv_b": row(ssd_conv_b),
        "dtb": small_row(ssd_dt_bias),
        "a_blk": small_row(-jnp.exp(ssd_A_log)),
        "d_row": jnp.repeat(ssd_D, SSD_HEAD_DIM).reshape(1, SSD_INNER),
        "snw": row(ssd_norm_w),
        "w_out": _bf(w_out),
        "ln_xa_pre": row(ln_xa_pre), "ln_xa_post": row(ln_xa_post), "mem_norm_w": row(mem_norm_w),
        "w_xq": _bf(w_xq * (XA_HEAD_DIM ** -0.5)), "w_xk": _bf(w_xk), "w_xv": _bf(w_xv), "w_xo": _bf(w_xo),
        "ln_ffn_pre": row(ln_ffn_pre), "ln_ffn_post": row(ln_ffn_post),
        "w_gate": _bf(w_gate), "w_up": _bf(w_up), "w_down": _bf(w_down),
    }


def kernel(x_prompt, x_sample, mem_prompt, state_gla, state_ssm, state_conv, cache_mem_k, cache_mem_v,
           ln_mix_pre, ln_mix_post, w_in, gla_gate_w2, gla_gate_b, gla_norm_w, ssd_conv_w, ssd_conv_b,
           ssd_dt_bias, ssd_A_log, ssd_D, ssd_norm_w, w_out, ln_xa_pre, ln_xa_post, mem_norm_w,
           w_xq, w_xk, w_xv, w_xo, ln_ffn_pre, ln_ffn_post, w_gate, w_up, w_down):
    assert w_in.shape[0] == 1, "single-layer kernel"
    layer = [a[0] for a in (ln_mix_pre, ln_mix_post, w_in, gla_gate_w2, gla_gate_b, gla_norm_w, ssd_conv_w,
                            ssd_conv_b, ssd_dt_bias, ssd_A_log, ssd_D, ssd_norm_w, w_out, ln_xa_pre,
                            ln_xa_post, mem_norm_w, w_xq, w_xk, w_xv, w_xo, ln_ffn_pre, ln_ffn_post,
                            w_gate, w_up, w_down)]
    layer[2] = _pack_w_in(w_in)
    p = _pack_params(*layer)
    bp, mem_len, _ = mem_prompt.shape
    bs, ts, _ = x_sample.shape
    assert ts == SSD_CONV, "the padded conv-tail layout assumes one new row per conv tap"

    mk, mv, mkb, mvb = _memkv(mem_prompt, p["mem_norm_w"], p["w_xk"], p["w_xv"])
    h_p, gla_p, ssm_p, conv_p = _mixer_prompt(x_prompt, p)
    y_p = _attn_ffn_prompt(h_p, mkb, mvb, p)

    xs = x_sample.reshape(bs * ts, D_MODEL)
    proj = _proj_sample(xs, p)
    tail = jnp.pad(state_conv[0], ((0, 0), (1, 0), (0, 0))).reshape(bs * ts, SSD_CONV_CH)
    mixed, gla_s, ssm_s = _mixer_sample(proj, tail, state_gla[0], state_ssm[0], p, ts)
    h_s, q_s = _post_mix_sample(xs, mixed, p)
    a_s = _attn_sample(q_s, cache_mem_k, cache_mem_v, ts)
    y_s = _post_attn_sample(h_s, a_s, p)
    conv_s = proj[:, COL_XBC:COL_XBC + SSD_CONV_CH].reshape(bs, ts, SSD_CONV_CH)[:, ts - (SSD_CONV - 1):]

    return (y_p, y_s.reshape(bs, ts, D_MODEL), gla_p[None], ssm_p[None], conv_p[None],
            mk, mv, gla_s[None], ssm_s[None], conv_s[None])
```

```python
import functools

import jax
import jax.numpy as jnp
from jax import lax
from jax.experimental import pallas as pl
from jax.experimental.pallas import tpu as pltpu

F32 = jnp.float32
BF16 = jnp.bfloat16

D_MODEL = 1024
GLA_HEADS = 4
GLA_DK = 64
GLA_DV = 128
GLA_KW = GLA_HEADS * GLA_DK
GLA_VW = GLA_HEADS * GLA_DV
GLA_GATE_RANK = 16
GLA_GATE_TAU = 16.0
SSD_INNER = 512
SSD_HEAD_DIM = 64
SSD_HEADS = 8
SSD_GROUPS = 2
SSD_STATE = 128
SSD_CONV = 4
SSD_CONV_CH = 1024
XA_HEADS = 4
XA_HEAD_DIM = 256
D_FF = 2816
EPS = 1e-6

COL_Q = 0
COL_K = 256
COL_V = 512
COL_G = 1024
COL_Z = 1536
COL_XBC = 2048
COL_SMALL = 3072
DT_LANE = GLA_GATE_RANK
PROJ_COLS = 3200

GLA_CHUNK = 64
TILE = 256
FFN_SUBTILES = 4
MIX_SEQS = 4
SAMPLE_SEQS = 16
ATTN_SEQS = 4
VMEM_LIMIT = 56 * 1024 * 1024


def _bf(x):
    return x.astype(BF16)


def _dot(a, b):
    return jnp.dot(a, b, preferred_element_type=F32)


def _dot_nt(a, b):
    return lax.dot_general(a, b, (((1,), (1,)), ((), ())), preferred_element_type=F32)


def _dot_tn(a, b):
    return lax.dot_general(a, b, (((0,), (0,)), ((), ())), preferred_element_type=F32)


def _dot3(m, a):
    hi = _bf(a)
    r = a - hi.astype(F32)
    mid = _bf(r)
    lo = _bf(r - mid.astype(F32))
    return _dot(m, hi) + _dot(m, mid) + _dot(m, lo)


def _rms(x, w):
    ms = jnp.mean(x * x, axis=-1, keepdims=True)
    return x * lax.rsqrt(ms + EPS) * w


LOG2E = 1.4426950408889634


def _exp_neg(x):
    return jnp.exp2(x * (-LOG2E))


def _silu(x):
    return x / (1.0 + _exp_neg(x))


def _softplus(x):
    e = _exp_neg(jnp.abs(x))
    u = 1.0 + e
    log1p_e = jnp.where(u == 1.0, e, jnp.log(u) * (e / (u - 1.0)))
    return jnp.maximum(x, 0.0) + log1p_e


def _log_sigmoid(x):
    return jnp.minimum(x, 0.0) - jnp.log(1.0 + _exp_neg(jnp.abs(x)))


def _seg_masks(rows, seq_len):
    t = lax.broadcasted_iota(jnp.int32, (rows, rows), 0)
    u = lax.broadcasted_iota(jnp.int32, (rows, rows), 1)
    if seq_len == rows:
        same = None
        tril = u <= t
    else:
        shift = seq_len.bit_length() - 1
        assert 1 << shift == seq_len
        same = (t >> shift) == (u >> shift)
        tril = jnp.logical_and(same, u <= t)
    return same, tril


def _mask_to_bf16(mask, rows):
    if mask is None:
        return jnp.ones((rows, rows), BF16)
    return jnp.where(mask, 1.0, 0.0).astype(BF16)


def _row_in_seq(rows, seq_len, s, width):
    r = lax.broadcasted_iota(jnp.int32, (rows, width), 0)
    return jnp.logical_and(r >= s * seq_len, r < (s + 1) * seq_len)


def _interleave(chains, offset=0):
    waiting = list(enumerate(chains))
    active = []
    rnd = 0
    while waiting or active:
        while waiting and waiting[0][0] * offset <= rnd:
            active.append(waiting.pop(0)[1])
        alive = []
        for c in active:
            try:
                next(c)
                alive.append(c)
            except StopIteration:
                pass
        active = alive
        rnd += 1


def _run(phases):
    for _ in phases:
        pass


def _pad_rows(x, rows):
    if x.shape[0] >= rows:
        return x
    return jnp.concatenate([x, jnp.zeros((rows - x.shape[0], x.shape[1]), x.dtype)], axis=0)


def _gla_phases(q, k, v, lf, get_state, put_state, seq_len, outs):
    rows = q.shape[0]
    n_seq = rows // seq_len
    same, tril = _seg_masks(rows, seq_len)
    tril_b = _mask_to_bf16(tril, rows)
    same_b = _mask_to_bf16(same, rows)

    b = _dot3(tril_b, lf)
    if n_seq == 1:
        bl = jnp.broadcast_to(b[rows - 1:rows, :], b.shape)
    else:
        bl = _dot3(same_b, lf)
    qt = q * jnp.exp(b)
    kt = k * jnp.exp(-b)
    kd = k * jnp.exp(bl - b)
    bl_t = _pad_rows(bl, 128).T
    yield

    lane = lax.broadcasted_iota(jnp.int32, (rows, GLA_KW), 1)
    lhs = jnp.concatenate(
        [jnp.where((lane >= h * GLA_DK) & (lane < (h + 1) * GLA_DK), qt, 0.0)
         for h in range(GLA_HEADS)], axis=0).astype(BF16)
    att = _dot_nt(lhs, kt.astype(BF16))
    t4 = lax.broadcasted_iota(jnp.int32, (GLA_HEADS * rows, rows), 0) & (rows - 1)
    u4 = lax.broadcasted_iota(jnp.int32, (GLA_HEADS * rows, rows), 1)
    causal4 = u4 <= t4
    if n_seq > 1:
        shift = seq_len.bit_length() - 1
        causal4 = jnp.logical_and(causal4, (t4 >> shift) == (u4 >> shift))
    att = jnp.where(causal4, att, 0.0).astype(BF16)
    vb = v.astype(BF16)
    kdb = kd.astype(BF16)
    yield

    o_inter = None
    r4 = lax.broadcasted_iota(jnp.int32, (GLA_HEADS * rows, GLA_DV), 0) & (rows - 1)
    for s in range(n_seq):
        st = get_state(s)
        oi = _dot(lhs, st.astype(BF16))
        if n_seq == 1:
            o_inter = oi
            kd_s = kdb
        else:
            m4 = jnp.logical_and(r4 >= s * seq_len, r4 < (s + 1) * seq_len)
            oi = jnp.where(m4, oi, 0.0)
            o_inter = oi if o_inter is None else o_inter + oi
            kd_s = jnp.where(_row_in_seq(rows, seq_len, s, GLA_KW), kdb, jnp.zeros_like(kdb))
        blocks = []
        for hp in range(GLA_HEADS // 2):
            upd = _dot_tn(kd_s[:, 2 * hp * GLA_DK:2 * (hp + 1) * GLA_DK],
                          vb[:, 2 * hp * GLA_DV:2 * (hp + 1) * GLA_DV])
            blocks += [upd[:GLA_DK, :GLA_DV], upd[GLA_DK:, GLA_DV:]]
        upd_d = jnp.concatenate(blocks, axis=0)
        c0 = s * seq_len
        decay = jnp.exp(jnp.broadcast_to(bl_t[:, c0:c0 + 1], (GLA_KW, GLA_DV)))
        put_state(s, decay * st + upd_d)
        if n_seq > 1 and s % 4 == 3:
            yield
    yield

    for h in range(GLA_HEADS):
        o_h = _dot(att[h * rows:(h + 1) * rows], vb[:, h * GLA_DV:(h + 1) * GLA_DV])
        outs.append(o_h + o_inter[h * rows:(h + 1) * rows])


def _ssd_phases(xs, bm, cm, dt_raw_blk, dtb_blk, a_blk, d_row, get_state, put_state, seq_len, ys):
    rows = xs.shape[0]
    n_seq = rows // seq_len
    same, tril = _seg_masks(rows, seq_len)
    tril_b = _mask_to_bf16(tril, rows)
    same_b = _mask_to_bf16(same, rows)

    dt = _softplus(dt_raw_blk + dtb_blk)
    dta = dt * a_blk
    lc = _dot3(tril_b, dta)
    if n_seq == 1:
        ll = jnp.broadcast_to(lc[rows - 1:rows, :], lc.shape)
    else:
        ll = _dot3(same_b, dta)
    elc = jnp.exp(lc)
    w = jnp.exp(ll - lc) * dt
    ell = jnp.exp(ll)
    lc_t = _pad_rows(lc, 128).T[:, :rows]
    yield

    bmb = bm.astype(BF16)
    cmb = cm.astype(BF16)
    cb = [_dot_nt(cmb[:, g * SSD_STATE:(g + 1) * SSD_STATE],
                  bmb[:, g * SSD_STATE:(g + 1) * SSD_STATE]) for g in range(SSD_GROUPS)]
    yield

    lane128 = lax.broadcasted_iota(jnp.int32, (rows, 128), 1)
    lo128 = lane128 < SSD_HEAD_DIM
    sub128 = lax.broadcasted_iota(jnp.int32, (128, 128), 0) < SSD_HEAD_DIM

    def col(x, h, width):
        return jnp.broadcast_to(x[:, DT_LANE + h:DT_LANE + h + 1], (x.shape[0], width))

    heads_per_group = SSD_HEADS // SSD_GROUPS
    for j in range(SSD_HEADS // 2):
        g = (2 * j) // heads_per_group
        xs_p = xs[:, 128 * j:128 * (j + 1)]
        dt_p = jnp.where(lo128, col(dt, 2 * j, 128), col(dt, 2 * j + 1, 128))
        xdt = xs_p * dt_p
        x_lo = jnp.where(lo128, xdt, 0.0).astype(BF16)
        x_hi = jnp.where(lo128, 0.0, xdt).astype(BF16)
        y_p = None
        for h, x_half in ((2 * j, x_lo), (2 * j + 1, x_hi)):
            seg = col(lc, h, rows) - jnp.broadcast_to(lc_t[DT_LANE + h:DT_LANE + h + 1, :], (rows, rows))
            w_h = jnp.where(tril, cb[g] * jnp.exp(seg), 0.0).astype(BF16)
            y_h = _dot(w_h, x_half)
            y_p = y_h if y_p is None else y_p + y_h
            yield

        elc_p = jnp.where(lo128, col(elc, 2 * j, 128), col(elc, 2 * j + 1, 128))
        w_p = jnp.where(lo128, col(w, 2 * j, 128), col(w, 2 * j + 1, 128))
        xw = (xs_p * w_p).astype(BF16)
        bm_g = bmb[:, g * SSD_STATE:(g + 1) * SSD_STATE]
        cm_g = cmb[:, g * SSD_STATE:(g + 1) * SSD_STATE]
        y_inter = None
        for s in range(n_seq):
            st = get_state(s, j)
            ci = _dot_nt(cm_g, st.astype(BF16))
            if n_seq == 1:
                y_inter = ci
                xw_s = xw
            else:
                msk = _row_in_seq(rows, seq_len, s, 128)
                ci = jnp.where(msk, ci, 0.0)
                y_inter = ci if y_inter is None else y_inter + ci
                xw_s = jnp.where(msk, xw, jnp.zeros_like(xw))
            upd = _dot_tn(xw_s, bm_g)
            r0 = s * seq_len
            e0 = jnp.broadcast_to(ell[r0:r0 + 1, DT_LANE + 2 * j:DT_LANE + 2 * j + 1], (128, 128))
            e1 = jnp.broadcast_to(ell[r0:r0 + 1, DT_LANE + 2 * j + 1:DT_LANE + 2 * j + 2], (128, 128))
            put_state(s, j, jnp.where(sub128, e0, e1) * st + upd)
        ys.append(y_p + y_inter * elc_p + d_row[:, 128 * j:128 * (j + 1)] * xs_p)
        yield


def _ssd_phases_seq(xs, bm, cm, dt_raw_blk, dtb_blk, a_blk, d_row, get_state, put_state, ys):
    rows = xs.shape[0]
    _, tril = _seg_masks(rows, rows)
    tril_b = _mask_to_bf16(tril, rows)
    src = lax.broadcasted_iota(jnp.int32, (rows, rows), 0)
    dst = lax.broadcasted_iota(jnp.int32, (rows, rows), 1)
    causal_t = src <= dst

    dt = _softplus(dt_raw_blk + dtb_blk)
    lc = _dot3(tril_b, dt * a_blk)
    lc_t = lc.T
    dt_t = dt.T
    elc_t = jnp.exp(lc_t)
    w_t = jnp.exp(lc_t[:, rows - 1:rows] - lc_t) * dt_t
    ell = jnp.exp(lc[rows - 1:rows, :])
    lc2 = lc * LOG2E
    lc2_t = lc_t * LOG2E
    yield

    xs_t = xs.T
    bmb = bm.astype(BF16)
    cmb = cm.astype(BF16)
    cb_t = [_dot_nt(bmb[:, g * SSD_STATE:(g + 1) * SSD_STATE],
                    cmb[:, g * SSD_STATE:(g + 1) * SSD_STATE]) for g in range(SSD_GROUPS)]
    yield

    heads_per_group = SSD_HEADS // SSD_GROUPS
    y_t = []
    for h in range(SSD_HEADS):
        g = h // heads_per_group
        lane = DT_LANE + h
        xs_h = xs_t[h * SSD_HEAD_DIM:(h + 1) * SSD_HEAD_DIM, :]
        seg_t = lc2_t[lane:lane + 1, :] - jnp.broadcast_to(lc2[:, lane:lane + 1], (rows, rows))
        w_h = jnp.where(causal_t, cb_t[g] * jnp.exp2(seg_t), 0.0).astype(BF16)
        y_h = _dot((xs_h * dt_t[lane:lane + 1, :]).astype(BF16), w_h)
        st = get_state(h)
        ci = _dot_nt(st.astype(BF16), cmb[:, g * SSD_STATE:(g + 1) * SSD_STATE])
        y_t.append(y_h + ci * elc_t[lane:lane + 1, :])
        upd = _dot((xs_h * w_t[lane:lane + 1, :]).astype(BF16), bmb[:, g * SSD_STATE:(g + 1) * SSD_STATE])
        put_state(h, jnp.broadcast_to(ell[:, lane:lane + 1], st.shape) * st + upd)
        yield
    ys.append(jnp.concatenate(y_t, axis=0).T + d_row * xs)


def _gate_logf(small_blk, gw2_ref, gb_ref):
    logits = _dot(_bf(small_blk), gw2_ref[...]) + gb_ref[...]
    return _log_sigmoid(logits) * (1.0 / GLA_GATE_TAU)


def _gla_finish(o_h, g_h, gnw):
    return _rms(o_h, gnw) * _silu(g_h)


def _ssd_finish(y, z, snw):
    y = y * _silu(z)
    gs = SSD_INNER // SSD_GROUPS
    parts = [_rms(y[:, g * gs:(g + 1) * gs], snw[:, g * gs:(g + 1) * gs]) for g in range(SSD_GROUPS)]
    return jnp.concatenate(parts, axis=1)


def _memkv_kernel(mem_ref, nw_ref, wk_ref, wv_ref, k_ref, v_ref, kb_ref, vb_ref):
    mn = _bf(_rms(mem_ref[0], nw_ref[...]))
    mk = _dot(mn, wk_ref[...])
    mv = _dot(mn, wv_ref[...])
    for h in range(XA_HEADS):
        k_ref[0, 0, :, h, :] = mk[:, h * XA_HEAD_DIM:(h + 1) * XA_HEAD_DIM]
        v_ref[0, 0, :, h, :] = mv[:, h * XA_HEAD_DIM:(h + 1) * XA_HEAD_DIM]
    kb_ref[0] = _bf(mk)
    vb_ref[0] = _bf(mv)


def _const_spec(shape):
    nd = len(shape)
    return pl.BlockSpec(shape, lambda *_: (0,) * nd, pipeline_mode=pl.Buffered(1))


def _memkv(mem, nw, wk, wv):
    b, m, d = mem.shape
    blk = pl.BlockSpec((1, m, d), lambda i: (i, 0, 0))
    cache_blk = pl.BlockSpec((1, 1, m, XA_HEADS, XA_HEAD_DIM), lambda i: (0, i, 0, 0, 0))
    cache_shape = jax.ShapeDtypeStruct((1, b, m, XA_HEADS, XA_HEAD_DIM), F32)
    return pl.pallas_call(
        _memkv_kernel,
        grid=(b,),
        in_specs=[blk, _const_spec((1, d)), _const_spec((d, d)), _const_spec((d, d))],
        out_specs=[cache_blk, cache_blk, blk, blk],
        out_shape=[cache_shape] * 2 + [jax.ShapeDtypeStruct((b, m, d), BF16)] * 2,
        compiler_params=pltpu.CompilerParams(dimension_semantics=("arbitrary",),
                                             vmem_limit_bytes=VMEM_LIMIT),
        name="mem_kv",
    )(mem, nw, wk, wv)


PROJ_CHUNK = 512


def _mixer_prompt_chain(x_ref, h_ref, proj_s, xpad_s, mixed_s, sg_s, ss_s,
                        lnpre_ref, win_ref, gw2_ref, gb_ref, gnw_ref, cw_ref, cb_ref,
                        dtb_ref, a_ref, d_ref, snw_ref, wout_ref, lnpost_ref):
    rows = TILE
    x = x_ref[...]
    hn = _rms(x, lnpre_ref[...]).astype(BF16)
    yield

    def proj_cols(c0, width):
        proj_s[:, c0:c0 + width] = _dot(hn, win_ref[:, c0:c0 + width])

    def late_proj():
        for c0 in (COL_Q, COL_V, COL_G, COL_Z):
            proj_cols(c0, PROJ_CHUNK)
            yield

    side = [late_proj()]

    def tick():
        if side[0] is not None:
            try:
                next(side[0])
            except StopIteration:
                side[0] = None

    for c in range(SSD_CONV_CH // PROJ_CHUNK):
        proj_cols(COL_XBC + c * PROJ_CHUNK, PROJ_CHUNK)
        yield
    proj_cols(COL_SMALL, 128)

    xa = []
    for c in range(SSD_CONV_CH // PROJ_CHUNK):
        cs = slice(c * PROJ_CHUNK, (c + 1) * PROJ_CHUNK)
        xpad_s[8:8 + rows, cs] = proj_s[:, COL_XBC + c * PROJ_CHUNK:COL_XBC + (c + 1) * PROJ_CHUNK]
        conv = cb_ref[:, cs]
        for j in range(SSD_CONV):
            conv = conv + xpad_s[5 + j:5 + j + rows, cs] * cw_ref[j:j + 1, cs]
        xpad_s[5:8, cs] = xpad_s[5 + rows:8 + rows, cs]
        xa.append(_silu(conv))
        tick()
        yield

    small = proj_s[:, COL_SMALL:COL_SMALL + 128]
    logf = _gate_logf(small, gw2_ref, gb_ref)

    def get_ss(h):
        return ss_s[SSD_HEAD_DIM * h:SSD_HEAD_DIM * (h + 1), :]

    def put_ss(h, val):
        ss_s[SSD_HEAD_DIM * h:SSD_HEAD_DIM * (h + 1), :] = val

    ys = []
    ssd = _ssd_phases_seq(xa[0], xa[1][:, :SSD_GROUPS * SSD_STATE], xa[1][:, SSD_GROUPS * SSD_STATE:],
                          small, dtb_ref[...], a_ref[...], d_ref[...], get_ss, put_ss, ys)

    def get_sg(s):
        return sg_s[...]

    def put_sg(s, val):
        sg_s[...] = val

    def gla_chunks():
        for c in range(rows // GLA_CHUNK):
            r0 = c * GLA_CHUNK
            sl = slice(r0, r0 + GLA_CHUNK)
            outs = []
            yield from _gla_phases(proj_s[sl, COL_Q:COL_Q + GLA_KW], proj_s[sl, COL_K:COL_K + GLA_KW],
                                   proj_s[sl, COL_V:COL_V + GLA_VW], logf[sl], get_sg, put_sg, GLA_CHUNK, outs)
            for h in range(GLA_HEADS):
                g_h = proj_s[sl, COL_G + h * GLA_DV:COL_G + (h + 1) * GLA_DV]
                mixed_s[sl, h * GLA_DV:(h + 1) * GLA_DV] = _gla_finish(outs[h], g_h, gnw_ref[...]).astype(BF16)
            yield

    while side[0] is not None:
        next(ssd)
        tick()
        yield
    alive = [ssd, gla_chunks()]
    while alive:
        for phases in list(alive):
            try:
                next(phases)
            except StopIteration:
                alive.remove(phases)
                continue
            yield
    mixed_s[:, GLA_VW:] = _ssd_finish(jnp.concatenate(ys, axis=1), proj_s[:, COL_Z:COL_Z + SSD_INNER],
                                      snw_ref[...]).astype(BF16)
    yield

    mixed = mixed_s[...]
    ms = []
    for c in range(D_MODEL // PROJ_CHUNK):
        ms.append(_dot(mixed, wout_ref[:, c * PROJ_CHUNK:(c + 1) * PROJ_CHUNK]))
        yield
    h_ref[...] = x + _rms(jnp.concatenate(ms, axis=1), lnpost_ref[...])


N_MIXER_WEIGHTS = 13
N_MIXER_SCRATCH = 5
MIX_OFFSET = 3


def _mixer_prompt_kernel(x_ref, *rest):
    consts = rest[:N_MIXER_WEIGHTS]
    h_ref, sg_out, ss_out, conv_out = rest[N_MIXER_WEIGHTS:N_MIXER_WEIGHTS + 4]
    scratch = rest[N_MIXER_WEIGHTS + 4:]
    per_seq = [scratch[i * N_MIXER_SCRATCH:(i + 1) * N_MIXER_SCRATCH] for i in range(MIX_SEQS)]
    t = pl.program_id(1)

    @pl.when(t == 0)
    def _():
        for _, xpad_s, _, sg_s, ss_s in per_seq:
            sg_s[...] = jnp.zeros_like(sg_s)
            ss_s[...] = jnp.zeros_like(ss_s)
            xpad_s[0:8, :] = jnp.zeros((8, SSD_CONV_CH), F32)

    _interleave([_mixer_prompt_chain(x_ref.at[i], h_ref.at[i], *per_seq[i], *consts) for i in range(MIX_SEQS)],
                offset=MIX_OFFSET)

    @pl.when(t == pl.num_programs(1) - 1)
    def _():
        for i, (_, xpad_s, _, sg_s, ss_s) in enumerate(per_seq):
            sg_out[i] = sg_s[...].reshape(GLA_HEADS, GLA_DK, GLA_DV)
            ss_out[i] = ss_s[...].reshape(SSD_HEADS, SSD_HEAD_DIM, SSD_STATE)
            conv_out[i] = xpad_s[5:8, :]


def _mixer_prompt(x, p):
    b, t, d = x.shape
    nt = t // TILE
    consts = [p["ln_mix_pre"], p["w_in"], p["gw2"], p["gb"], p["gnw"], p["conv_w"], p["conv_b"],
              p["dtb"], p["a_blk"], p["d_row"], p["snw"], p["w_out"], p["ln_mix_post"]]
    assert len(consts) == N_MIXER_WEIGHTS
    tok = pl.BlockSpec((MIX_SEQS, TILE, d), lambda i, j: (i, j, 0))
    return pl.pallas_call(
        _mixer_prompt_kernel,
        grid=(b // MIX_SEQS, nt),
        in_specs=[tok] + [_const_spec(c.shape) for c in consts],
        out_specs=[tok,
                   pl.BlockSpec((MIX_SEQS, GLA_HEADS, GLA_DK, GLA_DV), lambda i, j: (i, 0, 0, 0)),
                   pl.BlockSpec((MIX_SEQS, SSD_HEADS, SSD_HEAD_DIM, SSD_STATE), lambda i, j: (i, 0, 0, 0)),
                   pl.BlockSpec((MIX_SEQS, SSD_CONV - 1, SSD_CONV_CH), lambda i, j: (i, 0, 0))],
        out_shape=[jax.ShapeDtypeStruct((b, t, d), F32),
                   jax.ShapeDtypeStruct((b, GLA_HEADS, GLA_DK, GLA_DV), F32),
                   jax.ShapeDtypeStruct((b, SSD_HEADS, SSD_HEAD_DIM, SSD_STATE), F32),
                   jax.ShapeDtypeStruct((b, SSD_CONV - 1, SSD_CONV_CH), F32)],
        scratch_shapes=[pltpu.VMEM((TILE, PROJ_COLS), F32),
                        pltpu.VMEM((TILE + 8, SSD_CONV_CH), F32),
                        pltpu.VMEM((TILE, D_MODEL), BF16),
                        pltpu.VMEM((GLA_KW, GLA_DV), F32),
                        pltpu.VMEM((SSD_INNER, SSD_STATE), F32)] * MIX_SEQS,
        compiler_params=pltpu.CompilerParams(dimension_semantics=("arbitrary", "arbitrary"),
                                             vmem_limit_bytes=VMEM_LIMIT),
        name="mixer_prompt",
    )(x, *consts)


def _softmax_rows(s):
    e = jnp.exp(s - jnp.max(s, axis=-1, keepdims=True))
    return e / jnp.sum(e, axis=-1, keepdims=True)


FFN_CHUNK = 256
FFN_OFFSET = 1


def _attn_ffn_chain(h_ref, y_ref, mk_ref, mv_ref, lnxa_ref, wq_ref, wo_ref, lnxap_ref,
                    lnf_ref, lnfp_ref, wg_ref, wu_ref, wd_ref):
    h = h_ref[...]
    hn = _rms(h, lnxa_ref[...]).astype(BF16)
    yield
    heads_per_chunk = PROJ_CHUNK // XA_HEAD_DIM
    outs = []
    for c in range(D_MODEL // PROJ_CHUNK):
        q = _dot(hn, wq_ref[:, c * PROJ_CHUNK:(c + 1) * PROJ_CHUNK]).astype(BF16)
        yield
        for hh in range(heads_per_chunk):
            hd = c * heads_per_chunk + hh
            sl = slice(hd * XA_HEAD_DIM, (hd + 1) * XA_HEAD_DIM)
            s = _dot_nt(q[:, hh * XA_HEAD_DIM:(hh + 1) * XA_HEAD_DIM], mk_ref[0, :, sl])
            yield
            pr = _softmax_rows(s).astype(BF16)
            yield
            outs.append(_dot(pr, mv_ref[0, :, sl]).astype(BF16))
    a = jnp.concatenate(outs, axis=1)
    yield
    yield from _wo_ffn_phases(h, a, y_ref, wo_ref, lnxap_ref, lnf_ref, lnfp_ref, wg_ref, wu_ref, wd_ref)


def _wo_ffn_phases(h, a, y_ref, wo_ref, lnxap_ref, lnf_ref, lnfp_ref, wg_ref, wu_ref, wd_ref):
    parts = []
    for c in range(D_MODEL // PROJ_CHUNK):
        parts.append(_dot(a, wo_ref[:, c * PROJ_CHUNK:(c + 1) * PROJ_CHUNK]))
        yield
    h = h + _rms(jnp.concatenate(parts, axis=1), lnxap_ref[...])
    hf = _rms(h, lnf_ref[...]).astype(BF16)
    yield
    acts = []
    for c in range(D_FF // FFN_CHUNK):
        cs = slice(c * FFN_CHUNK, (c + 1) * FFN_CHUNK)
        gt = _dot(hf, wg_ref[:, cs])
        up = _dot(hf, wu_ref[:, cs])
        yield
        acts.append((_silu(gt) * up).astype(BF16))
    act = jnp.concatenate(acts, axis=1)
    yield
    parts = []
    for c in range(D_MODEL // PROJ_CHUNK):
        parts.append(_dot(act, wd_ref[:, c * PROJ_CHUNK:(c + 1) * PROJ_CHUNK]))
        yield
    y_ref[...] = h + _rms(jnp.concatenate(parts, axis=1), lnfp_ref[...])


def _attn_ffn_prompt_kernel(h_ref, mk_ref, mv_ref, *rest):
    consts, y_ref = rest[:-1], rest[-1]
    chains = []
    for sub in range(FFN_SUBTILES):
        rs = pl.ds(sub * TILE, TILE)
        chains.append(_attn_ffn_chain(h_ref.at[0, rs], y_ref.at[0, rs], mk_ref, mv_ref, *consts))
    _interleave(chains, offset=FFN_OFFSET)


def _attn_ffn_prompt(h, mkb, mvb, p):
    b, t, d = h.shape
    rows = FFN_SUBTILES * TILE
    consts = [p["ln_xa_pre"], p["w_xq"], p["w_xo"], p["ln_xa_post"], p["ln_ffn_pre"], p["ln_ffn_post"],
              p["w_gate"], p["w_up"], p["w_down"]]
    tok = pl.BlockSpec((1, rows, d), lambda i, j: (i, j, 0))
    mem = pl.BlockSpec((1, mkb.shape[1], d), lambda i, j: (i, 0, 0))
    return pl.pallas_call(
        _attn_ffn_prompt_kernel,
        grid=(b, t // rows),
        in_specs=[tok, mem, mem] + [_const_spec(c.shape) for c in consts],
        out_specs=tok,
        out_shape=jax.ShapeDtypeStruct((b, t, d), F32),
        compiler_params=pltpu.CompilerParams(dimension_semantics=("arbitrary", "arbitrary"),
                                             vmem_limit_bytes=VMEM_LIMIT),
        name="attn_ffn_prompt",
    )(h, mkb, mvb, *consts)


def _proj_sample_kernel(x_ref, lnpre_ref, win_ref, proj_ref):
    proj_ref[...] = _dot(_bf(_rms(x_ref[...], lnpre_ref[...])), win_ref[...])


def _proj_sample(x, p):
    n, d = x.shape
    return pl.pallas_call(
        _proj_sample_kernel,
        grid=(1,),
        in_specs=[_const_spec((n, d)), _const_spec((1, d)), _const_spec(p["w_in"].shape)],
        out_specs=pl.BlockSpec((n, PROJ_COLS), lambda i: (0, 0)),
        out_shape=jax.ShapeDtypeStruct((n, PROJ_COLS), F32),
        compiler_params=pltpu.CompilerParams(dimension_semantics=("arbitrary",),
                                             vmem_limit_bytes=VMEM_LIMIT),
        name="proj_sample",
    )(x, p["ln_mix_pre"], p["w_in"])


def _mixer_sample_kernel(seq_len, proj_ref, tail_ref, sg_ref, ss_ref, gw2_ref, gb_ref, gnw_ref,
                         cw_ref, cb_ref, dtb_ref, a_ref, d_ref, snw_ref,
                         mixed_ref, sg_out, ss_out, xpad_s, tpad_s):
    rows = proj_ref.shape[0]
    n_seq = rows // seq_len
    small = proj_ref[:, COL_SMALL:COL_SMALL + 128]
    logf = _gate_logf(small, gw2_ref, gb_ref)

    xbc = proj_ref[:, COL_XBC:COL_XBC + SSD_CONV_CH]
    xpad_s[0:8, :] = jnp.zeros((8, SSD_CONV_CH), F32)
    xpad_s[8:8 + rows, :] = xbc
    tpad_s[0:rows, :] = tail_ref[...]
    tpad_s[rows:rows + 8, :] = jnp.zeros((8, SSD_CONV_CH), F32)
    pos = lax.broadcasted_iota(jnp.int32, (rows, SSD_CONV_CH), 0) & (seq_len - 1)
    conv = cb_ref[...]
    for j in range(SSD_CONV - 1):
        back = SSD_CONV - 1 - j
        prev = jnp.where(pos >= back, xpad_s[8 - back:8 - back + rows, :],
                         tpad_s[seq_len - back:seq_len - back + rows, :])
        conv = conv + prev * cw_ref[j:j + 1, :]
    conv = conv + xbc * cw_ref[SSD_CONV - 1:SSD_CONV, :]
    xa = _silu(conv)

    def get_ss(s, j):
        return ss_ref[s, 2 * j:2 * j + 2].reshape(2 * SSD_HEAD_DIM, SSD_STATE)

    def put_ss(s, j, val):
        ss_out[s, 2 * j:2 * j + 2] = val.reshape(2, SSD_HEAD_DIM, SSD_STATE)

    ys = []
    _run(_ssd_phases(xa[:, :SSD_INNER], xa[:, SSD_INNER:SSD_INNER + 256], xa[:, SSD_INNER + 256:],
                     small, dtb_ref[...], a_ref[...], d_ref[...], get_ss, put_ss, seq_len, ys))
    mixed_ref[:, GLA_VW:] = _ssd_finish(jnp.concatenate(ys, axis=1), proj_ref[:, COL_Z:COL_Z + SSD_INNER],
                                        snw_ref[...])

    def get_sg(s):
        return sg_ref[s].reshape(GLA_KW, GLA_DV)

    def put_sg(s, val):
        sg_out[s] = val.reshape(GLA_HEADS, GLA_DK, GLA_DV)

    outs = []
    _run(_gla_phases(proj_ref[:, COL_Q:COL_Q + GLA_KW], proj_ref[:, COL_K:COL_K + GLA_KW],
                     proj_ref[:, COL_V:COL_V + GLA_VW], logf, get_sg, put_sg, seq_len, outs))
    for h in range(GLA_HEADS):
        g_h = proj_ref[:, COL_G + h * GLA_DV:COL_G + (h + 1) * GLA_DV]
        mixed_ref[:, h * GLA_DV:(h + 1) * GLA_DV] = _gla_finish(outs[h], g_h, gnw_ref[...])


def _mixer_sample(proj, tail, sg, ss, p, seq_len):
    n = proj.shape[0]
    rows = SAMPLE_SEQS * seq_len
    nb = sg.shape[0]
    consts = [p["gw2"], p["gb"], p["gnw"], p["conv_w"], p["conv_b"], p["dtb"], p["a_blk"], p["d_row"],
              p["snw"]]
    sg_spec = pl.BlockSpec((SAMPLE_SEQS, GLA_HEADS, GLA_DK, GLA_DV), lambda i: (i, 0, 0, 0))
    ss_spec = pl.BlockSpec((SAMPLE_SEQS, SSD_HEADS, SSD_HEAD_DIM, SSD_STATE), lambda i: (i, 0, 0, 0))
    return pl.pallas_call(
        functools.partial(_mixer_sample_kernel, seq_len),
        grid=(nb // SAMPLE_SEQS,),
        in_specs=[pl.BlockSpec((rows, PROJ_COLS), lambda i: (i, 0)),
                  pl.BlockSpec((rows, SSD_CONV_CH), lambda i: (i, 0)),
                  sg_spec, ss_spec] + [_const_spec(c.shape) for c in consts],
        out_specs=[pl.BlockSpec((rows, D_MODEL), lambda i: (i, 0)), sg_spec, ss_spec],
        out_shape=[jax.ShapeDtypeStruct((n, D_MODEL), F32),
                   jax.ShapeDtypeStruct(sg.shape, F32), jax.ShapeDtypeStruct(ss.shape, F32)],
        scratch_shapes=[pltpu.VMEM((rows + 8, SSD_CONV_CH), F32),
                        pltpu.VMEM((rows + 8, SSD_CONV_CH), F32)],
        compiler_params=pltpu.CompilerParams(dimension_semantics=("arbitrary",),
                                             vmem_limit_bytes=VMEM_LIMIT),
        name="mixer_sample",
    )(proj, tail, sg, ss, *consts)


def _post_mix_sample_kernel(x_ref, mixed_ref, wout_ref, lnpost_ref, lnxa_ref, wq_ref, h_ref, q_ref):
    m = _dot(_bf(mixed_ref[...]), wout_ref[...])
    h = x_ref[...] + _rms(m, lnpost_ref[...])
    h_ref[...] = h
    q_ref[...] = _bf(_dot(_bf(_rms(h, lnxa_ref[...])), wq_ref[...]))


def _post_mix_sample(x, mixed, p):
    n, d = x.shape
    full = pl.BlockSpec((n, d), lambda i: (0, 0))
    consts = [p["w_out"], p["ln_mix_post"], p["ln_xa_pre"], p["w_xq"]]
    return pl.pallas_call(
        _post_mix_sample_kernel,
        grid=(1,),
        in_specs=[full, full] + [_const_spec(c.shape) for c in consts],
        out_specs=[full, full],
        out_shape=[jax.ShapeDtypeStruct((n, d), F32), jax.ShapeDtypeStruct((n, d), BF16)],
        compiler_params=pltpu.CompilerParams(dimension_semantics=("arbitrary",),
                                             vmem_limit_bytes=VMEM_LIMIT),
        name="post_mix_sample",
    )(x, mixed, *consts)


def _attn_sample_kernel(seq_len, q_ref, k_hbm, v_hbm, a_ref, kbuf, vbuf, sem):
    step = pl.program_id(0)
    rows = q_ref.shape[0]
    n_seq = rows // seq_len

    def copies(at_step, slot):
        out = []
        for s in range(n_seq):
            for hd in range(XA_HEADS):
                b = at_step * n_seq + s
                out.append(pltpu.make_async_copy(k_hbm.at[0, b, :, hd, :], kbuf.at[slot, s, hd], sem.at[0, slot]))
                out.append(pltpu.make_async_copy(v_hbm.at[0, b, :, hd, :], vbuf.at[slot, s, hd], sem.at[1, slot]))
        return out

    @pl.when(step == 0)
    def _():
        for cp in copies(0, 0):
            cp.start()

    @pl.when(step + 1 < pl.num_programs(0))
    def _():
        for cp in copies(step + 1, (step + 1) % 2):
            cp.start()

    slot = step % 2
    for cp in copies(step, slot):
        cp.wait()

    q = q_ref[...]
    results = {}

    def one(s, hd):
        sc = _dot_nt(q[:, hd * XA_HEAD_DIM:(hd + 1) * XA_HEAD_DIM], _bf(kbuf[slot, s, hd]))
        yield
        e = jnp.exp(sc - jnp.max(sc, axis=-1, keepdims=True))
        pr = _bf(e / jnp.sum(e, axis=-1, keepdims=True))
        yield
        results[s, hd] = _dot(pr, _bf(vbuf[slot, s, hd]))

    _interleave([one(s, hd) for s in range(n_seq) for hd in range(XA_HEADS)])
    for hd in range(XA_HEADS):
        sl = slice(hd * XA_HEAD_DIM, (hd + 1) * XA_HEAD_DIM)
        out = results[0, hd]
        for s in range(1, n_seq):
            out = jnp.where(_row_in_seq(rows, seq_len, s, XA_HEAD_DIM), results[s, hd], out)
        a_ref[:, sl] = out


def _attn_sample(q, ck, cv, seq_len):
    n, d = q.shape
    _, nb, m, nh, hd = ck.shape
    rows = ATTN_SEQS * seq_len
    tok = pl.BlockSpec((rows, d), lambda i: (i, 0))
    hbm = pl.BlockSpec(memory_space=pl.ANY)
    return pl.pallas_call(
        functools.partial(_attn_sample_kernel, seq_len),
        grid=(nb // ATTN_SEQS,),
        in_specs=[tok, hbm, hbm],
        out_specs=tok,
        out_shape=jax.ShapeDtypeStruct((n, d), F32),
        scratch_shapes=[pltpu.VMEM((2, ATTN_SEQS, nh, m, hd), F32),
                        pltpu.VMEM((2, ATTN_SEQS, nh, m, hd), F32),
                        pltpu.SemaphoreType.DMA((2, 2))],
        compiler_params=pltpu.CompilerParams(dimension_semantics=("arbitrary",),
                                             vmem_limit_bytes=VMEM_LIMIT),
        name="attn_sample",
    )(q, ck, cv)


def _post_attn_sample_kernel(h_ref, a_ref, *rest):
    consts, y_ref = rest[:-1], rest[-1]
    chains = []
    for sub in range(h_ref.shape[0] // TILE):
        rs = pl.ds(sub * TILE, TILE)
        chains.append(_wo_ffn_phases(h_ref[rs, :], _bf(a_ref[rs, :]), y_ref.at[rs], *consts))
    _interleave(chains, offset=FFN_OFFSET)


def _post_attn_sample(h, a, p):
    n, d = h.shape
    full = pl.BlockSpec((n, d), lambda i: (0, 0))
    consts = [p["w_xo"], p["ln_xa_post"], p["ln_ffn_pre"], p["ln_ffn_post"], p["w_gate"], p["w_up"],
              p["w_down"]]
    return pl.pallas_call(
        _post_attn_sample_kernel,
        grid=(1,),
        in_specs=[full, full] + [_const_spec(c.shape) for c in consts],
        out_specs=full,
        out_shape=jax.ShapeDtypeStruct((n, d), F32),
        compiler_params=pltpu.CompilerParams(dimension_semantics=("arbitrary",),
                                             vmem_limit_bytes=VMEM_LIMIT),
        name="post_attn_sample",
    )(h, a, *consts)


SRC_GLR = GLA_KW + GLA_KW + GLA_VW + GLA_VW
SRC_Z = SRC_GLR + GLA_GATE_RANK
SRC_XBC = SRC_Z + SSD_INNER
SRC_DT = SRC_XBC + SSD_CONV_CH
IN_COLS = SRC_DT + SSD_HEADS


def _pack_w_in_kernel(wt_hbm, o_ref, buf, sem):
    pieces = [(0, COL_Q, SRC_GLR),
              (SRC_Z, COL_Z, SSD_INNER),
              (SRC_XBC, COL_XBC, SSD_CONV_CH),
              (SRC_GLR, COL_SMALL, GLA_GATE_RANK),
              (SRC_DT, COL_SMALL + DT_LANE, SSD_HEADS)]
    copies = [pltpu.make_async_copy(wt_hbm.at[0, pl.ds(src, n), :], buf.at[pl.ds(dst, n), :], sem.at[0])
              for src, dst, n in pieces]
    for cp in copies:
        cp.start()
    used = COL_SMALL + DT_LANE + SSD_HEADS
    buf[used:PROJ_COLS, :] = jnp.zeros((PROJ_COLS - used, D_MODEL), F32)
    for cp in copies:
        cp.wait()
    for j in range(PROJ_COLS // 128):
        blk = buf[128 * j:128 * (j + 1), :]
        if 128 * j < COL_K:
            blk = blk * (GLA_DK ** -0.5)
        o_ref[:, 128 * j:128 * (j + 1)] = blk.T.astype(BF16)


def _pack_w_in(w_in):
    wt = jnp.swapaxes(w_in, 1, 2)
    return pl.pallas_call(
        _pack_w_in_kernel,
        grid=(1,),
        in_specs=[pl.BlockSpec(memory_space=pl.ANY)],
        out_specs=pl.BlockSpec((D_MODEL, PROJ_COLS), lambda i: (0, 0)),
        out_shape=jax.ShapeDtypeStruct((D_MODEL, PROJ_COLS), BF16),
        scratch_shapes=[pltpu.VMEM((PROJ_COLS, D_MODEL), F32), pltpu.SemaphoreType.DMA((1,))],
        compiler_params=pltpu.CompilerParams(dimension_semantics=("arbitrary",),
                                             vmem_limit_bytes=VMEM_LIMIT),
        name="pack_w_in",
    )(wt)


def _pack_params(ln_mix_pre, ln_mix_post, w_in_packed, gla_gate_w2, gla_gate_b, gla_norm_w, ssd_conv_w,
                 ssd_conv_b, ssd_dt_bias, ssd_A_log, ssd_D, ssd_norm_w, w_out, ln_xa_pre, ln_xa_post,
                 mem_norm_w, w_xq, w_xk, w_xv, w_xo, ln_ffn_pre, ln_ffn_post, w_gate, w_up, w_down):

    def small_row(v):
        return jnp.zeros((1, 128), F32).at[0, DT_LANE:DT_LANE + SSD_HEADS].set(v)

    row = lambda v: v.reshape(1, -1)
    return {
        "ln_mix_pre": row(ln_mix_pre), "ln_mix_post": row(ln_mix_post),
        "w_in": w_in_packed,
        "gw2": _bf(jnp.zeros((128, GLA_KW), F32).at[:GLA_GATE_RANK].set(gla_gate_w2)),
        "gb": row(gla_gate_b), "gnw": row(gla_norm_w),
        "conv_w": ssd_conv_w, "conv_b": row(ssd_conv_b),
        "dtb": small_row(ssd_dt_bias),
        "a_blk": small_row(-jnp.exp(ssd_A_log)),
        "d_row": jnp.repeat(ssd_D, SSD_HEAD_DIM).reshape(1, SSD_INNER),
        "snw": row(ssd_norm_w),
        "w_out": _bf(w_out),
        "ln_xa_pre": row(ln_xa_pre), "ln_xa_post": row(ln_xa_post), "mem_norm_w": row(mem_norm_w),
        "w_xq": _bf(w_xq * (XA_HEAD_DIM ** -0.5)), "w_xk": _bf(w_xk), "w_xv": _bf(w_xv), "w_xo": _bf(w_xo),
        "ln_ffn_pre": row(ln_ffn_pre), "ln_ffn_post": row(ln_ffn_post),
        "w_gate": _bf(w_gate), "w_up": _bf(w_up), "w_down": _bf(w_down),
    }


def kernel(x_prompt, x_sample, mem_prompt, state_gla, state_ssm, state_conv, cache_mem_k, cache_mem_v,
           ln_mix_pre, ln_mix_post, w_in, gla_gate_w2, gla_gate_b, gla_norm_w, ssd_conv_w, ssd_conv_b,
           ssd_dt_bias, ssd_A_log, ssd_D, ssd_norm_w, w_out, ln_xa_pre, ln_xa_post, mem_norm_w,
           w_xq, w_xk, w_xv, w_xo, ln_ffn_pre, ln_ffn_post, w_gate, w_up, w_down):
    assert w_in.shape[0] == 1, "single-layer kernel"
    layer = [a[0] for a in (ln_mix_pre, ln_mix_post, w_in, gla_gate_w2, gla_gate_b, gla_norm_w, ssd_conv_w,
                            ssd_conv_b, ssd_dt_bias, ssd_A_log, ssd_D, ssd_norm_w, w_out, ln_xa_pre,
                            ln_xa_post, mem_norm_w, w_xq, w_xk, w_xv, w_xo, ln_ffn_pre, ln_ffn_post,
                            w_gate, w_up, w_down)]
    layer[2] = _pack_w_in(w_in)
    p = _pack_params(*layer)
    bp, mem_len, _ = mem_prompt.shape
    bs, ts, _ = x_sample.shape
    assert ts == SSD_CONV, "the padded conv-tail layout assumes one new row per conv tap"

    mk, mv, mkb, mvb = _memkv(mem_prompt, p["mem_norm_w"], p["w_xk"], p["w_xv"])
    h_p, gla_p, ssm_p, conv_p = _mixer_prompt(x_prompt, p)
    y_p = _attn_ffn_prompt(h_p, mkb, mvb, p)

    xs = x_sample.reshape(bs * ts, D_MODEL)
    proj = _proj_sample(xs, p)
    tail = jnp.pad(state_conv[0], ((0, 0), (1, 0), (0, 0))).reshape(bs * ts, SSD_CONV_CH)
    mixed, gla_s, ssm_s = _mixer_sample(proj, tail, state_gla[0], state_ssm[0], p, ts)
    h_s, q_s = _post_mix_sample(xs, mixed, p)
    a_s = _attn_sample(q_s, cache_mem_k, cache_mem_v, ts)
    y_s = _post_attn_sample(h_s, a_s, p)
    conv_s = proj[:, COL_XBC:COL_XBC + SSD_CONV_CH].reshape(bs, ts, SSD_CONV_CH)[:, ts - (SSD_CONV - 1):]

    return (y_p, y_s.reshape(bs, ts, D_MODEL), gla_p[None], ssm_p[None], conv_p[None],
            mk, mv, gla_s[None], ssm_s[None], conv_s[None])
```

```python
import functools

import jax
import jax.numpy as jnp
from jax import lax
from jax.experimental import pallas as pl
from jax.experimental.pallas import tpu as pltpu

F32 = jnp.float32
BF16 = jnp.bfloat16

D_MODEL = 1024
GLA_HEADS = 4
GLA_DK = 64
GLA_DV = 128
GLA_KW = GLA_HEADS * GLA_DK
GLA_VW = GLA_HEADS * GLA_DV
GLA_GATE_RANK = 16
GLA_GATE_TAU = 16.0
SSD_INNER = 512
SSD_HEAD_DIM = 64
SSD_HEADS = 8
SSD_GROUPS = 2
SSD_STATE = 128
SSD_CONV = 4
SSD_CONV_CH = 1024
XA_HEADS = 4
XA_HEAD_DIM = 256
D_FF = 2816
EPS = 1e-6

COL_Q = 0
COL_K = 256
COL_V = 512
COL_G = 1024
COL_Z = 1536
COL_XBC = 2048
COL_SMALL = 3072
DT_LANE = GLA_GATE_RANK
PROJ_COLS = 3200

GLA_CHUNK = 64
TILE = 256
FFN_SUBTILES = 4
MIX_SEQS = 4
SAMPLE_SEQS = 16
ATTN_SEQS = 4
VMEM_LIMIT = 56 * 1024 * 1024


def _bf(x):
    return x.astype(BF16)


def _dot(a, b):
    return jnp.dot(a, b, preferred_element_type=F32)


def _dot_nt(a, b):
    return lax.dot_general(a, b, (((1,), (1,)), ((), ())), preferred_element_type=F32)


def _dot_tn(a, b):
    return lax.dot_general(a, b, (((0,), (0,)), ((), ())), preferred_element_type=F32)


def _dot3(m, a):
    hi = _bf(a)
    r = a - hi.astype(F32)
    mid = _bf(r)
    lo = _bf(r - mid.astype(F32))
    return _dot(m, hi) + _dot(m, mid) + _dot(m, lo)


def _rms(x, w):
    ms = jnp.mean(x * x, axis=-1, keepdims=True)
    return x * lax.rsqrt(ms + EPS) * w


def _silu(x):
    return x / (1.0 + jnp.exp(-x))


def _softplus(x):
    e = jnp.exp(-jnp.abs(x))
    u = 1.0 + e
    log1p_e = jnp.where(u == 1.0, e, jnp.log(u) * (e / (u - 1.0)))
    return jnp.maximum(x, 0.0) + log1p_e


def _log_sigmoid(x):
    return jnp.minimum(x, 0.0) - jnp.log(1.0 + jnp.exp(-jnp.abs(x)))


def _seg_masks(rows, seq_len):
    t = lax.broadcasted_iota(jnp.int32, (rows, rows), 0)
    u = lax.broadcasted_iota(jnp.int32, (rows, rows), 1)
    if seq_len == rows:
        same = None
        tril = u <= t
    else:
        shift = seq_len.bit_length() - 1
        assert 1 << shift == seq_len
        same = (t >> shift) == (u >> shift)
        tril = jnp.logical_and(same, u <= t)
    return same, tril


def _mask_to_bf16(mask, rows):
    if mask is None:
        return jnp.ones((rows, rows), BF16)
    return jnp.where(mask, 1.0, 0.0).astype(BF16)


def _row_in_seq(rows, seq_len, s, width):
    r = lax.broadcasted_iota(jnp.int32, (rows, width), 0)
    return jnp.logical_and(r >= s * seq_len, r < (s + 1) * seq_len)


def _interleave(chains, offset=0):
    waiting = list(enumerate(chains))
    active = []
    rnd = 0
    while waiting or active:
        while waiting and waiting[0][0] * offset <= rnd:
            active.append(waiting.pop(0)[1])
        alive = []
        for c in active:
            try:
                next(c)
                alive.append(c)
            except StopIteration:
                pass
        active = alive
        rnd += 1


def _run(phases):
    for _ in phases:
        pass


def _pad_rows(x, rows):
    if x.shape[0] >= rows:
        return x
    return jnp.concatenate([x, jnp.zeros((rows - x.shape[0], x.shape[1]), x.dtype)], axis=0)


def _gla_phases(q, k, v, lf, get_state, put_state, seq_len, outs):
    rows = q.shape[0]
    n_seq = rows // seq_len
    same, tril = _seg_masks(rows, seq_len)
    tril_b = _mask_to_bf16(tril, rows)
    same_b = _mask_to_bf16(same, rows)

    b = _dot3(tril_b, lf)
    if n_seq == 1:
        bl = jnp.broadcast_to(b[rows - 1:rows, :], b.shape)
    else:
        bl = _dot3(same_b, lf)
    qt = q * jnp.exp(b)
    kt = k * jnp.exp(-b)
    kd = k * jnp.exp(bl - b)
    bl_t = _pad_rows(bl, 128).T
    yield

    lane = lax.broadcasted_iota(jnp.int32, (rows, GLA_KW), 1)
    lhs = jnp.concatenate(
        [jnp.where((lane >= h * GLA_DK) & (lane < (h + 1) * GLA_DK), qt, 0.0)
         for h in range(GLA_HEADS)], axis=0).astype(BF16)
    att = _dot_nt(lhs, kt.astype(BF16))
    t4 = lax.broadcasted_iota(jnp.int32, (GLA_HEADS * rows, rows), 0) & (rows - 1)
    u4 = lax.broadcasted_iota(jnp.int32, (GLA_HEADS * rows, rows), 1)
    causal4 = u4 <= t4
    if n_seq > 1:
        shift = seq_len.bit_length() - 1
        causal4 = jnp.logical_and(causal4, (t4 >> shift) == (u4 >> shift))
    att = jnp.where(causal4, att, 0.0).astype(BF16)
    vb = v.astype(BF16)
    kdb = kd.astype(BF16)
    yield

    o_inter = None
    r4 = lax.broadcasted_iota(jnp.int32, (GLA_HEADS * rows, GLA_DV), 0) & (rows - 1)
    for s in range(n_seq):
        st = get_state(s)
        oi = _dot(lhs, st.astype(BF16))
        if n_seq == 1:
            o_inter = oi
            kd_s = kdb
        else:
            m4 = jnp.logical_and(r4 >= s * seq_len, r4 < (s + 1) * seq_len)
            oi = jnp.where(m4, oi, 0.0)
            o_inter = oi if o_inter is None else o_inter + oi
            kd_s = jnp.where(_row_in_seq(rows, seq_len, s, GLA_KW), kdb, jnp.zeros_like(kdb))
        blocks = []
        for hp in range(GLA_HEADS // 2):
            upd = _dot_tn(kd_s[:, 2 * hp * GLA_DK:2 * (hp + 1) * GLA_DK],
                          vb[:, 2 * hp * GLA_DV:2 * (hp + 1) * GLA_DV])
            blocks += [upd[:GLA_DK, :GLA_DV], upd[GLA_DK:, GLA_DV:]]
        upd_d = jnp.concatenate(blocks, axis=0)
        c0 = s * seq_len
        decay = jnp.exp(jnp.broadcast_to(bl_t[:, c0:c0 + 1], (GLA_KW, GLA_DV)))
        put_state(s, decay * st + upd_d)
        if n_seq > 1 and s % 4 == 3:
            yield
    yield

    for h in range(GLA_HEADS):
        o_h = _dot(att[h * rows:(h + 1) * rows], vb[:, h * GLA_DV:(h + 1) * GLA_DV])
        outs.append(o_h + o_inter[h * rows:(h + 1) * rows])


def _ssd_phases(xs, bm, cm, dt_raw_blk, dtb_blk, a_blk, d_row, get_state, put_state, seq_len, ys):
    rows = xs.shape[0]
    n_seq = rows // seq_len
    same, tril = _seg_masks(rows, seq_len)
    tril_b = _mask_to_bf16(tril, rows)
    same_b = _mask_to_bf16(same, rows)

    dt = _softplus(dt_raw_blk + dtb_blk)
    dta = dt * a_blk
    lc = _dot3(tril_b, dta)
    if n_seq == 1:
        ll = jnp.broadcast_to(lc[rows - 1:rows, :], lc.shape)
    else:
        ll = _dot3(same_b, dta)
    elc = jnp.exp(lc)
    w = jnp.exp(ll - lc) * dt
    ell = jnp.exp(ll)
    lc_t = _pad_rows(lc, 128).T[:, :rows]
    yield

    bmb = bm.astype(BF16)
    cmb = cm.astype(BF16)
    cb = [_dot_nt(cmb[:, g * SSD_STATE:(g + 1) * SSD_STATE],
                  bmb[:, g * SSD_STATE:(g + 1) * SSD_STATE]) for g in range(SSD_GROUPS)]
    yield

    lane128 = lax.broadcasted_iota(jnp.int32, (rows, 128), 1)
    lo128 = lane128 < SSD_HEAD_DIM
    sub128 = lax.broadcasted_iota(jnp.int32, (128, 128), 0) < SSD_HEAD_DIM

    def col(x, h, width):
        return jnp.broadcast_to(x[:, DT_LANE + h:DT_LANE + h + 1], (x.shape[0], width))

    heads_per_group = SSD_HEADS // SSD_GROUPS
    for j in range(SSD_HEADS // 2):
        g = (2 * j) // heads_per_group
        xs_p = xs[:, 128 * j:128 * (j + 1)]
        dt_p = jnp.where(lo128, col(dt, 2 * j, 128), col(dt, 2 * j + 1, 128))
        xdt = xs_p * dt_p
        x_lo = jnp.where(lo128, xdt, 0.0).astype(BF16)
        x_hi = jnp.where(lo128, 0.0, xdt).astype(BF16)
        y_p = None
        for h, x_half in ((2 * j, x_lo), (2 * j + 1, x_hi)):
            seg = col(lc, h, rows) - jnp.broadcast_to(lc_t[DT_LANE + h:DT_LANE + h + 1, :], (rows, rows))
            w_h = jnp.where(tril, cb[g] * jnp.exp(seg), 0.0).astype(BF16)
            y_h = _dot(w_h, x_half)
            y_p = y_h if y_p is None else y_p + y_h
            yield

        elc_p = jnp.where(lo128, col(elc, 2 * j, 128), col(elc, 2 * j + 1, 128))
        w_p = jnp.where(lo128, col(w, 2 * j, 128), col(w, 2 * j + 1, 128))
        xw = (xs_p * w_p).astype(BF16)
        bm_g = bmb[:, g * SSD_STATE:(g + 1) * SSD_STATE]
        cm_g = cmb[:, g * SSD_STATE:(g + 1) * SSD_STATE]
        y_inter = None
        for s in range(n_seq):
            st = get_state(s, j)
            ci = _dot_nt(cm_g, st.astype(BF16))
            if n_seq == 1:
                y_inter = ci
                xw_s = xw
            else:
                msk = _row_in_seq(rows, seq_len, s, 128)
                ci = jnp.where(msk, ci, 0.0)
                y_inter = ci if y_inter is None else y_inter + ci
                xw_s = jnp.where(msk, xw, jnp.zeros_like(xw))
            upd = _dot_tn(xw_s, bm_g)
            r0 = s * seq_len
            e0 = jnp.broadcast_to(ell[r0:r0 + 1, DT_LANE + 2 * j:DT_LANE + 2 * j + 1], (128, 128))
            e1 = jnp.broadcast_to(ell[r0:r0 + 1, DT_LANE + 2 * j + 1:DT_LANE + 2 * j + 2], (128, 128))
            put_state(s, j, jnp.where(sub128, e0, e1) * st + upd)
        ys.append(y_p + y_inter * elc_p + d_row[:, 128 * j:128 * (j + 1)] * xs_p)
        yield


def _ssd_phases_seq(xs, bm, cm, dt_raw_blk, dtb_blk, a_blk, d_row, get_state, put_state, ys):
    rows = xs.shape[0]
    _, tril = _seg_masks(rows, rows)
    tril_b = _mask_to_bf16(tril, rows)
    src = lax.broadcasted_iota(jnp.int32, (rows, rows), 0)
    dst = lax.broadcasted_iota(jnp.int32, (rows, rows), 1)
    causal_t = src <= dst

    dt = _softplus(dt_raw_blk + dtb_blk)
    lc = _dot3(tril_b, dt * a_blk)
    lc_t = lc.T
    dt_t = dt.T
    elc_t = jnp.exp(lc_t)
    w_t = jnp.exp(lc_t[:, rows - 1:rows] - lc_t) * dt_t
    ell = jnp.exp(lc[rows - 1:rows, :])
    yield

    xs_t = xs.T
    bmb = bm.astype(BF16)
    cmb = cm.astype(BF16)
    cb_t = [_dot_nt(bmb[:, g * SSD_STATE:(g + 1) * SSD_STATE],
                    cmb[:, g * SSD_STATE:(g + 1) * SSD_STATE]) for g in range(SSD_GROUPS)]
    yield

    heads_per_group = SSD_HEADS // SSD_GROUPS
    y_t = []
    for h in range(SSD_HEADS):
        g = h // heads_per_group
        lane = DT_LANE + h
        xs_h = xs_t[h * SSD_HEAD_DIM:(h + 1) * SSD_HEAD_DIM, :]
        seg_t = lc_t[lane:lane + 1, :] - jnp.broadcast_to(lc[:, lane:lane + 1], (rows, rows))
        w_h = jnp.where(causal_t, cb_t[g] * jnp.exp(seg_t), 0.0).astype(BF16)
        y_h = _dot((xs_h * dt_t[lane:lane + 1, :]).astype(BF16), w_h)
        st = get_state(h)
        ci = _dot_nt(st.astype(BF16), cmb[:, g * SSD_STATE:(g + 1) * SSD_STATE])
        y_t.append(y_h + ci * elc_t[lane:lane + 1, :])
        upd = _dot((xs_h * w_t[lane:lane + 1, :]).astype(BF16), bmb[:, g * SSD_STATE:(g + 1) * SSD_STATE])
        put_state(h, jnp.broadcast_to(ell[:, lane:lane + 1], st.shape) * st + upd)
        yield
    ys.append(jnp.concatenate(y_t, axis=0).T + d_row * xs)


def _gate_logf(small_blk, gw2_ref, gb_ref):
    logits = _dot(_bf(small_blk), gw2_ref[...]) + gb_ref[...]
    return _log_sigmoid(logits) * (1.0 / GLA_GATE_TAU)


def _gla_finish(o_h, g_h, gnw):
    return _rms(o_h, gnw) * _silu(g_h)


def _ssd_finish(y, z, snw):
    y = y * _silu(z)
    gs = SSD_INNER // SSD_GROUPS
    parts = [_rms(y[:, g * gs:(g + 1) * gs], snw[:, g * gs:(g + 1) * gs]) for g in range(SSD_GROUPS)]
    return jnp.concatenate(parts, axis=1)


def _memkv_kernel(mem_ref, nw_ref, wk_ref, wv_ref, k_ref, v_ref, kb_ref, vb_ref):
    mn = _bf(_rms(mem_ref[0], nw_ref[...]))
    mk = _dot(mn, wk_ref[...])
    mv = _dot(mn, wv_ref[...])
    for h in range(XA_HEADS):
        k_ref[0, 0, :, h, :] = mk[:, h * XA_HEAD_DIM:(h + 1) * XA_HEAD_DIM]
        v_ref[0, 0, :, h, :] = mv[:, h * XA_HEAD_DIM:(h + 1) * XA_HEAD_DIM]
    kb_ref[0] = _bf(mk)
    vb_ref[0] = _bf(mv)


def _const_spec(shape):
    nd = len(shape)
    return pl.BlockSpec(shape, lambda *_: (0,) * nd, pipeline_mode=pl.Buffered(1))


def _memkv(mem, nw, wk, wv):
    b, m, d = mem.shape
    blk = pl.BlockSpec((1, m, d), lambda i: (i, 0, 0))
    cache_blk = pl.BlockSpec((1, 1, m, XA_HEADS, XA_HEAD_DIM), lambda i: (0, i, 0, 0, 0))
    cache_shape = jax.ShapeDtypeStruct((1, b, m, XA_HEADS, XA_HEAD_DIM), F32)
    return pl.pallas_call(
        _memkv_kernel,
        grid=(b,),
        in_specs=[blk, _const_spec((1, d)), _const_spec((d, d)), _const_spec((d, d))],
        out_specs=[cache_blk, cache_blk, blk, blk],
        out_shape=[cache_shape] * 2 + [jax.ShapeDtypeStruct((b, m, d), BF16)] * 2,
        compiler_params=pltpu.CompilerParams(dimension_semantics=("arbitrary",),
                                             vmem_limit_bytes=VMEM_LIMIT),
        name="mem_kv",
    )(mem, nw, wk, wv)


PROJ_CHUNK = 512


def _mixer_prompt_chain(x_ref, h_ref, proj_s, xpad_s, mixed_s, sg_s, ss_s,
                        lnpre_ref, win_ref, gw2_ref, gb_ref, gnw_ref, cw_ref, cb_ref,
                        dtb_ref, a_ref, d_ref, snw_ref, wout_ref, lnpost_ref):
    rows = TILE
    x = x_ref[...]
    hn = _rms(x, lnpre_ref[...]).astype(BF16)
    yield

    def proj_cols(c0, width):
        proj_s[:, c0:c0 + width] = _dot(hn, win_ref[:, c0:c0 + width])

    def late_proj():
        for c0 in (COL_Q, COL_V, COL_G, COL_Z):
            proj_cols(c0, PROJ_CHUNK)
            yield

    side = [late_proj()]

    def tick():
        if side[0] is not None:
            try:
                next(side[0])
            except StopIteration:
                side[0] = None

    for c in range(SSD_CONV_CH // PROJ_CHUNK):
        proj_cols(COL_XBC + c * PROJ_CHUNK, PROJ_CHUNK)
        yield
    proj_cols(COL_SMALL, 128)

    xa = []
    for c in range(SSD_CONV_CH // PROJ_CHUNK):
        cs = slice(c * PROJ_CHUNK, (c + 1) * PROJ_CHUNK)
        xpad_s[8:8 + rows, cs] = proj_s[:, COL_XBC + c * PROJ_CHUNK:COL_XBC + (c + 1) * PROJ_CHUNK]
        conv = cb_ref[:, cs]
        for j in range(SSD_CONV):
            conv = conv + xpad_s[5 + j:5 + j + rows, cs] * cw_ref[j:j + 1, cs]
        xpad_s[5:8, cs] = xpad_s[5 + rows:8 + rows, cs]
        xa.append(_silu(conv))
        tick()
        yield

    small = proj_s[:, COL_SMALL:COL_SMALL + 128]
    logf = _gate_logf(small, gw2_ref, gb_ref)

    def get_ss(h):
        return ss_s[SSD_HEAD_DIM * h:SSD_HEAD_DIM * (h + 1), :]

    def put_ss(h, val):
        ss_s[SSD_HEAD_DIM * h:SSD_HEAD_DIM * (h + 1), :] = val

    ys = []
    ssd = _ssd_phases_seq(xa[0], xa[1][:, :SSD_GROUPS * SSD_STATE], xa[1][:, SSD_GROUPS * SSD_STATE:],
                          small, dtb_ref[...], a_ref[...], d_ref[...], get_ss, put_ss, ys)

    def get_sg(s):
        return sg_s[...]

    def put_sg(s, val):
        sg_s[...] = val

    def gla_chunks():
        for c in range(rows // GLA_CHUNK):
            r0 = c * GLA_CHUNK
            sl = slice(r0, r0 + GLA_CHUNK)
            outs = []
            yield from _gla_phases(proj_s[sl, COL_Q:COL_Q + GLA_KW], proj_s[sl, COL_K:COL_K + GLA_KW],
                                   proj_s[sl, COL_V:COL_V + GLA_VW], logf[sl], get_sg, put_sg, GLA_CHUNK, outs)
            for h in range(GLA_HEADS):
                g_h = proj_s[sl, COL_G + h * GLA_DV:COL_G + (h + 1) * GLA_DV]
                mixed_s[sl, h * GLA_DV:(h + 1) * GLA_DV] = _gla_finish(outs[h], g_h, gnw_ref[...]).astype(BF16)
            yield

    while side[0] is not None:
        next(ssd)
        tick()
        yield
    alive = [ssd, gla_chunks()]
    while alive:
        for phases in list(alive):
            try:
                next(phases)
            except StopIteration:
                alive.remove(phases)
                continue
            yield
    mixed_s[:, GLA_VW:] = _ssd_finish(jnp.concatenate(ys, axis=1), proj_s[:, COL_Z:COL_Z + SSD_INNER],
                                      snw_ref[...]).astype(BF16)
    yield

    mixed = mixed_s[...]
    ms = []
    for c in range(D_MODEL // PROJ_CHUNK):
        ms.append(_dot(mixed, wout_ref[:, c * PROJ_CHUNK:(c + 1) * PROJ_CHUNK]))
        yield
    h_ref[...] = x + _rms(jnp.concatenate(ms, axis=1), lnpost_ref[...])


N_MIXER_WEIGHTS = 13
N_MIXER_SCRATCH = 5
MIX_OFFSET = 3


def _mixer_prompt_kernel(x_ref, *rest):
    consts = rest[:N_MIXER_WEIGHTS]
    h_ref, sg_out, ss_out, conv_out = rest[N_MIXER_WEIGHTS:N_MIXER_WEIGHTS + 4]
    scratch = rest[N_MIXER_WEIGHTS + 4:]
    per_seq = [scratch[i * N_MIXER_SCRATCH:(i + 1) * N_MIXER_SCRATCH] for i in range(MIX_SEQS)]
    t = pl.program_id(1)

    @pl.when(t == 0)
    def _():
        for _, xpad_s, _, sg_s, ss_s in per_seq:
            sg_s[...] = jnp.zeros_like(sg_s)
            ss_s[...] = jnp.zeros_like(ss_s)
            xpad_s[0:8, :] = jnp.zeros((8, SSD_CONV_CH), F32)

    _interleave([_mixer_prompt_chain(x_ref.at[i], h_ref.at[i], *per_seq[i], *consts) for i in range(MIX_SEQS)],
                offset=MIX_OFFSET)

    @pl.when(t == pl.num_programs(1) - 1)
    def _():
        for i, (_, xpad_s, _, sg_s, ss_s) in enumerate(per_seq):
            sg_out[i] = sg_s[...].reshape(GLA_HEADS, GLA_DK, GLA_DV)
            ss_out[i] = ss_s[...].reshape(SSD_HEADS, SSD_HEAD_DIM, SSD_STATE)
            conv_out[i] = xpad_s[5:8, :]


def _mixer_prompt(x, p):
    b, t, d = x.shape
    nt = t // TILE
    consts = [p["ln_mix_pre"], p["w_in"], p["gw2"], p["gb"], p["gnw"], p["conv_w"], p["conv_b"],
              p["dtb"], p["a_blk"], p["d_row"], p["snw"], p["w_out"], p["ln_mix_post"]]
    assert len(consts) == N_MIXER_WEIGHTS
    tok = pl.BlockSpec((MIX_SEQS, TILE, d), lambda i, j: (i, j, 0))
    return pl.pallas_call(
        _mixer_prompt_kernel,
        grid=(b // MIX_SEQS, nt),
        in_specs=[tok] + [_const_spec(c.shape) for c in consts],
        out_specs=[tok,
                   pl.BlockSpec((MIX_SEQS, GLA_HEADS, GLA_DK, GLA_DV), lambda i, j: (i, 0, 0, 0)),
                   pl.BlockSpec((MIX_SEQS, SSD_HEADS, SSD_HEAD_DIM, SSD_STATE), lambda i, j: (i, 0, 0, 0)),
                   pl.BlockSpec((MIX_SEQS, SSD_CONV - 1, SSD_CONV_CH), lambda i, j: (i, 0, 0))],
        out_shape=[jax.ShapeDtypeStruct((b, t, d), F32),
                   jax.ShapeDtypeStruct((b, GLA_HEADS, GLA_DK, GLA_DV), F32),
                   jax.ShapeDtypeStruct((b, SSD_HEADS, SSD_HEAD_DIM, SSD_STATE), F32),
                   jax.ShapeDtypeStruct((b, SSD_CONV - 1, SSD_CONV_CH), F32)],
        scratch_shapes=[pltpu.VMEM((TILE, PROJ_COLS), F32),
                        pltpu.VMEM((TILE + 8, SSD_CONV_CH), F32),
                        pltpu.VMEM((TILE, D_MODEL), BF16),
                        pltpu.VMEM((GLA_KW, GLA_DV), F32),
                        pltpu.VMEM((SSD_INNER, SSD_STATE), F32)] * MIX_SEQS,
        compiler_params=pltpu.CompilerParams(dimension_semantics=("arbitrary", "arbitrary"),
                                             vmem_limit_bytes=VMEM_LIMIT),
        name="mixer_prompt",
    )(x, *consts)


def _softmax_rows(s):
    e = jnp.exp(s - jnp.max(s, axis=-1, keepdims=True))
    return e / jnp.sum(e, axis=-1, keepdims=True)


FFN_CHUNK = 256
FFN_OFFSET = 1


def _attn_ffn_chain(h_ref, y_ref, mk_ref, mv_ref, lnxa_ref, wq_ref, wo_ref, lnxap_ref,
                    lnf_ref, lnfp_ref, wg_ref, wu_ref, wd_ref):
    h = h_ref[...]
    hn = _rms(h, lnxa_ref[...]).astype(BF16)
    yield
    heads_per_chunk = PROJ_CHUNK // XA_HEAD_DIM
    outs = []
    for c in range(D_MODEL // PROJ_CHUNK):
        q = _dot(hn, wq_ref[:, c * PROJ_CHUNK:(c + 1) * PROJ_CHUNK]).astype(BF16)
        yield
        for hh in range(heads_per_chunk):
            hd = c * heads_per_chunk + hh
            sl = slice(hd * XA_HEAD_DIM, (hd + 1) * XA_HEAD_DIM)
            s = _dot_nt(q[:, hh * XA_HEAD_DIM:(hh + 1) * XA_HEAD_DIM], mk_ref[0, :, sl])
            yield
            pr = _softmax_rows(s).astype(BF16)
            yield
            outs.append(_dot(pr, mv_ref[0, :, sl]).astype(BF16))
    a = jnp.concatenate(outs, axis=1)
    yield
    yield from _wo_ffn_phases(h, a, y_ref, wo_ref, lnxap_ref, lnf_ref, lnfp_ref, wg_ref, wu_ref, wd_ref)


def _wo_ffn_phases(h, a, y_ref, wo_ref, lnxap_ref, lnf_ref, lnfp_ref, wg_ref, wu_ref, wd_ref):
    parts = []
    for c in range(D_MODEL // PROJ_CHUNK):
        parts.append(_dot(a, wo_ref[:, c * PROJ_CHUNK:(c + 1) * PROJ_CHUNK]))
        yield
    h = h + _rms(jnp.concatenate(parts, axis=1), lnxap_ref[...])
    hf = _rms(h, lnf_ref[...]).astype(BF16)
    yield
    acts = []
    for c in range(D_FF // FFN_CHUNK):
        cs = slice(c * FFN_CHUNK, (c + 1) * FFN_CHUNK)
        gt = _dot(hf, wg_ref[:, cs])
        up = _dot(hf, wu_ref[:, cs])
        yield
        acts.append((_silu(gt) * up).astype(BF16))
    act = jnp.concatenate(acts, axis=1)
    yield
    parts = []
    for c in range(D_MODEL // PROJ_CHUNK):
        parts.append(_dot(act, wd_ref[:, c * PROJ_CHUNK:(c + 1) * PROJ_CHUNK]))
        yield
    y_ref[...] = h + _rms(jnp.concatenate(parts, axis=1), lnfp_ref[...])


def _attn_ffn_prompt_kernel(h_ref, mk_ref, mv_ref, *rest):
    consts, y_ref = rest[:-1], rest[-1]
    chains = []
    for sub in range(FFN_SUBTILES):
        rs = pl.ds(sub * TILE, TILE)
        chains.append(_attn_ffn_chain(h_ref.at[0, rs], y_ref.at[0, rs], mk_ref, mv_ref, *consts))
    _interleave(chains, offset=FFN_OFFSET)


def _attn_ffn_prompt(h, mkb, mvb, p):
    b, t, d = h.shape
    rows = FFN_SUBTILES * TILE
    consts = [p["ln_xa_pre"], p["w_xq"], p["w_xo"], p["ln_xa_post"], p["ln_ffn_pre"], p["ln_ffn_post"],
              p["w_gate"], p["w_up"], p["w_down"]]
    tok = pl.BlockSpec((1, rows, d), lambda i, j: (i, j, 0))
    mem = pl.BlockSpec((1, mkb.shape[1], d), lambda i, j: (i, 0, 0))
    return pl.pallas_call(
        _attn_ffn_prompt_kernel,
        grid=(b, t // rows),
        in_specs=[tok, mem, mem] + [_const_spec(c.shape) for c in consts],
        out_specs=tok,
        out_shape=jax.ShapeDtypeStruct((b, t, d), F32),
        compiler_params=pltpu.CompilerParams(dimension_semantics=("arbitrary", "arbitrary"),
                                             vmem_limit_bytes=VMEM_LIMIT),
        name="attn_ffn_prompt",
    )(h, mkb, mvb, *consts)


def _proj_sample_kernel(x_ref, lnpre_ref, win_ref, proj_ref):
    proj_ref[...] = _dot(_bf(_rms(x_ref[...], lnpre_ref[...])), win_ref[...])


def _proj_sample(x, p):
    n, d = x.shape
    return pl.pallas_call(
        _proj_sample_kernel,
        grid=(1,),
        in_specs=[_const_spec((n, d)), _const_spec((1, d)), _const_spec(p["w_in"].shape)],
        out_specs=pl.BlockSpec((n, PROJ_COLS), lambda i: (0, 0)),
        out_shape=jax.ShapeDtypeStruct((n, PROJ_COLS), F32),
        compiler_params=pltpu.CompilerParams(dimension_semantics=("arbitrary",),
                                             vmem_limit_bytes=VMEM_LIMIT),
        name="proj_sample",
    )(x, p["ln_mix_pre"], p["w_in"])


def _mixer_sample_kernel(seq_len, proj_ref, tail_ref, sg_ref, ss_ref, gw2_ref, gb_ref, gnw_ref,
                         cw_ref, cb_ref, dtb_ref, a_ref, d_ref, snw_ref,
                         mixed_ref, sg_out, ss_out, xpad_s, tpad_s):
    rows = proj_ref.shape[0]
    n_seq = rows // seq_len
    small = proj_ref[:, COL_SMALL:COL_SMALL + 128]
    logf = _gate_logf(small, gw2_ref, gb_ref)

    xbc = proj_ref[:, COL_XBC:COL_XBC + SSD_CONV_CH]
    xpad_s[0:8, :] = jnp.zeros((8, SSD_CONV_CH), F32)
    xpad_s[8:8 + rows, :] = xbc
    tpad_s[0:rows, :] = tail_ref[...]
    tpad_s[rows:rows + 8, :] = jnp.zeros((8, SSD_CONV_CH), F32)
    pos = lax.broadcasted_iota(jnp.int32, (rows, SSD_CONV_CH), 0) & (seq_len - 1)
    conv = cb_ref[...]
    for j in range(SSD_CONV - 1):
        back = SSD_CONV - 1 - j
        prev = jnp.where(pos >= back, xpad_s[8 - back:8 - back + rows, :],
                         tpad_s[seq_len - back:seq_len - back + rows, :])
        conv = conv + prev * cw_ref[j:j + 1, :]
    conv = conv + xbc * cw_ref[SSD_CONV - 1:SSD_CONV, :]
    xa = _silu(conv)

    def get_ss(s, j):
        return ss_ref[s, 2 * j:2 * j + 2].reshape(2 * SSD_HEAD_DIM, SSD_STATE)

    def put_ss(s, j, val):
        ss_out[s, 2 * j:2 * j + 2] = val.reshape(2, SSD_HEAD_DIM, SSD_STATE)

    ys = []
    _run(_ssd_phases(xa[:, :SSD_INNER], xa[:, SSD_INNER:SSD_INNER + 256], xa[:, SSD_INNER + 256:],
                     small, dtb_ref[...], a_ref[...], d_ref[...], get_ss, put_ss, seq_len, ys))
    mixed_ref[:, GLA_VW:] = _ssd_finish(jnp.concatenate(ys, axis=1), proj_ref[:, COL_Z:COL_Z + SSD_INNER],
                                        snw_ref[...])

    def get_sg(s):
        return sg_ref[s].reshape(GLA_KW, GLA_DV)

    def put_sg(s, val):
        sg_out[s] = val.reshape(GLA_HEADS, GLA_DK, GLA_DV)

    outs = []
    _run(_gla_phases(proj_ref[:, COL_Q:COL_Q + GLA_KW], proj_ref[:, COL_K:COL_K + GLA_KW],
                     proj_ref[:, COL_V:COL_V + GLA_VW], logf, get_sg, put_sg, seq_len, outs))
    for h in range(GLA_HEADS):
        g_h = proj_ref[:, COL_G + h * GLA_DV:COL_G + (h + 1) * GLA_DV]
        mixed_ref[:, h * GLA_DV:(h + 1) * GLA_DV] = _gla_finish(outs[h], g_h, gnw_ref[...])


def _mixer_sample(proj, tail, sg, ss, p, seq_len):
    n = proj.shape[0]
    rows = SAMPLE_SEQS * seq_len
    nb = sg.shape[0]
    consts = [p["gw2"], p["gb"], p["gnw"], p["conv_w"], p["conv_b"], p["dtb"], p["a_blk"], p["d_row"],
              p["snw"]]
    sg_spec = pl.BlockSpec((SAMPLE_SEQS, GLA_HEADS, GLA_DK, GLA_DV), lambda i: (i, 0, 0, 0))
    ss_spec = pl.BlockSpec((SAMPLE_SEQS, SSD_HEADS, SSD_HEAD_DIM, SSD_STATE), lambda i: (i, 0, 0, 0))
    return pl.pallas_call(
        functools.partial(_mixer_sample_kernel, seq_len),
        grid=(nb // SAMPLE_SEQS,),
        in_specs=[pl.BlockSpec((rows, PROJ_COLS), lambda i: (i, 0)),
                  pl.BlockSpec((rows, SSD_CONV_CH), lambda i: (i, 0)),
                  sg_spec, ss_spec] + [_const_spec(c.shape) for c in consts],
        out_specs=[pl.BlockSpec((rows, D_MODEL), lambda i: (i, 0)), sg_spec, ss_spec],
        out_shape=[jax.ShapeDtypeStruct((n, D_MODEL), F32),
                   jax.ShapeDtypeStruct(sg.shape, F32), jax.ShapeDtypeStruct(ss.shape, F32)],
        scratch_shapes=[pltpu.VMEM((rows + 8, SSD_CONV_CH), F32),
                        pltpu.VMEM((rows + 8, SSD_CONV_CH), F32)],
        compiler_params=pltpu.CompilerParams(dimension_semantics=("arbitrary",),
                                             vmem_limit_bytes=VMEM_LIMIT),
        name="mixer_sample",
    )(proj, tail, sg, ss, *consts)


def _post_mix_sample_kernel(x_ref, mixed_ref, wout_ref, lnpost_ref, lnxa_ref, wq_ref, h_ref, q_ref):
    m = _dot(_bf(mixed_ref[...]), wout_ref[...])
    h = x_ref[...] + _rms(m, lnpost_ref[...])
    h_ref[...] = h
    q_ref[...] = _bf(_dot(_bf(_rms(h, lnxa_ref[...])), wq_ref[...]))


def _post_mix_sample(x, mixed, p):
    n, d = x.shape
    full = pl.BlockSpec((n, d), lambda i: (0, 0))
    consts = [p["w_out"], p["ln_mix_post"], p["ln_xa_pre"], p["w_xq"]]
    return pl.pallas_call(
        _post_mix_sample_kernel,
        grid=(1,),
        in_specs=[full, full] + [_const_spec(c.shape) for c in consts],
        out_specs=[full, full],
        out_shape=[jax.ShapeDtypeStruct((n, d), F32), jax.ShapeDtypeStruct((n, d), BF16)],
        compiler_params=pltpu.CompilerParams(dimension_semantics=("arbitrary",),
                                             vmem_limit_bytes=VMEM_LIMIT),
        name="post_mix_sample",
    )(x, mixed, *consts)


def _attn_sample_kernel(seq_len, q_ref, k_hbm, v_hbm, a_ref, kbuf, vbuf, sem):
    step = pl.program_id(0)
    rows = q_ref.shape[0]
    n_seq = rows // seq_len

    def copies(at_step, slot):
        out = []
        for s in range(n_seq):
            for hd in range(XA_HEADS):
                b = at_step * n_seq + s
                out.append(pltpu.make_async_copy(k_hbm.at[0, b, :, hd, :], kbuf.at[slot, s, hd], sem.at[0, slot]))
                out.append(pltpu.make_async_copy(v_hbm.at[0, b, :, hd, :], vbuf.at[slot, s, hd], sem.at[1, slot]))
        return out

    @pl.when(step == 0)
    def _():
        for cp in copies(0, 0):
            cp.start()

    @pl.when(step + 1 < pl.num_programs(0))
    def _():
        for cp in copies(step + 1, (step + 1) % 2):
            cp.start()

    slot = step % 2
    for cp in copies(step, slot):
        cp.wait()

    q = q_ref[...]
    results = {}

    def one(s, hd):
        sc = _dot_nt(q[:, hd * XA_HEAD_DIM:(hd + 1) * XA_HEAD_DIM], _bf(kbuf[slot, s, hd]))
        yield
        e = jnp.exp(sc - jnp.max(sc, axis=-1, keepdims=True))
        pr = _bf(e / jnp.sum(e, axis=-1, keepdims=True))
        yield
        results[s, hd] = _dot(pr, _bf(vbuf[slot, s, hd]))

    _interleave([one(s, hd) for s in range(n_seq) for hd in range(XA_HEADS)])
    for hd in range(XA_HEADS):
        sl = slice(hd * XA_HEAD_DIM, (hd + 1) * XA_HEAD_DIM)
        out = results[0, hd]
        for s in range(1, n_seq):
            out = jnp.where(_row_in_seq(rows, seq_len, s, XA_HEAD_DIM), results[s, hd], out)
        a_ref[:, sl] = out


def _attn_sample(q, ck, cv, seq_len):
    n, d = q.shape
    _, nb, m, nh, hd = ck.shape
    rows = ATTN_SEQS * seq_len
    tok = pl.BlockSpec((rows, d), lambda i: (i, 0))
    hbm = pl.BlockSpec(memory_space=pl.ANY)
    return pl.pallas_call(
        functools.partial(_attn_sample_kernel, seq_len),
        grid=(nb // ATTN_SEQS,),
        in_specs=[tok, hbm, hbm],
        out_specs=tok,
        out_shape=jax.ShapeDtypeStruct((n, d), F32),
        scratch_shapes=[pltpu.VMEM((2, ATTN_SEQS, nh, m, hd), F32),
                        pltpu.VMEM((2, ATTN_SEQS, nh, m, hd), F32),
                        pltpu.SemaphoreType.DMA((2, 2))],
        compiler_params=pltpu.CompilerParams(dimension_semantics=("arbitrary",),
                                             vmem_limit_bytes=VMEM_LIMIT),
        name="attn_sample",
    )(q, ck, cv)


def _post_attn_sample_kernel(h_ref, a_ref, *rest):
    consts, y_ref = rest[:-1], rest[-1]
    chains = []
    for sub in range(h_ref.shape[0] // TILE):
        rs = pl.ds(sub * TILE, TILE)
        chains.append(_wo_ffn_phases(h_ref[rs, :], _bf(a_ref[rs, :]), y_ref.at[rs], *consts))
    _interleave(chains, offset=FFN_OFFSET)


def _post_attn_sample(h, a, p):
    n, d = h.shape
    full = pl.BlockSpec((n, d), lambda i: (0, 0))
    consts = [p["w_xo"], p["ln_xa_post"], p["ln_ffn_pre"], p["ln_ffn_post"], p["w_gate"], p["w_up"],
              p["w_down"]]
    return pl.pallas_call(
        _post_attn_sample_kernel,
        grid=(1,),
        in_specs=[full, full] + [_const_spec(c.shape) for c in consts],
        out_specs=full,
        out_shape=jax.ShapeDtypeStruct((n, d), F32),
        compiler_params=pltpu.CompilerParams(dimension_semantics=("arbitrary",),
                                             vmem_limit_bytes=VMEM_LIMIT),
        name="post_attn_sample",
    )(h, a, *consts)


SRC_GLR = GLA_KW + GLA_KW + GLA_VW + GLA_VW
SRC_Z = SRC_GLR + GLA_GATE_RANK
SRC_XBC = SRC_Z + SSD_INNER
SRC_DT = SRC_XBC + SSD_CONV_CH
IN_COLS = SRC_DT + SSD_HEADS


def _pack_w_in_kernel(wt_hbm, o_ref, buf, sem):
    pieces = [(0, COL_Q, SRC_GLR),
              (SRC_Z, COL_Z, SSD_INNER),
              (SRC_XBC, COL_XBC, SSD_CONV_CH),
              (SRC_GLR, COL_SMALL, GLA_GATE_RANK),
              (SRC_DT, COL_SMALL + DT_LANE, SSD_HEADS)]
    copies = [pltpu.make_async_copy(wt_hbm.at[0, pl.ds(src, n), :], buf.at[pl.ds(dst, n), :], sem.at[0])
              for src, dst, n in pieces]
    for cp in copies:
        cp.start()
    used = COL_SMALL + DT_LANE + SSD_HEADS
    buf[used:PROJ_COLS, :] = jnp.zeros((PROJ_COLS - used, D_MODEL), F32)
    for cp in copies:
        cp.wait()
    for j in range(PROJ_COLS // 128):
        blk = buf[128 * j:128 * (j + 1), :]
        if 128 * j < COL_K:
            blk = blk * (GLA_DK ** -0.5)
        o_ref[:, 128 * j:128 * (j + 1)] = blk.T.astype(BF16)


def _pack_w_in(w_in):
    wt = jnp.swapaxes(w_in, 1, 2)
    return pl.pallas_call(
        _pack_w_in_kernel,
        grid=(1,),
        in_specs=[pl.BlockSpec(memory_space=pl.ANY)],
        out_specs=pl.BlockSpec((D_MODEL, PROJ_COLS), lambda i: (0, 0)),
        out_shape=jax.ShapeDtypeStruct((D_MODEL, PROJ_COLS), BF16),
        scratch_shapes=[pltpu.VMEM((PROJ_COLS, D_MODEL), F32), pltpu.SemaphoreType.DMA((1,))],
        compiler_params=pltpu.CompilerParams(dimension_semantics=("arbitrary",),
                                             vmem_limit_bytes=VMEM_LIMIT),
        name="pack_w_in",
    )(wt)


def _pack_params(ln_mix_pre, ln_mix_post, w_in_packed, gla_gate_w2, gla_gate_b, gla_norm_w, ssd_conv_w,
                 ssd_conv_b, ssd_dt_bias, ssd_A_log, ssd_D, ssd_norm_w, w_out, ln_xa_pre, ln_xa_post,
                 mem_norm_w, w_xq, w_xk, w_xv, w_xo, ln_ffn_pre, ln_ffn_post, w_gate, w_up, w_down):

    def small_row(v):
        return jnp.zeros((1, 128), F32).at[0, DT_LANE:DT_LANE + SSD_HEADS].set(v)

    row = lambda v: v.reshape(1, -1)
    return {
        "ln_mix_pre": row(ln_mix_pre), "ln_mix_post": row(ln_mix_post),
        "w_in": w_in_packed,
        "gw2": _bf(jnp.zeros((128, GLA_KW), F32).at[:GLA_GATE_RANK].set(gla_gate_w2)),
        "gb": row(gla_gate_b), "gnw": row(gla_norm_w),
        "conv_w": ssd_conv_w, "conv_b": row(ssd_conv_b),
        "dtb": small_row(ssd_dt_bias),
        "a_blk": small_row(-jnp.exp(ssd_A_log)),
        "d_row": jnp.repeat(ssd_D, SSD_HEAD_DIM).reshape(1, SSD_INNER),
        "snw": row(ssd_norm_w),
        "w_out": _bf(w_out),
        "ln_xa_pre": row(ln_xa_pre), "ln_xa_post": row(ln_xa_post), "mem_norm_w": row(mem_norm_w),
        "w_xq": _bf(w_xq * (XA_HEAD_DIM ** -0.5)), "w_xk": _bf(w_xk), "w_xv": _bf(w_xv), "w_xo": _bf(w_xo),
        "ln_ffn_pre": row(ln_ffn_pre), "ln_ffn_post": row(ln_ffn_post),
        "w_gate": _bf(w_gate), "w_up": _bf(w_up), "w_down": _bf(w_down),
    }


def kernel(x_prompt, x_sample, mem_prompt, state_gla, state_ssm, state_conv, cache_mem_k, cache_mem_v,
           ln_mix_pre, ln_mix_post, w_in, gla_gate_w2, gla_gate_b, gla_norm_w, ssd_conv_w, ssd_conv_b,
           ssd_dt_bias, ssd_A_log, ssd_D, ssd_norm_w, w_out, ln_xa_pre, ln_xa_post, mem_norm_w,
           w_xq, w_xk, w_xv, w_xo, ln_ffn_pre, ln_ffn_post, w_gate, w_up, w_down):
    assert w_in.shape[0] == 1, "single-layer kernel"
    layer = [a[0] for a in (ln_mix_pre, ln_mix_post, w_in, gla_gate_w2, gla_gate_b, gla_norm_w, ssd_conv_w,
                            ssd_conv_b, ssd_dt_bias, ssd_A_log, ssd_D, ssd_norm_w, w_out, ln_xa_pre,
                            ln_xa_post, mem_norm_w, w_xq, w_xk, w_xv, w_xo, ln_ffn_pre, ln_ffn_post,
                            w_gate, w_up, w_down)]
    layer[2] = _pack_w_in(w_in)
    p = _pack_params(*layer)
    bp, mem_len, _ = mem_prompt.shape
    bs, ts, _ = x_sample.shape
    assert ts == SSD_CONV, "the padded conv-tail layout assumes one new row per conv tap"

    mk, mv, mkb, mvb = _memkv(mem_prompt, p["mem_norm_w"], p["w_xk"], p["w_xv"])
    h_p, gla_p, ssm_p, conv_p = _mixer_prompt(x_prompt, p)
    y_p = _attn_ffn_prompt(h_p, mkb, mvb, p)

    xs = x_sample.reshape(bs * ts, D_MODEL)
    proj = _proj_sample(xs, p)
    tail = jnp.pad(state_conv[0], ((0, 0), (1, 0), (0, 0))).reshape(bs * ts, SSD_CONV_CH)
    mixed, gla_s, ssm_s = _mixer_sample(proj, tail, state_gla[0], state_ssm[0], p, ts)
    h_s, q_s = _post_mix_sample(xs, mixed, p)
    a_s = _attn_sample(q_s, cache_mem_k, cache_mem_v, ts)
    y_s = _post_attn_sample(h_s, a_s, p)
    conv_s = proj[:, COL_XBC:COL_XBC + SSD_CONV_CH].reshape(bs, ts, SSD_CONV_CH)[:, ts - (SSD_CONV - 1):]

    return (y_p, y_s.reshape(bs, ts, D_MODEL), gla_p[None], ssm_p[None], conv_p[None],
            mk, mv, gla_s[None], ssm_s[None], conv_s[None])
```

```python
import functools

import jax
import jax.numpy as jnp
from jax import lax
from jax.experimental import pallas as pl
from jax.experimental.pallas import tpu as pltpu

F32 = jnp.float32
BF16 = jnp.bfloat16

D_MODEL = 1024
GLA_HEADS = 4
GLA_DK = 64
GLA_DV = 128
GLA_KW = GLA_HEADS * GLA_DK
GLA_VW = GLA_HEADS * GLA_DV
GLA_GATE_RANK = 16
GLA_GATE_TAU = 16.0
SSD_INNER = 512
SSD_HEAD_DIM = 64
SSD_HEADS = 8
SSD_GROUPS = 2
SSD_STATE = 128
SSD_CONV = 4
SSD_CONV_CH = 1024
XA_HEADS = 4
XA_HEAD_DIM = 256
D_FF = 2816
EPS = 1e-6

COL_Q = 0
COL_K = 256
COL_V = 512
COL_G = 1024
COL_Z = 1536
COL_XBC = 2048
COL_SMALL = 3072
DT_LANE = GLA_GATE_RANK
PROJ_COLS = 3200

GLA_CHUNK = 64
TILE = 256
FFN_SUBTILES = 4
MIX_SEQS = 4
SAMPLE_SEQS = 16
ATTN_SEQS = 4
VMEM_LIMIT = 56 * 1024 * 1024


def _bf(x):
    return x.astype(BF16)


def _dot(a, b):
    return jnp.dot(a, b, preferred_element_type=F32)


def _dot_nt(a, b):
    return lax.dot_general(a, b, (((1,), (1,)), ((), ())), preferred_element_type=F32)


def _dot_tn(a, b):
    return lax.dot_general(a, b, (((0,), (0,)), ((), ())), preferred_element_type=F32)


def _dot3(m, a):
    hi = _bf(a)
    r = a - hi.astype(F32)
    mid = _bf(r)
    lo = _bf(r - mid.astype(F32))
    return _dot(m, hi) + _dot(m, mid) + _dot(m, lo)


def _rms(x, w):
    ms = jnp.mean(x * x, axis=-1, keepdims=True)
    return x * lax.rsqrt(ms + EPS) * w


def _silu(x):
    return x / (1.0 + jnp.exp(-x))


def _softplus(x):
    e = jnp.exp(-jnp.abs(x))
    u = 1.0 + e
    log1p_e = jnp.where(u == 1.0, e, jnp.log(u) * (e / (u - 1.0)))
    return jnp.maximum(x, 0.0) + log1p_e


def _log_sigmoid(x):
    return jnp.minimum(x, 0.0) - jnp.log(1.0 + jnp.exp(-jnp.abs(x)))


def _seg_masks(rows, seq_len):
    t = lax.broadcasted_iota(jnp.int32, (rows, rows), 0)
    u = lax.broadcasted_iota(jnp.int32, (rows, rows), 1)
    if seq_len == rows:
        same = None
        tril = u <= t
    else:
        shift = seq_len.bit_length() - 1
        assert 1 << shift == seq_len
        same = (t >> shift) == (u >> shift)
        tril = jnp.logical_and(same, u <= t)
    return same, tril


def _mask_to_bf16(mask, rows):
    if mask is None:
        return jnp.ones((rows, rows), BF16)
    return jnp.where(mask, 1.0, 0.0).astype(BF16)


def _row_in_seq(rows, seq_len, s, width):
    r = lax.broadcasted_iota(jnp.int32, (rows, width), 0)
    return jnp.logical_and(r >= s * seq_len, r < (s + 1) * seq_len)


def _interleave(chains, offset=0):
    waiting = list(enumerate(chains))
    active = []
    rnd = 0
    while waiting or active:
        while waiting and waiting[0][0] * offset <= rnd:
            active.append(waiting.pop(0)[1])
        alive = []
        for c in active:
            try:
                next(c)
                alive.append(c)
            except StopIteration:
                pass
        active = alive
        rnd += 1


def _run(phases):
    for _ in phases:
        pass


def _pad_rows(x, rows):
    if x.shape[0] >= rows:
        return x
    return jnp.concatenate([x, jnp.zeros((rows - x.shape[0], x.shape[1]), x.dtype)], axis=0)


def _gla_phases(q, k, v, lf, get_state, put_state, seq_len, outs):
    rows = q.shape[0]
    n_seq = rows // seq_len
    same, tril = _seg_masks(rows, seq_len)
    tril_b = _mask_to_bf16(tril, rows)
    same_b = _mask_to_bf16(same, rows)

    b = _dot3(tril_b, lf)
    if n_seq == 1:
        bl = jnp.broadcast_to(b[rows - 1:rows, :], b.shape)
    else:
        bl = _dot3(same_b, lf)
    qt = q * jnp.exp(b)
    kt = k * jnp.exp(-b)
    kd = k * jnp.exp(bl - b)
    bl_t = _pad_rows(bl, 128).T
    yield

    lane = lax.broadcasted_iota(jnp.int32, (rows, GLA_KW), 1)
    lhs = jnp.concatenate(
        [jnp.where((lane >= h * GLA_DK) & (lane < (h + 1) * GLA_DK), qt, 0.0)
         for h in range(GLA_HEADS)], axis=0).astype(BF16)
    att = _dot_nt(lhs, kt.astype(BF16))
    t4 = lax.broadcasted_iota(jnp.int32, (GLA_HEADS * rows, rows), 0) & (rows - 1)
    u4 = lax.broadcasted_iota(jnp.int32, (GLA_HEADS * rows, rows), 1)
    causal4 = u4 <= t4
    if n_seq > 1:
        shift = seq_len.bit_length() - 1
        causal4 = jnp.logical_and(causal4, (t4 >> shift) == (u4 >> shift))
    att = jnp.where(causal4, att, 0.0).astype(BF16)
    vb = v.astype(BF16)
    kdb = kd.astype(BF16)
    yield

    o_inter = None
    r4 = lax.broadcasted_iota(jnp.int32, (GLA_HEADS * rows, GLA_DV), 0) & (rows - 1)
    for s in range(n_seq):
        st = get_state(s)
        oi = _dot(lhs, st.astype(BF16))
        if n_seq == 1:
            o_inter = oi
            kd_s = kdb
        else:
            m4 = jnp.logical_and(r4 >= s * seq_len, r4 < (s + 1) * seq_len)
            oi = jnp.where(m4, oi, 0.0)
            o_inter = oi if o_inter is None else o_inter + oi
            kd_s = jnp.where(_row_in_seq(rows, seq_len, s, GLA_KW), kdb, jnp.zeros_like(kdb))
        blocks = []
        for hp in range(GLA_HEADS // 2):
            upd = _dot_tn(kd_s[:, 2 * hp * GLA_DK:2 * (hp + 1) * GLA_DK],
                          vb[:, 2 * hp * GLA_DV:2 * (hp + 1) * GLA_DV])
            blocks += [upd[:GLA_DK, :GLA_DV], upd[GLA_DK:, GLA_DV:]]
        upd_d = jnp.concatenate(blocks, axis=0)
        c0 = s * seq_len
        decay = jnp.exp(jnp.broadcast_to(bl_t[:, c0:c0 + 1], (GLA_KW, GLA_DV)))
        put_state(s, decay * st + upd_d)
        if n_seq > 1 and s % 4 == 3:
            yield
    yield

    for h in range(GLA_HEADS):
        o_h = _dot(att[h * rows:(h + 1) * rows], vb[:, h * GLA_DV:(h + 1) * GLA_DV])
        outs.append(o_h + o_inter[h * rows:(h + 1) * rows])


def _ssd_phases(xs, bm, cm, dt_raw_blk, dtb_blk, a_blk, d_row, get_state, put_state, seq_len, ys):
    rows = xs.shape[0]
    n_seq = rows // seq_len
    same, tril = _seg_masks(rows, seq_len)
    tril_b = _mask_to_bf16(tril, rows)
    same_b = _mask_to_bf16(same, rows)

    dt = _softplus(dt_raw_blk + dtb_blk)
    dta = dt * a_blk
    lc = _dot3(tril_b, dta)
    if n_seq == 1:
        ll = jnp.broadcast_to(lc[rows - 1:rows, :], lc.shape)
    else:
        ll = _dot3(same_b, dta)
    elc = jnp.exp(lc)
    w = jnp.exp(ll - lc) * dt
    ell = jnp.exp(ll)
    lc_t = _pad_rows(lc, 128).T[:, :rows]
    yield

    bmb = bm.astype(BF16)
    cmb = cm.astype(BF16)
    cb = [_dot_nt(cmb[:, g * SSD_STATE:(g + 1) * SSD_STATE],
                  bmb[:, g * SSD_STATE:(g + 1) * SSD_STATE]) for g in range(SSD_GROUPS)]
    yield

    lane128 = lax.broadcasted_iota(jnp.int32, (rows, 128), 1)
    lo128 = lane128 < SSD_HEAD_DIM
    sub128 = lax.broadcasted_iota(jnp.int32, (128, 128), 0) < SSD_HEAD_DIM

    def col(x, h, width):
        return jnp.broadcast_to(x[:, DT_LANE + h:DT_LANE + h + 1], (x.shape[0], width))

    heads_per_group = SSD_HEADS // SSD_GROUPS
    for j in range(SSD_HEADS // 2):
        g = (2 * j) // heads_per_group
        xs_p = xs[:, 128 * j:128 * (j + 1)]
        dt_p = jnp.where(lo128, col(dt, 2 * j, 128), col(dt, 2 * j + 1, 128))
        xdt = xs_p * dt_p
        x_lo = jnp.where(lo128, xdt, 0.0).astype(BF16)
        x_hi = jnp.where(lo128, 0.0, xdt).astype(BF16)
        y_p = None
        for h, x_half in ((2 * j, x_lo), (2 * j + 1, x_hi)):
            seg = col(lc, h, rows) - jnp.broadcast_to(lc_t[DT_LANE + h:DT_LANE + h + 1, :], (rows, rows))
            w_h = jnp.where(tril, cb[g] * jnp.exp(seg), 0.0).astype(BF16)
            y_h = _dot(w_h, x_half)
            y_p = y_h if y_p is None else y_p + y_h
            yield

        elc_p = jnp.where(lo128, col(elc, 2 * j, 128), col(elc, 2 * j + 1, 128))
        w_p = jnp.where(lo128, col(w, 2 * j, 128), col(w, 2 * j + 1, 128))
        xw = (xs_p * w_p).astype(BF16)
        bm_g = bmb[:, g * SSD_STATE:(g + 1) * SSD_STATE]
        cm_g = cmb[:, g * SSD_STATE:(g + 1) * SSD_STATE]
        y_inter = None
        for s in range(n_seq):
            st = get_state(s, j)
            ci = _dot_nt(cm_g, st.astype(BF16))
            if n_seq == 1:
                y_inter = ci
                xw_s = xw
            else:
                msk = _row_in_seq(rows, seq_len, s, 128)
                ci = jnp.where(msk, ci, 0.0)
                y_inter = ci if y_inter is None else y_inter + ci
                xw_s = jnp.where(msk, xw, jnp.zeros_like(xw))
            upd = _dot_tn(xw_s, bm_g)
            r0 = s * seq_len
            e0 = jnp.broadcast_to(ell[r0:r0 + 1, DT_LANE + 2 * j:DT_LANE + 2 * j + 1], (128, 128))
            e1 = jnp.broadcast_to(ell[r0:r0 + 1, DT_LANE + 2 * j + 1:DT_LANE + 2 * j + 2], (128, 128))
            put_state(s, j, jnp.where(sub128, e0, e1) * st + upd)
        ys.append(y_p + y_inter * elc_p + d_row[:, 128 * j:128 * (j + 1)] * xs_p)
        yield


def _ssd_phases_seq(xs, bm, cm, dt_raw_blk, dtb_blk, a_blk, d_row, get_state, put_state, ys):
    rows = xs.shape[0]
    _, tril = _seg_masks(rows, rows)
    tril_b = _mask_to_bf16(tril, rows)
    src = lax.broadcasted_iota(jnp.int32, (rows, rows), 0)
    dst = lax.broadcasted_iota(jnp.int32, (rows, rows), 1)
    causal_t = src <= dst

    dt = _softplus(dt_raw_blk + dtb_blk)
    lc = _dot3(tril_b, dt * a_blk)
    lc_t = lc.T
    dt_t = dt.T
    elc_t = jnp.exp(lc_t)
    w_t = jnp.exp(lc_t[:, rows - 1:rows] - lc_t) * dt_t
    ell = jnp.exp(lc[rows - 1:rows, :])
    yield

    xs_t = xs.T
    bmb = bm.astype(BF16)
    cmb = cm.astype(BF16)
    cb_t = [_dot_nt(bmb[:, g * SSD_STATE:(g + 1) * SSD_STATE],
                    cmb[:, g * SSD_STATE:(g + 1) * SSD_STATE]) for g in range(SSD_GROUPS)]
    yield

    heads_per_group = SSD_HEADS // SSD_GROUPS
    y_t = []
    for h in range(SSD_HEADS):
        g = h // heads_per_group
        lane = DT_LANE + h
        xs_h = xs_t[h * SSD_HEAD_DIM:(h + 1) * SSD_HEAD_DIM, :]
        seg_t = lc_t[lane:lane + 1, :] - jnp.broadcast_to(lc[:, lane:lane + 1], (rows, rows))
        w_h = jnp.where(causal_t, cb_t[g] * jnp.exp(seg_t), 0.0).astype(BF16)
        y_h = _dot((xs_h * dt_t[lane:lane + 1, :]).astype(BF16), w_h)
        st = get_state(h)
        ci = _dot_nt(st.astype(BF16), cmb[:, g * SSD_STATE:(g + 1) * SSD_STATE])
        y_t.append(y_h + ci * elc_t[lane:lane + 1, :])
        upd = _dot((xs_h * w_t[lane:lane + 1, :]).astype(BF16), bmb[:, g * SSD_STATE:(g + 1) * SSD_STATE])
        put_state(h, jnp.broadcast_to(ell[:, lane:lane + 1], st.shape) * st + upd)
        yield
    ys.append(jnp.concatenate(y_t, axis=0).T + d_row * xs)


def _gate_logf(small_blk, gw2_ref, gb_ref):
    logits = _dot(_bf(small_blk), gw2_ref[...]) + gb_ref[...]
    return _log_sigmoid(logits) * (1.0 / GLA_GATE_TAU)


def _gla_finish(o_h, g_h, gnw):
    return _rms(o_h, gnw) * _silu(g_h)


def _ssd_finish(y, z, snw):
    y = y * _silu(z)
    gs = SSD_INNER // SSD_GROUPS
    parts = [_rms(y[:, g * gs:(g + 1) * gs], snw[:, g * gs:(g + 1) * gs]) for g in range(SSD_GROUPS)]
    return jnp.concatenate(parts, axis=1)


def _memkv_kernel(mem_ref, nw_ref, wk_ref, wv_ref, k_ref, v_ref, kb_ref, vb_ref):
    mn = _bf(_rms(mem_ref[0], nw_ref[...]))
    mk = _dot(mn, wk_ref[...])
    mv = _dot(mn, wv_ref[...])
    for h in range(XA_HEADS):
        k_ref[0, 0, :, h, :] = mk[:, h * XA_HEAD_DIM:(h + 1) * XA_HEAD_DIM]
        v_ref[0, 0, :, h, :] = mv[:, h * XA_HEAD_DIM:(h + 1) * XA_HEAD_DIM]
    kb_ref[0] = _bf(mk)
    vb_ref[0] = _bf(mv)


def _const_spec(shape):
    nd = len(shape)
    return pl.BlockSpec(shape, lambda *_: (0,) * nd, pipeline_mode=pl.Buffered(1))


def _memkv(mem, nw, wk, wv):
    b, m, d = mem.shape
    blk = pl.BlockSpec((1, m, d), lambda i: (i, 0, 0))
    cache_blk = pl.BlockSpec((1, 1, m, XA_HEADS, XA_HEAD_DIM), lambda i: (0, i, 0, 0, 0))
    cache_shape = jax.ShapeDtypeStruct((1, b, m, XA_HEADS, XA_HEAD_DIM), F32)
    return pl.pallas_call(
        _memkv_kernel,
        grid=(b,),
        in_specs=[blk, _const_spec((1, d)), _const_spec((d, d)), _const_spec((d, d))],
        out_specs=[cache_blk, cache_blk, blk, blk],
        out_shape=[cache_shape] * 2 + [jax.ShapeDtypeStruct((b, m, d), BF16)] * 2,
        compiler_params=pltpu.CompilerParams(dimension_semantics=("arbitrary",),
                                             vmem_limit_bytes=VMEM_LIMIT),
        name="mem_kv",
    )(mem, nw, wk, wv)


PROJ_CHUNK = 512


def _mixer_prompt_chain(x_ref, h_ref, proj_s, xpad_s, mixed_s, sg_s, ss_s,
                        lnpre_ref, win_ref, gw2_ref, gb_ref, gnw_ref, cw_ref, cb_ref,
                        dtb_ref, a_ref, d_ref, snw_ref, wout_ref, lnpost_ref):
    rows = TILE
    x = x_ref[...]
    hn = _rms(x, lnpre_ref[...]).astype(BF16)
    yield

    def proj_cols(c0, width):
        proj_s[:, c0:c0 + width] = _dot(hn, win_ref[:, c0:c0 + width])

    def late_proj():
        for c0 in (COL_Q, COL_V, COL_G, COL_Z):
            proj_cols(c0, PROJ_CHUNK)
            yield

    side = [late_proj()]

    def tick():
        if side[0] is not None:
            try:
                next(side[0])
            except StopIteration:
                side[0] = None

    for c in range(SSD_CONV_CH // PROJ_CHUNK):
        proj_cols(COL_XBC + c * PROJ_CHUNK, PROJ_CHUNK)
        yield
    proj_cols(COL_SMALL, 128)

    xa = []
    for c in range(SSD_CONV_CH // PROJ_CHUNK):
        cs = slice(c * PROJ_CHUNK, (c + 1) * PROJ_CHUNK)
        xpad_s[8:8 + rows, cs] = proj_s[:, COL_XBC + c * PROJ_CHUNK:COL_XBC + (c + 1) * PROJ_CHUNK]
        conv = cb_ref[:, cs]
        for j in range(SSD_CONV):
            conv = conv + xpad_s[5 + j:5 + j + rows, cs] * cw_ref[j:j + 1, cs]
        xpad_s[5:8, cs] = xpad_s[5 + rows:8 + rows, cs]
        xa.append(_silu(conv))
        tick()
        yield

    small = proj_s[:, COL_SMALL:COL_SMALL + 128]
    logf = _gate_logf(small, gw2_ref, gb_ref)

    def get_ss(h):
        return ss_s[SSD_HEAD_DIM * h:SSD_HEAD_DIM * (h + 1), :]

    def put_ss(h, val):
        ss_s[SSD_HEAD_DIM * h:SSD_HEAD_DIM * (h + 1), :] = val

    ys = []
    ssd = _ssd_phases_seq(xa[0], xa[1][:, :SSD_GROUPS * SSD_STATE], xa[1][:, SSD_GROUPS * SSD_STATE:],
                          small, dtb_ref[...], a_ref[...], d_ref[...], get_ss, put_ss, ys)

    def get_sg(s):
        return sg_s[...]

    def put_sg(s, val):
        sg_s[...] = val

    def gla_chunks():
        for c in range(rows // GLA_CHUNK):
            r0 = c * GLA_CHUNK
            sl = slice(r0, r0 + GLA_CHUNK)
            outs = []
            yield from _gla_phases(proj_s[sl, COL_Q:COL_Q + GLA_KW], proj_s[sl, COL_K:COL_K + GLA_KW],
                                   proj_s[sl, COL_V:COL_V + GLA_VW], logf[sl], get_sg, put_sg, GLA_CHUNK, outs)
            for h in range(GLA_HEADS):
                g_h = proj_s[sl, COL_G + h * GLA_DV:COL_G + (h + 1) * GLA_DV]
                mixed_s[sl, h * GLA_DV:(h + 1) * GLA_DV] = _gla_finish(outs[h], g_h, gnw_ref[...]).astype(BF16)
            yield

    while side[0] is not None:
        next(ssd)
        tick()
        yield
    alive = [ssd, gla_chunks()]
    while alive:
        for phases in list(alive):
            try:
                next(phases)
            except StopIteration:
                alive.remove(phases)
                continue
            yield
    mixed_s[:, GLA_VW:] = _ssd_finish(jnp.concatenate(ys, axis=1), proj_s[:, COL_Z:COL_Z + SSD_INNER],
                                      snw_ref[...]).astype(BF16)
    yield

    mixed = mixed_s[...]
    ms = []
    for c in range(D_MODEL // PROJ_CHUNK):
        ms.append(_dot(mixed, wout_ref[:, c * PROJ_CHUNK:(c + 1) * PROJ_CHUNK]))
        yield
    h_ref[...] = x + _rms(jnp.concatenate(ms, axis=1), lnpost_ref[...])


N_MIXER_WEIGHTS = 13
N_MIXER_SCRATCH = 5
MIX_OFFSET = 2


def _mixer_prompt_kernel(x_ref, *rest):
    consts = rest[:N_MIXER_WEIGHTS]
    h_ref, sg_out, ss_out, conv_out = rest[N_MIXER_WEIGHTS:N_MIXER_WEIGHTS + 4]
    scratch = rest[N_MIXER_WEIGHTS + 4:]
    per_seq = [scratch[i * N_MIXER_SCRATCH:(i + 1) * N_MIXER_SCRATCH] for i in range(MIX_SEQS)]
    t = pl.program_id(1)

    @pl.when(t == 0)
    def _():
        for _, xpad_s, _, sg_s, ss_s in per_seq:
            sg_s[...] = jnp.zeros_like(sg_s)
            ss_s[...] = jnp.zeros_like(ss_s)
            xpad_s[0:8, :] = jnp.zeros((8, SSD_CONV_CH), F32)

    _interleave([_mixer_prompt_chain(x_ref.at[i], h_ref.at[i], *per_seq[i], *consts) for i in range(MIX_SEQS)],
                offset=MIX_OFFSET)

    @pl.when(t == pl.num_programs(1) - 1)
    def _():
        for i, (_, xpad_s, _, sg_s, ss_s) in enumerate(per_seq):
            sg_out[i] = sg_s[...].reshape(GLA_HEADS, GLA_DK, GLA_DV)
            ss_out[i] = ss_s[...].reshape(SSD_HEADS, SSD_HEAD_DIM, SSD_STATE)
            conv_out[i] = xpad_s[5:8, :]


def _mixer_prompt(x, p):
    b, t, d = x.shape
    nt = t // TILE
    consts = [p["ln_mix_pre"], p["w_in"], p["gw2"], p["gb"], p["gnw"], p["conv_w"], p["conv_b"],
              p["dtb"], p["a_blk"], p["d_row"], p["snw"], p["w_out"], p["ln_mix_post"]]
    assert len(consts) == N_MIXER_WEIGHTS
    tok = pl.BlockSpec((MIX_SEQS, TILE, d), lambda i, j: (i, j, 0))
    return pl.pallas_call(
        _mixer_prompt_kernel,
        grid=(b // MIX_SEQS, nt),
        in_specs=[tok] + [_const_spec(c.shape) for c in consts],
        out_specs=[tok,
                   pl.BlockSpec((MIX_SEQS, GLA_HEADS, GLA_DK, GLA_DV), lambda i, j: (i, 0, 0, 0)),
                   pl.BlockSpec((MIX_SEQS, SSD_HEADS, SSD_HEAD_DIM, SSD_STATE), lambda i, j: (i, 0, 0, 0)),
                   pl.BlockSpec((MIX_SEQS, SSD_CONV - 1, SSD_CONV_CH), lambda i, j: (i, 0, 0))],
        out_shape=[jax.ShapeDtypeStruct((b, t, d), F32),
                   jax.ShapeDtypeStruct((b, GLA_HEADS, GLA_DK, GLA_DV), F32),
                   jax.ShapeDtypeStruct((b, SSD_HEADS, SSD_HEAD_DIM, SSD_STATE), F32),
                   jax.ShapeDtypeStruct((b, SSD_CONV - 1, SSD_CONV_CH), F32)],
        scratch_shapes=[pltpu.VMEM((TILE, PROJ_COLS), F32),
                        pltpu.VMEM((TILE + 8, SSD_CONV_CH), F32),
                        pltpu.VMEM((TILE, D_MODEL), BF16),
                        pltpu.VMEM((GLA_KW, GLA_DV), F32),
                        pltpu.VMEM((SSD_INNER, SSD_STATE), F32)] * MIX_SEQS,
        compiler_params=pltpu.CompilerParams(dimension_semantics=("arbitrary", "arbitrary"),
                                             vmem_limit_bytes=VMEM_LIMIT),
        name="mixer_prompt",
    )(x, *consts)


def _softmax_rows(s):
    e = jnp.exp(s - jnp.max(s, axis=-1, keepdims=True))
    return e / jnp.sum(e, axis=-1, keepdims=True)


FFN_CHUNK = 256
FFN_OFFSET = 1


def _attn_ffn_chain(h_ref, y_ref, mk_ref, mv_ref, lnxa_ref, wq_ref, wo_ref, lnxap_ref,
                    lnf_ref, lnfp_ref, wg_ref, wu_ref, wd_ref):
    h = h_ref[...]
    hn = _rms(h, lnxa_ref[...]).astype(BF16)
    yield
    heads_per_chunk = PROJ_CHUNK // XA_HEAD_DIM
    outs = []
    for c in range(D_MODEL // PROJ_CHUNK):
        q = _dot(hn, wq_ref[:, c * PROJ_CHUNK:(c + 1) * PROJ_CHUNK]).astype(BF16)
        yield
        for hh in range(heads_per_chunk):
            hd = c * heads_per_chunk + hh
            sl = slice(hd * XA_HEAD_DIM, (hd + 1) * XA_HEAD_DIM)
            s = _dot_nt(q[:, hh * XA_HEAD_DIM:(hh + 1) * XA_HEAD_DIM], mk_ref[0, :, sl])
            yield
            pr = _softmax_rows(s).astype(BF16)
            yield
            outs.append(_dot(pr, mv_ref[0, :, sl]).astype(BF16))
    a = jnp.concatenate(outs, axis=1)
    yield
    yield from _wo_ffn_phases(h, a, y_ref, wo_ref, lnxap_ref, lnf_ref, lnfp_ref, wg_ref, wu_ref, wd_ref)


def _wo_ffn_phases(h, a, y_ref, wo_ref, lnxap_ref, lnf_ref, lnfp_ref, wg_ref, wu_ref, wd_ref):
    parts = []
    for c in range(D_MODEL // PROJ_CHUNK):
        parts.append(_dot(a, wo_ref[:, c * PROJ_CHUNK:(c + 1) * PROJ_CHUNK]))
        yield
    h = h + _rms(jnp.concatenate(parts, axis=1), lnxap_ref[...])
    hf = _rms(h, lnf_ref[...]).astype(BF16)
    yield
    acts = []
    for c in range(D_FF // FFN_CHUNK):
        cs = slice(c * FFN_CHUNK, (c + 1) * FFN_CHUNK)
        gt = _dot(hf, wg_ref[:, cs])
        up = _dot(hf, wu_ref[:, cs])
        yield
        acts.append((_silu(gt) * up).astype(BF16))
    act = jnp.concatenate(acts, axis=1)
    yield
    parts = []
    for c in range(D_MODEL // PROJ_CHUNK):
        parts.append(_dot(act, wd_ref[:, c * PROJ_CHUNK:(c + 1) * PROJ_CHUNK]))
        yield
    y_ref[...] = h + _rms(jnp.concatenate(parts, axis=1), lnfp_ref[...])


def _attn_ffn_prompt_kernel(h_ref, mk_ref, mv_ref, *rest):
    consts, y_ref = rest[:-1], rest[-1]
    chains = []
    for sub in range(FFN_SUBTILES):
        rs = pl.ds(sub * TILE, TILE)
        chains.append(_attn_ffn_chain(h_ref.at[0, rs], y_ref.at[0, rs], mk_ref, mv_ref, *consts))
    _interleave(chains, offset=FFN_OFFSET)


def _attn_ffn_prompt(h, mkb, mvb, p):
    b, t, d = h.shape
    rows = FFN_SUBTILES * TILE
    consts = [p["ln_xa_pre"], p["w_xq"], p["w_xo"], p["ln_xa_post"], p["ln_ffn_pre"], p["ln_ffn_post"],
              p["w_gate"], p["w_up"], p["w_down"]]
    tok = pl.BlockSpec((1, rows, d), lambda i, j: (i, j, 0))
    mem = pl.BlockSpec((1, mkb.shape[1], d), lambda i, j: (i, 0, 0))
    return pl.pallas_call(
        _attn_ffn_prompt_kernel,
        grid=(b, t // rows),
        in_specs=[tok, mem, mem] + [_const_spec(c.shape) for c in consts],
        out_specs=tok,
        out_shape=jax.ShapeDtypeStruct((b, t, d), F32),
        compiler_params=pltpu.CompilerParams(dimension_semantics=("arbitrary", "arbitrary"),
                                             vmem_limit_bytes=VMEM_LIMIT),
        name="attn_ffn_prompt",
    )(h, mkb, mvb, *consts)


def _proj_sample_kernel(x_ref, lnpre_ref, win_ref, proj_ref):
    proj_ref[...] = _dot(_bf(_rms(x_ref[...], lnpre_ref[...])), win_ref[...])


def _proj_sample(x, p):
    n, d = x.shape
    return pl.pallas_call(
        _proj_sample_kernel,
        grid=(1,),
        in_specs=[_const_spec((n, d)), _const_spec((1, d)), _const_spec(p["w_in"].shape)],
        out_specs=pl.BlockSpec((n, PROJ_COLS), lambda i: (0, 0)),
        out_shape=jax.ShapeDtypeStruct((n, PROJ_COLS), F32),
        compiler_params=pltpu.CompilerParams(dimension_semantics=("arbitrary",),
                                             vmem_limit_bytes=VMEM_LIMIT),
        name="proj_sample",
    )(x, p["ln_mix_pre"], p["w_in"])


def _mixer_sample_kernel(seq_len, proj_ref, tail_ref, sg_ref, ss_ref, gw2_ref, gb_ref, gnw_ref,
                         cw_ref, cb_ref, dtb_ref, a_ref, d_ref, snw_ref,
                         mixed_ref, sg_out, ss_out, xpad_s, tpad_s):
    rows = proj_ref.shape[0]
    n_seq = rows // seq_len
    small = proj_ref[:, COL_SMALL:COL_SMALL + 128]
    logf = _gate_logf(small, gw2_ref, gb_ref)

    xbc = proj_ref[:, COL_XBC:COL_XBC + SSD_CONV_CH]
    xpad_s[0:8, :] = jnp.zeros((8, SSD_CONV_CH), F32)
    xpad_s[8:8 + rows, :] = xbc
    tpad_s[0:rows, :] = tail_ref[...]
    tpad_s[rows:rows + 8, :] = jnp.zeros((8, SSD_CONV_CH), F32)
    pos = lax.broadcasted_iota(jnp.int32, (rows, SSD_CONV_CH), 0) & (seq_len - 1)
    conv = cb_ref[...]
    for j in range(SSD_CONV - 1):
        back = SSD_CONV - 1 - j
        prev = jnp.where(pos >= back, xpad_s[8 - back:8 - back + rows, :],
                         tpad_s[seq_len - back:seq_len - back + rows, :])
        conv = conv + prev * cw_ref[j:j + 1, :]
    conv = conv + xbc * cw_ref[SSD_CONV - 1:SSD_CONV, :]
    xa = _silu(conv)

    def get_ss(s, j):
        return ss_ref[s, 2 * j:2 * j + 2].reshape(2 * SSD_HEAD_DIM, SSD_STATE)

    def put_ss(s, j, val):
        ss_out[s, 2 * j:2 * j + 2] = val.reshape(2, SSD_HEAD_DIM, SSD_STATE)

    ys = []
    _run(_ssd_phases(xa[:, :SSD_INNER], xa[:, SSD_INNER:SSD_INNER + 256], xa[:, SSD_INNER + 256:],
                     small, dtb_ref[...], a_ref[...], d_ref[...], get_ss, put_ss, seq_len, ys))
    mixed_ref[:, GLA_VW:] = _ssd_finish(jnp.concatenate(ys, axis=1), proj_ref[:, COL_Z:COL_Z + SSD_INNER],
                                        snw_ref[...])

    def get_sg(s):
        return sg_ref[s].reshape(GLA_KW, GLA_DV)

    def put_sg(s, val):
        sg_out[s] = val.reshape(GLA_HEADS, GLA_DK, GLA_DV)

    outs = []
    _run(_gla_phases(proj_ref[:, COL_Q:COL_Q + GLA_KW], proj_ref[:, COL_K:COL_K + GLA_KW],
                     proj_ref[:, COL_V:COL_V + GLA_VW], logf, get_sg, put_sg, seq_len, outs))
    for h in range(GLA_HEADS):
        g_h = proj_ref[:, COL_G + h * GLA_DV:COL_G + (h + 1) * GLA_DV]
        mixed_ref[:, h * GLA_DV:(h + 1) * GLA_DV] = _gla_finish(outs[h], g_h, gnw_ref[...])


def _mixer_sample(proj, tail, sg, ss, p, seq_len):
    n = proj.shape[0]
    rows = SAMPLE_SEQS * seq_len
    nb = sg.shape[0]
    consts = [p["gw2"], p["gb"], p["gnw"], p["conv_w"], p["conv_b"], p["dtb"], p["a_blk"], p["d_row"],
              p["snw"]]
    sg_spec = pl.BlockSpec((SAMPLE_SEQS, GLA_HEADS, GLA_DK, GLA_DV), lambda i: (i, 0, 0, 0))
    ss_spec = pl.BlockSpec((SAMPLE_SEQS, SSD_HEADS, SSD_HEAD_DIM, SSD_STATE), lambda i: (i, 0, 0, 0))
    return pl.pallas_call(
        functools.partial(_mixer_sample_kernel, seq_len),
        grid=(nb // SAMPLE_SEQS,),
        in_specs=[pl.BlockSpec((rows, PROJ_COLS), lambda i: (i, 0)),
                  pl.BlockSpec((rows, SSD_CONV_CH), lambda i: (i, 0)),
                  sg_spec, ss_spec] + [_const_spec(c.shape) for c in consts],
        out_specs=[pl.BlockSpec((rows, D_MODEL), lambda i: (i, 0)), sg_spec, ss_spec],
        out_shape=[jax.ShapeDtypeStruct((n, D_MODEL), F32),
                   jax.ShapeDtypeStruct(sg.shape, F32), jax.ShapeDtypeStruct(ss.shape, F32)],
        scratch_shapes=[pltpu.VMEM((rows + 8, SSD_CONV_CH), F32),
                        pltpu.VMEM((rows + 8, SSD_CONV_CH), F32)],
        compiler_params=pltpu.CompilerParams(dimension_semantics=("arbitrary",),
                                             vmem_limit_bytes=VMEM_LIMIT),
        name="mixer_sample",
    )(proj, tail, sg, ss, *consts)


def _post_mix_sample_kernel(x_ref, mixed_ref, wout_ref, lnpost_ref, lnxa_ref, wq_ref, h_ref, q_ref):
    m = _dot(_bf(mixed_ref[...]), wout_ref[...])
    h = x_ref[...] + _rms(m, lnpost_ref[...])
    h_ref[...] = h
    q_ref[...] = _bf(_dot(_bf(_rms(h, lnxa_ref[...])), wq_ref[...]))


def _post_mix_sample(x, mixed, p):
    n, d = x.shape
    full = pl.BlockSpec((n, d), lambda i: (0, 0))
    consts = [p["w_out"], p["ln_mix_post"], p["ln_xa_pre"], p["w_xq"]]
    return pl.pallas_call(
        _post_mix_sample_kernel,
        grid=(1,),
        in_specs=[full, full] + [_const_spec(c.shape) for c in consts],
        out_specs=[full, full],
        out_shape=[jax.ShapeDtypeStruct((n, d), F32), jax.ShapeDtypeStruct((n, d), BF16)],
        compiler_params=pltpu.CompilerParams(dimension_semantics=("arbitrary",),
                                             vmem_limit_bytes=VMEM_LIMIT),
        name="post_mix_sample",
    )(x, mixed, *consts)


def _attn_sample_kernel(seq_len, q_ref, k_hbm, v_hbm, a_ref, kbuf, vbuf, sem):
    step = pl.program_id(0)
    rows = q_ref.shape[0]
    n_seq = rows // seq_len

    def copies(at_step, slot):
        out = []
        for s in range(n_seq):
            for hd in range(XA_HEADS):
                b = at_step * n_seq + s
                out.append(pltpu.make_async_copy(k_hbm.at[0, b, :, hd, :], kbuf.at[slot, s, hd], sem.at[0, slot]))
                out.append(pltpu.make_async_copy(v_hbm.at[0, b, :, hd, :], vbuf.at[slot, s, hd], sem.at[1, slot]))
        return out

    @pl.when(step == 0)
    def _():
        for cp in copies(0, 0):
            cp.start()

    @pl.when(step + 1 < pl.num_programs(0))
    def _():
        for cp in copies(step + 1, (step + 1) % 2):
            cp.start()

    slot = step % 2
    for cp in copies(step, slot):
        cp.wait()

    q = q_ref[...]
    results = {}

    def one(s, hd):
        sc = _dot_nt(q[:, hd * XA_HEAD_DIM:(hd + 1) * XA_HEAD_DIM], _bf(kbuf[slot, s, hd]))
        yield
        e = jnp.exp(sc - jnp.max(sc, axis=-1, keepdims=True))
        pr = _bf(e / jnp.sum(e, axis=-1, keepdims=True))
        yield
        results[s, hd] = _dot(pr, _bf(vbuf[slot, s, hd]))

    _interleave([one(s, hd) for s in range(n_seq) for hd in range(XA_HEADS)])
    for hd in range(XA_HEADS):
        sl = slice(hd * XA_HEAD_DIM, (hd + 1) * XA_HEAD_DIM)
        out = results[0, hd]
        for s in range(1, n_seq):
            out = jnp.where(_row_in_seq(rows, seq_len, s, XA_HEAD_DIM), results[s, hd], out)
        a_ref[:, sl] = out


def _attn_sample(q, ck, cv, seq_len):
    n, d = q.shape
    _, nb, m, nh, hd = ck.shape
    rows = ATTN_SEQS * seq_len
    tok = pl.BlockSpec((rows, d), lambda i: (i, 0))
    hbm = pl.BlockSpec(memory_space=pl.ANY)
    return pl.pallas_call(
        functools.partial(_attn_sample_kernel, seq_len),
        grid=(nb // ATTN_SEQS,),
        in_specs=[tok, hbm, hbm],
        out_specs=tok,
        out_shape=jax.ShapeDtypeStruct((n, d), F32),
        scratch_shapes=[pltpu.VMEM((2, ATTN_SEQS, nh, m, hd), F32),
                        pltpu.VMEM((2, ATTN_SEQS, nh, m, hd), F32),
                        pltpu.SemaphoreType.DMA((2, 2))],
        compiler_params=pltpu.CompilerParams(dimension_semantics=("arbitrary",),
                                             vmem_limit_bytes=VMEM_LIMIT),
        name="attn_sample",
    )(q, ck, cv)


def _post_attn_sample_kernel(h_ref, a_ref, *rest):
    consts, y_ref = rest[:-1], rest[-1]
    chains = []
    for sub in range(h_ref.shape[0] // TILE):
        rs = pl.ds(sub * TILE, TILE)
        chains.append(_wo_ffn_phases(h_ref[rs, :], _bf(a_ref[rs, :]), y_ref.at[rs], *consts))
    _interleave(chains, offset=FFN_OFFSET)


def _post_attn_sample(h, a, p):
    n, d = h.shape
    full = pl.BlockSpec((n, d), lambda i: (0, 0))
    consts = [p["w_xo"], p["ln_xa_post"], p["ln_ffn_pre"], p["ln_ffn_post"], p["w_gate"], p["w_up"],
              p["w_down"]]
    return pl.pallas_call(
        _post_attn_sample_kernel,
        grid=(1,),
        in_specs=[full, full] + [_const_spec(c.shape) for c in consts],
        out_specs=full,
        out_shape=jax.ShapeDtypeStruct((n, d), F32),
        compiler_params=pltpu.CompilerParams(dimension_semantics=("arbitrary",),
                                             vmem_limit_bytes=VMEM_LIMIT),
        name="post_attn_sample",
    )(h, a, *consts)


SRC_GLR = GLA_KW + GLA_KW + GLA_VW + GLA_VW
SRC_Z = SRC_GLR + GLA_GATE_RANK
SRC_XBC = SRC_Z + SSD_INNER
SRC_DT = SRC_XBC + SSD_CONV_CH
IN_COLS = SRC_DT + SSD_HEADS


def _pack_w_in_kernel(wt_hbm, o_ref, buf, sem):
    pieces = [(0, COL_Q, SRC_GLR),
              (SRC_Z, COL_Z, SSD_INNER),
              (SRC_XBC, COL_XBC, SSD_CONV_CH),
              (SRC_GLR, COL_SMALL, GLA_GATE_RANK),
              (SRC_DT, COL_SMALL + DT_LANE, SSD_HEADS)]
    copies = [pltpu.make_async_copy(wt_hbm.at[0, pl.ds(src, n), :], buf.at[pl.ds(dst, n), :], sem.at[0])
              for src, dst, n in pieces]
    for cp in copies:
        cp.start()
    used = COL_SMALL + DT_LANE + SSD_HEADS
    buf[used:PROJ_COLS, :] = jnp.zeros((PROJ_COLS - used, D_MODEL), F32)
    for cp in copies:
        cp.wait()
    for j in range(PROJ_COLS // 128):
        blk = buf[128 * j:128 * (j + 1), :]
        if 128 * j < COL_K:
            blk = blk * (GLA_DK ** -0.5)
        o_ref[:, 128 * j:128 * (j + 1)] = blk.T.astype(BF16)


def _pack_w_in(w_in):
    wt = jnp.swapaxes(w_in, 1, 2)
    return pl.pallas_call(
        _pack_w_in_kernel,
        grid=(1,),
        in_specs=[pl.BlockSpec(memory_space=pl.ANY)],
        out_specs=pl.BlockSpec((D_MODEL, PROJ_COLS), lambda i: (0, 0)),
        out_shape=jax.ShapeDtypeStruct((D_MODEL, PROJ_COLS), BF16),
        scratch_shapes=[pltpu.VMEM((PROJ_COLS, D_MODEL), F32), pltpu.SemaphoreType.DMA((1,))],
        compiler_params=pltpu.CompilerParams(dimension_semantics=("arbitrary",),
                                             vmem_limit_bytes=VMEM_LIMIT),
        name="pack_w_in",
    )(wt)


def _pack_params(ln_mix_pre, ln_mix_post, w_in_packed, gla_gate_w2, gla_gate_b, gla_norm_w, ssd_conv_w,
                 ssd_conv_b, ssd_dt_bias, ssd_A_log, ssd_D, ssd_norm_w, w_out, ln_xa_pre, ln_xa_post,
                 mem_norm_w, w_xq, w_xk, w_xv, w_xo, ln_ffn_pre, ln_ffn_post, w_gate, w_up, w_down):

    def small_row(v):
        return jnp.zeros((1, 128), F32).at[0, DT_LANE:DT_LANE + SSD_HEADS].set(v)

    row = lambda v: v.reshape(1, -1)
    return {
        "ln_mix_pre": row(ln_mix_pre), "ln_mix_post": row(ln_mix_post),
        "w_in": w_in_packed,
        "gw2": _bf(jnp.zeros((128, GLA_KW), F32).at[:GLA_GATE_RANK].set(gla_gate_w2)),
        "gb": row(gla_gate_b), "gnw": row(gla_norm_w),
        "conv_w": ssd_conv_w, "conv_b": row(ssd_conv_b),
        "dtb": small_row(ssd_dt_bias),
        "a_blk": small_row(-jnp.exp(ssd_A_log)),
        "d_row": jnp.repeat(ssd_D, SSD_HEAD_DIM).reshape(1, SSD_INNER),
        "snw": row(ssd_norm_w),
        "w_out": _bf(w_out),
        "ln_xa_pre": row(ln_xa_pre), "ln_xa_post": row(ln_xa_post), "mem_norm_w": row(mem_norm_w),
        "w_xq": _bf(w_xq * (XA_HEAD_DIM ** -0.5)), "w_xk": _bf(w_xk), "w_xv": _bf(w_xv), "w_xo": _bf(w_xo),
        "ln_ffn_pre": row(ln_ffn_pre), "ln_ffn_post": row(ln_ffn_post),
        "w_gate": _bf(w_gate), "w_up": _bf(w_up), "w_down": _bf(w_down),
    }


def kernel(x_prompt, x_sample, mem_prompt, state_gla, state_ssm, state_conv, cache_mem_k, cache_mem_v,
           ln_mix_pre, ln_mix_post, w_in, gla_gate_w2, gla_gate_b, gla_norm_w, ssd_conv_w, ssd_conv_b,
           ssd_dt_bias, ssd_A_log, ssd_D, ssd_norm_w, w_out, ln_xa_pre, ln_xa_post, mem_norm_w,
           w_xq, w_xk, w_xv, w_xo, ln_ffn_pre, ln_ffn_post, w_gate, w_up, w_down):
    assert w_in.shape[0] == 1, "single-layer kernel"
    layer = [a[0] for a in (ln_mix_pre, ln_mix_post, w_in, gla_gate_w2, gla_gate_b, gla_norm_w, ssd_conv_w,
                            ssd_conv_b, ssd_dt_bias, ssd_A_log, ssd_D, ssd_norm_w, w_out, ln_xa_pre,
                            ln_xa_post, mem_norm_w, w_xq, w_xk, w_xv, w_xo, ln_ffn_pre, ln_ffn_post,
                            w_gate, w_up, w_down)]
    layer[2] = _pack_w_in(w_in)
    p = _pack_params(*layer)
    bp, mem_len, _ = mem_prompt.shape
    bs, ts, _ = x_sample.shape
    assert ts == SSD_CONV, "the padded conv-tail layout assumes one new row per conv tap"

    mk, mv, mkb, mvb = _memkv(mem_prompt, p["mem_norm_w"], p["w_xk"], p["w_xv"])
    h_p, gla_p, ssm_p, conv_p = _mixer_prompt(x_prompt, p)
    y_p = _attn_ffn_prompt(h_p, mkb, mvb, p)

    xs = x_sample.reshape(bs * ts, D_MODEL)
    proj = _proj_sample(xs, p)
    tail = jnp.pad(state_conv[0], ((0, 0), (1, 0), (0, 0))).reshape(bs * ts, SSD_CONV_CH)
    mixed, gla_s, ssm_s = _mixer_sample(proj, tail, state_gla[0], state_ssm[0], p, ts)
    h_s, q_s = _post_mix_sample(xs, mixed, p)
    a_s = _attn_sample(q_s, cache_mem_k, cache_mem_v, ts)
    y_s = _post_attn_sample(h_s, a_s, p)
    conv_s = proj[:, COL_XBC:COL_XBC + SSD_CONV_CH].reshape(bs, ts, SSD_CONV_CH)[:, ts - (SSD_CONV - 1):]

    return (y_p, y_s.reshape(bs, ts, D_MODEL), gla_p[None], ssm_p[None], conv_p[None],
            mk, mv, gla_s[None], ssm_s[None], conv_s[None])
```

```python
import functools

import jax
import jax.numpy as jnp
from jax import lax
from jax.experimental import pallas as pl
from jax.experimental.pallas import tpu as pltpu

F32 = jnp.float32
BF16 = jnp.bfloat16

D_MODEL = 1024
GLA_HEADS = 4
GLA_DK = 64
GLA_DV = 128
GLA_KW = GLA_HEADS * GLA_DK
GLA_VW = GLA_HEADS * GLA_DV
GLA_GATE_RANK = 16
GLA_GATE_TAU = 16.0
SSD_INNER = 512
SSD_HEAD_DIM = 64
SSD_HEADS = 8
SSD_GROUPS = 2
SSD_STATE = 128
SSD_CONV = 4
SSD_CONV_CH = 1024
XA_HEADS = 4
XA_HEAD_DIM = 256
D_FF = 2816
EPS = 1e-6

COL_Q = 0
COL_K = 256
COL_V = 512
COL_G = 1024
COL_Z = 1536
COL_XBC = 2048
COL_SMALL = 3072
DT_LANE = GLA_GATE_RANK
PROJ_COLS = 3200

GLA_CHUNK = 64
TILE = 256
FFN_SUBTILES = 4
MIX_SEQS = 4
SAMPLE_SEQS = 16
ATTN_SEQS = 4
VMEM_LIMIT = 56 * 1024 * 1024


def _bf(x):
    return x.astype(BF16)


def _dot(a, b):
    return jnp.dot(a, b, preferred_element_type=F32)


def _dot_nt(a, b):
    return lax.dot_general(a, b, (((1,), (1,)), ((), ())), preferred_element_type=F32)


def _dot_tn(a, b):
    return lax.dot_general(a, b, (((0,), (0,)), ((), ())), preferred_element_type=F32)


def _dot3(m, a):
    hi = _bf(a)
    r = a - hi.astype(F32)
    mid = _bf(r)
    lo = _bf(r - mid.astype(F32))
    return _dot(m, hi) + _dot(m, mid) + _dot(m, lo)


def _rms(x, w):
    ms = jnp.mean(x * x, axis=-1, keepdims=True)
    return x * lax.rsqrt(ms + EPS) * w


def _silu(x):
    return x / (1.0 + jnp.exp(-x))


def _softplus(x):
    e = jnp.exp(-jnp.abs(x))
    u = 1.0 + e
    log1p_e = jnp.where(u == 1.0, e, jnp.log(u) * (e / (u - 1.0)))
    return jnp.maximum(x, 0.0) + log1p_e


def _log_sigmoid(x):
    return jnp.minimum(x, 0.0) - jnp.log(1.0 + jnp.exp(-jnp.abs(x)))


def _seg_masks(rows, seq_len):
    t = lax.broadcasted_iota(jnp.int32, (rows, rows), 0)
    u = lax.broadcasted_iota(jnp.int32, (rows, rows), 1)
    if seq_len == rows:
        same = None
        tril = u <= t
    else:
        shift = seq_len.bit_length() - 1
        assert 1 << shift == seq_len
        same = (t >> shift) == (u >> shift)
        tril = jnp.logical_and(same, u <= t)
    return same, tril


def _mask_to_bf16(mask, rows):
    if mask is None:
        return jnp.ones((rows, rows), BF16)
    return jnp.where(mask, 1.0, 0.0).astype(BF16)


def _row_in_seq(rows, seq_len, s, width):
    r = lax.broadcasted_iota(jnp.int32, (rows, width), 0)
    return jnp.logical_and(r >= s * seq_len, r < (s + 1) * seq_len)


def _interleave(chains, offset=0):
    waiting = list(enumerate(chains))
    active = []
    rnd = 0
    while waiting or active:
        while waiting and waiting[0][0] * offset <= rnd:
            active.append(waiting.pop(0)[1])
        alive = []
        for c in active:
            try:
                next(c)
                alive.append(c)
            except StopIteration:
                pass
        active = alive
        rnd += 1


def _run(phases):
    for _ in phases:
        pass


def _pad_rows(x, rows):
    if x.shape[0] >= rows:
        return x
    return jnp.concatenate([x, jnp.zeros((rows - x.shape[0], x.shape[1]), x.dtype)], axis=0)


def _gla_phases(q, k, v, lf, get_state, put_state, seq_len, outs):
    rows = q.shape[0]
    n_seq = rows // seq_len
    same, tril = _seg_masks(rows, seq_len)
    tril_b = _mask_to_bf16(tril, rows)
    same_b = _mask_to_bf16(same, rows)

    b = _dot3(tril_b, lf)
    if n_seq == 1:
        bl = jnp.broadcast_to(b[rows - 1:rows, :], b.shape)
    else:
        bl = _dot3(same_b, lf)
    qt = q * jnp.exp(b)
    kt = k * jnp.exp(-b)
    kd = k * jnp.exp(bl - b)
    bl_t = _pad_rows(bl, 128).T
    yield

    lane = lax.broadcasted_iota(jnp.int32, (rows, GLA_KW), 1)
    lhs = jnp.concatenate(
        [jnp.where((lane >= h * GLA_DK) & (lane < (h + 1) * GLA_DK), qt, 0.0)
         for h in range(GLA_HEADS)], axis=0).astype(BF16)
    att = _dot_nt(lhs, kt.astype(BF16))
    t4 = lax.broadcasted_iota(jnp.int32, (GLA_HEADS * rows, rows), 0) & (rows - 1)
    u4 = lax.broadcasted_iota(jnp.int32, (GLA_HEADS * rows, rows), 1)
    causal4 = u4 <= t4
    if n_seq > 1:
        shift = seq_len.bit_length() - 1
        causal4 = jnp.logical_and(causal4, (t4 >> shift) == (u4 >> shift))
    att = jnp.where(causal4, att, 0.0).astype(BF16)
    vb = v.astype(BF16)
    kdb = kd.astype(BF16)
    yield

    o_inter = None
    r4 = lax.broadcasted_iota(jnp.int32, (GLA_HEADS * rows, GLA_DV), 0) & (rows - 1)
    for s in range(n_seq):
        st = get_state(s)
        oi = _dot(lhs, st.astype(BF16))
        if n_seq == 1:
            o_inter = oi
            kd_s = kdb
        else:
            m4 = jnp.logical_and(r4 >= s * seq_len, r4 < (s + 1) * seq_len)
            oi = jnp.where(m4, oi, 0.0)
            o_inter = oi if o_inter is None else o_inter + oi
            kd_s = jnp.where(_row_in_seq(rows, seq_len, s, GLA_KW), kdb, jnp.zeros_like(kdb))
        blocks = []
        for hp in range(GLA_HEADS // 2):
            upd = _dot_tn(kd_s[:, 2 * hp * GLA_DK:2 * (hp + 1) * GLA_DK],
                          vb[:, 2 * hp * GLA_DV:2 * (hp + 1) * GLA_DV])
            blocks += [upd[:GLA_DK, :GLA_DV], upd[GLA_DK:, GLA_DV:]]
        upd_d = jnp.concatenate(blocks, axis=0)
        c0 = s * seq_len
        decay = jnp.exp(jnp.broadcast_to(bl_t[:, c0:c0 + 1], (GLA_KW, GLA_DV)))
        put_state(s, decay * st + upd_d)
        if n_seq > 1 and s % 4 == 3:
            yield
    yield

    for h in range(GLA_HEADS):
        o_h = _dot(att[h * rows:(h + 1) * rows], vb[:, h * GLA_DV:(h + 1) * GLA_DV])
        outs.append(o_h + o_inter[h * rows:(h + 1) * rows])


def _ssd_phases(xs, bm, cm, dt_raw_blk, dtb_blk, a_blk, d_row, get_state, put_state, seq_len, ys):
    rows = xs.shape[0]
    n_seq = rows // seq_len
    same, tril = _seg_masks(rows, seq_len)
    tril_b = _mask_to_bf16(tril, rows)
    same_b = _mask_to_bf16(same, rows)

    dt = _softplus(dt_raw_blk + dtb_blk)
    dta = dt * a_blk
    lc = _dot3(tril_b, dta)
    if n_seq == 1:
        ll = jnp.broadcast_to(lc[rows - 1:rows, :], lc.shape)
    else:
        ll = _dot3(same_b, dta)
    elc = jnp.exp(lc)
    w = jnp.exp(ll - lc) * dt
    ell = jnp.exp(ll)
    lc_t = _pad_rows(lc, 128).T[:, :rows]
    yield

    bmb = bm.astype(BF16)
    cmb = cm.astype(BF16)
    cb = [_dot_nt(cmb[:, g * SSD_STATE:(g + 1) * SSD_STATE],
                  bmb[:, g * SSD_STATE:(g + 1) * SSD_STATE]) for g in range(SSD_GROUPS)]
    yield

    lane128 = lax.broadcasted_iota(jnp.int32, (rows, 128), 1)
    lo128 = lane128 < SSD_HEAD_DIM
    sub128 = lax.broadcasted_iota(jnp.int32, (128, 128), 0) < SSD_HEAD_DIM

    def col(x, h, width):
        return jnp.broadcast_to(x[:, DT_LANE + h:DT_LANE + h + 1], (x.shape[0], width))

    heads_per_group = SSD_HEADS // SSD_GROUPS
    for j in range(SSD_HEADS // 2):
        g = (2 * j) // heads_per_group
        xs_p = xs[:, 128 * j:128 * (j + 1)]
        dt_p = jnp.where(lo128, col(dt, 2 * j, 128), col(dt, 2 * j + 1, 128))
        xdt = xs_p * dt_p
        x_lo = jnp.where(lo128, xdt, 0.0).astype(BF16)
        x_hi = jnp.where(lo128, 0.0, xdt).astype(BF16)
        y_p = None
        for h, x_half in ((2 * j, x_lo), (2 * j + 1, x_hi)):
            seg = col(lc, h, rows) - jnp.broadcast_to(lc_t[DT_LANE + h:DT_LANE + h + 1, :], (rows, rows))
            w_h = jnp.where(tril, cb[g] * jnp.exp(seg), 0.0).astype(BF16)
            y_h = _dot(w_h, x_half)
            y_p = y_h if y_p is None else y_p + y_h
            yield

        elc_p = jnp.where(lo128, col(elc, 2 * j, 128), col(elc, 2 * j + 1, 128))
        w_p = jnp.where(lo128, col(w, 2 * j, 128), col(w, 2 * j + 1, 128))
        xw = (xs_p * w_p).astype(BF16)
        bm_g = bmb[:, g * SSD_STATE:(g + 1) * SSD_STATE]
        cm_g = cmb[:, g * SSD_STATE:(g + 1) * SSD_STATE]
        y_inter = None
        for s in range(n_seq):
            st = get_state(s, j)
            ci = _dot_nt(cm_g, st.astype(BF16))
            if n_seq == 1:
                y_inter = ci
                xw_s = xw
            else:
                msk = _row_in_seq(rows, seq_len, s, 128)
                ci = jnp.where(msk, ci, 0.0)
                y_inter = ci if y_inter is None else y_inter + ci
                xw_s = jnp.where(msk, xw, jnp.zeros_like(xw))
            upd = _dot_tn(xw_s, bm_g)
            r0 = s * seq_len
            e0 = jnp.broadcast_to(ell[r0:r0 + 1, DT_LANE + 2 * j:DT_LANE + 2 * j + 1], (128, 128))
            e1 = jnp.broadcast_to(ell[r0:r0 + 1, DT_LANE + 2 * j + 1:DT_LANE + 2 * j + 2], (128, 128))
            put_state(s, j, jnp.where(sub128, e0, e1) * st + upd)
        ys.append(y_p + y_inter * elc_p + d_row[:, 128 * j:128 * (j + 1)] * xs_p)
        yield


def _ssd_phases_seq(xs, bm, cm, dt_raw_blk, dtb_blk, a_blk, d_row, get_state, put_state, ys):
    rows = xs.shape[0]
    _, tril = _seg_masks(rows, rows)
    tril_b = _mask_to_bf16(tril, rows)
    src = lax.broadcasted_iota(jnp.int32, (rows, rows), 0)
    dst = lax.broadcasted_iota(jnp.int32, (rows, rows), 1)
    causal_t = src <= dst

    dt = _softplus(dt_raw_blk + dtb_blk)
    lc = _dot3(tril_b, dt * a_blk)
    lc_t = lc.T
    dt_t = dt.T
    elc_t = jnp.exp(lc_t)
    w_t = jnp.exp(lc_t[:, rows - 1:rows] - lc_t) * dt_t
    ell = jnp.exp(lc[rows - 1:rows, :])
    yield

    xs_t = xs.T
    bmb = bm.astype(BF16)
    cmb = cm.astype(BF16)
    cb_t = [_dot_nt(bmb[:, g * SSD_STATE:(g + 1) * SSD_STATE],
                    cmb[:, g * SSD_STATE:(g + 1) * SSD_STATE]) for g in range(SSD_GROUPS)]
    yield

    heads_per_group = SSD_HEADS // SSD_GROUPS
    y_t = []
    for h in range(SSD_HEADS):
        g = h // heads_per_group
        lane = DT_LANE + h
        xs_h = xs_t[h * SSD_HEAD_DIM:(h + 1) * SSD_HEAD_DIM, :]
        seg_t = lc_t[lane:lane + 1, :] - jnp.broadcast_to(lc[:, lane:lane + 1], (rows, rows))
        w_h = jnp.where(causal_t, cb_t[g] * jnp.exp(seg_t), 0.0).astype(BF16)
        y_h = _dot((xs_h * dt_t[lane:lane + 1, :]).astype(BF16), w_h)
        st = get_state(h)
        ci = _dot_nt(st.astype(BF16), cmb[:, g * SSD_STATE:(g + 1) * SSD_STATE])
        y_t.append(y_h + ci * elc_t[lane:lane + 1, :])
        upd = _dot((xs_h * w_t[lane:lane + 1, :]).astype(BF16), bmb[:, g * SSD_STATE:(g + 1) * SSD_STATE])
        put_state(h, jnp.broadcast_to(ell[:, lane:lane + 1], st.shape) * st + upd)
        yield
    ys.append(jnp.concatenate(y_t, axis=0).T + d_row * xs)


def _gate_logf(small_blk, gw2_ref, gb_ref):
    logits = _dot(_bf(small_blk), gw2_ref[...]) + gb_ref[...]
    return _log_sigmoid(logits) * (1.0 / GLA_GATE_TAU)


def _gla_finish(o_h, g_h, gnw):
    return _rms(o_h, gnw) * _silu(g_h)


def _ssd_finish(y, z, snw):
    y = y * _silu(z)
    gs = SSD_INNER // SSD_GROUPS
    parts = [_rms(y[:, g * gs:(g + 1) * gs], snw[:, g * gs:(g + 1) * gs]) for g in range(SSD_GROUPS)]
    return jnp.concatenate(parts, axis=1)


def _memkv_kernel(mem_ref, nw_ref, wk_ref, wv_ref, k_ref, v_ref, kb_ref, vb_ref):
    mn = _bf(_rms(mem_ref[0], nw_ref[...]))
    mk = _dot(mn, wk_ref[...])
    mv = _dot(mn, wv_ref[...])
    for h in range(XA_HEADS):
        k_ref[0, 0, :, h, :] = mk[:, h * XA_HEAD_DIM:(h + 1) * XA_HEAD_DIM]
        v_ref[0, 0, :, h, :] = mv[:, h * XA_HEAD_DIM:(h + 1) * XA_HEAD_DIM]
    kb_ref[0] = _bf(mk)
    vb_ref[0] = _bf(mv)


def _const_spec(shape):
    nd = len(shape)
    return pl.BlockSpec(shape, lambda *_: (0,) * nd, pipeline_mode=pl.Buffered(1))


def _memkv(mem, nw, wk, wv):
    b, m, d = mem.shape
    blk = pl.BlockSpec((1, m, d), lambda i: (i, 0, 0))
    cache_blk = pl.BlockSpec((1, 1, m, XA_HEADS, XA_HEAD_DIM), lambda i: (0, i, 0, 0, 0))
    cache_shape = jax.ShapeDtypeStruct((1, b, m, XA_HEADS, XA_HEAD_DIM), F32)
    return pl.pallas_call(
        _memkv_kernel,
        grid=(b,),
        in_specs=[blk, _const_spec((1, d)), _const_spec((d, d)), _const_spec((d, d))],
        out_specs=[cache_blk, cache_blk, blk, blk],
        out_shape=[cache_shape] * 2 + [jax.ShapeDtypeStruct((b, m, d), BF16)] * 2,
        compiler_params=pltpu.CompilerParams(dimension_semantics=("arbitrary",),
                                             vmem_limit_bytes=VMEM_LIMIT),
        name="mem_kv",
    )(mem, nw, wk, wv)


PROJ_CHUNK = 512


def _mixer_prompt_chain(x_ref, h_ref, proj_s, xpad_s, mixed_s, sg_s, ss_s,
                        lnpre_ref, win_ref, gw2_ref, gb_ref, gnw_ref, cw_ref, cb_ref,
                        dtb_ref, a_ref, d_ref, snw_ref, wout_ref, lnpost_ref):
    rows = TILE
    x = x_ref[...]
    hn = _rms(x, lnpre_ref[...]).astype(BF16)
    yield

    def proj_cols(c0, width):
        proj_s[:, c0:c0 + width] = _dot(hn, win_ref[:, c0:c0 + width])

    def late_proj():
        for c0 in (COL_Q, COL_V, COL_G, COL_Z):
            proj_cols(c0, PROJ_CHUNK)
            yield

    side = [late_proj()]

    def tick():
        if side[0] is not None:
            try:
                next(side[0])
            except StopIteration:
                side[0] = None

    for c in range(SSD_CONV_CH // PROJ_CHUNK):
        proj_cols(COL_XBC + c * PROJ_CHUNK, PROJ_CHUNK)
        yield
    proj_cols(COL_SMALL, 128)

    xa = []
    for c in range(SSD_CONV_CH // PROJ_CHUNK):
        cs = slice(c * PROJ_CHUNK, (c + 1) * PROJ_CHUNK)
        xpad_s[8:8 + rows, cs] = proj_s[:, COL_XBC + c * PROJ_CHUNK:COL_XBC + (c + 1) * PROJ_CHUNK]
        conv = cb_ref[:, cs]
        for j in range(SSD_CONV):
            conv = conv + xpad_s[5 + j:5 + j + rows, cs] * cw_ref[j:j + 1, cs]
        xpad_s[5:8, cs] = xpad_s[5 + rows:8 + rows, cs]
        xa.append(_silu(conv))
        tick()
        yield

    small = proj_s[:, COL_SMALL:COL_SMALL + 128]
    logf = _gate_logf(small, gw2_ref, gb_ref)

    def get_ss(h):
        return ss_s[SSD_HEAD_DIM * h:SSD_HEAD_DIM * (h + 1), :]

    def put_ss(h, val):
        ss_s[SSD_HEAD_DIM * h:SSD_HEAD_DIM * (h + 1), :] = val

    ys = []
    ssd = _ssd_phases_seq(xa[0], xa[1][:, :SSD_GROUPS * SSD_STATE], xa[1][:, SSD_GROUPS * SSD_STATE:],
                          small, dtb_ref[...], a_ref[...], d_ref[...], get_ss, put_ss, ys)

    def get_sg(s):
        return sg_s[...]

    def put_sg(s, val):
        sg_s[...] = val

    def gla_chunks():
        for c in range(rows // GLA_CHUNK):
            r0 = c * GLA_CHUNK
            sl = slice(r0, r0 + GLA_CHUNK)
            outs = []
            yield from _gla_phases(proj_s[sl, COL_Q:COL_Q + GLA_KW], proj_s[sl, COL_K:COL_K + GLA_KW],
                                   proj_s[sl, COL_V:COL_V + GLA_VW], logf[sl], get_sg, put_sg, GLA_CHUNK, outs)
            for h in range(GLA_HEADS):
                g_h = proj_s[sl, COL_G + h * GLA_DV:COL_G + (h + 1) * GLA_DV]
                mixed_s[sl, h * GLA_DV:(h + 1) * GLA_DV] = _gla_finish(outs[h], g_h, gnw_ref[...]).astype(BF16)
            yield

    while side[0] is not None:
        next(ssd)
        tick()
        yield
    alive = [ssd, gla_chunks()]
    while alive:
        for phases in list(alive):
            try:
                next(phases)
            except StopIteration:
                alive.remove(phases)
                continue
            yield
    mixed_s[:, GLA_VW:] = _ssd_finish(jnp.concatenate(ys, axis=1), proj_s[:, COL_Z:COL_Z + SSD_INNER],
                                      snw_ref[...]).astype(BF16)
    yield

    mixed = mixed_s[...]
    ms = []
    for c in range(D_MODEL // PROJ_CHUNK):
        ms.append(_dot(mixed, wout_ref[:, c * PROJ_CHUNK:(c + 1) * PROJ_CHUNK]))
        yield
    h_ref[...] = x + _rms(jnp.concatenate(ms, axis=1), lnpost_ref[...])


N_MIXER_WEIGHTS = 13
N_MIXER_SCRATCH = 5
MIX_OFFSET = 1


def _mixer_prompt_kernel(x_ref, *rest):
    consts = rest[:N_MIXER_WEIGHTS]
    h_ref, sg_out, ss_out, conv_out = rest[N_MIXER_WEIGHTS:N_MIXER_WEIGHTS + 4]
    scratch = rest[N_MIXER_WEIGHTS + 4:]
    per_seq = [scratch[i * N_MIXER_SCRATCH:(i + 1) * N_MIXER_SCRATCH] for i in range(MIX_SEQS)]
    t = pl.program_id(1)

    @pl.when(t == 0)
    def _():
        for _, xpad_s, _, sg_s, ss_s in per_seq:
            sg_s[...] = jnp.zeros_like(sg_s)
            ss_s[...] = jnp.zeros_like(ss_s)
            xpad_s[0:8, :] = jnp.zeros((8, SSD_CONV_CH), F32)

    _interleave([_mixer_prompt_chain(x_ref.at[i], h_ref.at[i], *per_seq[i], *consts) for i in range(MIX_SEQS)],
                offset=MIX_OFFSET)

    @pl.when(t == pl.num_programs(1) - 1)
    def _():
        for i, (_, xpad_s, _, sg_s, ss_s) in enumerate(per_seq):
            sg_out[i] = sg_s[...].reshape(GLA_HEADS, GLA_DK, GLA_DV)
            ss_out[i] = ss_s[...].reshape(SSD_HEADS, SSD_HEAD_DIM, SSD_STATE)
            conv_out[i] = xpad_s[5:8, :]


def _mixer_prompt(x, p):
    b, t, d = x.shape
    nt = t // TILE
    consts = [p["ln_mix_pre"], p["w_in"], p["gw2"], p["gb"], p["gnw"], p["conv_w"], p["conv_b"],
              p["dtb"], p["a_blk"], p["d_row"], p["snw"], p["w_out"], p["ln_mix_post"]]
    assert len(consts) == N_MIXER_WEIGHTS
    tok = pl.BlockSpec((MIX_SEQS, TILE, d), lambda i, j: (i, j, 0))
    return pl.pallas_call(
        _mixer_prompt_kernel,
        grid=(b // MIX_SEQS, nt),
        in_specs=[tok] + [_const_spec(c.shape) for c in consts],
        out_specs=[tok,
                   pl.BlockSpec((MIX_SEQS, GLA_HEADS, GLA_DK, GLA_DV), lambda i, j: (i, 0, 0, 0)),
                   pl.BlockSpec((MIX_SEQS, SSD_HEADS, SSD_HEAD_DIM, SSD_STATE), lambda i, j: (i, 0, 0, 0)),
                   pl.BlockSpec((MIX_SEQS, SSD_CONV - 1, SSD_CONV_CH), lambda i, j: (i, 0, 0))],
        out_shape=[jax.ShapeDtypeStruct((b, t, d), F32),
                   jax.ShapeDtypeStruct((b, GLA_HEADS, GLA_DK, GLA_DV), F32),
                   jax.ShapeDtypeStruct((b, SSD_HEADS, SSD_HEAD_DIM, SSD_STATE), F32),
                   jax.ShapeDtypeStruct((b, SSD_CONV - 1, SSD_CONV_CH), F32)],
        scratch_shapes=[pltpu.VMEM((TILE, PROJ_COLS), F32),
                        pltpu.VMEM((TILE + 8, SSD_CONV_CH), F32),
                        pltpu.VMEM((TILE, D_MODEL), BF16),
                        pltpu.VMEM((GLA_KW, GLA_DV), F32),
                        pltpu.VMEM((SSD_INNER, SSD_STATE), F32)] * MIX_SEQS,
        compiler_params=pltpu.CompilerParams(dimension_semantics=("arbitrary", "arbitrary"),
                                             vmem_limit_bytes=VMEM_LIMIT),
        name="mixer_prompt",
    )(x, *consts)


def _softmax_rows(s):
    e = jnp.exp(s - jnp.max(s, axis=-1, keepdims=True))
    return e / jnp.sum(e, axis=-1, keepdims=True)


FFN_CHUNK = 256
FFN_OFFSET = 1


def _attn_ffn_chain(h_ref, y_ref, mk_ref, mv_ref, lnxa_ref, wq_ref, wo_ref, lnxap_ref,
                    lnf_ref, lnfp_ref, wg_ref, wu_ref, wd_ref):
    h = h_ref[...]
    hn = _rms(h, lnxa_ref[...]).astype(BF16)
    yield
    heads_per_chunk = PROJ_CHUNK // XA_HEAD_DIM
    outs = []
    for c in range(D_MODEL // PROJ_CHUNK):
        q = _dot(hn, wq_ref[:, c * PROJ_CHUNK:(c + 1) * PROJ_CHUNK]).astype(BF16)
        yield
        for hh in range(heads_per_chunk):
            hd = c * heads_per_chunk + hh
            sl = slice(hd * XA_HEAD_DIM, (hd + 1) * XA_HEAD_DIM)
            s = _dot_nt(q[:, hh * XA_HEAD_DIM:(hh + 1) * XA_HEAD_DIM], mk_ref[0, :, sl])
            yield
            pr = _softmax_rows(s).astype(BF16)
            yield
            outs.append(_dot(pr, mv_ref[0, :, sl]).astype(BF16))
    a = jnp.concatenate(outs, axis=1)
    yield
    yield from _wo_ffn_phases(h, a, y_ref, wo_ref, lnxap_ref, lnf_ref, lnfp_ref, wg_ref, wu_ref, wd_ref)


def _wo_ffn_phases(h, a, y_ref, wo_ref, lnxap_ref, lnf_ref, lnfp_ref, wg_ref, wu_ref, wd_ref):
    parts = []
    for c in range(D_MODEL // PROJ_CHUNK):
        parts.append(_dot(a, wo_ref[:, c * PROJ_CHUNK:(c + 1) * PROJ_CHUNK]))
        yield
    h = h + _rms(jnp.concatenate(parts, axis=1), lnxap_ref[...])
    hf = _rms(h, lnf_ref[...]).astype(BF16)
    yield
    acts = []
    for c in range(D_FF // FFN_CHUNK):
        cs = slice(c * FFN_CHUNK, (c + 1) * FFN_CHUNK)
        gt = _dot(hf, wg_ref[:, cs])
        up = _dot(hf, wu_ref[:, cs])
        yield
        acts.append((_silu(gt) * up).astype(BF16))
    act = jnp.concatenate(acts, axis=1)
    yield
    parts = []
    for c in range(D_MODEL // PROJ_CHUNK):
        parts.append(_dot(act, wd_ref[:, c * PROJ_CHUNK:(c + 1) * PROJ_CHUNK]))
        yield
    y_ref[...] = h + _rms(jnp.concatenate(parts, axis=1), lnfp_ref[...])


def _attn_ffn_prompt_kernel(h_ref, mk_ref, mv_ref, *rest):
    consts, y_ref = rest[:-1], rest[-1]
    chains = []
    for sub in range(FFN_SUBTILES):
        rs = pl.ds(sub * TILE, TILE)
        chains.append(_attn_ffn_chain(h_ref.at[0, rs], y_ref.at[0, rs], mk_ref, mv_ref, *consts))
    _interleave(chains, offset=FFN_OFFSET)


def _attn_ffn_prompt(h, mkb, mvb, p):
    b, t, d = h.shape
    rows = FFN_SUBTILES * TILE
    consts = [p["ln_xa_pre"], p["w_xq"], p["w_xo"], p["ln_xa_post"], p["ln_ffn_pre"], p["ln_ffn_post"],
              p["w_gate"], p["w_up"], p["w_down"]]
    tok = pl.BlockSpec((1, rows, d), lambda i, j: (i, j, 0))
    mem = pl.BlockSpec((1, mkb.shape[1], d), lambda i, j: (i, 0, 0))
    return pl.pallas_call(
        _attn_ffn_prompt_kernel,
        grid=(b, t // rows),
        in_specs=[tok, mem, mem] + [_const_spec(c.shape) for c in consts],
        out_specs=tok,
        out_shape=jax.ShapeDtypeStruct((b, t, d), F32),
        compiler_params=pltpu.CompilerParams(dimension_semantics=("arbitrary", "arbitrary"),
                                             vmem_limit_bytes=VMEM_LIMIT),
        name="attn_ffn_prompt",
    )(h, mkb, mvb, *consts)


def _proj_sample_kernel(x_ref, lnpre_ref, win_ref, proj_ref):
    proj_ref[...] = _dot(_bf(_rms(x_ref[...], lnpre_ref[...])), win_ref[...])


def _proj_sample(x, p):
    n, d = x.shape
    return pl.pallas_call(
        _proj_sample_kernel,
        grid=(1,),
        in_specs=[_const_spec((n, d)), _const_spec((1, d)), _const_spec(p["w_in"].shape)],
        out_specs=pl.BlockSpec((n, PROJ_COLS), lambda i: (0, 0)),
        out_shape=jax.ShapeDtypeStruct((n, PROJ_COLS), F32),
        compiler_params=pltpu.CompilerParams(dimension_semantics=("arbitrary",),
                                             vmem_limit_bytes=VMEM_LIMIT),
        name="proj_sample",
    )(x, p["ln_mix_pre"], p["w_in"])


def _mixer_sample_kernel(seq_len, proj_ref, tail_ref, sg_ref, ss_ref, gw2_ref, gb_ref, gnw_ref,
                         cw_ref, cb_ref, dtb_ref, a_ref, d_ref, snw_ref,
                         mixed_ref, sg_out, ss_out, xpad_s, tpad_s):
    rows = proj_ref.shape[0]
    n_seq = rows // seq_len
    small = proj_ref[:, COL_SMALL:COL_SMALL + 128]
    logf = _gate_logf(small, gw2_ref, gb_ref)

    xbc = proj_ref[:, COL_XBC:COL_XBC + SSD_CONV_CH]
    xpad_s[0:8, :] = jnp.zeros((8, SSD_CONV_CH), F32)
    xpad_s[8:8 + rows, :] = xbc
    tpad_s[0:rows, :] = tail_ref[...]
    tpad_s[rows:rows + 8, :] = jnp.zeros((8, SSD_CONV_CH), F32)
    pos = lax.broadcasted_iota(jnp.int32, (rows, SSD_CONV_CH), 0) & (seq_len - 1)
    conv = cb_ref[...]
    for j in range(SSD_CONV - 1):
        back = SSD_CONV - 1 - j
        prev = jnp.where(pos >= back, xpad_s[8 - back:8 - back + rows, :],
                         tpad_s[seq_len - back:seq_len - back + rows, :])
        conv = conv + prev * cw_ref[j:j + 1, :]
    conv = conv + xbc * cw_ref[SSD_CONV - 1:SSD_CONV, :]
    xa = _silu(conv)

    def get_ss(s, j):
        return ss_ref[s, 2 * j:2 * j + 2].reshape(2 * SSD_HEAD_DIM, SSD_STATE)

    def put_ss(s, j, val):
        ss_out[s, 2 * j:2 * j + 2] = val.reshape(2, SSD_HEAD_DIM, SSD_STATE)

    ys = []
    _run(_ssd_phases(xa[:, :SSD_INNER], xa[:, SSD_INNER:SSD_INNER + 256], xa[:, SSD_INNER + 256:],
                     small, dtb_ref[...], a_ref[...], d_ref[...], get_ss, put_ss, seq_len, ys))
    mixed_ref[:, GLA_VW:] = _ssd_finish(jnp.concatenate(ys, axis=1), proj_ref[:, COL_Z:COL_Z + SSD_INNER],
                                        snw_ref[...])

    def get_sg(s):
        return sg_ref[s].reshape(GLA_KW, GLA_DV)

    def put_sg(s, val):
        sg_out[s] = val.reshape(GLA_HEADS, GLA_DK, GLA_DV)

    outs = []
    _run(_gla_phases(proj_ref[:, COL_Q:COL_Q + GLA_KW], proj_ref[:, COL_K:COL_K + GLA_KW],
                     proj_ref[:, COL_V:COL_V + GLA_VW], logf, get_sg, put_sg, seq_len, outs))
    for h in range(GLA_HEADS):
        g_h = proj_ref[:, COL_G + h * GLA_DV:COL_G + (h + 1) * GLA_DV]
        mixed_ref[:, h * GLA_DV:(h + 1) * GLA_DV] = _gla_finish(outs[h], g_h, gnw_ref[...])


def _mixer_sample(proj, tail, sg, ss, p, seq_len):
    n = proj.shape[0]
    rows = SAMPLE_SEQS * seq_len
    nb = sg.shape[0]
    consts = [p["gw2"], p["gb"], p["gnw"], p["conv_w"], p["conv_b"], p["dtb"], p["a_blk"], p["d_row"],
              p["snw"]]
    sg_spec = pl.BlockSpec((SAMPLE_SEQS, GLA_HEADS, GLA_DK, GLA_DV), lambda i: (i, 0, 0, 0))
    ss_spec = pl.BlockSpec((SAMPLE_SEQS, SSD_HEADS, SSD_HEAD_DIM, SSD_STATE), lambda i: (i, 0, 0, 0))
    return pl.pallas_call(
        functools.partial(_mixer_sample_kernel, seq_len),
        grid=(nb // SAMPLE_SEQS,),
        in_specs=[pl.BlockSpec((rows, PROJ_COLS), lambda i: (i, 0)),
                  pl.BlockSpec((rows, SSD_CONV_CH), lambda i: (i, 0)),
                  sg_spec, ss_spec] + [_const_spec(c.shape) for c in consts],
        out_specs=[pl.BlockSpec((rows, D_MODEL), lambda i: (i, 0)), sg_spec, ss_spec],
        out_shape=[jax.ShapeDtypeStruct((n, D_MODEL), F32),
                   jax.ShapeDtypeStruct(sg.shape, F32), jax.ShapeDtypeStruct(ss.shape, F32)],
        scratch_shapes=[pltpu.VMEM((rows + 8, SSD_CONV_CH), F32),
                        pltpu.VMEM((rows + 8, SSD_CONV_CH), F32)],
        compiler_params=pltpu.CompilerParams(dimension_semantics=("arbitrary",),
                                             vmem_limit_bytes=VMEM_LIMIT),
        name="mixer_sample",
    )(proj, tail, sg, ss, *consts)


def _post_mix_sample_kernel(x_ref, mixed_ref, wout_ref, lnpost_ref, lnxa_ref, wq_ref, h_ref, q_ref):
    m = _dot(_bf(mixed_ref[...]), wout_ref[...])
    h = x_ref[...] + _rms(m, lnpost_ref[...])
    h_ref[...] = h
    q_ref[...] = _bf(_dot(_bf(_rms(h, lnxa_ref[...])), wq_ref[...]))


def _post_mix_sample(x, mixed, p):
    n, d = x.shape
    full = pl.BlockSpec((n, d), lambda i: (0, 0))
    consts = [p["w_out"], p["ln_mix_post"], p["ln_xa_pre"], p["w_xq"]]
    return pl.pallas_call(
        _post_mix_sample_kernel,
        grid=(1,),
        in_specs=[full, full] + [_const_spec(c.shape) for c in consts],
        out_specs=[full, full],
        out_shape=[jax.ShapeDtypeStruct((n, d), F32), jax.ShapeDtypeStruct((n, d), BF16)],
        compiler_params=pltpu.CompilerParams(dimension_semantics=("arbitrary",),
                                             vmem_limit_bytes=VMEM_LIMIT),
        name="post_mix_sample",
    )(x, mixed, *consts)


def _attn_sample_kernel(seq_len, q_ref, k_hbm, v_hbm, a_ref, kbuf, vbuf, sem):
    step = pl.program_id(0)
    rows = q_ref.shape[0]
    n_seq = rows // seq_len

    def copies(at_step, slot):
        out = []
        for s in range(n_seq):
            for hd in range(XA_HEADS):
                b = at_step * n_seq + s
                out.append(pltpu.make_async_copy(k_hbm.at[0, b, :, hd, :], kbuf.at[slot, s, hd], sem.at[0, slot]))
                out.append(pltpu.make_async_copy(v_hbm.at[0, b, :, hd, :], vbuf.at[slot, s, hd], sem.at[1, slot]))
        return out

    @pl.when(step == 0)
    def _():
        for cp in copies(0, 0):
            cp.start()

    @pl.when(step + 1 < pl.num_programs(0))
    def _():
        for cp in copies(step + 1, (step + 1) % 2):
            cp.start()

    slot = step % 2
    for cp in copies(step, slot):
        cp.wait()

    q = q_ref[...]
    results = {}

    def one(s, hd):
        sc = _dot_nt(q[:, hd * XA_HEAD_DIM:(hd + 1) * XA_HEAD_DIM], _bf(kbuf[slot, s, hd]))
        yield
        e = jnp.exp(sc - jnp.max(sc, axis=-1, keepdims=True))
        pr = _bf(e / jnp.sum(e, axis=-1, keepdims=True))
        yield
        results[s, hd] = _dot(pr, _bf(vbuf[slot, s, hd]))

    _interleave([one(s, hd) for s in range(n_seq) for hd in range(XA_HEADS)])
    for hd in range(XA_HEADS):
        sl = slice(hd * XA_HEAD_DIM, (hd + 1) * XA_HEAD_DIM)
        out = results[0, hd]
        for s in range(1, n_seq):
            out = jnp.where(_row_in_seq(rows, seq_len, s, XA_HEAD_DIM), results[s, hd], out)
        a_ref[:, sl] = out


def _attn_sample(q, ck, cv, seq_len):
    n, d = q.shape
    _, nb, m, nh, hd = ck.shape
    rows = ATTN_SEQS * seq_len
    tok = pl.BlockSpec((rows, d), lambda i: (i, 0))
    hbm = pl.BlockSpec(memory_space=pl.ANY)
    return pl.pallas_call(
        functools.partial(_attn_sample_kernel, seq_len),
        grid=(nb // ATTN_SEQS,),
        in_specs=[tok, hbm, hbm],
        out_specs=tok,
        out_shape=jax.ShapeDtypeStruct((n, d), F32),
        scratch_shapes=[pltpu.VMEM((2, ATTN_SEQS, nh, m, hd), F32),
                        pltpu.VMEM((2, ATTN_SEQS, nh, m, hd), F32),
                        pltpu.SemaphoreType.DMA((2, 2))],
        compiler_params=pltpu.CompilerParams(dimension_semantics=("arbitrary",),
                                             vmem_limit_bytes=VMEM_LIMIT),
        name="attn_sample",
    )(q, ck, cv)


def _post_attn_sample_kernel(h_ref, a_ref, *rest):
    consts, y_ref = rest[:-1], rest[-1]
    chains = []
    for sub in range(h_ref.shape[0] // TILE):
        rs = pl.ds(sub * TILE, TILE)
        chains.append(_wo_ffn_phases(h_ref[rs, :], _bf(a_ref[rs, :]), y_ref.at[rs], *consts))
    _interleave(chains, offset=FFN_OFFSET)


def _post_attn_sample(h, a, p):
    n, d = h.shape
    full = pl.BlockSpec((n, d), lambda i: (0, 0))
    consts = [p["w_xo"], p["ln_xa_post"], p["ln_ffn_pre"], p["ln_ffn_post"], p["w_gate"], p["w_up"],
              p["w_down"]]
    return pl.pallas_call(
        _post_attn_sample_kernel,
        grid=(1,),
        in_specs=[full, full] + [_const_spec(c.shape) for c in consts],
        out_specs=full,
        out_shape=jax.ShapeDtypeStruct((n, d), F32),
        compiler_params=pltpu.CompilerParams(dimension_semantics=("arbitrary",),
                                             vmem_limit_bytes=VMEM_LIMIT),
        name="post_attn_sample",
    )(h, a, *consts)


SRC_GLR = GLA_KW + GLA_KW + GLA_VW + GLA_VW
SRC_Z = SRC_GLR + GLA_GATE_RANK
SRC_XBC = SRC_Z + SSD_INNER
SRC_DT = SRC_XBC + SSD_CONV_CH
IN_COLS = SRC_DT + SSD_HEADS


def _pack_w_in_kernel(wt_hbm, o_ref, buf, sem):
    pieces = [(0, COL_Q, SRC_GLR),
              (SRC_Z, COL_Z, SSD_INNER),
              (SRC_XBC, COL_XBC, SSD_CONV_CH),
              (SRC_GLR, COL_SMALL, GLA_GATE_RANK),
              (SRC_DT, COL_SMALL + DT_LANE, SSD_HEADS)]
    copies = [pltpu.make_async_copy(wt_hbm.at[0, pl.ds(src, n), :], buf.at[pl.ds(dst, n), :], sem.at[0])
              for src, dst, n in pieces]
    for cp in copies:
        cp.start()
    used = COL_SMALL + DT_LANE + SSD_HEADS
    buf[used:PROJ_COLS, :] = jnp.zeros((PROJ_COLS - used, D_MODEL), F32)
    for cp in copies:
        cp.wait()
    for j in range(PROJ_COLS // 128):
        blk = buf[128 * j:128 * (j + 1), :]
        if 128 * j < COL_K:
            blk = blk * (GLA_DK ** -0.5)
        o_ref[:, 128 * j:128 * (j + 1)] = blk.T.astype(BF16)


def _pack_w_in(w_in):
    wt = jnp.swapaxes(w_in, 1, 2)
    return pl.pallas_call(
        _pack_w_in_kernel,
        grid=(1,),
        in_specs=[pl.BlockSpec(memory_space=pl.ANY)],
        out_specs=pl.BlockSpec((D_MODEL, PROJ_COLS), lambda i: (0, 0)),
        out_shape=jax.ShapeDtypeStruct((D_MODEL, PROJ_COLS), BF16),
        scratch_shapes=[pltpu.VMEM((PROJ_COLS, D_MODEL), F32), pltpu.SemaphoreType.DMA((1,))],
        compiler_params=pltpu.CompilerParams(dimension_semantics=("arbitrary",),
                                             vmem_limit_bytes=VMEM_LIMIT),
        name="pack_w_in",
    )(wt)


def _pack_params(ln_mix_pre, ln_mix_post, w_in_packed, gla_gate_w2, gla_gate_b, gla_norm_w, ssd_conv_w,
                 ssd_conv_b, ssd_dt_bias, ssd_A_log, ssd_D, ssd_norm_w, w_out, ln_xa_pre, ln_xa_post,
                 mem_norm_w, w_xq, w_xk, w_xv, w_xo, ln_ffn_pre, ln_ffn_post, w_gate, w_up, w_down):

    def small_row(v):
        return jnp.zeros((1, 128), F32).at[0, DT_LANE:DT_LANE + SSD_HEADS].set(v)

    row = lambda v: v.reshape(1, -1)
    return {
        "ln_mix_pre": row(ln_mix_pre), "ln_mix_post": row(ln_mix_post),
        "w_in": w_in_packed,
        "gw2": _bf(jnp.zeros((128, GLA_KW), F32).at[:GLA_GATE_RANK].set(gla_gate_w2)),
        "gb": row(gla_gate_b), "gnw": row(gla_norm_w),
        "conv_w": ssd_conv_w, "conv_b": row(ssd_conv_b),
        "dtb": small_row(ssd_dt_bias),
        "a_blk": small_row(-jnp.exp(ssd_A_log)),
        "d_row": jnp.repeat(ssd_D, SSD_HEAD_DIM).reshape(1, SSD_INNER),
        "snw": row(ssd_norm_w),
        "w_out": _bf(w_out),
        "ln_xa_pre": row(ln_xa_pre), "ln_xa_post": row(ln_xa_post), "mem_norm_w": row(mem_norm_w),
        "w_xq": _bf(w_xq * (XA_HEAD_DIM ** -0.5)), "w_xk": _bf(w_xk), "w_xv": _bf(w_xv), "w_xo": _bf(w_xo),
        "ln_ffn_pre": row(ln_ffn_pre), "ln_ffn_post": row(ln_ffn_post),
        "w_gate": _bf(w_gate), "w_up": _bf(w_up), "w_down": _bf(w_down),
    }


def kernel(x_prompt, x_sample, mem_prompt, state_gla, state_ssm, state_conv, cache_mem_k, cache_mem_v,
           ln_mix_pre, ln_mix_post, w_in, gla_gate_w2, gla_gate_b, gla_norm_w, ssd_conv_w, ssd_conv_b,
           ssd_dt_bias, ssd_A_log, ssd_D, ssd_norm_w, w_out, ln_xa_pre, ln_xa_post, mem_norm_w,
           w_xq, w_xk, w_xv, w_xo, ln_ffn_pre, ln_ffn_post, w_gate, w_up, w_down):
    assert w_in.shape[0] == 1, "single-layer kernel"
    layer = [a[0] for a in (ln_mix_pre, ln_mix_post, w_in, gla_gate_w2, gla_gate_b, gla_norm_w, ssd_conv_w,
                            ssd_conv_b, ssd_dt_bias, ssd_A_log, ssd_D, ssd_norm_w, w_out, ln_xa_pre,
                            ln_xa_post, mem_norm_w, w_xq, w_xk, w_xv, w_xo, ln_ffn_pre, ln_ffn_post,
                            w_gate, w_up, w_down)]
    layer[2] = _pack_w_in(w_in)
    p = _pack_params(*layer)
    bp, mem_len, _ = mem_prompt.shape
    bs, ts, _ = x_sample.shape
    assert ts == SSD_CONV, "the padded conv-tail layout assumes one new row per conv tap"

    mk, mv, mkb, mvb = _memkv(mem_prompt, p["mem_norm_w"], p["w_xk"], p["w_xv"])
    h_p, gla_p, ssm_p, conv_p = _mixer_prompt(x_prompt, p)
    y_p = _attn_ffn_prompt(h_p, mkb, mvb, p)

    xs = x_sample.reshape(bs * ts, D_MODEL)
    proj = _proj_sample(xs, p)
    tail = jnp.pad(state_conv[0], ((0, 0), (1, 0), (0, 0))).reshape(bs * ts, SSD_CONV_CH)
    mixed, gla_s, ssm_s = _mixer_sample(proj, tail, state_gla[0], state_ssm[0], p, ts)
    h_s, q_s = _post_mix_sample(xs, mixed, p)
    a_s = _attn_sample(q_s, cache_mem_k, cache_mem_v, ts)
    y_s = _post_attn_sample(h_s, a_s, p)
    conv_s = proj[:, COL_XBC:COL_XBC + SSD_CONV_CH].reshape(bs, ts, SSD_CONV_CH)[:, ts - (SSD_CONV - 1):]

    return (y_p, y_s.reshape(bs, ts, D_MODEL), gla_p[None], ssm_p[None], conv_p[None],
            mk, mv, gla_s[None], ssm_s[None], conv_s[None])
```

```python
import functools

import jax
import jax.numpy as jnp
from jax import lax
from jax.experimental import pallas as pl
from jax.experimental.pallas import tpu as pltpu

F32 = jnp.float32
BF16 = jnp.bfloat16

D_MODEL = 1024
GLA_HEADS = 4
GLA_DK = 64
GLA_DV = 128
GLA_KW = GLA_HEADS * GLA_DK
GLA_VW = GLA_HEADS * GLA_DV
GLA_GATE_RANK = 16
GLA_GATE_TAU = 16.0
SSD_INNER = 512
SSD_HEAD_DIM = 64
SSD_HEADS = 8
SSD_GROUPS = 2
SSD_STATE = 128
SSD_CONV = 4
SSD_CONV_CH = 1024
XA_HEADS = 4
XA_HEAD_DIM = 256
D_FF = 2816
EPS = 1e-6

COL_Q = 0
COL_K = 256
COL_V = 512
COL_G = 1024
COL_Z = 1536
COL_XBC = 2048
COL_SMALL = 3072
DT_LANE = GLA_GATE_RANK
PROJ_COLS = 3200

GLA_CHUNK = 64
TILE = 256
FFN_SUBTILES = 4
MIX_SEQS = 4
SAMPLE_SEQS = 16
ATTN_SEQS = 4
VMEM_LIMIT = 56 * 1024 * 1024


def _bf(x):
    return x.astype(BF16)


def _dot(a, b):
    return jnp.dot(a, b, preferred_element_type=F32)


def _dot_nt(a, b):
    return lax.dot_general(a, b, (((1,), (1,)), ((), ())), preferred_element_type=F32)


def _dot_tn(a, b):
    return lax.dot_general(a, b, (((0,), (0,)), ((), ())), preferred_element_type=F32)


def _dot3(m, a):
    hi = _bf(a)
    r = a - hi.astype(F32)
    mid = _bf(r)
    lo = _bf(r - mid.astype(F32))
    return _dot(m, hi) + _dot(m, mid) + _dot(m, lo)


def _rms(x, w):
    ms = jnp.mean(x * x, axis=-1, keepdims=True)
    return x * lax.rsqrt(ms + EPS) * w


def _silu(x):
    return x / (1.0 + jnp.exp(-x))


def _softplus(x):
    e = jnp.exp(-jnp.abs(x))
    u = 1.0 + e
    log1p_e = jnp.where(u == 1.0, e, jnp.log(u) * (e / (u - 1.0)))
    return jnp.maximum(x, 0.0) + log1p_e


def _log_sigmoid(x):
    return jnp.minimum(x, 0.0) - jnp.log(1.0 + jnp.exp(-jnp.abs(x)))


def _seg_masks(rows, seq_len):
    t = lax.broadcasted_iota(jnp.int32, (rows, rows), 0)
    u = lax.broadcasted_iota(jnp.int32, (rows, rows), 1)
    if seq_len == rows:
        same = None
        tril = u <= t
    else:
        shift = seq_len.bit_length() - 1
        assert 1 << shift == seq_len
        same = (t >> shift) == (u >> shift)
        tril = jnp.logical_and(same, u <= t)
    return same, tril


def _mask_to_bf16(mask, rows):
    if mask is None:
        return jnp.ones((rows, rows), BF16)
    return jnp.where(mask, 1.0, 0.0).astype(BF16)


def _row_in_seq(rows, seq_len, s, width):
    r = lax.broadcasted_iota(jnp.int32, (rows, width), 0)
    return jnp.logical_and(r >= s * seq_len, r < (s + 1) * seq_len)


def _interleave(chains, offset=0):
    waiting = list(enumerate(chains))
    active = []
    rnd = 0
    while waiting or active:
        while waiting and waiting[0][0] * offset <= rnd:
            active.append(waiting.pop(0)[1])
        alive = []
        for c in active:
            try:
                next(c)
                alive.append(c)
            except StopIteration:
                pass
        active = alive
        rnd += 1


def _run(phases):
    for _ in phases:
        pass


def _pad_rows(x, rows):
    if x.shape[0] >= rows:
        return x
    return jnp.concatenate([x, jnp.zeros((rows - x.shape[0], x.shape[1]), x.dtype)], axis=0)


def _gla_phases(q, k, v, lf, get_state, put_state, seq_len, outs):
    rows = q.shape[0]
    n_seq = rows // seq_len
    same, tril = _seg_masks(rows, seq_len)
    tril_b = _mask_to_bf16(tril, rows)
    same_b = _mask_to_bf16(same, rows)

    b = _dot3(tril_b, lf)
    if n_seq == 1:
        bl = jnp.broadcast_to(b[rows - 1:rows, :], b.shape)
    else:
        bl = _dot3(same_b, lf)
    qt = q * jnp.exp(b)
    kt = k * jnp.exp(-b)
    kd = k * jnp.exp(bl - b)
    bl_t = _pad_rows(bl, 128).T
    yield

    lane = lax.broadcasted_iota(jnp.int32, (rows, GLA_KW), 1)
    lhs = jnp.concatenate(
        [jnp.where((lane >= h * GLA_DK) & (lane < (h + 1) * GLA_DK), qt, 0.0)
         for h in range(GLA_HEADS)], axis=0).astype(BF16)
    att = _dot_nt(lhs, kt.astype(BF16))
    t4 = lax.broadcasted_iota(jnp.int32, (GLA_HEADS * rows, rows), 0) & (rows - 1)
    u4 = lax.broadcasted_iota(jnp.int32, (GLA_HEADS * rows, rows), 1)
    causal4 = u4 <= t4
    if n_seq > 1:
        shift = seq_len.bit_length() - 1
        causal4 = jnp.logical_and(causal4, (t4 >> shift) == (u4 >> shift))
    att = jnp.where(causal4, att, 0.0).astype(BF16)
    vb = v.astype(BF16)
    kdb = kd.astype(BF16)
    yield

    o_inter = None
    r4 = lax.broadcasted_iota(jnp.int32, (GLA_HEADS * rows, GLA_DV), 0) & (rows - 1)
    for s in range(n_seq):
        st = get_state(s)
        oi = _dot(lhs, st.astype(BF16))
        if n_seq == 1:
            o_inter = oi
            kd_s = kdb
        else:
            m4 = jnp.logical_and(r4 >= s * seq_len, r4 < (s + 1) * seq_len)
            oi = jnp.where(m4, oi, 0.0)
            o_inter = oi if o_inter is None else o_inter + oi
            kd_s = jnp.where(_row_in_seq(rows, seq_len, s, GLA_KW), kdb, jnp.zeros_like(kdb))
        blocks = []
        for hp in range(GLA_HEADS // 2):
            upd = _dot_tn(kd_s[:, 2 * hp * GLA_DK:2 * (hp + 1) * GLA_DK],
                          vb[:, 2 * hp * GLA_DV:2 * (hp + 1) * GLA_DV])
            blocks += [upd[:GLA_DK, :GLA_DV], upd[GLA_DK:, GLA_DV:]]
        upd_d = jnp.concatenate(blocks, axis=0)
        c0 = s * seq_len
        decay = jnp.exp(jnp.broadcast_to(bl_t[:, c0:c0 + 1], (GLA_KW, GLA_DV)))
        put_state(s, decay * st + upd_d)
        if n_seq > 1 and s % 4 == 3:
            yield
    yield

    for h in range(GLA_HEADS):
        o_h = _dot(att[h * rows:(h + 1) * rows], vb[:, h * GLA_DV:(h + 1) * GLA_DV])
        outs.append(o_h + o_inter[h * rows:(h + 1) * rows])


def _ssd_phases(xs, bm, cm, dt_raw_blk, dtb_blk, a_blk, d_row, get_state, put_state, seq_len, ys):
    rows = xs.shape[0]
    n_seq = rows // seq_len
    same, tril = _seg_masks(rows, seq_len)
    tril_b = _mask_to_bf16(tril, rows)
    same_b = _mask_to_bf16(same, rows)

    dt = _softplus(dt_raw_blk + dtb_blk)
    dta = dt * a_blk
    lc = _dot3(tril_b, dta)
    if n_seq == 1:
        ll = jnp.broadcast_to(lc[rows - 1:rows, :], lc.shape)
    else:
        ll = _dot3(same_b, dta)
    elc = jnp.exp(lc)
    w = jnp.exp(ll - lc) * dt
    ell = jnp.exp(ll)
    lc_t = _pad_rows(lc, 128).T[:, :rows]
    yield

    bmb = bm.astype(BF16)
    cmb = cm.astype(BF16)
    cb = [_dot_nt(cmb[:, g * SSD_STATE:(g + 1) * SSD_STATE],
                  bmb[:, g * SSD_STATE:(g + 1) * SSD_STATE]) for g in range(SSD_GROUPS)]
    yield

    lane128 = lax.broadcasted_iota(jnp.int32, (rows, 128), 1)
    lo128 = lane128 < SSD_HEAD_DIM
    sub128 = lax.broadcasted_iota(jnp.int32, (128, 128), 0) < SSD_HEAD_DIM

    def col(x, h, width):
        return jnp.broadcast_to(x[:, DT_LANE + h:DT_LANE + h + 1], (x.shape[0], width))

    heads_per_group = SSD_HEADS // SSD_GROUPS
    for j in range(SSD_HEADS // 2):
        g = (2 * j) // heads_per_group
        xs_p = xs[:, 128 * j:128 * (j + 1)]
        dt_p = jnp.where(lo128, col(dt, 2 * j, 128), col(dt, 2 * j + 1, 128))
        xdt = xs_p * dt_p
        x_lo = jnp.where(lo128, xdt, 0.0).astype(BF16)
        x_hi = jnp.where(lo128, 0.0, xdt).astype(BF16)
        y_p = None
        for h, x_half in ((2 * j, x_lo), (2 * j + 1, x_hi)):
            seg = col(lc, h, rows) - jnp.broadcast_to(lc_t[DT_LANE + h:DT_LANE + h + 1, :], (rows, rows))
            w_h = jnp.where(tril, cb[g] * jnp.exp(seg), 0.0).astype(BF16)
            y_h = _dot(w_h, x_half)
            y_p = y_h if y_p is None else y_p + y_h
            yield

        elc_p = jnp.where(lo128, col(elc, 2 * j, 128), col(elc, 2 * j + 1, 128))
        w_p = jnp.where(lo128, col(w, 2 * j, 128), col(w, 2 * j + 1, 128))
        xw = (xs_p * w_p).astype(BF16)
        bm_g = bmb[:, g * SSD_STATE:(g + 1) * SSD_STATE]
        cm_g = cmb[:, g * SSD_STATE:(g + 1) * SSD_STATE]
        y_inter = None
        for s in range(n_seq):
            st = get_state(s, j)
            ci = _dot_nt(cm_g, st.astype(BF16))
            if n_seq == 1:
                y_inter = ci
                xw_s = xw
            else:
                msk = _row_in_seq(rows, seq_len, s, 128)
                ci = jnp.where(msk, ci, 0.0)
                y_inter = ci if y_inter is None else y_inter + ci
                xw_s = jnp.where(msk, xw, jnp.zeros_like(xw))
            upd = _dot_tn(xw_s, bm_g)
            r0 = s * seq_len
            e0 = jnp.broadcast_to(ell[r0:r0 + 1, DT_LANE + 2 * j:DT_LANE + 2 * j + 1], (128, 128))
            e1 = jnp.broadcast_to(ell[r0:r0 + 1, DT_LANE + 2 * j + 1:DT_LANE + 2 * j + 2], (128, 128))
            put_state(s, j, jnp.where(sub128, e0, e1) * st + upd)
        ys.append(y_p + y_inter * elc_p + d_row[:, 128 * j:128 * (j + 1)] * xs_p)
        yield


def _ssd_phases_seq(xs, bm, cm, dt_raw_blk, dtb_blk, a_blk, d_row, get_state, put_state, ys):
    rows = xs.shape[0]
    _, tril = _seg_masks(rows, rows)
    tril_b = _mask_to_bf16(tril, rows)
    src = lax.broadcasted_iota(jnp.int32, (rows, rows), 0)
    dst = lax.broadcasted_iota(jnp.int32, (rows, rows), 1)
    causal_t = src <= dst

    dt = _softplus(dt_raw_blk + dtb_blk)
    lc = _dot3(tril_b, dt * a_blk)
    lc_t = lc.T
    dt_t = dt.T
    elc_t = jnp.exp(lc_t)
    w_t = jnp.exp(lc_t[:, rows - 1:rows] - lc_t) * dt_t
    ell = jnp.exp(lc[rows - 1:rows, :])
    yield

    xs_t = xs.T
    bmb = bm.astype(BF16)
    cmb = cm.astype(BF16)
    cb_t = [_dot_nt(bmb[:, g * SSD_STATE:(g + 1) * SSD_STATE],
                    cmb[:, g * SSD_STATE:(g + 1) * SSD_STATE]) for g in range(SSD_GROUPS)]
    yield

    heads_per_group = SSD_HEADS // SSD_GROUPS
    y_t = []
    for h in range(SSD_HEADS):
        g = h // heads_per_group
        lane = DT_LANE + h
        xs_h = xs_t[h * SSD_HEAD_DIM:(h + 1) * SSD_HEAD_DIM, :]
        seg_t = lc_t[lane:lane + 1, :] - jnp.broadcast_to(lc[:, lane:lane + 1], (rows, rows))
        w_h = jnp.where(causal_t, cb_t[g] * jnp.exp(seg_t), 0.0).astype(BF16)
        y_h = _dot((xs_h * dt_t[lane:lane + 1, :]).astype(BF16), w_h)
        st = get_state(h)
        ci = _dot_nt(st.astype(BF16), cmb[:, g * SSD_STATE:(g + 1) * SSD_STATE])
        y_t.append(y_h + ci * elc_t[lane:lane + 1, :])
        upd = _dot((xs_h * w_t[lane:lane + 1, :]).astype(BF16), bmb[:, g * SSD_STATE:(g + 1) * SSD_STATE])
        put_state(h, jnp.broadcast_to(ell[:, lane:lane + 1], st.shape) * st + upd)
        yield
    ys.append(jnp.concatenate(y_t, axis=0).T + d_row * xs)


def _gate_logf(small_blk, gw2_ref, gb_ref):
    logits = _dot(_bf(small_blk), gw2_ref[...]) + gb_ref[...]
    return _log_sigmoid(logits) * (1.0 / GLA_GATE_TAU)


def _gla_finish(o_h, g_h, gnw):
    return _rms(o_h, gnw) * _silu(g_h)


def _ssd_finish(y, z, snw):
    y = y * _silu(z)
    gs = SSD_INNER // SSD_GROUPS
    parts = [_rms(y[:, g * gs:(g + 1) * gs], snw[:, g * gs:(g + 1) * gs]) for g in range(SSD_GROUPS)]
    return jnp.concatenate(parts, axis=1)


def _memkv_kernel(mem_ref, nw_ref, wk_ref, wv_ref, k_ref, v_ref, kb_ref, vb_ref):
    mn = _bf(_rms(mem_ref[0], nw_ref[...]))
    mk = _dot(mn, wk_ref[...])
    mv = _dot(mn, wv_ref[...])
    for h in range(XA_HEADS):
        k_ref[0, 0, :, h, :] = mk[:, h * XA_HEAD_DIM:(h + 1) * XA_HEAD_DIM]
        v_ref[0, 0, :, h, :] = mv[:, h * XA_HEAD_DIM:(h + 1) * XA_HEAD_DIM]
    kb_ref[0] = _bf(mk)
    vb_ref[0] = _bf(mv)


def _const_spec(shape):
    nd = len(shape)
    return pl.BlockSpec(shape, lambda *_: (0,) * nd, pipeline_mode=pl.Buffered(1))


def _memkv(mem, nw, wk, wv):
    b, m, d = mem.shape
    blk = pl.BlockSpec((1, m, d), lambda i: (i, 0, 0))
    cache_blk = pl.BlockSpec((1, 1, m, XA_HEADS, XA_HEAD_DIM), lambda i: (0, i, 0, 0, 0))
    cache_shape = jax.ShapeDtypeStruct((1, b, m, XA_HEADS, XA_HEAD_DIM), F32)
    return pl.pallas_call(
        _memkv_kernel,
        grid=(b,),
        in_specs=[blk, _const_spec((1, d)), _const_spec((d, d)), _const_spec((d, d))],
        out_specs=[cache_blk, cache_blk, blk, blk],
        out_shape=[cache_shape] * 2 + [jax.ShapeDtypeStruct((b, m, d), BF16)] * 2,
        compiler_params=pltpu.CompilerParams(dimension_semantics=("arbitrary",),
                                             vmem_limit_bytes=VMEM_LIMIT),
        name="mem_kv",
    )(mem, nw, wk, wv)


PROJ_CHUNK = 512


def _mixer_prompt_chain(x_ref, h_ref, proj_s, xpad_s, mixed_s, sg_s, ss_s,
                        lnpre_ref, win_ref, gw2_ref, gb_ref, gnw_ref, cw_ref, cb_ref,
                        dtb_ref, a_ref, d_ref, snw_ref, wout_ref, lnpost_ref):
    rows = TILE
    x = x_ref[...]
    hn = _rms(x, lnpre_ref[...]).astype(BF16)
    yield

    def proj_cols(c0, width):
        proj_s[:, c0:c0 + width] = _dot(hn, win_ref[:, c0:c0 + width])

    def late_proj():
        for c0 in (COL_Q, COL_V, COL_G, COL_Z):
            proj_cols(c0, PROJ_CHUNK)
            yield

    side = [late_proj()]

    def tick():
        if side[0] is not None:
            try:
                next(side[0])
            except StopIteration:
                side[0] = None

    for c in range(SSD_CONV_CH // PROJ_CHUNK):
        proj_cols(COL_XBC + c * PROJ_CHUNK, PROJ_CHUNK)
        yield
    proj_cols(COL_SMALL, 128)

    xa = []
    for c in range(SSD_CONV_CH // PROJ_CHUNK):
        cs = slice(c * PROJ_CHUNK, (c + 1) * PROJ_CHUNK)
        xpad_s[8:8 + rows, cs] = proj_s[:, COL_XBC + c * PROJ_CHUNK:COL_XBC + (c + 1) * PROJ_CHUNK]
        conv = cb_ref[:, cs]
        for j in range(SSD_CONV):
            conv = conv + xpad_s[5 + j:5 + j + rows, cs] * cw_ref[j:j + 1, cs]
        xpad_s[5:8, cs] = xpad_s[5 + rows:8 + rows, cs]
        xa.append(_silu(conv))
        tick()
        yield

    small = proj_s[:, COL_SMALL:COL_SMALL + 128]
    logf = _gate_logf(small, gw2_ref, gb_ref)

    def get_ss(h):
        return ss_s[SSD_HEAD_DIM * h:SSD_HEAD_DIM * (h + 1), :]

    def put_ss(h, val):
        ss_s[SSD_HEAD_DIM * h:SSD_HEAD_DIM * (h + 1), :] = val

    ys = []
    ssd = _ssd_phases_seq(xa[0], xa[1][:, :SSD_GROUPS * SSD_STATE], xa[1][:, SSD_GROUPS * SSD_STATE:],
                          small, dtb_ref[...], a_ref[...], d_ref[...], get_ss, put_ss, ys)

    def get_sg(s):
        return sg_s[...]

    def put_sg(s, val):
        sg_s[...] = val

    def gla_chunks():
        for c in range(rows // GLA_CHUNK):
            r0 = c * GLA_CHUNK
            sl = slice(r0, r0 + GLA_CHUNK)
            outs = []
            yield from _gla_phases(proj_s[sl, COL_Q:COL_Q + GLA_KW], proj_s[sl, COL_K:COL_K + GLA_KW],
                                   proj_s[sl, COL_V:COL_V + GLA_VW], logf[sl], get_sg, put_sg, GLA_CHUNK, outs)
            for h in range(GLA_HEADS):
                g_h = proj_s[sl, COL_G + h * GLA_DV:COL_G + (h + 1) * GLA_DV]
                mixed_s[sl, h * GLA_DV:(h + 1) * GLA_DV] = _gla_finish(outs[h], g_h, gnw_ref[...]).astype(BF16)
            yield

    while side[0] is not None:
        next(ssd)
        tick()
        yield
    alive = [ssd, gla_chunks()]
    while alive:
        for phases in list(alive):
            try:
                next(phases)
            except StopIteration:
                alive.remove(phases)
                continue
            yield
    mixed_s[:, GLA_VW:] = _ssd_finish(jnp.concatenate(ys, axis=1), proj_s[:, COL_Z:COL_Z + SSD_INNER],
                                      snw_ref[...]).astype(BF16)
    yield

    mixed = mixed_s[...]
    ms = []
    for c in range(D_MODEL // PROJ_CHUNK):
        ms.append(_dot(mixed, wout_ref[:, c * PROJ_CHUNK:(c + 1) * PROJ_CHUNK]))
        yield
    h_ref[...] = x + _rms(jnp.concatenate(ms, axis=1), lnpost_ref[...])


N_MIXER_WEIGHTS = 13
N_MIXER_SCRATCH = 5
MIX_OFFSET = 2


def _mixer_prompt_kernel(x_ref, *rest):
    consts = rest[:N_MIXER_WEIGHTS]
    h_ref, sg_out, ss_out, conv_out = rest[N_MIXER_WEIGHTS:N_MIXER_WEIGHTS + 4]
    scratch = rest[N_MIXER_WEIGHTS + 4:]
    per_seq = [scratch[i * N_MIXER_SCRATCH:(i + 1) * N_MIXER_SCRATCH] for i in range(MIX_SEQS)]
    t = pl.program_id(1)

    @pl.when(t == 0)
    def _():
        for _, xpad_s, _, sg_s, ss_s in per_seq:
            sg_s[...] = jnp.zeros_like(sg_s)
            ss_s[...] = jnp.zeros_like(ss_s)
            xpad_s[0:8, :] = jnp.zeros((8, SSD_CONV_CH), F32)

    _interleave([_mixer_prompt_chain(x_ref.at[i], h_ref.at[i], *per_seq[i], *consts) for i in range(MIX_SEQS)],
                offset=MIX_OFFSET)

    @pl.when(t == pl.num_programs(1) - 1)
    def _():
        for i, (_, xpad_s, _, sg_s, ss_s) in enumerate(per_seq):
            sg_out[i] = sg_s[...].reshape(GLA_HEADS, GLA_DK, GLA_DV)
            ss_out[i] = ss_s[...].reshape(SSD_HEADS, SSD_HEAD_DIM, SSD_STATE)
            conv_out[i] = xpad_s[5:8, :]


def _mixer_prompt(x, p):
    b, t, d = x.shape
    nt = t // TILE
    consts = [p["ln_mix_pre"], p["w_in"], p["gw2"], p["gb"], p["gnw"], p["conv_w"], p["conv_b"],
              p["dtb"], p["a_blk"], p["d_row"], p["snw"], p["w_out"], p["ln_mix_post"]]
    assert len(consts) == N_MIXER_WEIGHTS
    tok = pl.BlockSpec((MIX_SEQS, TILE, d), lambda i, j: (i, j, 0))
    return pl.pallas_call(
        _mixer_prompt_kernel,
        grid=(b // MIX_SEQS, nt),
        in_specs=[tok] + [_const_spec(c.shape) for c in consts],
        out_specs=[tok,
                   pl.BlockSpec((MIX_SEQS, GLA_HEADS, GLA_DK, GLA_DV), lambda i, j: (i, 0, 0, 0)),
                   pl.BlockSpec((MIX_SEQS, SSD_HEADS, SSD_HEAD_DIM, SSD_STATE), lambda i, j: (i, 0, 0, 0)),
                   pl.BlockSpec((MIX_SEQS, SSD_CONV - 1, SSD_CONV_CH), lambda i, j: (i, 0, 0))],
        out_shape=[jax.ShapeDtypeStruct((b, t, d), F32),
                   jax.ShapeDtypeStruct((b, GLA_HEADS, GLA_DK, GLA_DV), F32),
                   jax.ShapeDtypeStruct((b, SSD_HEADS, SSD_HEAD_DIM, SSD_STATE), F32),
                   jax.ShapeDtypeStruct((b, SSD_CONV - 1, SSD_CONV_CH), F32)],
        scratch_shapes=[pltpu.VMEM((TILE, PROJ_COLS), F32),
                        pltpu.VMEM((TILE + 8, SSD_CONV_CH), F32),
                        pltpu.VMEM((TILE, D_MODEL), BF16),
                        pltpu.VMEM((GLA_KW, GLA_DV), F32),
                        pltpu.VMEM((SSD_INNER, SSD_STATE), F32)] * MIX_SEQS,
        compiler_params=pltpu.CompilerParams(dimension_semantics=("arbitrary", "arbitrary"),
                                             vmem_limit_bytes=VMEM_LIMIT),
        name="mixer_prompt",
    )(x, *consts)


def _softmax_rows(s):
    e = jnp.exp(s - jnp.max(s, axis=-1, keepdims=True))
    return e / jnp.sum(e, axis=-1, keepdims=True)


FFN_CHUNK = 256
FFN_OFFSET = 1


def _attn_ffn_chain(h_ref, y_ref, mk_ref, mv_ref, lnxa_ref, wq_ref, wo_ref, lnxap_ref,
                    lnf_ref, lnfp_ref, wg_ref, wu_ref, wd_ref):
    h = h_ref[...]
    hn = _rms(h, lnxa_ref[...]).astype(BF16)
    yield
    heads_per_chunk = PROJ_CHUNK // XA_HEAD_DIM
    outs = []
    for c in range(D_MODEL // PROJ_CHUNK):
        q = _dot(hn, wq_ref[:, c * PROJ_CHUNK:(c + 1) * PROJ_CHUNK]).astype(BF16)
        yield
        for hh in range(heads_per_chunk):
            hd = c * heads_per_chunk + hh
            sl = slice(hd * XA_HEAD_DIM, (hd + 1) * XA_HEAD_DIM)
            s = _dot_nt(q[:, hh * XA_HEAD_DIM:(hh + 1) * XA_HEAD_DIM], mk_ref[0, :, sl])
            yield
            pr = _softmax_rows(s).astype(BF16)
            yield
            outs.append(_dot(pr, mv_ref[0, :, sl]).astype(BF16))
    a = jnp.concatenate(outs, axis=1)
    yield
    yield from _wo_ffn_phases(h, a, y_ref, wo_ref, lnxap_ref, lnf_ref, lnfp_ref, wg_ref, wu_ref, wd_ref)


def _wo_ffn_phases(h, a, y_ref, wo_ref, lnxap_ref, lnf_ref, lnfp_ref, wg_ref, wu_ref, wd_ref):
    parts = []
    for c in range(D_MODEL // PROJ_CHUNK):
        parts.append(_dot(a, wo_ref[:, c * PROJ_CHUNK:(c + 1) * PROJ_CHUNK]))
        yield
    h = h + _rms(jnp.concatenate(parts, axis=1), lnxap_ref[...])
    hf = _rms(h, lnf_ref[...]).astype(BF16)
    yield
    acts = []
    for c in range(D_FF // FFN_CHUNK):
        cs = slice(c * FFN_CHUNK, (c + 1) * FFN_CHUNK)
        gt = _dot(hf, wg_ref[:, cs])
        up = _dot(hf, wu_ref[:, cs])
        yield
        acts.append((_silu(gt) * up).astype(BF16))
    act = jnp.concatenate(acts, axis=1)
    yield
    parts = []
    for c in range(D_MODEL // PROJ_CHUNK):
        parts.append(_dot(act, wd_ref[:, c * PROJ_CHUNK:(c + 1) * PROJ_CHUNK]))
        yield
    y_ref[...] = h + _rms(jnp.concatenate(parts, axis=1), lnfp_ref[...])


def _attn_ffn_prompt_kernel(h_ref, mk_ref, mv_ref, *rest):
    consts, y_ref = rest[:-1], rest[-1]
    chains = []
    for sub in range(FFN_SUBTILES):
        rs = pl.ds(sub * TILE, TILE)
        chains.append(_attn_ffn_chain(h_ref.at[0, rs], y_ref.at[0, rs], mk_ref, mv_ref, *consts))
    _interleave(chains, offset=FFN_OFFSET)


def _attn_ffn_prompt(h, mkb, mvb, p):
    b, t, d = h.shape
    rows = FFN_SUBTILES * TILE
    consts = [p["ln_xa_pre"], p["w_xq"], p["w_xo"], p["ln_xa_post"], p["ln_ffn_pre"], p["ln_ffn_post"],
              p["w_gate"], p["w_up"], p["w_down"]]
    tok = pl.BlockSpec((1, rows, d), lambda i, j: (i, j, 0))
    mem = pl.BlockSpec((1, mkb.shape[1], d), lambda i, j: (i, 0, 0))
    return pl.pallas_call(
        _attn_ffn_prompt_kernel,
        grid=(b, t // rows),
        in_specs=[tok, mem, mem] + [_const_spec(c.shape) for c in consts],
        out_specs=tok,
        out_shape=jax.ShapeDtypeStruct((b, t, d), F32),
        compiler_params=pltpu.CompilerParams(dimension_semantics=("arbitrary", "arbitrary"),
                                             vmem_limit_bytes=VMEM_LIMIT),
        name="attn_ffn_prompt",
    )(h, mkb, mvb, *consts)


def _proj_sample_kernel(x_ref, lnpre_ref, win_ref, proj_ref):
    proj_ref[...] = _dot(_bf(_rms(x_ref[...], lnpre_ref[...])), win_ref[...])


def _proj_sample(x, p):
    n, d = x.shape
    return pl.pallas_call(
        _proj_sample_kernel,
        grid=(1,),
        in_specs=[_const_spec((n, d)), _const_spec((1, d)), _const_spec(p["w_in"].shape)],
        out_specs=pl.BlockSpec((n, PROJ_COLS), lambda i: (0, 0)),
        out_shape=jax.ShapeDtypeStruct((n, PROJ_COLS), F32),
        compiler_params=pltpu.CompilerParams(dimension_semantics=("arbitrary",),
                                             vmem_limit_bytes=VMEM_LIMIT),
        name="proj_sample",
    )(x, p["ln_mix_pre"], p["w_in"])


def _mixer_sample_kernel(seq_len, proj_ref, tail_ref, sg_ref, ss_ref, gw2_ref, gb_ref, gnw_ref,
                         cw_ref, cb_ref, dtb_ref, a_ref, d_ref, snw_ref,
                         mixed_ref, sg_out, ss_out, xpad_s, tpad_s):
    rows = proj_ref.shape[0]
    n_seq = rows // seq_len
    small = proj_ref[:, COL_SMALL:COL_SMALL + 128]
    logf = _gate_logf(small, gw2_ref, gb_ref)

    xbc = proj_ref[:, COL_XBC:COL_XBC + SSD_CONV_CH]
    xpad_s[0:8, :] = jnp.zeros((8, SSD_CONV_CH), F32)
    xpad_s[8:8 + rows, :] = xbc
    tpad_s[0:rows, :] = tail_ref[...]
    tpad_s[rows:rows + 8, :] = jnp.zeros((8, SSD_CONV_CH), F32)
    pos = lax.broadcasted_iota(jnp.int32, (rows, SSD_CONV_CH), 0) & (seq_len - 1)
    conv = cb_ref[...]
    for j in range(SSD_CONV - 1):
        back = SSD_CONV - 1 - j
        prev = jnp.where(pos >= back, xpad_s[8 - back:8 - back + rows, :],
                         tpad_s[seq_len - back:seq_len - back + rows, :])
        conv = conv + prev * cw_ref[j:j + 1, :]
    conv = conv + xbc * cw_ref[SSD_CONV - 1:SSD_CONV, :]
    xa = _silu(conv)

    def get_ss(s, j):
        return ss_ref[s, 2 * j:2 * j + 2].reshape(2 * SSD_HEAD_DIM, SSD_STATE)

    def put_ss(s, j, val):
        ss_out[s, 2 * j:2 * j + 2] = val.reshape(2, SSD_HEAD_DIM, SSD_STATE)

    def get_sg(s):
        return sg_ref[s].reshape(GLA_KW, GLA_DV)

    def put_sg(s, val):
        sg_out[s] = val.reshape(GLA_HEADS, GLA_DK, GLA_DV)

    ys = []
    outs = []
    _interleave([
        _ssd_phases(xa[:, :SSD_INNER], xa[:, SSD_INNER:SSD_INNER + 256], xa[:, SSD_INNER + 256:],
                    small, dtb_ref[...], a_ref[...], d_ref[...], get_ss, put_ss, seq_len, ys),
        _gla_phases(proj_ref[:, COL_Q:COL_Q + GLA_KW], proj_ref[:, COL_K:COL_K + GLA_KW],
                    proj_ref[:, COL_V:COL_V + GLA_VW], logf, get_sg, put_sg, seq_len, outs)])
    mixed_ref[:, GLA_VW:] = _ssd_finish(jnp.concatenate(ys, axis=1), proj_ref[:, COL_Z:COL_Z + SSD_INNER],
                                        snw_ref[...])
    for h in range(GLA_HEADS):
        g_h = proj_ref[:, COL_G + h * GLA_DV:COL_G + (h + 1) * GLA_DV]
        mixed_ref[:, h * GLA_DV:(h + 1) * GLA_DV] = _gla_finish(outs[h], g_h, gnw_ref[...])


def _mixer_sample(proj, tail, sg, ss, p, seq_len):
    n = proj.shape[0]
    rows = SAMPLE_SEQS * seq_len
    nb = sg.shape[0]
    consts = [p["gw2"], p["gb"], p["gnw"], p["conv_w"], p["conv_b"], p["dtb"], p["a_blk"], p["d_row"],
              p["snw"]]
    sg_spec = pl.BlockSpec((SAMPLE_SEQS, GLA_HEADS, GLA_DK, GLA_DV), lambda i: (i, 0, 0, 0))
    ss_spec = pl.BlockSpec((SAMPLE_SEQS, SSD_HEADS, SSD_HEAD_DIM, SSD_STATE), lambda i: (i, 0, 0, 0))
    return pl.pallas_call(
        functools.partial(_mixer_sample_kernel, seq_len),
        grid=(nb // SAMPLE_SEQS,),
        in_specs=[pl.BlockSpec((rows, PROJ_COLS), lambda i: (i, 0)),
                  pl.BlockSpec((rows, SSD_CONV_CH), lambda i: (i, 0)),
                  sg_spec, ss_spec] + [_const_spec(c.shape) for c in consts],
        out_specs=[pl.BlockSpec((rows, D_MODEL), lambda i: (i, 0)), sg_spec, ss_spec],
        out_shape=[jax.ShapeDtypeStruct((n, D_MODEL), F32),
                   jax.ShapeDtypeStruct(sg.shape, F32), jax.ShapeDtypeStruct(ss.shape, F32)],
        scratch_shapes=[pltpu.VMEM((rows + 8, SSD_CONV_CH), F32),
                        pltpu.VMEM((rows + 8, SSD_CONV_CH), F32)],
        compiler_params=pltpu.CompilerParams(dimension_semantics=("arbitrary",),
                                             vmem_limit_bytes=VMEM_LIMIT),
        name="mixer_sample",
    )(proj, tail, sg, ss, *consts)


def _post_mix_sample_kernel(x_ref, mixed_ref, wout_ref, lnpost_ref, lnxa_ref, wq_ref, h_ref, q_ref):
    m = _dot(_bf(mixed_ref[...]), wout_ref[...])
    h = x_ref[...] + _rms(m, lnpost_ref[...])
    h_ref[...] = h
    q_ref[...] = _bf(_dot(_bf(_rms(h, lnxa_ref[...])), wq_ref[...]))


def _post_mix_sample(x, mixed, p):
    n, d = x.shape
    full = pl.BlockSpec((n, d), lambda i: (0, 0))
    consts = [p["w_out"], p["ln_mix_post"], p["ln_xa_pre"], p["w_xq"]]
    return pl.pallas_call(
        _post_mix_sample_kernel,
        grid=(1,),
        in_specs=[full, full] + [_const_spec(c.shape) for c in consts],
        out_specs=[full, full],
        out_shape=[jax.ShapeDtypeStruct((n, d), F32), jax.ShapeDtypeStruct((n, d), BF16)],
        compiler_params=pltpu.CompilerParams(dimension_semantics=("arbitrary",),
                                             vmem_limit_bytes=VMEM_LIMIT),
        name="post_mix_sample",
    )(x, mixed, *consts)


def _attn_sample_kernel(seq_len, q_ref, k_hbm, v_hbm, a_ref, kbuf, vbuf, sem):
    step = pl.program_id(0)
    rows = q_ref.shape[0]
    n_seq = rows // seq_len

    def copies(at_step, slot):
        out = []
        for s in range(n_seq):
            for hd in range(XA_HEADS):
                b = at_step * n_seq + s
                out.append(pltpu.make_async_copy(k_hbm.at[0, b, :, hd, :], kbuf.at[slot, s, hd], sem.at[0, slot]))
                out.append(pltpu.make_async_copy(v_hbm.at[0, b, :, hd, :], vbuf.at[slot, s, hd], sem.at[1, slot]))
        return out

    @pl.when(step == 0)
    def _():
        for cp in copies(0, 0):
            cp.start()

    @pl.when(step + 1 < pl.num_programs(0))
    def _():
        for cp in copies(step + 1, (step + 1) % 2):
            cp.start()

    slot = step % 2
    for cp in copies(step, slot):
        cp.wait()

    q = q_ref[...]
    results = {}

    def one(s, hd):
        sc = _dot_nt(q[:, hd * XA_HEAD_DIM:(hd + 1) * XA_HEAD_DIM], _bf(kbuf[slot, s, hd]))
        yield
        e = jnp.exp(sc - jnp.max(sc, axis=-1, keepdims=True))
        pr = _bf(e / jnp.sum(e, axis=-1, keepdims=True))
        yield
        results[s, hd] = _dot(pr, _bf(vbuf[slot, s, hd]))

    _interleave([one(s, hd) for s in range(n_seq) for hd in range(XA_HEADS)])
    for hd in range(XA_HEADS):
        sl = slice(hd * XA_HEAD_DIM, (hd + 1) * XA_HEAD_DIM)
        out = results[0, hd]
        for s in range(1, n_seq):
            out = jnp.where(_row_in_seq(rows, seq_len, s, XA_HEAD_DIM), results[s, hd], out)
        a_ref[:, sl] = out


def _attn_sample(q, ck, cv, seq_len):
    n, d = q.shape
    _, nb, m, nh, hd = ck.shape
    rows = ATTN_SEQS * seq_len
    tok = pl.BlockSpec((rows, d), lambda i: (i, 0))
    hbm = pl.BlockSpec(memory_space=pl.ANY)
    return pl.pallas_call(
        functools.partial(_attn_sample_kernel, seq_len),
        grid=(nb // ATTN_SEQS,),
        in_specs=[tok, hbm, hbm],
        out_specs=tok,
        out_shape=jax.ShapeDtypeStruct((n, d), F32),
        scratch_shapes=[pltpu.VMEM((2, ATTN_SEQS, nh, m, hd), F32),
                        pltpu.VMEM((2, ATTN_SEQS, nh, m, hd), F32),
                        pltpu.SemaphoreType.DMA((2, 2))],
        compiler_params=pltpu.CompilerParams(dimension_semantics=("arbitrary",),
                                             vmem_limit_bytes=VMEM_LIMIT),
        name="attn_sample",
    )(q, ck, cv)


def _post_attn_sample_kernel(h_ref, a_ref, *rest):
    consts, y_ref = rest[:-1], rest[-1]
    chains = []
    for sub in range(h_ref.shape[0] // TILE):
        rs = pl.ds(sub * TILE, TILE)
        chains.append(_wo_ffn_phases(h_ref[rs, :], _bf(a_ref[rs, :]), y_ref.at[rs], *consts))
    _interleave(chains, offset=FFN_OFFSET)


def _post_attn_sample(h, a, p):
    n, d = h.shape
    full = pl.BlockSpec((n, d), lambda i: (0, 0))
    consts = [p["w_xo"], p["ln_xa_post"], p["ln_ffn_pre"], p["ln_ffn_post"], p["w_gate"], p["w_up"],
              p["w_down"]]
    return pl.pallas_call(
        _post_attn_sample_kernel,
        grid=(1,),
        in_specs=[full, full] + [_const_spec(c.shape) for c in consts],
        out_specs=full,
        out_shape=jax.ShapeDtypeStruct((n, d), F32),
        compiler_params=pltpu.CompilerParams(dimension_semantics=("arbitrary",),
                                             vmem_limit_bytes=VMEM_LIMIT),
        name="post_attn_sample",
    )(h, a, *consts)


SRC_GLR = GLA_KW + GLA_KW + GLA_VW + GLA_VW
SRC_Z = SRC_GLR + GLA_GATE_RANK
SRC_XBC = SRC_Z + SSD_INNER
SRC_DT = SRC_XBC + SSD_CONV_CH
IN_COLS = SRC_DT + SSD_HEADS


def _pack_w_in_kernel(wt_hbm, o_ref, buf, sem):
    pieces = [(0, COL_Q, SRC_GLR),
              (SRC_Z, COL_Z, SSD_INNER),
              (SRC_XBC, COL_XBC, SSD_CONV_CH),
              (SRC_GLR, COL_SMALL, GLA_GATE_RANK),
              (SRC_DT, COL_SMALL + DT_LANE, SSD_HEADS)]
    copies = [pltpu.make_async_copy(wt_hbm.at[0, pl.ds(src, n), :], buf.at[pl.ds(dst, n), :], sem.at[0])
              for src, dst, n in pieces]
    for cp in copies:
        cp.start()
    used = COL_SMALL + DT_LANE + SSD_HEADS
    buf[used:PROJ_COLS, :] = jnp.zeros((PROJ_COLS - used, D_MODEL), F32)
    for cp in copies:
        cp.wait()
    for j in range(PROJ_COLS // 128):
        blk = buf[128 * j:128 * (j + 1), :]
        if 128 * j < COL_K:
            blk = blk * (GLA_DK ** -0.5)
        o_ref[:, 128 * j:128 * (j + 1)] = blk.T.astype(BF16)


def _pack_w_in(w_in):
    wt = jnp.swapaxes(w_in, 1, 2)
    return pl.pallas_call(
        _pack_w_in_kernel,
        grid=(1,),
        in_specs=[pl.BlockSpec(memory_space=pl.ANY)],
        out_specs=pl.BlockSpec((D_MODEL, PROJ_COLS), lambda i: (0, 0)),
        out_shape=jax.ShapeDtypeStruct((D_MODEL, PROJ_COLS), BF16),
        scratch_shapes=[pltpu.VMEM((PROJ_COLS, D_MODEL), F32), pltpu.SemaphoreType.DMA((1,))],
        compiler_params=pltpu.CompilerParams(dimension_semantics=("arbitrary",),
                                             vmem_limit_bytes=VMEM_LIMIT),
        name="pack_w_in",
    )(wt)


def _pack_params(ln_mix_pre, ln_mix_post, w_in_packed, gla_gate_w2, gla_gate_b, gla_norm_w, ssd_conv_w,
                 ssd_conv_b, ssd_dt_bias, ssd_A_log, ssd_D, ssd_norm_w, w_out, ln_xa_pre, ln_xa_post,
                 mem_norm_w, w_xq, w_xk, w_xv, w_xo, ln_ffn_pre, ln_ffn_post, w_gate, w_up, w_down):

    def small_row(v):
        return jnp.zeros((1, 128), F32).at[0, DT_LANE:DT_LANE + SSD_HEADS].set(v)

    row = lambda v: v.reshape(1, -1)
    return {
        "ln_mix_pre": row(ln_mix_pre), "ln_mix_post": row(ln_mix_post),
        "w_in": w_in_packed,
        "gw2": _bf(jnp.zeros((128, GLA_KW), F32).at[:GLA_GATE_RANK].set(gla_gate_w2)),
        "gb": row(gla_gate_b), "gnw": row(gla_norm_w),
        "conv_w": ssd_conv_w, "conv_b": row(ssd_conv_b),
        "dtb": small_row(ssd_dt_bias),
        "a_blk": small_row(-jnp.exp(ssd_A_log)),
        "d_row": jnp.repeat(ssd_D, SSD_HEAD_DIM).reshape(1, SSD_INNER),
        "snw": row(ssd_norm_w),
        "w_out": _bf(w_out),
        "ln_xa_pre": row(ln_xa_pre), "ln_xa_post": row(ln_xa_post), "mem_norm_w": row(mem_norm_w),
        "w_xq": _bf(w_xq * (XA_HEAD_DIM ** -0.5)), "w_xk": _bf(w_xk), "w_xv": _bf(w_xv), "w_xo": _bf(w_xo),
        "ln_ffn_pre": row(ln_ffn_pre), "ln_ffn_post": row(ln_ffn_post),
        "w_gate": _bf(w_gate), "w_up": _bf(w_up), "w_down": _bf(w_down),
    }


def kernel(x_prompt, x_sample, mem_prompt, state_gla, state_ssm, state_conv, cache_mem_k, cache_mem_v,
           ln_mix_pre, ln_mix_post, w_in, gla_gate_w2, gla_gate_b, gla_norm_w, ssd_conv_w, ssd_conv_b,
           ssd_dt_bias, ssd_A_log, ssd_D, ssd_norm_w, w_out, ln_xa_pre, ln_xa_post, mem_norm_w,
           w_xq, w_xk, w_xv, w_xo, ln_ffn_pre, ln_ffn_post, w_gate, w_up, w_down):
    assert w_in.shape[0] == 1, "single-layer kernel"
    layer = [a[0] for a in (ln_mix_pre, ln_mix_post, w_in, gla_gate_w2, gla_gate_b, gla_norm_w, ssd_conv_w,
                            ssd_conv_b, ssd_dt_bias, ssd_A_log, ssd_D, ssd_norm_w, w_out, ln_xa_pre,
                            ln_xa_post, mem_norm_w, w_xq, w_xk, w_xv, w_xo, ln_ffn_pre, ln_ffn_post,
                            w_gate, w_up, w_down)]
    layer[2] = _pack_w_in(w_in)
    p = _pack_params(*layer)
    bp, mem_len, _ = mem_prompt.shape
    bs, ts, _ = x_sample.shape
    assert ts == SSD_CONV, "the padded conv-tail layout assumes one new row per conv tap"

    mk, mv, mkb, mvb = _memkv(mem_prompt, p["mem_norm_w"], p["w_xk"], p["w_xv"])
    h_p, gla_p, ssm_p, conv_p = _mixer_prompt(x_prompt, p)
    y_p = _attn_ffn_prompt(h_p, mkb, mvb, p)

    xs = x_sample.reshape(bs * ts, D_MODEL)
    proj = _proj_sample(xs, p)
    tail = jnp.pad(state_conv[0], ((0, 0), (1, 0), (0, 0))).reshape(bs * ts, SSD_CONV_CH)
    mixed, gla_s, ssm_s = _mixer_sample(proj, tail, state_gla[0], state_ssm[0], p, ts)
    h_s, q_s = _post_mix_sample(xs, mixed, p)
    a_s = _attn_sample(q_s, cache_mem_k, cache_mem_v, ts)
    y_s = _post_attn_sample(h_s, a_s, p)
    conv_s = proj[:, COL_XBC:COL_XBC + SSD_CONV_CH].reshape(bs, ts, SSD_CONV_CH)[:, ts - (SSD_CONV - 1):]

    return (y_p, y_s.reshape(bs, ts, D_MODEL), gla_p[None], ssm_p[None], conv_p[None],
            mk, mv, gla_s[None], ssm_s[None], conv_s[None])
```

```python
import functools

import jax
import jax.numpy as jnp
from jax import lax
from jax.experimental import pallas as pl
from jax.experimental.pallas import tpu as pltpu

F32 = jnp.float32
BF16 = jnp.bfloat16

D_MODEL = 1024
GLA_HEADS = 4
GLA_DK = 64
GLA_DV = 128
GLA_KW = GLA_HEADS * GLA_DK
GLA_VW = GLA_HEADS * GLA_DV
GLA_GATE_RANK = 16
GLA_GATE_TAU = 16.0
SSD_INNER = 512
SSD_HEAD_DIM = 64
SSD_HEADS = 8
SSD_GROUPS = 2
SSD_STATE = 128
SSD_CONV = 4
SSD_CONV_CH = 1024
XA_HEADS = 4
XA_HEAD_DIM = 256
D_FF = 2816
EPS = 1e-6

COL_Q = 0
COL_K = 256
COL_V = 512
COL_G = 1024
COL_Z = 1536
COL_XBC = 2048
COL_SMALL = 3072
DT_LANE = GLA_GATE_RANK
PROJ_COLS = 3200

GLA_CHUNK = 64
TILE = 256
FFN_SUBTILES = 4
MIX_SEQS = 4
SAMPLE_SEQS = 16
ATTN_SEQS = 4
VMEM_LIMIT = 56 * 1024 * 1024


def _bf(x):
    return x.astype(BF16)


def _dot(a, b):
    return jnp.dot(a, b, preferred_element_type=F32)


def _dot_nt(a, b):
    return lax.dot_general(a, b, (((1,), (1,)), ((), ())), preferred_element_type=F32)


def _dot_tn(a, b):
    return lax.dot_general(a, b, (((0,), (0,)), ((), ())), preferred_element_type=F32)


def _dot3(m, a):
    hi = _bf(a)
    r = a - hi.astype(F32)
    mid = _bf(r)
    lo = _bf(r - mid.astype(F32))
    return _dot(m, hi) + _dot(m, mid) + _dot(m, lo)


def _rms(x, w):
    ms = jnp.mean(x * x, axis=-1, keepdims=True)
    return x * lax.rsqrt(ms + EPS) * w


def _silu(x):
    return x / (1.0 + jnp.exp(-x))


def _softplus(x):
    e = jnp.exp(-jnp.abs(x))
    u = 1.0 + e
    log1p_e = jnp.where(u == 1.0, e, jnp.log(u) * (e / (u - 1.0)))
    return jnp.maximum(x, 0.0) + log1p_e


def _log_sigmoid(x):
    return jnp.minimum(x, 0.0) - jnp.log(1.0 + jnp.exp(-jnp.abs(x)))


def _seg_masks(rows, seq_len):
    t = lax.broadcasted_iota(jnp.int32, (rows, rows), 0)
    u = lax.broadcasted_iota(jnp.int32, (rows, rows), 1)
    if seq_len == rows:
        same = None
        tril = u <= t
    else:
        shift = seq_len.bit_length() - 1
        assert 1 << shift == seq_len
        same = (t >> shift) == (u >> shift)
        tril = jnp.logical_and(same, u <= t)
    return same, tril


def _mask_to_bf16(mask, rows):
    if mask is None:
        return jnp.ones((rows, rows), BF16)
    return jnp.where(mask, 1.0, 0.0).astype(BF16)


def _row_in_seq(rows, seq_len, s, width):
    r = lax.broadcasted_iota(jnp.int32, (rows, width), 0)
    return jnp.logical_and(r >= s * seq_len, r < (s + 1) * seq_len)


def _interleave(chains, offset=0):
    waiting = list(enumerate(chains))
    active = []
    rnd = 0
    while waiting or active:
        while waiting and waiting[0][0] * offset <= rnd:
            active.append(waiting.pop(0)[1])
        alive = []
        for c in active:
            try:
                next(c)
                alive.append(c)
            except StopIteration:
                pass
        active = alive
        rnd += 1


def _pad_rows(x, rows):
    if x.shape[0] >= rows:
        return x
    return jnp.concatenate([x, jnp.zeros((rows - x.shape[0], x.shape[1]), x.dtype)], axis=0)


def _gla_phases(q, k, v, lf, get_state, put_state, seq_len, outs):
    rows = q.shape[0]
    n_seq = rows // seq_len
    same, tril = _seg_masks(rows, seq_len)
    tril_b = _mask_to_bf16(tril, rows)
    same_b = _mask_to_bf16(same, rows)

    b = _dot3(tril_b, lf)
    if n_seq == 1:
        bl = jnp.broadcast_to(b[rows - 1:rows, :], b.shape)
    else:
        bl = _dot3(same_b, lf)
    qt = q * jnp.exp(b)
    kt = k * jnp.exp(-b)
    kd = k * jnp.exp(bl - b)
    bl_t = _pad_rows(bl, 128).T
    yield

    lane = lax.broadcasted_iota(jnp.int32, (rows, GLA_KW), 1)
    lhs = jnp.concatenate(
        [jnp.where((lane >= h * GLA_DK) & (lane < (h + 1) * GLA_DK), qt, 0.0)
         for h in range(GLA_HEADS)], axis=0).astype(BF16)
    att = _dot_nt(lhs, kt.astype(BF16))
    t4 = lax.broadcasted_iota(jnp.int32, (GLA_HEADS * rows, rows), 0) & (rows - 1)
    u4 = lax.broadcasted_iota(jnp.int32, (GLA_HEADS * rows, rows), 1)
    causal4 = u4 <= t4
    if n_seq > 1:
        shift = seq_len.bit_length() - 1
        causal4 = jnp.logical_and(causal4, (t4 >> shift) == (u4 >> shift))
    att = jnp.where(causal4, att, 0.0).astype(BF16)
    vb = v.astype(BF16)
    kdb = kd.astype(BF16)
    yield

    o_inter = None
    r4 = lax.broadcasted_iota(jnp.int32, (GLA_HEADS * rows, GLA_DV), 0) & (rows - 1)
    for s in range(n_seq):
        st = get_state(s)
        oi = _dot(lhs, st.astype(BF16))
        if n_seq == 1:
            o_inter = oi
            kd_s = kdb
        else:
            m4 = jnp.logical_and(r4 >= s * seq_len, r4 < (s + 1) * seq_len)
            oi = jnp.where(m4, oi, 0.0)
            o_inter = oi if o_inter is None else o_inter + oi
            kd_s = jnp.where(_row_in_seq(rows, seq_len, s, GLA_KW), kdb, jnp.zeros_like(kdb))
        blocks = []
        for hp in range(GLA_HEADS // 2):
            upd = _dot_tn(kd_s[:, 2 * hp * GLA_DK:2 * (hp + 1) * GLA_DK],
                          vb[:, 2 * hp * GLA_DV:2 * (hp + 1) * GLA_DV])
            blocks += [upd[:GLA_DK, :GLA_DV], upd[GLA_DK:, GLA_DV:]]
        upd_d = jnp.concatenate(blocks, axis=0)
        c0 = s * seq_len
        decay = jnp.exp(jnp.broadcast_to(bl_t[:, c0:c0 + 1], (GLA_KW, GLA_DV)))
        put_state(s, decay * st + upd_d)
        if n_seq > 1 and s % 4 == 3:
            yield
    yield

    for h in range(GLA_HEADS):
        o_h = _dot(att[h * rows:(h + 1) * rows], vb[:, h * GLA_DV:(h + 1) * GLA_DV])
        outs.append(o_h + o_inter[h * rows:(h + 1) * rows])


def _ssd_phases(xs, bm, cm, dt_raw_blk, dtb_blk, a_blk, d_row, get_state, put_state, seq_len, ys):
    rows = xs.shape[0]
    n_seq = rows // seq_len
    same, tril = _seg_masks(rows, seq_len)
    tril_b = _mask_to_bf16(tril, rows)
    same_b = _mask_to_bf16(same, rows)

    dt = _softplus(dt_raw_blk + dtb_blk)
    dta = dt * a_blk
    lc = _dot3(tril_b, dta)
    if n_seq == 1:
        ll = jnp.broadcast_to(lc[rows - 1:rows, :], lc.shape)
    else:
        ll = _dot3(same_b, dta)
    elc = jnp.exp(lc)
    w = jnp.exp(ll - lc) * dt
    ell = jnp.exp(ll)
    lc_t = _pad_rows(lc, 128).T[:, :rows]
    yield

    bmb = bm.astype(BF16)
    cmb = cm.astype(BF16)
    cb = [_dot_nt(cmb[:, g * SSD_STATE:(g + 1) * SSD_STATE],
                  bmb[:, g * SSD_STATE:(g + 1) * SSD_STATE]) for g in range(SSD_GROUPS)]
    yield

    lane128 = lax.broadcasted_iota(jnp.int32, (rows, 128), 1)
    lo128 = lane128 < SSD_HEAD_DIM
    sub128 = lax.broadcasted_iota(jnp.int32, (128, 128), 0) < SSD_HEAD_DIM

    def col(x, h, width):
        return jnp.broadcast_to(x[:, DT_LANE + h:DT_LANE + h + 1], (x.shape[0], width))

    heads_per_group = SSD_HEADS // SSD_GROUPS
    for j in range(SSD_HEADS // 2):
        g = (2 * j) // heads_per_group
        xs_p = xs[:, 128 * j:128 * (j + 1)]
        dt_p = jnp.where(lo128, col(dt, 2 * j, 128), col(dt, 2 * j + 1, 128))
        xdt = xs_p * dt_p
        x_lo = jnp.where(lo128, xdt, 0.0).astype(BF16)
        x_hi = jnp.where(lo128, 0.0, xdt).astype(BF16)
        y_p = None
        for h, x_half in ((2 * j, x_lo), (2 * j + 1, x_hi)):
            seg = col(lc, h, rows) - jnp.broadcast_to(lc_t[DT_LANE + h:DT_LANE + h + 1, :], (rows, rows))
            w_h = jnp.where(tril, cb[g] * jnp.exp(seg), 0.0).astype(BF16)
            y_h = _dot(w_h, x_half)
            y_p = y_h if y_p is None else y_p + y_h
            yield

        elc_p = jnp.where(lo128, col(elc, 2 * j, 128), col(elc, 2 * j + 1, 128))
        w_p = jnp.where(lo128, col(w, 2 * j, 128), col(w, 2 * j + 1, 128))
        xw = (xs_p * w_p).astype(BF16)
        bm_g = bmb[:, g * SSD_STATE:(g + 1) * SSD_STATE]
        cm_g = cmb[:, g * SSD_STATE:(g + 1) * SSD_STATE]
        y_inter = None
        for s in range(n_seq):
            st = get_state(s, j)
            ci = _dot_nt(cm_g, st.astype(BF16))
            if n_seq == 1:
                y_inter = ci
                xw_s = xw
            else:
                msk = _row_in_seq(rows, seq_len, s, 128)
                ci = jnp.where(msk, ci, 0.0)
                y_inter = ci if y_inter is None else y_inter + ci
                xw_s = jnp.where(msk, xw, jnp.zeros_like(xw))
            upd = _dot_tn(xw_s, bm_g)
            r0 = s * seq_len
            e0 = jnp.broadcast_to(ell[r0:r0 + 1, DT_LANE + 2 * j:DT_LANE + 2 * j + 1], (128, 128))
            e1 = jnp.broadcast_to(ell[r0:r0 + 1, DT_LANE + 2 * j + 1:DT_LANE + 2 * j + 2], (128, 128))
            put_state(s, j, jnp.where(sub128, e0, e1) * st + upd)
        ys.append(y_p + y_inter * elc_p + d_row[:, 128 * j:128 * (j + 1)] * xs_p)
        yield


def _ssd_phases_seq(xs, bm, cm, dt_raw_blk, dtb_blk, a_blk, d_row, get_state, put_state, ys):
    rows = xs.shape[0]
    _, tril = _seg_masks(rows, rows)
    tril_b = _mask_to_bf16(tril, rows)
    src = lax.broadcasted_iota(jnp.int32, (rows, rows), 0)
    dst = lax.broadcasted_iota(jnp.int32, (rows, rows), 1)
    causal_t = src <= dst

    dt = _softplus(dt_raw_blk + dtb_blk)
    lc = _dot3(tril_b, dt * a_blk)
    lc_t = lc.T
    dt_t = dt.T
    elc_t = jnp.exp(lc_t)
    w_t = jnp.exp(lc_t[:, rows - 1:rows] - lc_t) * dt_t
    ell = jnp.exp(lc[rows - 1:rows, :])
    yield

    xs_t = xs.T
    bmb = bm.astype(BF16)
    cmb = cm.astype(BF16)
    cb_t = [_dot_nt(bmb[:, g * SSD_STATE:(g + 1) * SSD_STATE],
                    cmb[:, g * SSD_STATE:(g + 1) * SSD_STATE]) for g in range(SSD_GROUPS)]
    yield

    heads_per_group = SSD_HEADS // SSD_GROUPS
    y_t = []
    for h in range(SSD_HEADS):
        g = h // heads_per_group
        lane = DT_LANE + h
        xs_h = xs_t[h * SSD_HEAD_DIM:(h + 1) * SSD_HEAD_DIM, :]
        seg_t = lc_t[lane:lane + 1, :] - jnp.broadcast_to(lc[:, lane:lane + 1], (rows, rows))
        w_h = jnp.where(causal_t, cb_t[g] * jnp.exp(seg_t), 0.0).astype(BF16)
        y_h = _dot((xs_h * dt_t[lane:lane + 1, :]).astype(BF16), w_h)
        st = get_state(h)
        ci = _dot_nt(st.astype(BF16), cmb[:, g * SSD_STATE:(g + 1) * SSD_STATE])
        y_t.append(y_h + ci * elc_t[lane:lane + 1, :])
        upd = _dot((xs_h * w_t[lane:lane + 1, :]).astype(BF16), bmb[:, g * SSD_STATE:(g + 1) * SSD_STATE])
        put_state(h, jnp.broadcast_to(ell[:, lane:lane + 1], st.shape) * st + upd)
        yield
    ys.append(jnp.concatenate(y_t, axis=0).T + d_row * xs)


def _gate_logf(small_blk, gw2_ref, gb_ref):
    logits = _dot(_bf(small_blk), gw2_ref[...]) + gb_ref[...]
    return _log_sigmoid(logits) * (1.0 / GLA_GATE_TAU)


def _gla_finish(o_h, g_h, gnw):
    return _rms(o_h, gnw) * _silu(g_h)


def _ssd_finish(y, z, snw):
    y = y * _silu(z)
    gs = SSD_INNER // SSD_GROUPS
    parts = [_rms(y[:, g * gs:(g + 1) * gs], snw[:, g * gs:(g + 1) * gs]) for g in range(SSD_GROUPS)]
    return jnp.concatenate(parts, axis=1)


def _memkv_kernel(mem_ref, nw_ref, wk_ref, wv_ref, k_ref, v_ref, kb_ref, vb_ref):
    mn = _bf(_rms(mem_ref[0], nw_ref[...]))
    mk = _dot(mn, wk_ref[...])
    mv = _dot(mn, wv_ref[...])
    for h in range(XA_HEADS):
        k_ref[0, 0, :, h, :] = mk[:, h * XA_HEAD_DIM:(h + 1) * XA_HEAD_DIM]
        v_ref[0, 0, :, h, :] = mv[:, h * XA_HEAD_DIM:(h + 1) * XA_HEAD_DIM]
    kb_ref[0] = _bf(mk)
    vb_ref[0] = _bf(mv)


def _const_spec(shape):
    nd = len(shape)
    return pl.BlockSpec(shape, lambda *_: (0,) * nd, pipeline_mode=pl.Buffered(1))


def _memkv(mem, nw, wk, wv):
    b, m, d = mem.shape
    blk = pl.BlockSpec((1, m, d), lambda i: (i, 0, 0))
    cache_blk = pl.BlockSpec((1, 1, m, XA_HEADS, XA_HEAD_DIM), lambda i: (0, i, 0, 0, 0))
    cache_shape = jax.ShapeDtypeStruct((1, b, m, XA_HEADS, XA_HEAD_DIM), F32)
    return pl.pallas_call(
        _memkv_kernel,
        grid=(b,),
        in_specs=[blk, _const_spec((1, d)), _const_spec((d, d)), _const_spec((d, d))],
        out_specs=[cache_blk, cache_blk, blk, blk],
        out_shape=[cache_shape] * 2 + [jax.ShapeDtypeStruct((b, m, d), BF16)] * 2,
        compiler_params=pltpu.CompilerParams(dimension_semantics=("arbitrary",),
                                             vmem_limit_bytes=VMEM_LIMIT),
        name="mem_kv",
    )(mem, nw, wk, wv)


PROJ_CHUNK = 512


def _mixer_prompt_chain(x_ref, h_ref, proj_s, xpad_s, mixed_s, sg_s, ss_s,
                        lnpre_ref, win_ref, gw2_ref, gb_ref, gnw_ref, cw_ref, cb_ref,
                        dtb_ref, a_ref, d_ref, snw_ref, wout_ref, lnpost_ref):
    rows = TILE
    x = x_ref[...]
    hn = _rms(x, lnpre_ref[...]).astype(BF16)
    yield

    def proj_cols(c0, width):
        proj_s[:, c0:c0 + width] = _dot(hn, win_ref[:, c0:c0 + width])

    def late_proj():
        for c0 in (COL_Q, COL_V, COL_G, COL_Z):
            proj_cols(c0, PROJ_CHUNK)
            yield

    side = [late_proj()]

    def tick():
        if side[0] is not None:
            try:
                next(side[0])
            except StopIteration:
                side[0] = None

    for c in range(SSD_CONV_CH // PROJ_CHUNK):
        proj_cols(COL_XBC + c * PROJ_CHUNK, PROJ_CHUNK)
        yield
    proj_cols(COL_SMALL, 128)

    xa = []
    for c in range(SSD_CONV_CH // PROJ_CHUNK):
        cs = slice(c * PROJ_CHUNK, (c + 1) * PROJ_CHUNK)
        xpad_s[8:8 + rows, cs] = proj_s[:, COL_XBC + c * PROJ_CHUNK:COL_XBC + (c + 1) * PROJ_CHUNK]
        conv = cb_ref[:, cs]
        for j in range(SSD_CONV):
            conv = conv + xpad_s[5 + j:5 + j + rows, cs] * cw_ref[j:j + 1, cs]
        xpad_s[5:8, cs] = xpad_s[5 + rows:8 + rows, cs]
        xa.append(_silu(conv))
        tick()
        yield

    small = proj_s[:, COL_SMALL:COL_SMALL + 128]
    logf = _gate_logf(small, gw2_ref, gb_ref)

    def get_ss(h):
        return ss_s[SSD_HEAD_DIM * h:SSD_HEAD_DIM * (h + 1), :]

    def put_ss(h, val):
        ss_s[SSD_HEAD_DIM * h:SSD_HEAD_DIM * (h + 1), :] = val

    ys = []
    ssd = _ssd_phases_seq(xa[0], xa[1][:, :SSD_GROUPS * SSD_STATE], xa[1][:, SSD_GROUPS * SSD_STATE:],
                          small, dtb_ref[...], a_ref[...], d_ref[...], get_ss, put_ss, ys)

    def get_sg(s):
        return sg_s[...]

    def put_sg(s, val):
        sg_s[...] = val

    def gla_chunks():
        for c in range(rows // GLA_CHUNK):
            r0 = c * GLA_CHUNK
            sl = slice(r0, r0 + GLA_CHUNK)
            outs = []
            yield from _gla_phases(proj_s[sl, COL_Q:COL_Q + GLA_KW], proj_s[sl, COL_K:COL_K + GLA_KW],
                                   proj_s[sl, COL_V:COL_V + GLA_VW], logf[sl], get_sg, put_sg, GLA_CHUNK, outs)
            for h in range(GLA_HEADS):
                g_h = proj_s[sl, COL_G + h * GLA_DV:COL_G + (h + 1) * GLA_DV]
                mixed_s[sl, h * GLA_DV:(h + 1) * GLA_DV] = _gla_finish(outs[h], g_h, gnw_ref[...]).astype(BF16)
            yield

    while side[0] is not None:
        next(ssd)
        tick()
        yield
    alive = [ssd, gla_chunks()]
    while alive:
        for phases in list(alive):
            try:
                next(phases)
            except StopIteration:
                alive.remove(phases)
                continue
            yield
    mixed_s[:, GLA_VW:] = _ssd_finish(jnp.concatenate(ys, axis=1), proj_s[:, COL_Z:COL_Z + SSD_INNER],
                                      snw_ref[...]).astype(BF16)
    yield

    mixed = mixed_s[...]
    ms = []
    for c in range(D_MODEL // PROJ_CHUNK):
        ms.append(_dot(mixed, wout_ref[:, c * PROJ_CHUNK:(c + 1) * PROJ_CHUNK]))
        yield
    h_ref[...] = x + _rms(jnp.concatenate(ms, axis=1), lnpost_ref[...])


N_MIXER_WEIGHTS = 13
N_MIXER_SCRATCH = 5
MIX_OFFSET = 2


def _mixer_prompt_kernel(x_ref, *rest):
    consts = rest[:N_MIXER_WEIGHTS]
    h_ref, sg_out, ss_out, conv_out = rest[N_MIXER_WEIGHTS:N_MIXER_WEIGHTS + 4]
    scratch = rest[N_MIXER_WEIGHTS + 4:]
    per_seq = [scratch[i * N_MIXER_SCRATCH:(i + 1) * N_MIXER_SCRATCH] for i in range(MIX_SEQS)]
    t = pl.program_id(1)

    @pl.when(t == 0)
    def _():
        for _, xpad_s, _, sg_s, ss_s in per_seq:
            sg_s[...] = jnp.zeros_like(sg_s)
            ss_s[...] = jnp.zeros_like(ss_s)
            xpad_s[0:8, :] = jnp.zeros((8, SSD_CONV_CH), F32)

    _interleave([_mixer_prompt_chain(x_ref.at[i], h_ref.at[i], *per_seq[i], *consts) for i in range(MIX_SEQS)],
                offset=MIX_OFFSET)

    @pl.when(t == pl.num_programs(1) - 1)
    def _():
        for i, (_, xpad_s, _, sg_s, ss_s) in enumerate(per_seq):
            sg_out[i] = sg_s[...].reshape(GLA_HEADS, GLA_DK, GLA_DV)
            ss_out[i] = ss_s[...].reshape(SSD_HEADS, SSD_HEAD_DIM, SSD_STATE)
            conv_out[i] = xpad_s[5:8, :]


def _mixer_prompt(x, p):
    b, t, d = x.shape
    nt = t // TILE
    consts = [p["ln_mix_pre"], p["w_in"], p["gw2"], p["gb"], p["gnw"], p["conv_w"], p["conv_b"],
              p["dtb"], p["a_blk"], p["d_row"], p["snw"], p["w_out"], p["ln_mix_post"]]
    assert len(consts) == N_MIXER_WEIGHTS
    tok = pl.BlockSpec((MIX_SEQS, TILE, d), lambda i, j: (i, j, 0))
    return pl.pallas_call(
        _mixer_prompt_kernel,
        grid=(b // MIX_SEQS, nt),
        in_specs=[tok] + [_const_spec(c.shape) for c in consts],
        out_specs=[tok,
                   pl.BlockSpec((MIX_SEQS, GLA_HEADS, GLA_DK, GLA_DV), lambda i, j: (i, 0, 0, 0)),
                   pl.BlockSpec((MIX_SEQS, SSD_HEADS, SSD_HEAD_DIM, SSD_STATE), lambda i, j: (i, 0, 0, 0)),
                   pl.BlockSpec((MIX_SEQS, SSD_CONV - 1, SSD_CONV_CH), lambda i, j: (i, 0, 0))],
        out_shape=[jax.ShapeDtypeStruct((b, t, d), F32),
                   jax.ShapeDtypeStruct((b, GLA_HEADS, GLA_DK, GLA_DV), F32),
                   jax.ShapeDtypeStruct((b, SSD_HEADS, SSD_HEAD_DIM, SSD_STATE), F32),
                   jax.ShapeDtypeStruct((b, SSD_CONV - 1, SSD_CONV_CH), F32)],
        scratch_shapes=[pltpu.VMEM((TILE, PROJ_COLS), F32),
                        pltpu.VMEM((TILE + 8, SSD_CONV_CH), F32),
                        pltpu.VMEM((TILE, D_MODEL), BF16),
                        pltpu.VMEM((GLA_KW, GLA_DV), F32),
                        pltpu.VMEM((SSD_INNER, SSD_STATE), F32)] * MIX_SEQS,
        compiler_params=pltpu.CompilerParams(dimension_semantics=("arbitrary", "arbitrary"),
                                             vmem_limit_bytes=VMEM_LIMIT),
        name="mixer_prompt",
    )(x, *consts)


def _softmax_rows(s):
    e = jnp.exp(s - jnp.max(s, axis=-1, keepdims=True))
    return e / jnp.sum(e, axis=-1, keepdims=True)


FFN_CHUNK = 256
FFN_OFFSET = 1


def _attn_ffn_chain(h_ref, y_ref, mk_ref, mv_ref, lnxa_ref, wq_ref, wo_ref, lnxap_ref,
                    lnf_ref, lnfp_ref, wg_ref, wu_ref, wd_ref):
    h = h_ref[...]
    hn = _rms(h, lnxa_ref[...]).astype(BF16)
    yield
    heads_per_chunk = PROJ_CHUNK // XA_HEAD_DIM
    outs = []
    for c in range(D_MODEL // PROJ_CHUNK):
        q = _dot(hn, wq_ref[:, c * PROJ_CHUNK:(c + 1) * PROJ_CHUNK]).astype(BF16)
        yield
        for hh in range(heads_per_chunk):
            hd = c * heads_per_chunk + hh
            sl = slice(hd * XA_HEAD_DIM, (hd + 1) * XA_HEAD_DIM)
            s = _dot_nt(q[:, hh * XA_HEAD_DIM:(hh + 1) * XA_HEAD_DIM], mk_ref[0, :, sl])
            yield
            pr = _softmax_rows(s).astype(BF16)
            yield
            outs.append(_dot(pr, mv_ref[0, :, sl]).astype(BF16))
    a = jnp.concatenate(outs, axis=1)
    yield
    yield from _wo_ffn_phases(h, a, y_ref, wo_ref, lnxap_ref, lnf_ref, lnfp_ref, wg_ref, wu_ref, wd_ref)


def _wo_ffn_phases(h, a, y_ref, wo_ref, lnxap_ref, lnf_ref, lnfp_ref, wg_ref, wu_ref, wd_ref):
    parts = []
    for c in range(D_MODEL // PROJ_CHUNK):
        parts.append(_dot(a, wo_ref[:, c * PROJ_CHUNK:(c + 1) * PROJ_CHUNK]))
        yield
    h = h + _rms(jnp.concatenate(parts, axis=1), lnxap_ref[...])
    hf = _rms(h, lnf_ref[...]).astype(BF16)
    yield
    acts = []
    for c in range(D_FF // FFN_CHUNK):
        cs = slice(c * FFN_CHUNK, (c + 1) * FFN_CHUNK)
        gt = _dot(hf, wg_ref[:, cs])
        up = _dot(hf, wu_ref[:, cs])
        yield
        acts.append((_silu(gt) * up).astype(BF16))
    act = jnp.concatenate(acts, axis=1)
    yield
    parts = []
    for c in range(D_MODEL // PROJ_CHUNK):
        parts.append(_dot(act, wd_ref[:, c * PROJ_CHUNK:(c + 1) * PROJ_CHUNK]))
        yield
    y_ref[...] = h + _rms(jnp.concatenate(parts, axis=1), lnfp_ref[...])


def _attn_ffn_prompt_kernel(h_ref, mk_ref, mv_ref, *rest):
    consts, y_ref = rest[:-1], rest[-1]
    chains = []
    for sub in range(FFN_SUBTILES):
        rs = pl.ds(sub * TILE, TILE)
        chains.append(_attn_ffn_chain(h_ref.at[0, rs], y_ref.at[0, rs], mk_ref, mv_ref, *consts))
    _interleave(chains, offset=FFN_OFFSET)


def _attn_ffn_prompt(h, mkb, mvb, p):
    b, t, d = h.shape
    rows = FFN_SUBTILES * TILE
    consts = [p["ln_xa_pre"], p["w_xq"], p["w_xo"], p["ln_xa_post"], p["ln_ffn_pre"], p["ln_ffn_post"],
              p["w_gate"], p["w_up"], p["w_down"]]
    tok = pl.BlockSpec((1, rows, d), lambda i, j: (i, j, 0))
    mem = pl.BlockSpec((1, mkb.shape[1], d), lambda i, j: (i, 0, 0))
    return pl.pallas_call(
        _attn_ffn_prompt_kernel,
        grid=(b, t // rows),
        in_specs=[tok, mem, mem] + [_const_spec(c.shape) for c in consts],
        out_specs=tok,
        out_shape=jax.ShapeDtypeStruct((b, t, d), F32),
        compiler_params=pltpu.CompilerParams(dimension_semantics=("arbitrary", "arbitrary"),
                                             vmem_limit_bytes=VMEM_LIMIT),
        name="attn_ffn_prompt",
    )(h, mkb, mvb, *consts)


def _proj_sample_kernel(x_ref, lnpre_ref, win_ref, proj_ref):
    proj_ref[...] = _dot(_bf(_rms(x_ref[...], lnpre_ref[...])), win_ref[...])


def _proj_sample(x, p):
    n, d = x.shape
    return pl.pallas_call(
        _proj_sample_kernel,
        grid=(1,),
        in_specs=[_const_spec((n, d)), _const_spec((1, d)), _const_spec(p["w_in"].shape)],
        out_specs=pl.BlockSpec((n, PROJ_COLS), lambda i: (0, 0)),
        out_shape=jax.ShapeDtypeStruct((n, PROJ_COLS), F32),
        compiler_params=pltpu.CompilerParams(dimension_semantics=("arbitrary",),
                                             vmem_limit_bytes=VMEM_LIMIT),
        name="proj_sample",
    )(x, p["ln_mix_pre"], p["w_in"])


def _mixer_sample_kernel(seq_len, proj_ref, tail_ref, sg_ref, ss_ref, gw2_ref, gb_ref, gnw_ref,
                         cw_ref, cb_ref, dtb_ref, a_ref, d_ref, snw_ref,
                         mixed_ref, sg_out, ss_out, xpad_s, tpad_s):
    rows = proj_ref.shape[0]
    n_seq = rows // seq_len
    small = proj_ref[:, COL_SMALL:COL_SMALL + 128]
    logf = _gate_logf(small, gw2_ref, gb_ref)

    xbc = proj_ref[:, COL_XBC:COL_XBC + SSD_CONV_CH]
    xpad_s[0:8, :] = jnp.zeros((8, SSD_CONV_CH), F32)
    xpad_s[8:8 + rows, :] = xbc
    tpad_s[0:rows, :] = tail_ref[...]
    tpad_s[rows:rows + 8, :] = jnp.zeros((8, SSD_CONV_CH), F32)
    pos = lax.broadcasted_iota(jnp.int32, (rows, SSD_CONV_CH), 0) & (seq_len - 1)
    conv = cb_ref[...]
    for j in range(SSD_CONV - 1):
        back = SSD_CONV - 1 - j
        prev = jnp.where(pos >= back, xpad_s[8 - back:8 - back + rows, :],
                         tpad_s[seq_len - back:seq_len - back + rows, :])
        conv = conv + prev * cw_ref[j:j + 1, :]
    conv = conv + xbc * cw_ref[SSD_CONV - 1:SSD_CONV, :]
    xa = _silu(conv)

    def get_ss(s, j):
        return ss_ref[s, 2 * j:2 * j + 2].reshape(2 * SSD_HEAD_DIM, SSD_STATE)

    def put_ss(s, j, val):
        ss_out[s, 2 * j:2 * j + 2] = val.reshape(2, SSD_HEAD_DIM, SSD_STATE)

    def get_sg(s):
        return sg_ref[s].reshape(GLA_KW, GLA_DV)

    def put_sg(s, val):
        sg_out[s] = val.reshape(GLA_HEADS, GLA_DK, GLA_DV)

    ys = []
    outs = []
    _interleave([
        _ssd_phases(xa[:, :SSD_INNER], xa[:, SSD_INNER:SSD_INNER + 256], xa[:, SSD_INNER + 256:],
                    small, dtb_ref[...], a_ref[...], d_ref[...], get_ss, put_ss, seq_len, ys),
        _gla_phases(proj_ref[:, COL_Q:COL_Q + GLA_KW], proj_ref[:, COL_K:COL_K + GLA_KW],
                    proj_ref[:, COL_V:COL_V + GLA_VW], logf, get_sg, put_sg, seq_len, outs)])
    mixed_ref[:, GLA_VW:] = _ssd_finish(jnp.concatenate(ys, axis=1), proj_ref[:, COL_Z:COL_Z + SSD_INNER],
                                        snw_ref[...])
    for h in range(GLA_HEADS):
        g_h = proj_ref[:, COL_G + h * GLA_DV:COL_G + (h + 1) * GLA_DV]
        mixed_ref[:, h * GLA_DV:(h + 1) * GLA_DV] = _gla_finish(outs[h], g_h, gnw_ref[...])


def _mixer_sample(proj, tail, sg, ss, p, seq_len):
    n = proj.shape[0]
    rows = SAMPLE_SEQS * seq_len
    nb = sg.shape[0]
    consts = [p["gw2"], p["gb"], p["gnw"], p["conv_w"], p["conv_b"], p["dtb"], p["a_blk"], p["d_row"],
              p["snw"]]
    sg_spec = pl.BlockSpec((SAMPLE_SEQS, GLA_HEADS, GLA_DK, GLA_DV), lambda i: (i, 0, 0, 0))
    ss_spec = pl.BlockSpec((SAMPLE_SEQS, SSD_HEADS, SSD_HEAD_DIM, SSD_STATE), lambda i: (i, 0, 0, 0))
    return pl.pallas_call(
        functools.partial(_mixer_sample_kernel, seq_len),
        grid=(nb // SAMPLE_SEQS,),
        in_specs=[pl.BlockSpec((rows, PROJ_COLS), lambda i: (i, 0)),
                  pl.BlockSpec((rows, SSD_CONV_CH), lambda i: (i, 0)),
                  sg_spec, ss_spec] + [_const_spec(c.shape) for c in consts],
        out_specs=[pl.BlockSpec((rows, D_MODEL), lambda i: (i, 0)), sg_spec, ss_spec],
        out_shape=[jax.ShapeDtypeStruct((n, D_MODEL), F32),
                   jax.ShapeDtypeStruct(sg.shape, F32), jax.ShapeDtypeStruct(ss.shape, F32)],
        scratch_shapes=[pltpu.VMEM((rows + 8, SSD_CONV_CH), F32),
                        pltpu.VMEM((rows + 8, SSD_CONV_CH), F32)],
        compiler_params=pltpu.CompilerParams(dimension_semantics=("arbitrary",),
                                             vmem_limit_bytes=VMEM_LIMIT),
        name="mixer_sample",
    )(proj, tail, sg, ss, *consts)


def _post_mix_sample_kernel(x_ref, mixed_ref, wout_ref, lnpost_ref, lnxa_ref, wq_ref, h_ref, q_ref):
    m = _dot(_bf(mixed_ref[...]), wout_ref[...])
    h = x_ref[...] + _rms(m, lnpost_ref[...])
    h_ref[...] = h
    q_ref[...] = _bf(_dot(_bf(_rms(h, lnxa_ref[...])), wq_ref[...]))


def _post_mix_sample(x, mixed, p):
    n, d = x.shape
    full = pl.BlockSpec((n, d), lambda i: (0, 0))
    consts = [p["w_out"], p["ln_mix_post"], p["ln_xa_pre"], p["w_xq"]]
    return pl.pallas_call(
        _post_mix_sample_kernel,
        grid=(1,),
        in_specs=[full, full] + [_const_spec(c.shape) for c in consts],
        out_specs=[full, full],
        out_shape=[jax.ShapeDtypeStruct((n, d), F32), jax.ShapeDtypeStruct((n, d), BF16)],
        compiler_params=pltpu.CompilerParams(dimension_semantics=("arbitrary",),
                                             vmem_limit_bytes=VMEM_LIMIT),
        name="post_mix_sample",
    )(x, mixed, *consts)


def _attn_sample_kernel(seq_len, q_ref, k_hbm, v_hbm, a_ref, kbuf, vbuf, sem):
    step = pl.program_id(0)
    rows = q_ref.shape[0]
    n_seq = rows // seq_len

    def copies(at_step, slot):
        out = []
        for s in range(n_seq):
            for hd in range(XA_HEADS):
                b = at_step * n_seq + s
                out.append(pltpu.make_async_copy(k_hbm.at[0, b, :, hd, :], kbuf.at[slot, s, hd], sem.at[0, slot]))
                out.append(pltpu.make_async_copy(v_hbm.at[0, b, :, hd, :], vbuf.at[slot, s, hd], sem.at[1, slot]))
        return out

    @pl.when(step == 0)
    def _():
        for cp in copies(0, 0):
            cp.start()

    @pl.when(step + 1 < pl.num_programs(0))
    def _():
        for cp in copies(step + 1, (step + 1) % 2):
            cp.start()

    slot = step % 2
    for cp in copies(step, slot):
        cp.wait()

    q = q_ref[...]
    results = {}

    def one(s, hd):
        sc = _dot_nt(q[:, hd * XA_HEAD_DIM:(hd + 1) * XA_HEAD_DIM], _bf(kbuf[slot, s, hd]))
        yield
        e = jnp.exp(sc - jnp.max(sc, axis=-1, keepdims=True))
        pr = _bf(e / jnp.sum(e, axis=-1, keepdims=True))
        yield
        results[s, hd] = _dot(pr, _bf(vbuf[slot, s, hd]))

    _interleave([one(s, hd) for s in range(n_seq) for hd in range(XA_HEADS)])
    for hd in range(XA_HEADS):
        sl = slice(hd * XA_HEAD_DIM, (hd + 1) * XA_HEAD_DIM)
        out = results[0, hd]
        for s in range(1, n_seq):
            out = jnp.where(_row_in_seq(rows, seq_len, s, XA_HEAD_DIM), results[s, hd], out)
        a_ref[:, sl] = out


def _attn_sample(q, ck, cv, seq_len):
    n, d = q.shape
    _, nb, m, nh, hd = ck.shape
    rows = ATTN_SEQS * seq_len
    tok = pl.BlockSpec((rows, d), lambda i: (i, 0))
    hbm = pl.BlockSpec(memory_space=pl.ANY)
    return pl.pallas_call(
        functools.partial(_attn_sample_kernel, seq_len),
        grid=(nb // ATTN_SEQS,),
        in_specs=[tok, hbm, hbm],
        out_specs=tok,
        out_shape=jax.ShapeDtypeStruct((n, d), F32),
        scratch_shapes=[pltpu.VMEM((2, ATTN_SEQS, nh, m, hd), F32),
                        pltpu.VMEM((2, ATTN_SEQS, nh, m, hd), F32),
                        pltpu.SemaphoreType.DMA((2, 2))],
        compiler_params=pltpu.CompilerParams(dimension_semantics=("arbitrary",),
                                             vmem_limit_bytes=VMEM_LIMIT),
        name="attn_sample",
    )(q, ck, cv)


def _post_attn_sample_kernel(h_ref, a_ref, *rest):
    consts, y_ref = rest[:-1], rest[-1]
    chains = []
    for sub in range(h_ref.shape[0] // TILE):
        rs = pl.ds(sub * TILE, TILE)
        chains.append(_wo_ffn_phases(h_ref[rs, :], _bf(a_ref[rs, :]), y_ref.at[rs], *consts))
    _interleave(chains, offset=FFN_OFFSET)


def _post_attn_sample(h, a, p):
    n, d = h.shape
    full = pl.BlockSpec((n, d), lambda i: (0, 0))
    consts = [p["w_xo"], p["ln_xa_post"], p["ln_ffn_pre"], p["ln_ffn_post"], p["w_gate"], p["w_up"],
              p["w_down"]]
    return pl.pallas_call(
        _post_attn_sample_kernel,
        grid=(1,),
        in_specs=[full, full] + [_const_spec(c.shape) for c in consts],
        out_specs=full,
        out_shape=jax.ShapeDtypeStruct((n, d), F32),
        compiler_params=pltpu.CompilerParams(dimension_semantics=("arbitrary",),
                                             vmem_limit_bytes=VMEM_LIMIT),
        name="post_attn_sample",
    )(h, a, *consts)


SRC_GLR = GLA_KW + GLA_KW + GLA_VW + GLA_VW
SRC_Z = SRC_GLR + GLA_GATE_RANK
SRC_XBC = SRC_Z + SSD_INNER
SRC_DT = SRC_XBC + SSD_CONV_CH
IN_COLS = SRC_DT + SSD_HEADS


def _pack_w_in_kernel(wt_hbm, o_ref, buf, sem):
    pieces = [(0, COL_Q, SRC_GLR),
              (SRC_Z, COL_Z, SSD_INNER),
              (SRC_XBC, COL_XBC, SSD_CONV_CH),
              (SRC_GLR, COL_SMALL, GLA_GATE_RANK),
              (SRC_DT, COL_SMALL + DT_LANE, SSD_HEADS)]
    copies = [pltpu.make_async_copy(wt_hbm.at[0, pl.ds(src, n), :], buf.at[pl.ds(dst, n), :], sem.at[0])
              for src, dst, n in pieces]
    for cp in copies:
        cp.start()
    used = COL_SMALL + DT_LANE + SSD_HEADS
    buf[used:PROJ_COLS, :] = jnp.zeros((PROJ_COLS - used, D_MODEL), F32)
    for cp in copies:
        cp.wait()
    for j in range(PROJ_COLS // 128):
        blk = buf[128 * j:128 * (j + 1), :]
        if 128 * j < COL_K:
            blk = blk * (GLA_DK ** -0.5)
        o_ref[:, 128 * j:128 * (j + 1)] = blk.T.astype(BF16)


def _pack_w_in(w_in):
    wt = jnp.swapaxes(w_in, 1, 2)
    return pl.pallas_call(
        _pack_w_in_kernel,
        grid=(1,),
        in_specs=[pl.BlockSpec(memory_space=pl.ANY)],
        out_specs=pl.BlockSpec((D_MODEL, PROJ_COLS), lambda i: (0, 0)),
        out_shape=jax.ShapeDtypeStruct((D_MODEL, PROJ_COLS), BF16),
        scratch_shapes=[pltpu.VMEM((PROJ_COLS, D_MODEL), F32), pltpu.SemaphoreType.DMA((1,))],
        compiler_params=pltpu.CompilerParams(dimension_semantics=("arbitrary",),
                                             vmem_limit_bytes=VMEM_LIMIT),
        name="pack_w_in",
    )(wt)


def _pack_params(ln_mix_pre, ln_mix_post, w_in_packed, gla_gate_w2, gla_gate_b, gla_norm_w, ssd_conv_w,
                 ssd_conv_b, ssd_dt_bias, ssd_A_log, ssd_D, ssd_norm_w, w_out, ln_xa_pre, ln_xa_post,
                 mem_norm_w, w_xq, w_xk, w_xv, w_xo, ln_ffn_pre, ln_ffn_post, w_gate, w_up, w_down):

    def small_row(v):
        return jnp.zeros((1, 128), F32).at[0, DT_LANE:DT_LANE + SSD_HEADS].set(v)

    row = lambda v: v.reshape(1, -1)
    return {
        "ln_mix_pre": row(ln_mix_pre), "ln_mix_post": row(ln_mix_post),
        "w_in": w_in_packed,
        "gw2": _bf(jnp.zeros((128, GLA_KW), F32).at[:GLA_GATE_RANK].set(gla_gate_w2)),
        "gb": row(gla_gate_b), "gnw": row(gla_norm_w),
        "conv_w": ssd_conv_w, "conv_b": row(ssd_conv_b),
        "dtb": small_row(ssd_dt_bias),
        "a_blk": small_row(-jnp.exp(ssd_A_log)),
        "d_row": jnp.repeat(ssd_D, SSD_HEAD_DIM).reshape(1, SSD_INNER),
        "snw": row(ssd_norm_w),
        "w_out": _bf(w_out),
        "ln_xa_pre": row(ln_xa_pre), "ln_xa_post": row(ln_xa_post), "mem_norm_w": row(mem_norm_w),
        "w_xq": _bf(w_xq * (XA_HEAD_DIM ** -0.5)), "w_xk": _bf(w_xk), "w_xv": _bf(w_xv), "w_xo": _bf(w_xo),
        "ln_ffn_pre": row(ln_ffn_pre), "ln_ffn_post": row(ln_ffn_post),
        "w_gate": _bf(w_gate), "w_up": _bf(w_up), "w_down": _bf(w_down),
    }


def kernel(x_prompt, x_sample, mem_prompt, state_gla, state_ssm, state_conv, cache_mem_k, cache_mem_v,
           ln_mix_pre, ln_mix_post, w_in, gla_gate_w2, gla_gate_b, gla_norm_w, ssd_conv_w, ssd_conv_b,
           ssd_dt_bias, ssd_A_log, ssd_D, ssd_norm_w, w_out, ln_xa_pre, ln_xa_post, mem_norm_w,
           w_xq, w_xk, w_xv, w_xo, ln_ffn_pre, ln_ffn_post, w_gate, w_up, w_down):
    assert w_in.shape[0] == 1, "single-layer kernel"
    layer = [a[0] for a in (ln_mix_pre, ln_mix_post, w_in, gla_gate_w2, gla_gate_b, gla_norm_w, ssd_conv_w,
                            ssd_conv_b, ssd_dt_bias, ssd_A_log, ssd_D, ssd_norm_w, w_out, ln_xa_pre,
                            ln_xa_post, mem_norm_w, w_xq, w_xk, w_xv, w_xo, ln_ffn_pre, ln_ffn_post,
                            w_gate, w_up, w_down)]
    layer[2] = _pack_w_in(w_in)
    p = _pack_params(*layer)
    bp, mem_len, _ = mem_prompt.shape
    bs, ts, _ = x_sample.shape
    assert ts == SSD_CONV, "the padded conv-tail layout assumes one new row per conv tap"

    mk, mv, mkb, mvb = _memkv(mem_prompt, p["mem_norm_w"], p["w_xk"], p["w_xv"])
    h_p, gla_p, ssm_p, conv_p = _mixer_prompt(x_prompt, p)
    y_p = _attn_ffn_prompt(h_p, mkb, mvb, p)

    xs = x_sample.reshape(bs * ts, D_MODEL)
    proj = _proj_sample(xs, p)
    tail = jnp.pad(state_conv[0], ((0, 0), (1, 0), (0, 0))).reshape(bs * ts, SSD_CONV_CH)
    mixed, gla_s, ssm_s = _mixer_sample(proj, tail, state_gla[0], state_ssm[0], p, ts)
    h_s, q_s = _post_mix_sample(xs, mixed, p)
    a_s = _attn_sample(q_s, cache_mem_k, cache_mem_v, ts)
    y_s = _post_attn_sample(h_s, a_s, p)
    conv_s = proj[:, COL_XBC:COL_XBC + SSD_CONV_CH].reshape(bs, ts, SSD_CONV_CH)[:, ts - (SSD_CONV - 1):]

    return (y_p, y_s.reshape(bs, ts, D_MODEL), gla_p[None], ssm_p[None], conv_p[None],
            mk, mv, gla_s[None], ssm_s[None], conv_s[None])
```
